```python
import jax, jax.numpy as jnp
from jax import lax
import numpy as np

D_MODEL = 1024
BATCH = 8
SEQ = 2048
DEPTH = 1
DEC_BATCH = 128
DEC_SEQ = 4
PAST_LEN = 16384
PAGE_SIZE = 128

D_A = D_MODEL
D_P = D_MODEL
POOL_WINDOWS = (2, 4, 8, 16)
N_POOL_GROUPS = len(POOL_WINDOWS)
POOL_GROUP = D_P // N_POOL_GROUPS
POOL_MAX = max(POOL_WINDOWS)
CONV_W = 3
D_FF = 2816
PLE_DIM = 256
EPS = 1e-6
D_IN_ALL = 3 * D_A + D_P + 2 * D_MODEL

kernel_name = "hybrid_shortconv_pool_convffn_step"


def rmsnorm(x, g):
    xf = x.astype(jnp.float32)
    ms = jnp.mean(xf * xf, axis=-1, keepdims=True)
    return (xf * lax.rsqrt(ms + EPS)).astype(x.dtype) * g


def causal_dwconv(v, buf, w):
    L = v.shape[1]
    ext = jnp.concatenate([buf.astype(v.dtype), v], axis=1)
    y = ext[:, 0:L] * w[0]
    for k in range(1, CONV_W):
        y = y + ext[:, k:k + L] * w[k]
    return y, ext[:, L:]


def multiscale_pool(u, buf, start_pos, w_grp, scale):
    Bsz, L, C = u.shape
    ext = jnp.concatenate([buf.astype(u.dtype), u], axis=1)
    cs = jnp.cumsum(ext.astype(jnp.float32), axis=1)
    cs = jnp.concatenate([jnp.zeros((Bsz, 1, C), jnp.float32), cs], axis=1)
    csg = cs.reshape(Bsz, POOL_MAX + L, N_POOL_GROUPS, POOL_GROUP)
    end = csg[:, POOL_MAX:]
    pos = start_pos + jnp.arange(L)
    outs = []
    for gi, w in enumerate(POOL_WINDOWS):
        s = csg[:, POOL_MAX - w:POOL_MAX - w + L, gi]
        cnt = jnp.minimum(pos + 1, w).astype(jnp.float32)
        outs.append((end[:, :, gi] - s) / cnt[None, :, None])
    pooled = jnp.stack(outs, axis=2)
    ug = u.reshape(Bsz, L, N_POOL_GROUPS, POOL_GROUP).astype(jnp.float32)
    d = (pooled - ug).astype(u.dtype)
    y = jnp.einsum('blgc,gcd->blgd', d, w_grp).reshape(Bsz, L, C) * scale
    return y, ext[:, L:]


def hybrid_layer(x, p, conv_buf, pool_buf, ffn_buf, start_pos,
                 g_pre_mix, w_in, conv_a_w, w_a_out, pool_w, pool_scale, w_o, g_post_mix,
                 g_pre_ffn, w_up, ffn_conv_w, w_down, g_post_ffn, w_ple_proj, w_ple_gate):
    xn = rmsnorm(x, g_pre_mix)
    proj = xn @ w_in
    splits = np.cumsum([D_A, D_A, D_A, D_P, D_MODEL])
    b_a, c_a, h_a, u_p, gate_a, gate_p = jnp.split(proj, splits, axis=-1)
    conv_out, new_conv = causal_dwconv(c_a * h_a, conv_buf, conv_a_w)
    y_a = (b_a * conv_out) @ w_a_out
    y_p, new_pool = multiscale_pool(u_p, pool_buf, start_pos, pool_w, pool_scale)
    merged = jax.nn.sigmoid(gate_a) * y_a + jax.nn.sigmoid(gate_p) * y_p
    x = x + rmsnorm(merged @ w_o, g_post_mix)
    hn = rmsnorm(x, g_pre_ffn)
    up = hn @ w_up
    a, g = jnp.split(up, [D_FF], axis=-1)
    a_c, new_ffn = causal_dwconv(a, ffn_buf, ffn_conv_w)
    f = (jax.nn.gelu(a_c, approximate=True) * g) @ w_down
    x = x + rmsnorm(f, g_post_ffn)
    x = x + jax.nn.sigmoid(x @ w_ple_gate) * (p @ w_ple_proj)
    return x, new_conv, new_pool, new_ffn


def setup_inputs(seed: int = 0) -> dict:
    key = jax.random.key(seed)
    ks = jax.random.split(key, 24)
    f32 = jnp.float32

    def nrm(k, shape, scale):
        return jax.random.normal(k, shape, f32) * scale

    def gain(k, shape):
        return 1.0 + 0.05 * jax.random.normal(k, shape, f32)

    return {
        "x_prompt": nrm(ks[0], (BATCH, SEQ, D_MODEL), 1.0),
        "x_sample": nrm(ks[1], (DEC_BATCH, DEC_SEQ, D_MODEL), 1.0),
        "p_prompt": nrm(ks[2], (DEPTH, BATCH, SEQ, PLE_DIM), 1.0),
        "p_sample": nrm(ks[3], (DEPTH, DEC_BATCH, DEC_SEQ, PLE_DIM), 1.0),
        "state_conv_a": nrm(ks[4], (DEPTH, DEC_BATCH, CONV_W - 1, D_A), 1.0),
        "state_pool": nrm(ks[5], (DEPTH, DEC_BATCH, POOL_MAX - 1, D_P), 1.0),
        "state_ffn_conv": nrm(ks[6], (DEPTH, DEC_BATCH, CONV_W - 1, D_FF), 1.0),
        "g_pre_mix": gain(ks[7], (DEPTH, D_MODEL)),
        "w_in": nrm(ks[8], (DEPTH, D_MODEL, D_IN_ALL), D_MODEL ** -0.5),
        "conv_a_w": nrm(ks[9], (DEPTH, CONV_W, D_A), CONV_W ** -0.5),
        "w_a_out": nrm(ks[10], (DEPTH, D_A, D_MODEL), D_A ** -0.5),
        "pool_w": nrm(ks[11], (DEPTH, N_POOL_GROUPS, POOL_GROUP, POOL_GROUP), POOL_GROUP ** -0.5),
        "pool_scale": gain(ks[12], (DEPTH, D_P)),
        "w_o": nrm(ks[13], (DEPTH, D_MODEL, D_MODEL), D_MODEL ** -0.5),
        "g_post_mix": gain(ks[14], (DEPTH, D_MODEL)),
        "g_pre_ffn": gain(ks[15], (DEPTH, D_MODEL)),
        "w_up": nrm(ks[16], (DEPTH, D_MODEL, 2 * D_FF), D_MODEL ** -0.5),
        "ffn_conv_w": nrm(ks[17], (DEPTH, CONV_W, D_FF), CONV_W ** -0.5),
        "w_down": nrm(ks[18], (DEPTH, D_FF, D_MODEL), D_FF ** -0.5),
        "g_post_ffn": gain(ks[19], (DEPTH, D_MODEL)),
        "w_ple_proj": nrm(ks[20], (DEPTH, PLE_DIM, D_MODEL), PLE_DIM ** -0.5),
        "w_ple_gate": nrm(ks[21], (DEPTH, D_MODEL, D_MODEL), D_MODEL ** -0.5),
    }


def reference(x_prompt, x_sample, p_prompt, p_sample, state_conv_a, state_pool, state_ffn_conv,
              g_pre_mix, w_in, conv_a_w, w_a_out, pool_w, pool_scale, w_o, g_post_mix,
              g_pre_ffn, w_up, ffn_conv_w, w_down, g_post_ffn, w_ple_proj, w_ple_gate):
    xp, xs = x_prompt, x_sample
    conv_p, pool_p, ffn_p = [], [], []
    conv_s, pool_s, ffn_s = [], [], []
    for i in range(DEPTH):
        w = (g_pre_mix[i], w_in[i], conv_a_w[i], w_a_out[i], pool_w[i], pool_scale[i], w_o[i],
             g_post_mix[i], g_pre_ffn[i], w_up[i], ffn_conv_w[i], w_down[i], g_post_ffn[i],
             w_ple_proj[i], w_ple_gate[i])
        zc = jnp.zeros((BATCH, CONV_W - 1, D_A), xp.dtype)
        zp = jnp.zeros((BATCH, POOL_MAX - 1, D_P), xp.dtype)
        zf = jnp.zeros((BATCH, CONV_W - 1, D_FF), xp.dtype)
        xp, c1, p1, f1 = hybrid_layer(xp, p_prompt[i], zc, zp, zf, 0, *w)
        xs, c2, p2, f2 = hybrid_layer(xs, p_sample[i], state_conv_a[i], state_pool[i],
                                      state_ffn_conv[i], PAST_LEN, *w)
        conv_p.append(c1); pool_p.append(p1); ffn_p.append(f1)
        conv_s.append(c2); pool_s.append(p2); ffn_s.append(f2)
    new_conv_a_prompt = jnp.stack(conv_p, axis=0)
    new_pool_prompt = jnp.stack(pool_p, axis=0)
    new_ffn_prompt = jnp.stack(ffn_p, axis=0)
    new_conv_a_sample = jnp.stack(conv_s, axis=0)
    new_pool_sample = jnp.stack(pool_s, axis=0)
    new_ffn_sample = jnp.stack(ffn_s, axis=0)
    return (xp, xs, new_conv_a_prompt, new_pool_prompt, new_ffn_prompt,
            new_conv_a_sample, new_pool_sample, new_ffn_sample)
```

```python
import functools

import jax
import jax.numpy as jnp
from jax import lax
from jax.experimental import pallas as pl
from jax.experimental.pallas import tpu as pltpu

D_MODEL = 1024
D_FF = 2816
PLE_DIM = 256
CONV_W = 3
POOL_WINDOWS = (2, 4, 8, 16)
POOL_MAX = 16
EPS = 1e-6

CHUNK = 256
N_MIX_CHUNKS = D_MODEL // CHUNK
N_FF_CHUNKS = D_FF // CHUNK
TILE_ROWS = 512
CONV_HALO = 16
POOL_HALO = 32
POOL_STAGE_START = 16
VMEM_LIMIT_BYTES = 60 * 1024 * 1024

F32 = jnp.float32
BF16 = jnp.bfloat16


def _rmsnorm(x, g):
    ms = jnp.mean(x * x, axis=-1, keepdims=True)
    return (x * lax.rsqrt(ms + EPS)) * g


def _dot(a, b):
    return jnp.dot(a, b, preferred_element_type=F32)


def _causal_conv(buf, cols, v, w_ref, halo, step, rows):
    buf[pl.ds(halo, rows), cols] = v
    y = buf[pl.ds(halo - 2 * step, rows), cols] * w_ref[0:1, cols]
    y = y + buf[pl.ds(halo - step, rows), cols] * w_ref[1:2, cols]
    return y + v * w_ref[2:3, cols]


def _mixer_core(x, w, conv_in, pool_diff, rows):
    xb = _rmsnorm(x, w["g_pre"][...]).astype(BF16)

    def proj(k, j):
        lo = k * D_MODEL + j * CHUNK
        return _dot(xb, w["w_in"][:, lo:lo + CHUNK])

    y_a = None
    y_p = []
    for j in range(N_MIX_CHUNKS):
        cols = slice(j * CHUNK, (j + 1) * CHUNK)
        conv = conv_in(j, proj(1, j) * proj(2, j))
        z = (proj(0, j) * conv).astype(BF16)
        t = _dot(z, w["w_a_out"][cols, :])
        y_a = t if y_a is None else y_a + t
        d = pool_diff(j, proj(3, j)).astype(BF16)
        y_p.append(_dot(d, w["pool_w"][j]) * w["pool_scale"][:, cols])
    mix = None
    for j in range(N_MIX_CHUNKS):
        cols = slice(j * CHUNK, (j + 1) * CHUNK)
        merged = jax.nn.sigmoid(proj(4, j)) * y_a[:, cols] + jax.nn.sigmoid(proj(5, j)) * y_p[j]
        t = _dot(merged.astype(BF16), w["w_o"][cols, :])
        mix = t if mix is None else mix + t
    return x + _rmsnorm(mix, w["g_post"][...])


def _ffn_core(x, p, w, conv_in):
    hb = _rmsnorm(x, w["g_pre"][...]).astype(BF16)
    f = None
    for j in range(N_FF_CHUNKS):
        cols = slice(j * CHUNK, (j + 1) * CHUNK)
        a = _dot(hb, w["w_up"][:, j * CHUNK:(j + 1) * CHUNK])
        g = _dot(hb, w["w_up"][:, D_FF + j * CHUNK:D_FF + (j + 1) * CHUNK])
        h = jax.nn.gelu(conv_in(j, a), approximate=True) * g
        t = _dot(h.astype(BF16), w["w_down"][cols, :])
        f = t if f is None else f + t
    x = x + _rmsnorm(f, w["g_post"][...])
    gate = jax.nn.sigmoid(_dot(x.astype(BF16), w["w_gate"][...]))
    return x + gate * _dot(p.astype(BF16), w["w_proj"][...])


def _mixer_prompt_kernel(x_ref, g_pre, w_in, conv_w, w_a_out, pool_w, pool_scale, w_o, g_post,
                         x1_ref, nconv_ref, npool_ref, cbuf, ubuf, e0, e1, e2, dbuf,
                         *, rows, tiles_per_seq):
    i = pl.program_id(0)
    seq_start = (i % tiles_per_seq) == 0
    ebufs = (e0, e1, e2)

    @pl.when(i == 0)
    def _():
        for e in ebufs:
            e[0:POOL_STAGE_START, :] = jnp.zeros((POOL_STAGE_START, CHUNK), F32)

    @pl.when(seq_start)
    def _():
        cbuf[0:CONV_HALO, :] = jnp.zeros((CONV_HALO, D_MODEL), F32)
        ubuf[0:POOL_HALO, :] = jnp.zeros((POOL_HALO, D_MODEL), F32)

    def conv_in(j, v):
        cols = slice(j * CHUNK, (j + 1) * CHUNK)
        return _causal_conv(cbuf, cols, v, conv_w, CONV_HALO, 1, rows)

    def pool_diff(j, u):
        cols = slice(j * CHUNK, (j + 1) * CHUNK)
        window = POOL_WINDOWS[j]
        ubuf[pl.ds(POOL_HALO, rows), cols] = u
        n_stages = j + 1
        tsum = None
        for k in range(n_stages):
            last = k == n_stages - 1
            lo = POOL_HALO if last else POOL_STAGE_START
            n = POOL_HALO + rows - lo
            if k == 0:
                tsum = ubuf[pl.ds(lo, n), cols] + ubuf[pl.ds(lo - 1, n), cols]
            else:
                src = ebufs[k - 1]
                tsum = src[pl.ds(lo, n), :] + src[pl.ds(lo - (1 << k), n), :]
            if not last:
                ebufs[k][pl.ds(lo, n), :] = tsum
        dbuf[...] = tsum * (1.0 / window) - u

        @pl.when(seq_start)
        def _():
            pos = lax.broadcasted_iota(jnp.int32, (POOL_MAX, CHUNK), 0)
            cnt = jnp.minimum(pos + 1, window).astype(F32)
            dbuf[0:POOL_MAX, :] = tsum[0:POOL_MAX, :] / cnt - u[0:POOL_MAX, :]

        return dbuf[...]

    w = dict(g_pre=g_pre, w_in=w_in, w_a_out=w_a_out, pool_w=pool_w, pool_scale=pool_scale, w_o=w_o,
             g_post=g_post)
    x1_ref[...] = _mixer_core(x_ref[...], w, conv_in, pool_diff, rows)

    nconv_ref[0, 0] = cbuf[pl.ds(CONV_HALO + rows - (CONV_W - 1), CONV_W - 1), :]
    npool_ref[0, 0] = ubuf[pl.ds(POOL_HALO + rows - (POOL_MAX - 1), POOL_MAX - 1), :]
    cbuf[0:CONV_HALO, :] = cbuf[pl.ds(rows, CONV_HALO), :]
    ubuf[POOL_HALO - 16:POOL_HALO, :] = ubuf[pl.ds(POOL_HALO + rows - 16, 16), :]


def _ffn_prompt_kernel(x_ref, p_ref, g_pre, w_up, conv_w, w_down, g_post, w_proj, w_gate,
                       y_ref, nffn_ref, fbuf, *, rows, tiles_per_seq):
    i = pl.program_id(0)

    @pl.when((i % tiles_per_seq) == 0)
    def _():
        fbuf[0:CONV_HALO, :] = jnp.zeros((CONV_HALO, D_FF), F32)

    def conv_in(j, v):
        cols = slice(j * CHUNK, (j + 1) * CHUNK)
        return _causal_conv(fbuf, cols, v, conv_w, CONV_HALO, 1, rows)

    w = dict(g_pre=g_pre, w_up=w_up, w_down=w_down, g_post=g_post, w_proj=w_proj, w_gate=w_gate)
    y_ref[...] = _ffn_core(x_ref[...], p_ref[...], w, conv_in)
    nffn_ref[0, 0] = fbuf[pl.ds(CONV_HALO + rows - (CONV_W - 1), CONV_W - 1), :]
    fbuf[0:CONV_HALO, :] = fbuf[pl.ds(rows, CONV_HALO), :]


def _resident(shape):
    return pl.BlockSpec(shape, lambda i: (0,) * len(shape), pipeline_mode=pl.Buffered(1))


def _prompt_layer(x, p, wts):
    batch, seq, _ = x.shape
    rows = TILE_ROWS
    tiles_per_seq = seq // rows
    n_tiles = batch * tiles_per_seq
    x2d = x.reshape(batch * seq, D_MODEL)
    p2d = p.reshape(batch * seq, PLE_DIM)
    params = pltpu.CompilerParams(dimension_semantics=("arbitrary",), vmem_limit_bytes=VMEM_LIMIT_BYTES)
    row_block = lambda width: pl.BlockSpec((rows, width), lambda i: (i, 0))
    state_block = lambda n, width: pl.BlockSpec((1, 1, n, width), lambda i: (0, i // tiles_per_seq, 0, 0))

    mixer_w = (wts["g_pre_mix"], wts["w_in"], wts["conv_a_w"], wts["w_a_out"], wts["pool_w"],
               wts["pool_scale"], wts["w_o"], wts["g_post_mix"])
    x1, nconv, npool = pl.pallas_call(
        functools.partial(_mixer_prompt_kernel, rows=rows, tiles_per_seq=tiles_per_seq),
        grid=(n_tiles,),
        in_specs=[row_block(D_MODEL)] + [_resident(a.shape) for a in mixer_w],
        out_specs=[row_block(D_MODEL), state_block(CONV_W - 1, D_MODEL), state_block(POOL_MAX - 1, D_MODEL)],
        out_shape=[jax.ShapeDtypeStruct((batch * seq, D_MODEL), F32),
                   jax.ShapeDtypeStruct((1, batch, CONV_W - 1, D_MODEL), F32),
                   jax.ShapeDtypeStruct((1, batch, POOL_MAX - 1, D_MODEL), F32)],
        scratch_shapes=[pltpu.VMEM((CONV_HALO + rows, D_MODEL), F32),
                        pltpu.VMEM((POOL_HALO + rows, D_MODEL), F32),
                        pltpu.VMEM((POOL_HALO + rows, CHUNK), F32),
                        pltpu.VMEM((POOL_HALO + rows, CHUNK), F32),
                        pltpu.VMEM((POOL_HALO + rows, CHUNK), F32),
                        pltpu.VMEM((rows, CHUNK), F32)],
        compiler_params=params,
        name="mixer_prompt",
    )(x2d, *mixer_w)

    ffn_w = (wts["g_pre_ffn"], wts["w_up"], wts["ffn_conv_w"], wts["w_down"], wts["g_post_ffn"],
             wts["w_ple_proj"], wts["w_ple_gate"])
    y, nffn = pl.pallas_call(
        functools.partial(_ffn_prompt_kernel, rows=rows, tiles_per_seq=tiles_per_seq),
        grid=(n_tiles,),
        in_specs=[row_block(D_MODEL), row_block(PLE_DIM)] + [_resident(a.shape) for a in ffn_w],
        out_specs=[row_block(D_MODEL), state_block(CONV_W - 1, D_FF)],
        out_shape=[jax.ShapeDtypeStruct((batch * seq, D_MODEL), F32),
                   jax.ShapeDtypeStruct((1, batch, CONV_W - 1, D_FF), F32)],
        scratch_shapes=[pltpu.VMEM((CONV_HALO + rows, D_FF), F32)],
        compiler_params=params,
        name="ffn_prompt",
    )(x1, p2d, *ffn_w)
    return y.reshape(batch, seq, D_MODEL), nconv, npool, nffn


def _mixer_sample_kernel(xs_ref, sconv_ref, spool_ref, g_pre, w_in, conv_w, w_a_out, pool_w, pool_scale,
                         w_o, g_post, x1_ref, nconv_ref, npool_ref, cbuf, *, batch, steps):
    rows = batch * steps
    n_conv = CONV_W - 1
    n_pool = POOL_MAX - 1
    for k in range(n_conv):
        cbuf[k * batch:(k + 1) * batch, :] = sconv_ref[:, k * D_MODEL:(k + 1) * D_MODEL]

    def conv_in(j, v):
        cols = slice(j * CHUNK, (j + 1) * CHUNK)
        return _causal_conv(cbuf, cols, v, conv_w, n_conv * batch, batch, rows)

    def pool_diff(j, u):
        window = POOL_WINDOWS[j]

        def ext(e):
            if e < n_pool:
                lo = e * D_MODEL + j * CHUNK
                return spool_ref[:, lo:lo + CHUNK]
            return u[(e - n_pool) * batch:(e - n_pool + 1) * batch, :]

        diffs = []
        for t in range(steps):
            tsum = ext(n_pool + t - window + 1)
            for e in range(n_pool + t - window + 2, n_pool + t + 1):
                tsum = tsum + ext(e)
            diffs.append(tsum * (1.0 / window) - ext(n_pool + t))
        for m in range(n_pool):
            lo = m * D_MODEL + j * CHUNK
            npool_ref[:, lo:lo + CHUNK] = ext(steps + m)
        return jnp.concatenate(diffs, axis=0)

    x = jnp.concatenate([xs_ref[:, t * D_MODEL:(t + 1) * D_MODEL] for t in range(steps)], axis=0)
    w = dict(g_pre=g_pre, w_in=w_in, w_a_out=w_a_out, pool_w=pool_w, pool_scale=pool_scale, w_o=w_o,
             g_post=g_post)
    x1_ref[...] = _mixer_core(x, w, conv_in, pool_diff, rows)
    for k in range(n_conv):
        nconv_ref[:, k * D_MODEL:(k + 1) * D_MODEL] = cbuf[(steps + k) * batch:(steps + k + 1) * batch, :]


def _ffn_sample_kernel(x_ref, ps_ref, sffn_ref, g_pre, w_up, conv_w, w_down, g_post, w_proj, w_gate,
                       ys_ref, nffn_ref, fbuf, *, batch, steps):
    rows = batch * steps
    n_conv = CONV_W - 1
    for k in range(n_conv):
        fbuf[k * batch:(k + 1) * batch, :] = sffn_ref[:, k * D_FF:(k + 1) * D_FF]

    def conv_in(j, v):
        cols = slice(j * CHUNK, (j + 1) * CHUNK)
        return _causal_conv(fbuf, cols, v, conv_w, n_conv * batch, batch, rows)

    p = jnp.concatenate([ps_ref[:, t * PLE_DIM:(t + 1) * PLE_DIM] for t in range(steps)], axis=0)
    w = dict(g_pre=g_pre, w_up=w_up, w_down=w_down, g_post=g_post, w_proj=w_proj, w_gate=w_gate)
    y = _ffn_core(x_ref[...], p, w, conv_in)
    for t in range(steps):
        ys_ref[:, t * D_MODEL:(t + 1) * D_MODEL] = y[t * batch:(t + 1) * batch, :]
    for k in range(n_conv):
        nffn_ref[:, k * D_FF:(k + 1) * D_FF] = fbuf[(steps + k) * batch:(steps + k + 1) * batch, :]


def _sample_layer(x, p, sconv, spool, sffn, wts):
    batch, steps, _ = x.shape
    rows = batch * steps
    params = pltpu.CompilerParams(vmem_limit_bytes=VMEM_LIMIT_BYTES)
    xs = x.reshape(batch, steps * D_MODEL)
    ps = p.reshape(batch, steps * PLE_DIM)
    sconv2d = sconv.reshape(batch, (CONV_W - 1) * D_MODEL)
    spool2d = spool.reshape(batch, (POOL_MAX - 1) * D_MODEL)
    sffn2d = sffn.reshape(batch, (CONV_W - 1) * D_FF)

    x1, nconv, npool = pl.pallas_call(
        functools.partial(_mixer_sample_kernel, batch=batch, steps=steps),
        out_shape=[jax.ShapeDtypeStruct((rows, D_MODEL), F32),
                   jax.ShapeDtypeStruct(sconv2d.shape, F32),
                   jax.ShapeDtypeStruct(spool2d.shape, F32)],
        scratch_shapes=[pltpu.VMEM(((CONV_W - 1) * batch + rows, D_MODEL), F32)],
        compiler_params=params,
        name="mixer_sample",
    )(xs, sconv2d, spool2d, wts["g_pre_mix"], wts["w_in"], wts["conv_a_w"], wts["w_a_out"], wts["pool_w"],
      wts["pool_scale"], wts["w_o"], wts["g_post_mix"])

    ys, nffn = pl.pallas_call(
        functools.partial(_ffn_sample_kernel, batch=batch, steps=steps),
        out_shape=[jax.ShapeDtypeStruct(xs.shape, F32), jax.ShapeDtypeStruct(sffn2d.shape, F32)],
        scratch_shapes=[pltpu.VMEM(((CONV_W - 1) * batch + rows, D_FF), F32)],
        compiler_params=params,
        name="ffn_sample",
    )(x1, ps, sffn2d, wts["g_pre_ffn"], wts["w_up"], wts["ffn_conv_w"], wts["w_down"], wts["g_post_ffn"],
      wts["w_ple_proj"], wts["w_ple_gate"])
    return (ys.reshape(batch, steps, D_MODEL), nconv.reshape(1, batch, CONV_W - 1, D_MODEL),
            npool.reshape(1, batch, POOL_MAX - 1, D_MODEL), nffn.reshape(1, batch, CONV_W - 1, D_FF))


def kernel(x_prompt, x_sample, p_prompt, p_sample, state_conv_a, state_pool, state_ffn_conv, g_pre_mix, w_in,
           conv_a_w, w_a_out, pool_w, pool_scale, w_o, g_post_mix, g_pre_ffn, w_up, ffn_conv_w, w_down,
           g_post_ffn, w_ple_proj, w_ple_gate):
    depth = w_in.shape[0]
    xp, xs = x_prompt, x_sample
    prompt_states, sample_states = [], []
    for i in range(depth):
        wts = dict(
            g_pre_mix=g_pre_mix[i][None], g_post_mix=g_post_mix[i][None], g_pre_ffn=g_pre_ffn[i][None],
            g_post_ffn=g_post_ffn[i][None], pool_scale=pool_scale[i][None], conv_a_w=conv_a_w[i],
            ffn_conv_w=ffn_conv_w[i],
            w_in=w_in[i].astype(BF16), w_a_out=w_a_out[i].astype(BF16), pool_w=pool_w[i].astype(BF16),
            w_o=w_o[i].astype(BF16), w_up=w_up[i].astype(BF16), w_down=w_down[i].astype(BF16),
            w_ple_proj=w_ple_proj[i].astype(BF16), w_ple_gate=w_ple_gate[i].astype(BF16))
        xp, c1, p1, f1 = _prompt_layer(xp, p_prompt[i], wts)
        xs, c2, p2, f2 = _sample_layer(xs, p_sample[i], state_conv_a[i], state_pool[i], state_ffn_conv[i], wts)
        prompt_states.append((c1, p1, f1))
        sample_states.append((c2, p2, f2))
    cat = lambda states, k: jnp.concatenate([s[k] for s in states], axis=0)
    return (xp, xs, cat(prompt_states, 0), cat(prompt_states, 1), cat(prompt_states, 2),
            cat(sample_states, 0), cat(sample_states, 1), cat(sample_states, 2))
```

```python
import functools

import jax
import jax.numpy as jnp
from jax import lax
from jax.experimental import pallas as pl
from jax.experimental.pallas import tpu as pltpu

D_MODEL = 1024
D_FF = 2816
PLE_DIM = 256
CONV_W = 3
POOL_WINDOWS = (2, 4, 8, 16)
POOL_MAX = 16
EPS = 1e-6

N_CONV = CONV_W - 1
N_POOL = POOL_MAX - 1
CHUNK = 256
N_MIX_CHUNKS = D_MODEL // CHUNK
N_FF_CHUNKS = D_FF // CHUNK
TIME_STEPS = 32
VMEM_LIMIT_BYTES = 60 * 1024 * 1024

F32 = jnp.float32
BF16 = jnp.bfloat16


def _rmsnorm(x, g):
    ms = jnp.mean(x * x, axis=-1, keepdims=True)
    return (x * lax.rsqrt(ms + EPS)) * g


def _dot(a, b):
    return jnp.dot(a, b, preferred_element_type=F32)


def _causal_conv(buf, cols, v, w_ref, halo, step, rows):
    buf[pl.ds(halo, rows), cols] = v
    y = buf[pl.ds(halo - 2 * step, rows), cols] * w_ref[0:1, cols]
    y = y + buf[pl.ds(halo - step, rows), cols] * w_ref[1:2, cols]
    return y + v * w_ref[2:3, cols]


def _mixer_core(x, w, conv_in, pool_diff):
    xb = _rmsnorm(x, w["g_pre"][...]).astype(BF16)

    def proj(k, j):
        lo = k * D_MODEL + j * CHUNK
        return _dot(xb, w["w_in"][:, lo:lo + CHUNK])

    def branch_proj(j):
        return tuple(proj(k, j) for k in range(4))

    def gate_proj(j):
        return proj(4, j), proj(5, j)

    nxt = branch_proj(0)
    y_a = None
    y_p = []
    for j in range(N_MIX_CHUNKS):
        cols = slice(j * CHUNK, (j + 1) * CHUNK)
        b, c, h, u = nxt
        nxt = branch_proj(j + 1) if j + 1 < N_MIX_CHUNKS else gate_proj(0)
        z = (b * conv_in(j, c * h)).astype(BF16)
        t = _dot(z, w["w_a_out"][cols, :])
        y_a = t if y_a is None else y_a + t
        d = pool_diff(j, u).astype(BF16)
        y_p.append(_dot(d, w["pool_w"][j]) * w["pool_scale"][:, cols])
    mix = None
    for j in range(N_MIX_CHUNKS):
        cols = slice(j * CHUNK, (j + 1) * CHUNK)
        ga, gp = nxt
        if j + 1 < N_MIX_CHUNKS:
            nxt = gate_proj(j + 1)
        merged = jax.nn.sigmoid(ga) * y_a[:, cols] + jax.nn.sigmoid(gp) * y_p[j]
        t = _dot(merged.astype(BF16), w["w_o"][cols, :])
        mix = t if mix is None else mix + t
    return x + _rmsnorm(mix, w["g_post"][...])


def _ffn_core(x, p, w, conv_in):
    hb = _rmsnorm(x, w["g_pre"][...]).astype(BF16)

    def up(j):
        return (_dot(hb, w["w_up"][:, j * CHUNK:(j + 1) * CHUNK]),
                _dot(hb, w["w_up"][:, D_FF + j * CHUNK:D_FF + (j + 1) * CHUNK]))

    nxt = up(0)
    f = None
    for j in range(N_FF_CHUNKS):
        cols = slice(j * CHUNK, (j + 1) * CHUNK)
        a, g = nxt
        if j + 1 < N_FF_CHUNKS:
            nxt = up(j + 1)
        h = jax.nn.gelu(conv_in(j, a), approximate=True) * g
        t = _dot(h.astype(BF16), w["w_down"][cols, :])
        f = t if f is None else f + t
    x = x + _rmsnorm(f, w["g_post"][...])
    gate = jax.nn.sigmoid(_dot(x.astype(BF16), w["w_gate"][...]))
    return x + gate * _dot(p.astype(BF16), w["w_proj"][...])


def _block_copy(hbm, buf, sem, block, slot, seq, steps, *, to_hbm=False):
    hbm_view = hbm.at[seq, pl.ds(block * steps, steps), :]
    buf_view = buf.at[slot, :, seq, :]
    src, dst = (buf_view, hbm_view) if to_hbm else (hbm_view, buf_view)
    return pltpu.make_async_copy(src, dst, sem.at[slot, seq])


def _fetch_block(hbm, buf, sem, steps, n_seq):
    i = pl.program_id(0)
    slot = i % 2

    def start(block, slot):
        for s in range(n_seq):
            _block_copy(hbm, buf, sem, block, slot, s, steps).start()

    @pl.when(i == 0)
    def _():
        start(0, 0)

    @pl.when(i + 1 < pl.num_programs(0))
    def _():
        start(i + 1, 1 - slot)

    for s in range(n_seq):
        _block_copy(hbm, buf, sem, i, slot, s, steps).wait()
    return buf[slot]


def _mixer_prompt_kernel(x_hbm, g_pre, w_in, conv_w, w_a_out, pool_w, pool_scale, w_o, g_post,
                         x1_ref, nconv_ref, npool_ref, xbuf, xsem, cbuf, ubuf, e0, e1, e2, dbuf,
                         *, steps, n_seq):
    i = pl.program_id(0)
    rows = steps * n_seq
    conv_halo = N_CONV * n_seq
    pool_halo = N_POOL * n_seq
    ebufs = (e0, e1, e2)

    @pl.when(i == 0)
    def _():
        cbuf[0:conv_halo, :] = jnp.zeros((conv_halo, D_MODEL), F32)
        ubuf[0:pool_halo, :] = jnp.zeros((pool_halo, D_MODEL), F32)

    def conv_in(j, v):
        cols = slice(j * CHUNK, (j + 1) * CHUNK)
        y = _causal_conv(cbuf, cols, v, conv_w, conv_halo, n_seq, rows)
        cbuf[0:conv_halo, cols] = cbuf[pl.ds(rows, conv_halo), cols]
        return y

    def pool_diff(j, u):
        cols = slice(j * CHUNK, (j + 1) * CHUNK)
        window = POOL_WINDOWS[j]
        ubuf[pl.ds(pool_halo, rows), cols] = u
        n_stages = j + 1
        tsum = None
        for k in range(n_stages):
            lo = pool_halo - (window - (2 << k)) * n_seq
            n = pool_halo + rows - lo
            shift = (1 << k) * n_seq
            if k == 0:
                tsum = ubuf[pl.ds(lo, n), cols] + ubuf[pl.ds(lo - shift, n), cols]
            else:
                tsum = ebufs[k - 1][pl.ds(lo, n), :] + ebufs[k - 1][pl.ds(lo - shift, n), :]
            if k < n_stages - 1:
                ebufs[k][pl.ds(lo, n), :] = tsum
        keep = (window - 1) * n_seq
        ubuf[pl.ds(pool_halo - keep, keep), cols] = ubuf[pl.ds(pool_halo + rows - keep, keep), cols]
        dbuf[...] = tsum * (1.0 / window) - u

        @pl.when(i == 0)
        def _():
            head = POOL_MAX * n_seq
            pos = lax.broadcasted_iota(jnp.int32, (POOL_MAX, n_seq, CHUNK), 0).reshape(head, CHUNK)
            cnt = jnp.minimum(pos + 1, window).astype(F32)
            dbuf[0:head, :] = tsum[0:head, :] / cnt - u[0:head, :]

        return dbuf[...]

    x = _fetch_block(x_hbm, xbuf, xsem, steps, n_seq).reshape(rows, D_MODEL)
    w = dict(g_pre=g_pre, w_in=w_in, w_a_out=w_a_out, pool_w=pool_w, pool_scale=pool_scale, w_o=w_o,
             g_post=g_post)
    x1_ref[...] = _mixer_core(x, w, conv_in, pool_diff)

    @pl.when(i == pl.num_programs(0) - 1)
    def _():
        nconv_ref[...] = cbuf[0:conv_halo, :].reshape(N_CONV, n_seq, D_MODEL)
        npool_ref[...] = ubuf[pl.ds(rows, pool_halo), :].reshape(N_POOL, n_seq, D_MODEL)


def _ffn_prompt_kernel(x_ref, p_hbm, g_pre, w_up, conv_w, w_down, g_post, w_proj, w_gate,
                       y_hbm, nffn_ref, pbuf, psem, ybuf, ysem, fbuf, *, steps, n_seq):
    i = pl.program_id(0)
    last = pl.num_programs(0) - 1
    slot = i % 2
    rows = steps * n_seq
    conv_halo = N_CONV * n_seq

    @pl.when(i == 0)
    def _():
        fbuf[0:conv_halo, :] = jnp.zeros((conv_halo, D_FF), F32)

    def conv_in(j, v):
        cols = slice(j * CHUNK, (j + 1) * CHUNK)
        y = _causal_conv(fbuf, cols, v, conv_w, conv_halo, n_seq, rows)
        fbuf[0:conv_halo, cols] = fbuf[pl.ds(rows, conv_halo), cols]
        return y

    def wait_store(block, slot):
        for s in range(n_seq):
            _block_copy(y_hbm, ybuf, ysem, block, slot, s, steps, to_hbm=True).wait()

    p = _fetch_block(p_hbm, pbuf, psem, steps, n_seq).reshape(rows, PLE_DIM)
    w = dict(g_pre=g_pre, w_up=w_up, w_down=w_down, g_post=g_post, w_proj=w_proj, w_gate=w_gate)
    y = _ffn_core(x_ref[...], p, w, conv_in)

    @pl.when(i >= 2)
    def _():
        wait_store(i - 2, slot)

    ybuf[slot] = y.reshape(steps, n_seq, D_MODEL)
    for s in range(n_seq):
        _block_copy(y_hbm, ybuf, ysem, i, slot, s, steps, to_hbm=True).start()

    @pl.when(i == last)
    def _():
        nffn_ref[...] = fbuf[0:conv_halo, :].reshape(N_CONV, n_seq, D_FF)
        wait_store(i - 1, 1 - slot)
        wait_store(i, slot)


def _resident(shape):
    return pl.BlockSpec(shape, lambda i: (0,) * len(shape), pipeline_mode=pl.Buffered(1))


def _prompt_layer(x, p, wts):
    n_seq, seq_len, _ = x.shape
    steps = TIME_STEPS
    n_blocks = seq_len // steps
    assert n_blocks * steps == seq_len and n_blocks >= 2 and steps >= POOL_MAX and n_seq % 8 == 0
    rows = steps * n_seq
    params = pltpu.CompilerParams(dimension_semantics=("arbitrary",), vmem_limit_bytes=VMEM_LIMIT_BYTES)
    any_space = pl.BlockSpec(memory_space=pl.ANY)
    row_block = pl.BlockSpec((rows, D_MODEL), lambda i: (i, 0))
    state_block = lambda n, width: pl.BlockSpec((n, n_seq, width), lambda i: (0, 0, 0))

    mixer_w = (wts["g_pre_mix"], wts["w_in"], wts["conv_a_w"], wts["w_a_out"], wts["pool_w"],
               wts["pool_scale"], wts["w_o"], wts["g_post_mix"])
    stage_rows = N_POOL * n_seq + rows
    x1, nconv, npool = pl.pallas_call(
        functools.partial(_mixer_prompt_kernel, steps=steps, n_seq=n_seq),
        grid=(n_blocks,),
        in_specs=[any_space] + [_resident(a.shape) for a in mixer_w],
        out_specs=[row_block, state_block(N_CONV, D_MODEL), state_block(N_POOL, D_MODEL)],
        out_shape=[jax.ShapeDtypeStruct((n_blocks * rows, D_MODEL), F32),
                   jax.ShapeDtypeStruct((N_CONV, n_seq, D_MODEL), F32),
                   jax.ShapeDtypeStruct((N_POOL, n_seq, D_MODEL), F32)],
        scratch_shapes=[pltpu.VMEM((2, steps, n_seq, D_MODEL), F32), pltpu.SemaphoreType.DMA((2, n_seq)),
                        pltpu.VMEM((N_CONV * n_seq + rows, D_MODEL), F32),
                        pltpu.VMEM((stage_rows, D_MODEL), F32),
                        pltpu.VMEM((stage_rows, CHUNK), F32),
                        pltpu.VMEM((stage_rows, CHUNK), F32),
                        pltpu.VMEM((stage_rows, CHUNK), F32),
                        pltpu.VMEM((rows, CHUNK), F32)],
        compiler_params=params,
        name="mixer_prompt",
    )(x, *mixer_w)

    ffn_w = (wts["g_pre_ffn"], wts["w_up"], wts["ffn_conv_w"], wts["w_down"], wts["g_post_ffn"],
             wts["w_ple_proj"], wts["w_ple_gate"])
    y, nffn = pl.pallas_call(
        functools.partial(_ffn_prompt_kernel, steps=steps, n_seq=n_seq),
        grid=(n_blocks,),
        in_specs=[row_block, any_space] + [_resident(a.shape) for a in ffn_w],
        out_specs=[any_space, state_block(N_CONV, D_FF)],
        out_shape=[jax.ShapeDtypeStruct((n_seq, seq_len, D_MODEL), F32),
                   jax.ShapeDtypeStruct((N_CONV, n_seq, D_FF), F32)],
        scratch_shapes=[pltpu.VMEM((2, steps, n_seq, PLE_DIM), F32), pltpu.SemaphoreType.DMA((2, n_seq)),
                        pltpu.VMEM((2, steps, n_seq, D_MODEL), F32), pltpu.SemaphoreType.DMA((2, n_seq)),
                        pltpu.VMEM((N_CONV * n_seq + rows, D_FF), F32)],
        compiler_params=params,
        name="ffn_prompt",
    )(x1, p, *ffn_w)
    to_batch_major = lambda s: jnp.transpose(s, (1, 0, 2))[None]
    return y, to_batch_major(nconv), to_batch_major(npool), to_batch_major(nffn)


def _mixer_sample_kernel(xs_ref, sconv_ref, spool_ref, g_pre, w_in, conv_w, w_a_out, pool_w, pool_scale,
                         w_o, g_post, x1_ref, nconv_ref, npool_ref, cbuf, *, batch, steps):
    rows = batch * steps
    for k in range(N_CONV):
        cbuf[k * batch:(k + 1) * batch, :] = sconv_ref[:, k * D_MODEL:(k + 1) * D_MODEL]

    def conv_in(j, v):
        cols = slice(j * CHUNK, (j + 1) * CHUNK)
        return _causal_conv(cbuf, cols, v, conv_w, N_CONV * batch, batch, rows)

    def pool_diff(j, u):
        window = POOL_WINDOWS[j]

        def ext(e):
            if e < N_POOL:
                lo = e * D_MODEL + j * CHUNK
                return spool_ref[:, lo:lo + CHUNK]
            return u[(e - N_POOL) * batch:(e - N_POOL + 1) * batch, :]

        diffs = []
        for t in range(steps):
            tsum = ext(N_POOL + t - window + 1)
            for e in range(N_POOL + t - window + 2, N_POOL + t + 1):
                tsum = tsum + ext(e)
            diffs.append(tsum * (1.0 / window) - ext(N_POOL + t))
        for m in range(N_POOL):
            lo = m * D_MODEL + j * CHUNK
            npool_ref[:, lo:lo + CHUNK] = ext(steps + m)
        return jnp.concatenate(diffs, axis=0)

    x = jnp.concatenate([xs_ref[:, t * D_MODEL:(t + 1) * D_MODEL] for t in range(steps)], axis=0)
    w = dict(g_pre=g_pre, w_in=w_in, w_a_out=w_a_out, pool_w=pool_w, pool_scale=pool_scale, w_o=w_o,
             g_post=g_post)
    x1_ref[...] = _mixer_core(x, w, conv_in, pool_diff)
    for k in range(N_CONV):
        nconv_ref[:, k * D_MODEL:(k + 1) * D_MODEL] = cbuf[(steps + k) * batch:(steps + k + 1) * batch, :]


def _ffn_sample_kernel(x_ref, ps_ref, sffn_ref, g_pre, w_up, conv_w, w_down, g_post, w_proj, w_gate,
                       ys_ref, nffn_ref, fbuf, *, batch, steps):
    rows = batch * steps
    for k in range(N_CONV):
        fbuf[k * batch:(k + 1) * batch, :] = sffn_ref[:, k * D_FF:(k + 1) * D_FF]

    def conv_in(j, v):
        cols = slice(j * CHUNK, (j + 1) * CHUNK)
        return _causal_conv(fbuf, cols, v, conv_w, N_CONV * batch, batch, rows)

    p = jnp.concatenate([ps_ref[:, t * PLE_DIM:(t + 1) * PLE_DIM] for t in range(steps)], axis=0)
    w = dict(g_pre=g_pre, w_up=w_up, w_down=w_down, g_post=g_post, w_proj=w_proj, w_gate=w_gate)
    y = _ffn_core(x_ref[...], p, w, conv_in)
    for t in range(steps):
        ys_ref[:, t * D_MODEL:(t + 1) * D_MODEL] = y[t * batch:(t + 1) * batch, :]
    for k in range(N_CONV):
        nffn_ref[:, k * D_FF:(k + 1) * D_FF] = fbuf[(steps + k) * batch:(steps + k + 1) * batch, :]


def _sample_layer(x, p, sconv, spool, sffn, wts):
    batch, steps, _ = x.shape
    rows = batch * steps
    params = pltpu.CompilerParams(vmem_limit_bytes=VMEM_LIMIT_BYTES)
    xs = x.reshape(batch, steps * D_MODEL)
    ps = p.reshape(batch, steps * PLE_DIM)
    sconv2d = sconv.reshape(batch, N_CONV * D_MODEL)
    spool2d = spool.reshape(batch, N_POOL * D_MODEL)
    sffn2d = sffn.reshape(batch, N_CONV * D_FF)

    x1, nconv, npool = pl.pallas_call(
        functools.partial(_mixer_sample_kernel, batch=batch, steps=steps),
        out_shape=[jax.ShapeDtypeStruct((rows, D_MODEL), F32),
                   jax.ShapeDtypeStruct(sconv2d.shape, F32),
                   jax.ShapeDtypeStruct(spool2d.shape, F32)],
        scratch_shapes=[pltpu.VMEM((N_CONV * batch + rows, D_MODEL), F32)],
        compiler_params=params,
        name="mixer_sample",
    )(xs, sconv2d, spool2d, wts["g_pre_mix"], wts["w_in"], wts["conv_a_w"], wts["w_a_out"], wts["pool_w"],
      wts["pool_scale"], wts["w_o"], wts["g_post_mix"])

    ys, nffn = pl.pallas_call(
        functools.partial(_ffn_sample_kernel, batch=batch, steps=steps),
        out_shape=[jax.ShapeDtypeStruct(xs.shape, F32), jax.ShapeDtypeStruct(sffn2d.shape, F32)],
        scratch_shapes=[pltpu.VMEM((N_CONV * batch + rows, D_FF), F32)],
        compiler_params=params,
        name="ffn_sample",
    )(x1, ps, sffn2d, wts["g_pre_ffn"], wts["w_up"], wts["ffn_conv_w"], wts["w_down"], wts["g_post_ffn"],
      wts["w_ple_proj"], wts["w_ple_gate"])
    return (ys.reshape(batch, steps, D_MODEL), nconv.reshape(1, batch, N_CONV, D_MODEL),
            npool.reshape(1, batch, N_POOL, D_MODEL), nffn.reshape(1, batch, N_CONV, D_FF))


def kernel(x_prompt, x_sample, p_prompt, p_sample, state_conv_a, state_pool, state_ffn_conv, g_pre_mix, w_in,
           conv_a_w, w_a_out, pool_w, pool_scale, w_o, g_post_mix, g_pre_ffn, w_up, ffn_conv_w, w_down,
           g_post_ffn, w_ple_proj, w_ple_gate):
    depth = w_in.shape[0]
    xp, xs = x_prompt, x_sample
    prompt_states, sample_states = [], []
    for i in range(depth):
        wts = dict(
            g_pre_mix=g_pre_mix[i][None], g_post_mix=g_post_mix[i][None], g_pre_ffn=g_pre_ffn[i][None],
            g_post_ffn=g_post_ffn[i][None], pool_scale=pool_scale[i][None], conv_a_w=conv_a_w[i],
            ffn_conv_w=ffn_conv_w[i],
            w_in=w_in[i].astype(BF16), w_a_out=w_a_out[i].astype(BF16), pool_w=pool_w[i].astype(BF16),
            w_o=w_o[i].astype(BF16), w_up=w_up[i].astype(BF16), w_down=w_down[i].astype(BF16),
            w_ple_proj=w_ple_proj[i].astype(BF16), w_ple_gate=w_ple_gate[i].astype(BF16))
        xp, c1, p1, f1 = _prompt_layer(xp, p_prompt[i], wts)
        xs, c2, p2, f2 = _sample_layer(xs, p_sample[i], state_conv_a[i], state_pool[i], state_ffn_conv[i], wts)
        prompt_states.append((c1, p1, f1))
        sample_states.append((c2, p2, f2))
    cat = lambda states, k: jnp.concatenate([s[k] for s in states], axis=0)
    return (xp, xs, cat(prompt_states, 0), cat(prompt_states, 1), cat(prompt_states, 2),
            cat(sample_states, 0), cat(sample_states, 1), cat(sample_states, 2))
```

```python
import functools

import jax
import jax.numpy as jnp
from jax import lax
from jax.experimental import pallas as pl
from jax.experimental.pallas import tpu as pltpu

D_MODEL = 1024
D_FF = 2816
PLE_DIM = 256
CONV_W = 3
POOL_WINDOWS = (2, 4, 8, 16)
POOL_MAX = 16
EPS = 1e-6

N_CONV = CONV_W - 1
N_POOL = POOL_MAX - 1
CHUNK = 256
N_MIX_CHUNKS = D_MODEL // CHUNK
N_FF_CHUNKS = D_FF // CHUNK
TIME_STEPS = 64
SUB_STEPS = 32
VMEM_LIMIT_BYTES = 60 * 1024 * 1024

F32 = jnp.float32
BF16 = jnp.bfloat16


def _rmsnorm(x, g):
    ms = jnp.mean(x * x, axis=-1, keepdims=True)
    return (x * lax.rsqrt(ms + EPS)) * g


def _dot(a, b):
    return jnp.dot(a, b, preferred_element_type=F32)


def _causal_conv(buf, cols, v, w_ref, halo, step, rows):
    buf[pl.ds(halo, rows), cols] = v
    y = buf[pl.ds(halo - 2 * step, rows), cols] * w_ref[0:1, cols]
    y = y + buf[pl.ds(halo - step, rows), cols] * w_ref[1:2, cols]
    return y + v * w_ref[2:3, cols]


def _mixer_core(x, w, conv_in, pool_diff):
    xb = _rmsnorm(x, w["g_pre"][...]).astype(BF16)

    def proj(k, j):
        lo = k * D_MODEL + j * CHUNK
        return _dot(xb, w["w_in"][:, lo:lo + CHUNK])

    def branch_proj(j):
        return tuple(proj(k, j) for k in range(4))

    def gate_proj(j):
        return proj(4, j), proj(5, j)

    nxt = branch_proj(0)
    y_a = None
    y_p = []
    for j in range(N_MIX_CHUNKS):
        cols = slice(j * CHUNK, (j + 1) * CHUNK)
        b, c, h, u = nxt
        nxt = branch_proj(j + 1) if j + 1 < N_MIX_CHUNKS else gate_proj(0)
        z = (b * conv_in(j, c * h)).astype(BF16)
        t = _dot(z, w["w_a_out"][cols, :])
        y_a = t if y_a is None else y_a + t
        d = pool_diff(j, u).astype(BF16)
        y_p.append(_dot(d, w["pool_w"][j]) * w["pool_scale"][:, cols])
    mix = None
    for j in range(N_MIX_CHUNKS):
        cols = slice(j * CHUNK, (j + 1) * CHUNK)
        ga, gp = nxt
        if j + 1 < N_MIX_CHUNKS:
            nxt = gate_proj(j + 1)
        merged = jax.nn.sigmoid(ga) * y_a[:, cols] + jax.nn.sigmoid(gp) * y_p[j]
        t = _dot(merged.astype(BF16), w["w_o"][cols, :])
        mix = t if mix is None else mix + t
    return x + _rmsnorm(mix, w["g_post"][...])


def _ffn_core(x, p, w, conv_in):
    hb = _rmsnorm(x, w["g_pre"][...]).astype(BF16)

    def up(j):
        return (_dot(hb, w["w_up"][:, j * CHUNK:(j + 1) * CHUNK]),
                _dot(hb, w["w_up"][:, D_FF + j * CHUNK:D_FF + (j + 1) * CHUNK]))

    nxt = up(0)
    f = None
    for j in range(N_FF_CHUNKS):
        cols = slice(j * CHUNK, (j + 1) * CHUNK)
        a, g = nxt
        if j + 1 < N_FF_CHUNKS:
            nxt = up(j + 1)
        h = jax.nn.gelu(conv_in(j, a), approximate=True) * g
        t = _dot(h.astype(BF16), w["w_down"][cols, :])
        f = t if f is None else f + t
    x = x + _rmsnorm(f, w["g_post"][...])
    gate = jax.nn.sigmoid(_dot(x.astype(BF16), w["w_gate"][...]))
    return x + gate * _dot(p.astype(BF16), w["w_proj"][...])


def _block_copy(hbm, buf, sem, block, slot, seq, steps, *, to_hbm=False):
    hbm_view = hbm.at[seq, pl.ds(block * steps, steps), :]
    buf_view = buf.at[slot, :, seq, :]
    src, dst = (buf_view, hbm_view) if to_hbm else (hbm_view, buf_view)
    return pltpu.make_async_copy(src, dst, sem.at[slot, seq])


def _fetch_block(hbm, buf, sem, steps, n_seq):
    i = pl.program_id(0)
    slot = i % 2

    def start(block, slot):
        for s in range(n_seq):
            _block_copy(hbm, buf, sem, block, slot, s, steps).start()

    @pl.when(i == 0)
    def _():
        start(0, 0)

    @pl.when(i + 1 < pl.num_programs(0))
    def _():
        start(i + 1, 1 - slot)

    for s in range(n_seq):
        _block_copy(hbm, buf, sem, i, slot, s, steps).wait()
    return slot


def _mixer_prompt_kernel(x_hbm, g_pre, w_in, conv_w, w_a_out, pool_w, pool_scale, w_o, g_post,
                         x1_ref, nconv_ref, npool_ref, xbuf, xsem, cbuf, ubuf, e0, e1, e2, inv_cnt,
                         *, steps, sub_steps, n_seq):
    i = pl.program_id(0)
    rows = steps * n_seq
    sub_rows = sub_steps * n_seq
    n_sub = steps // sub_steps
    conv_halo = N_CONV * n_seq
    pool_halo = N_POOL * n_seq
    head = POOL_MAX * n_seq
    ebufs = (e0, e1, e2)

    @pl.when(i == 0)
    def _():
        cbuf[0:conv_halo, :] = jnp.zeros((conv_halo, D_MODEL), F32)
        ubuf[0:pool_halo, :] = jnp.zeros((pool_halo, D_MODEL), F32)
        pos = lax.broadcasted_iota(jnp.int32, (POOL_MAX, n_seq, CHUNK), 0).reshape(head, CHUNK)
        for j, window in enumerate(POOL_WINDOWS):
            inv_cnt[j] = 1.0 / jnp.minimum(pos + 1, window).astype(F32)

    @pl.when(i == 1)
    def _():
        for j, window in enumerate(POOL_WINDOWS):
            inv_cnt[j] = jnp.full((head, CHUNK), 1.0 / window, F32)

    slot = _fetch_block(x_hbm, xbuf, xsem, steps, n_seq)
    w = dict(g_pre=g_pre, w_in=w_in, w_a_out=w_a_out, pool_w=pool_w, pool_scale=pool_scale, w_o=w_o,
             g_post=g_post)

    for sb in range(n_sub):
        off = sb * sub_rows
        last_sub = sb == n_sub - 1

        def conv_in(j, v, off=off, last_sub=last_sub):
            cols = slice(j * CHUNK, (j + 1) * CHUNK)
            y = _causal_conv(cbuf, cols, v, conv_w, conv_halo + off, n_seq, sub_rows)
            if last_sub:
                cbuf[0:conv_halo, cols] = cbuf[pl.ds(rows, conv_halo), cols]
            return y

        def pool_diff(j, u, sb=sb, off=off, last_sub=last_sub):
            cols = slice(j * CHUNK, (j + 1) * CHUNK)
            window = POOL_WINDOWS[j]
            base = pool_halo + off
            ubuf[pl.ds(base, sub_rows), cols] = u
            n_stages = j + 1
            tsum = None
            for k in range(n_stages):
                lo = base if sb > 0 else pool_halo - (window - (2 << k)) * n_seq
                n = base + sub_rows - lo
                shift = (1 << k) * n_seq
                if k == 0:
                    tsum = ubuf[pl.ds(lo, n), cols] + ubuf[pl.ds(lo - shift, n), cols]
                else:
                    tsum = ebufs[k - 1][pl.ds(lo, n), cols] + ebufs[k - 1][pl.ds(lo - shift, n), cols]
                if k < n_stages - 1:
                    ebufs[k][pl.ds(lo, n), cols] = tsum
            if last_sub:
                keep = (window - 1) * n_seq
                ubuf[pl.ds(pool_halo - keep, keep), cols] = ubuf[pl.ds(pool_halo + rows - keep, keep), cols]
            if sb > 0:
                return tsum * (1.0 / window) - u
            return jnp.concatenate([tsum[0:head, :] * inv_cnt[j] - u[0:head, :],
                                    tsum[head:, :] * (1.0 / window) - u[head:, :]], axis=0)

        x = xbuf[slot, sb * sub_steps:(sb + 1) * sub_steps].reshape(sub_rows, D_MODEL)
        x1_ref[off:off + sub_rows, :] = _mixer_core(x, w, conv_in, pool_diff)

    @pl.when(i == pl.num_programs(0) - 1)
    def _():
        nconv_ref[...] = cbuf[0:conv_halo, :].reshape(N_CONV, n_seq, D_MODEL)
        npool_ref[...] = ubuf[pl.ds(rows, pool_halo), :].reshape(N_POOL, n_seq, D_MODEL)


def _ffn_prompt_kernel(x_ref, p_hbm, g_pre, w_up, conv_w, w_down, g_post, w_proj, w_gate,
                       y_hbm, nffn_ref, pbuf, psem, ybuf, ysem, fbuf, *, steps, sub_steps, n_seq):
    i = pl.program_id(0)
    last = pl.num_programs(0) - 1
    rows = steps * n_seq
    sub_rows = sub_steps * n_seq
    n_sub = steps // sub_steps
    conv_halo = N_CONV * n_seq

    @pl.when(i == 0)
    def _():
        fbuf[0:conv_halo, :] = jnp.zeros((conv_halo, D_FF), F32)

    def wait_store(block, slot):
        for s in range(n_seq):
            _block_copy(y_hbm, ybuf, ysem, block, slot, s, steps, to_hbm=True).wait()

    slot = _fetch_block(p_hbm, pbuf, psem, steps, n_seq)

    @pl.when(i >= 2)
    def _():
        wait_store(i - 2, slot)

    w = dict(g_pre=g_pre, w_up=w_up, w_down=w_down, g_post=g_post, w_proj=w_proj, w_gate=w_gate)
    for sb in range(n_sub):
        off = sb * sub_rows
        last_sub = sb == n_sub - 1

        def conv_in(j, v, off=off, last_sub=last_sub):
            cols = slice(j * CHUNK, (j + 1) * CHUNK)
            y = _causal_conv(fbuf, cols, v, conv_w, conv_halo + off, n_seq, sub_rows)
            if last_sub:
                fbuf[0:conv_halo, cols] = fbuf[pl.ds(rows, conv_halo), cols]
            return y

        sub = slice(sb * sub_steps, (sb + 1) * sub_steps)
        p = pbuf[slot, sub].reshape(sub_rows, PLE_DIM)
        y = _ffn_core(x_ref[off:off + sub_rows, :], p, w, conv_in)
        ybuf[slot, sub] = y.reshape(sub_steps, n_seq, D_MODEL)

    for s in range(n_seq):
        _block_copy(y_hbm, ybuf, ysem, i, slot, s, steps, to_hbm=True).start()

    @pl.when(i == last)
    def _():
        nffn_ref[...] = fbuf[0:conv_halo, :].reshape(N_CONV, n_seq, D_FF)
        wait_store(i - 1, 1 - slot)
        wait_store(i, slot)


def _resident(shape):
    return pl.BlockSpec(shape, lambda i: (0,) * len(shape), pipeline_mode=pl.Buffered(1))


def _prompt_layer(x, p, wts):
    n_seq, seq_len, _ = x.shape
    steps, sub_steps = TIME_STEPS, SUB_STEPS
    n_blocks = seq_len // steps
    assert n_blocks * steps == seq_len and n_blocks >= 2 and n_seq % 8 == 0
    assert steps % sub_steps == 0 and sub_steps > POOL_MAX
    rows = steps * n_seq
    params = pltpu.CompilerParams(dimension_semantics=("arbitrary",), vmem_limit_bytes=VMEM_LIMIT_BYTES)
    any_space = pl.BlockSpec(memory_space=pl.ANY)
    row_block = pl.BlockSpec((rows, D_MODEL), lambda i: (i, 0))
    state_block = lambda n, width: pl.BlockSpec((n, n_seq, width), lambda i: (0, 0, 0))

    mixer_w = (wts["g_pre_mix"], wts["w_in"], wts["conv_a_w"], wts["w_a_out"], wts["pool_w"],
               wts["pool_scale"], wts["w_o"], wts["g_post_mix"])
    stage_rows = N_POOL * n_seq + rows
    x1, nconv, npool = pl.pallas_call(
        functools.partial(_mixer_prompt_kernel, steps=steps, sub_steps=sub_steps, n_seq=n_seq),
        grid=(n_blocks,),
        in_specs=[any_space] + [_resident(a.shape) for a in mixer_w],
        out_specs=[row_block, state_block(N_CONV, D_MODEL), state_block(N_POOL, D_MODEL)],
        out_shape=[jax.ShapeDtypeStruct((n_blocks * rows, D_MODEL), F32),
                   jax.ShapeDtypeStruct((N_CONV, n_seq, D_MODEL), F32),
                   jax.ShapeDtypeStruct((N_POOL, n_seq, D_MODEL), F32)],
        scratch_shapes=[pltpu.VMEM((2, steps, n_seq, D_MODEL), F32), pltpu.SemaphoreType.DMA((2, n_seq)),
                        pltpu.VMEM((N_CONV * n_seq + rows, D_MODEL), F32),
                        pltpu.VMEM((stage_rows, D_MODEL), F32),
                        pltpu.VMEM((stage_rows, D_MODEL), F32),
                        pltpu.VMEM((stage_rows, D_MODEL), F32),
                        pltpu.VMEM((stage_rows, D_MODEL), F32),
                        pltpu.VMEM((len(POOL_WINDOWS), POOL_MAX * n_seq, CHUNK), F32)],
        compiler_params=params,
        name="mixer_prompt",
    )(x, *mixer_w)

    ffn_w = (wts["g_pre_ffn"], wts["w_up"], wts["ffn_conv_w"], wts["w_down"], wts["g_post_ffn"],
             wts["w_ple_proj"], wts["w_ple_gate"])
    y, nffn = pl.pallas_call(
        functools.partial(_ffn_prompt_kernel, steps=steps, sub_steps=sub_steps, n_seq=n_seq),
        grid=(n_blocks,),
        in_specs=[row_block, any_space] + [_resident(a.shape) for a in ffn_w],
        out_specs=[any_space, state_block(N_CONV, D_FF)],
        out_shape=[jax.ShapeDtypeStruct((n_seq, seq_len, D_MODEL), F32),
                   jax.ShapeDtypeStruct((N_CONV, n_seq, D_FF), F32)],
        scratch_shapes=[pltpu.VMEM((2, steps, n_seq, PLE_DIM), F32), pltpu.SemaphoreType.DMA((2, n_seq)),
                        pltpu.VMEM((2, steps, n_seq, D_MODEL), F32), pltpu.SemaphoreType.DMA((2, n_seq)),
                        pltpu.VMEM((N_CONV * n_seq + rows, D_FF), F32)],
        compiler_params=params,
        name="ffn_prompt",
    )(x1, p, *ffn_w)
    to_batch_major = lambda s: jnp.transpose(s, (1, 0, 2))[None]
    return y, to_batch_major(nconv), to_batch_major(npool), to_batch_major(nffn)


def _mixer_sample_kernel(xs_ref, sconv_ref, spool_ref, g_pre, w_in, conv_w, w_a_out, pool_w, pool_scale,
                         w_o, g_post, x1_ref, nconv_ref, npool_ref, cbuf, *, batch, steps):
    rows = batch * steps
    for k in range(N_CONV):
        cbuf[k * batch:(k + 1) * batch, :] = sconv_ref[:, k * D_MODEL:(k + 1) * D_MODEL]

    def conv_in(j, v):
        cols = slice(j * CHUNK, (j + 1) * CHUNK)
        return _causal_conv(cbuf, cols, v, conv_w, N_CONV * batch, batch, rows)

    def pool_diff(j, u):
        window = POOL_WINDOWS[j]

        def ext(e):
            if e < N_POOL:
                lo = e * D_MODEL + j * CHUNK
                return spool_ref[:, lo:lo + CHUNK]
            return u[(e - N_POOL) * batch:(e - N_POOL + 1) * batch, :]

        diffs = []
        for t in range(steps):
            tsum = ext(N_POOL + t - window + 1)
            for e in range(N_POOL + t - window + 2, N_POOL + t + 1):
                tsum = tsum + ext(e)
            diffs.append(tsum * (1.0 / window) - ext(N_POOL + t))
        for m in range(N_POOL):
            lo = m * D_MODEL + j * CHUNK
            npool_ref[:, lo:lo + CHUNK] = ext(steps + m)
        return jnp.concatenate(diffs, axis=0)

    x = jnp.concatenate([xs_ref[:, t * D_MODEL:(t + 1) * D_MODEL] for t in range(steps)], axis=0)
    w = dict(g_pre=g_pre, w_in=w_in, w_a_out=w_a_out, pool_w=pool_w, pool_scale=pool_scale, w_o=w_o,
             g_post=g_post)
    x1_ref[...] = _mixer_core(x, w, conv_in, pool_diff)
    for k in range(N_CONV):
        nconv_ref[:, k * D_MODEL:(k + 1) * D_MODEL] = cbuf[(steps + k) * batch:(steps + k + 1) * batch, :]


def _ffn_sample_kernel(x_ref, ps_ref, sffn_ref, g_pre, w_up, conv_w, w_down, g_post, w_proj, w_gate,
                       ys_ref, nffn_ref, fbuf, *, batch, steps):
    rows = batch * steps
    for k in range(N_CONV):
        fbuf[k * batch:(k + 1) * batch, :] = sffn_ref[:, k * D_FF:(k + 1) * D_FF]

    def conv_in(j, v):
        cols = slice(j * CHUNK, (j + 1) * CHUNK)
        return _causal_conv(fbuf, cols, v, conv_w, N_CONV * batch, batch, rows)

    p = jnp.concatenate([ps_ref[:, t * PLE_DIM:(t + 1) * PLE_DIM] for t in range(steps)], axis=0)
    w = dict(g_pre=g_pre, w_up=w_up, w_down=w_down, g_post=g_post, w_proj=w_proj, w_gate=w_gate)
    y = _ffn_core(x_ref[...], p, w, conv_in)
    for t in range(steps):
        ys_ref[:, t * D_MODEL:(t + 1) * D_MODEL] = y[t * batch:(t + 1) * batch, :]
    for k in range(N_CONV):
        nffn_ref[:, k * D_FF:(k + 1) * D_FF] = fbuf[(steps + k) * batch:(steps + k + 1) * batch, :]


def _sample_layer(x, p, sconv, spool, sffn, wts):
    batch, steps, _ = x.shape
    rows = batch * steps
    params = pltpu.CompilerParams(vmem_limit_bytes=VMEM_LIMIT_BYTES)
    xs = x.reshape(batch, steps * D_MODEL)
    ps = p.reshape(batch, steps * PLE_DIM)
    sconv2d = sconv.reshape(batch, N_CONV * D_MODEL)
    spool2d = spool.reshape(batch, N_POOL * D_MODEL)
    sffn2d = sffn.reshape(batch, N_CONV * D_FF)

    x1, nconv, npool = pl.pallas_call(
        functools.partial(_mixer_sample_kernel, batch=batch, steps=steps),
        out_shape=[jax.ShapeDtypeStruct((rows, D_MODEL), F32),
                   jax.ShapeDtypeStruct(sconv2d.shape, F32),
                   jax.ShapeDtypeStruct(spool2d.shape, F32)],
        scratch_shapes=[pltpu.VMEM((N_CONV * batch + rows, D_MODEL), F32)],
        compiler_params=params,
        name="mixer_sample",
    )(xs, sconv2d, spool2d, wts["g_pre_mix"], wts["w_in"], wts["conv_a_w"], wts["w_a_out"], wts["pool_w"],
      wts["pool_scale"], wts["w_o"], wts["g_post_mix"])

    ys, nffn = pl.pallas_call(
        functools.partial(_ffn_sample_kernel, batch=batch, steps=steps),
        out_shape=[jax.ShapeDtypeStruct(xs.shape, F32), jax.ShapeDtypeStruct(sffn2d.shape, F32)],
        scratch_shapes=[pltpu.VMEM((N_CONV * batch + rows, D_FF), F32)],
        compiler_params=params,
        name="ffn_sample",
    )(x1, ps, sffn2d, wts["g_pre_ffn"], wts["w_up"], wts["ffn_conv_w"], wts["w_down"], wts["g_post_ffn"],
      wts["w_ple_proj"], wts["w_ple_gate"])
    return (ys.reshape(batch, steps, D_MODEL), nconv.reshape(1, batch, N_CONV, D_MODEL),
            npool.reshape(1, batch, N_POOL, D_MODEL), nffn.reshape(1, batch, N_CONV, D_FF))


def kernel(x_prompt, x_sample, p_prompt, p_sample, state_conv_a, state_pool, state_ffn_conv, g_pre_mix, w_in,
           conv_a_w, w_a_out, pool_w, pool_scale, w_o, g_post_mix, g_pre_ffn, w_up, ffn_conv_w, w_down,
           g_post_ffn, w_ple_proj, w_ple_gate):
    depth = w_in.shape[0]
    xp, xs = x_prompt, x_sample
    prompt_states, sample_states = [], []
    for i in range(depth):
        wts = dict(
            g_pre_mix=g_pre_mix[i][None], g_post_mix=g_post_mix[i][None], g_pre_ffn=g_pre_ffn[i][None],
            g_post_ffn=g_post_ffn[i][None], pool_scale=pool_scale[i][None], conv_a_w=conv_a_w[i],
            ffn_conv_w=ffn_conv_w[i],
            w_in=w_in[i].astype(BF16), w_a_out=w_a_out[i].astype(BF16), pool_w=pool_w[i].astype(BF16),
            w_o=w_o[i].astype(BF16), w_up=w_up[i].astype(BF16), w_down=w_down[i].astype(BF16),
            w_ple_proj=w_ple_proj[i].astype(BF16), w_ple_gate=w_ple_gate[i].astype(BF16))
        xp, c1, p1, f1 = _prompt_layer(xp, p_prompt[i], wts)
        xs, c2, p2, f2 = _sample_layer(xs, p_sample[i], state_conv_a[i], state_pool[i], state_ffn_conv[i], wts)
        prompt_states.append((c1, p1, f1))
        sample_states.append((c2, p2, f2))
    cat = lambda states, k: jnp.concatenate([s[k] for s in states], axis=0)
    return (xp, xs, cat(prompt_states, 0), cat(prompt_states, 1), cat(prompt_states, 2),
            cat(sample_states, 0), cat(sample_states, 1), cat(sample_states, 2))
```

```python
import functools

import jax
import jax.numpy as jnp
from jax import lax
from jax.experimental import pallas as pl
from jax.experimental.pallas import tpu as pltpu

D_MODEL = 1024
D_FF = 2816
PLE_DIM = 256
CONV_W = 3
POOL_WINDOWS = (2, 4, 8, 16)
POOL_MAX = 16
EPS = 1e-6

N_CONV = CONV_W - 1
N_POOL = POOL_MAX - 1
CHUNK = 256
N_MIX_CHUNKS = D_MODEL // CHUNK
N_FF_CHUNKS = D_FF // CHUNK
TIME_STEPS = 64
SUB_STEPS = 32
SAMPLE_SUB_ROWS = 256
VMEM_LIMIT_BYTES = 60 * 1024 * 1024

F32 = jnp.float32
BF16 = jnp.bfloat16


def _rmsnorm(x, g):
    ms = jnp.mean(x * x, axis=-1, keepdims=True)
    return (x * lax.rsqrt(ms + EPS)) * g


def _dot(a, b):
    return jnp.dot(a, b, preferred_element_type=F32)


def _causal_conv(buf, cols, v, w_ref, halo, step, rows):
    buf[pl.ds(halo, rows), cols] = v
    y = buf[pl.ds(halo - 2 * step, rows), cols] * w_ref[0:1, cols]
    y = y + buf[pl.ds(halo - step, rows), cols] * w_ref[1:2, cols]
    return y + v * w_ref[2:3, cols]


def _mixer_core(x, w, conv_in, pool_diff):
    xb = _rmsnorm(x, w["g_pre"][...]).astype(BF16)

    def proj(k, j):
        lo = k * D_MODEL + j * CHUNK
        return _dot(xb, w["w_in"][:, lo:lo + CHUNK])

    def branch_proj(j):
        return tuple(proj(k, j) for k in range(4))

    def gate_proj(j):
        return proj(4, j), proj(5, j)

    nxt = branch_proj(0)
    y_a = None
    y_p = []
    for j in range(N_MIX_CHUNKS):
        cols = slice(j * CHUNK, (j + 1) * CHUNK)
        b, c, h, u = nxt
        nxt = branch_proj(j + 1) if j + 1 < N_MIX_CHUNKS else gate_proj(0)
        z = (b * conv_in(j, c * h)).astype(BF16)
        t = _dot(z, w["w_a_out"][cols, :])
        y_a = t if y_a is None else y_a + t
        d = pool_diff(j, u).astype(BF16)
        y_p.append(_dot(d, w["pool_w"][j]) * w["pool_scale"][:, cols])
    mix = None
    for j in range(N_MIX_CHUNKS):
        cols = slice(j * CHUNK, (j + 1) * CHUNK)
        ga, gp = nxt
        if j + 1 < N_MIX_CHUNKS:
            nxt = gate_proj(j + 1)
        merged = jax.nn.sigmoid(ga) * y_a[:, cols] + jax.nn.sigmoid(gp) * y_p[j]
        t = _dot(merged.astype(BF16), w["w_o"][cols, :])
        mix = t if mix is None else mix + t
    return x + _rmsnorm(mix, w["g_post"][...])


def _ffn_core(x, p, w, conv_in):
    hb = _rmsnorm(x, w["g_pre"][...]).astype(BF16)

    def up(j):
        return (_dot(hb, w["w_up"][:, j * CHUNK:(j + 1) * CHUNK]),
                _dot(hb, w["w_up"][:, D_FF + j * CHUNK:D_FF + (j + 1) * CHUNK]))

    nxt = up(0)
    f = None
    for j in range(N_FF_CHUNKS):
        cols = slice(j * CHUNK, (j + 1) * CHUNK)
        a, g = nxt
        if j + 1 < N_FF_CHUNKS:
            nxt = up(j + 1)
        h = jax.nn.gelu(conv_in(j, a), approximate=True) * g
        t = _dot(h.astype(BF16), w["w_down"][cols, :])
        f = t if f is None else f + t
    x = x + _rmsnorm(f, w["g_post"][...])
    gate = jax.nn.sigmoid(_dot(x.astype(BF16), w["w_gate"][...]))
    return x + gate * _dot(p.astype(BF16), w["w_proj"][...])


def _block_copy(hbm, buf, sem, block, slot, seq, steps, *, to_hbm=False):
    hbm_view = hbm.at[seq, pl.ds(block * steps, steps), :]
    buf_view = buf.at[slot, :, seq, :]
    src, dst = (buf_view, hbm_view) if to_hbm else (hbm_view, buf_view)
    return pltpu.make_async_copy(src, dst, sem.at[slot, seq])


def _fetch_block(hbm, buf, sem, steps, n_seq):
    i = pl.program_id(0)
    slot = i % 2

    def start(block, slot):
        for s in range(n_seq):
            _block_copy(hbm, buf, sem, block, slot, s, steps).start()

    @pl.when(i == 0)
    def _():
        start(0, 0)

    @pl.when(i + 1 < pl.num_programs(0))
    def _():
        start(i + 1, 1 - slot)

    for s in range(n_seq):
        _block_copy(hbm, buf, sem, i, slot, s, steps).wait()
    return slot


def _mixer_prompt_kernel(x_hbm, g_pre, w_in, conv_w, w_a_out, pool_w, pool_scale, w_o, g_post,
                         x1_ref, nconv_ref, npool_ref, xbuf, xsem, cbuf, ubuf, e0, e1, e2, inv_cnt,
                         *, steps, sub_steps, n_seq):
    i = pl.program_id(0)
    rows = steps * n_seq
    sub_rows = sub_steps * n_seq
    n_sub = steps // sub_steps
    conv_halo = N_CONV * n_seq
    pool_halo = N_POOL * n_seq
    head = POOL_MAX * n_seq
    ebufs = (e0, e1, e2)

    @pl.when(i == 0)
    def _():
        cbuf[0:conv_halo, :] = jnp.zeros((conv_halo, D_MODEL), F32)
        ubuf[0:pool_halo, :] = jnp.zeros((pool_halo, D_MODEL), F32)
        pos = lax.broadcasted_iota(jnp.int32, (POOL_MAX, n_seq, CHUNK), 0).reshape(head, CHUNK)
        for j, window in enumerate(POOL_WINDOWS):
            inv_cnt[j] = 1.0 / jnp.minimum(pos + 1, window).astype(F32)

    @pl.when(i == 1)
    def _():
        for j, window in enumerate(POOL_WINDOWS):
            inv_cnt[j] = jnp.full((head, CHUNK), 1.0 / window, F32)

    slot = _fetch_block(x_hbm, xbuf, xsem, steps, n_seq)
    w = dict(g_pre=g_pre, w_in=w_in, w_a_out=w_a_out, pool_w=pool_w, pool_scale=pool_scale, w_o=w_o,
             g_post=g_post)

    for sb in range(n_sub):
        off = sb * sub_rows
        last_sub = sb == n_sub - 1

        def conv_in(j, v, off=off, last_sub=last_sub):
            cols = slice(j * CHUNK, (j + 1) * CHUNK)
            y = _causal_conv(cbuf, cols, v, conv_w, conv_halo + off, n_seq, sub_rows)
            if last_sub:
                cbuf[0:conv_halo, cols] = cbuf[pl.ds(rows, conv_halo), cols]
            return y

        def pool_diff(j, u, sb=sb, off=off, last_sub=last_sub):
            cols = slice(j * CHUNK, (j + 1) * CHUNK)
            window = POOL_WINDOWS[j]
            base = pool_halo + off
            ubuf[pl.ds(base, sub_rows), cols] = u
            n_stages = j + 1
            tsum = None
            for k in range(n_stages):
                lo = base if sb > 0 else pool_halo - (window - (2 << k)) * n_seq
                n = base + sub_rows - lo
                shift = (1 << k) * n_seq
                if k == 0:
                    tsum = ubuf[pl.ds(lo, n), cols] + ubuf[pl.ds(lo - shift, n), cols]
                else:
                    tsum = ebufs[k - 1][pl.ds(lo, n), cols] + ebufs[k - 1][pl.ds(lo - shift, n), cols]
                if k < n_stages - 1:
                    ebufs[k][pl.ds(lo, n), cols] = tsum
            if last_sub:
                keep = (window - 1) * n_seq
                ubuf[pl.ds(pool_halo - keep, keep), cols] = ubuf[pl.ds(pool_halo + rows - keep, keep), cols]
            if sb > 0:
                return tsum * (1.0 / window) - u
            return jnp.concatenate([tsum[0:head, :] * inv_cnt[j] - u[0:head, :],
                                    tsum[head:, :] * (1.0 / window) - u[head:, :]], axis=0)

        x = xbuf[slot, sb * sub_steps:(sb + 1) * sub_steps].reshape(sub_rows, D_MODEL)
        x1_ref[off:off + sub_rows, :] = _mixer_core(x, w, conv_in, pool_diff)

    @pl.when(i == pl.num_programs(0) - 1)
    def _():
        nconv_ref[...] = cbuf[0:conv_halo, :].reshape(N_CONV, n_seq, D_MODEL)
        npool_ref[...] = ubuf[pl.ds(rows, pool_halo), :].reshape(N_POOL, n_seq, D_MODEL)


def _ffn_prompt_kernel(x_ref, p_hbm, g_pre, w_up, conv_w, w_down, g_post, w_proj, w_gate,
                       y_hbm, nffn_ref, pbuf, psem, ybuf, ysem, fbuf, *, steps, sub_steps, n_seq):
    i = pl.program_id(0)
    last = pl.num_programs(0) - 1
    rows = steps * n_seq
    sub_rows = sub_steps * n_seq
    n_sub = steps // sub_steps
    conv_halo = N_CONV * n_seq

    @pl.when(i == 0)
    def _():
        fbuf[0:conv_halo, :] = jnp.zeros((conv_halo, D_FF), F32)

    def wait_store(block, slot):
        for s in range(n_seq):
            _block_copy(y_hbm, ybuf, ysem, block, slot, s, steps, to_hbm=True).wait()

    slot = _fetch_block(p_hbm, pbuf, psem, steps, n_seq)

    @pl.when(i >= 2)
    def _():
        wait_store(i - 2, slot)

    w = dict(g_pre=g_pre, w_up=w_up, w_down=w_down, g_post=g_post, w_proj=w_proj, w_gate=w_gate)
    for sb in range(n_sub):
        off = sb * sub_rows
        last_sub = sb == n_sub - 1

        def conv_in(j, v, off=off, last_sub=last_sub):
            cols = slice(j * CHUNK, (j + 1) * CHUNK)
            y = _causal_conv(fbuf, cols, v, conv_w, conv_halo + off, n_seq, sub_rows)
            if last_sub:
                fbuf[0:conv_halo, cols] = fbuf[pl.ds(rows, conv_halo), cols]
            return y

        sub = slice(sb * sub_steps, (sb + 1) * sub_steps)
        p = pbuf[slot, sub].reshape(sub_rows, PLE_DIM)
        y = _ffn_core(x_ref[off:off + sub_rows, :], p, w, conv_in)
        ybuf[slot, sub] = y.reshape(sub_steps, n_seq, D_MODEL)

    for s in range(n_seq):
        _block_copy(y_hbm, ybuf, ysem, i, slot, s, steps, to_hbm=True).start()

    @pl.when(i == last)
    def _():
        nffn_ref[...] = fbuf[0:conv_halo, :].reshape(N_CONV, n_seq, D_FF)
        wait_store(i - 1, 1 - slot)
        wait_store(i, slot)


def _resident(shape):
    return pl.BlockSpec(shape, lambda i: (0,) * len(shape), pipeline_mode=pl.Buffered(1))


def _prompt_layer(x, p, wts):
    n_seq, seq_len, _ = x.shape
    steps, sub_steps = TIME_STEPS, SUB_STEPS
    n_blocks = seq_len // steps
    assert n_blocks * steps == seq_len and n_blocks >= 2 and n_seq % 8 == 0
    assert steps % sub_steps == 0 and sub_steps > POOL_MAX
    rows = steps * n_seq
    params = pltpu.CompilerParams(dimension_semantics=("arbitrary",), vmem_limit_bytes=VMEM_LIMIT_BYTES)
    any_space = pl.BlockSpec(memory_space=pl.ANY)
    row_block = pl.BlockSpec((rows, D_MODEL), lambda i: (i, 0))
    state_block = lambda n, width: pl.BlockSpec((n, n_seq, width), lambda i: (0, 0, 0))

    mixer_w = (wts["g_pre_mix"], wts["w_in"], wts["conv_a_w"], wts["w_a_out"], wts["pool_w"],
               wts["pool_scale"], wts["w_o"], wts["g_post_mix"])
    stage_rows = N_POOL * n_seq + rows
    x1, nconv, npool = pl.pallas_call(
        functools.partial(_mixer_prompt_kernel, steps=steps, sub_steps=sub_steps, n_seq=n_seq),
        grid=(n_blocks,),
        in_specs=[any_space] + [_resident(a.shape) for a in mixer_w],
        out_specs=[row_block, state_block(N_CONV, D_MODEL), state_block(N_POOL, D_MODEL)],
        out_shape=[jax.ShapeDtypeStruct((n_blocks * rows, D_MODEL), F32),
                   jax.ShapeDtypeStruct((N_CONV, n_seq, D_MODEL), F32),
                   jax.ShapeDtypeStruct((N_POOL, n_seq, D_MODEL), F32)],
        scratch_shapes=[pltpu.VMEM((2, steps, n_seq, D_MODEL), F32), pltpu.SemaphoreType.DMA((2, n_seq)),
                        pltpu.VMEM((N_CONV * n_seq + rows, D_MODEL), F32),
                        pltpu.VMEM((stage_rows, D_MODEL), F32),
                        pltpu.VMEM((stage_rows, D_MODEL), F32),
                        pltpu.VMEM((stage_rows, D_MODEL), F32),
                        pltpu.VMEM((stage_rows, D_MODEL), F32),
                        pltpu.VMEM((len(POOL_WINDOWS), POOL_MAX * n_seq, CHUNK), F32)],
        compiler_params=params,
        name="mixer_prompt",
    )(x, *mixer_w)

    ffn_w = (wts["g_pre_ffn"], wts["w_up"], wts["ffn_conv_w"], wts["w_down"], wts["g_post_ffn"],
             wts["w_ple_proj"], wts["w_ple_gate"])
    y, nffn = pl.pallas_call(
        functools.partial(_ffn_prompt_kernel, steps=steps, sub_steps=sub_steps, n_seq=n_seq),
        grid=(n_blocks,),
        in_specs=[row_block, any_space] + [_resident(a.shape) for a in ffn_w],
        out_specs=[any_space, state_block(N_CONV, D_FF)],
        out_shape=[jax.ShapeDtypeStruct((n_seq, seq_len, D_MODEL), F32),
                   jax.ShapeDtypeStruct((N_CONV, n_seq, D_FF), F32)],
        scratch_shapes=[pltpu.VMEM((2, steps, n_seq, PLE_DIM), F32), pltpu.SemaphoreType.DMA((2, n_seq)),
                        pltpu.VMEM((2, steps, n_seq, D_MODEL), F32), pltpu.SemaphoreType.DMA((2, n_seq)),
                        pltpu.VMEM((N_CONV * n_seq + rows, D_FF), F32)],
        compiler_params=params,
        name="ffn_prompt",
    )(x1, p, *ffn_w)
    to_batch_major = lambda s: jnp.transpose(s, (1, 0, 2))[None]
    return y, to_batch_major(nconv), to_batch_major(npool), to_batch_major(nffn)


def _mixer_sample_kernel(x_ref, sconv_ref, spool_ref, g_pre, w_in, conv_w, w_a_out, pool_w, pool_scale, w_o,
                         g_post, x1_ref, nconv_ref, npool_ref, cbuf, ubuf, *, batch, steps, sub_steps):
    sub_rows = sub_steps * batch
    for k in range(N_CONV):
        cbuf[k * batch:(k + 1) * batch, :] = sconv_ref[:, k, :]

    w = dict(g_pre=g_pre, w_in=w_in, w_a_out=w_a_out, pool_w=pool_w, pool_scale=pool_scale, w_o=w_o,
             g_post=g_post)
    for sb in range(steps // sub_steps):
        t0 = sb * sub_steps
        off = t0 * batch

        def conv_in(j, v, off=off):
            cols = slice(j * CHUNK, (j + 1) * CHUNK)
            return _causal_conv(cbuf, cols, v, conv_w, N_CONV * batch + off, batch, sub_rows)

        def pool_diff(j, u, t0=t0):
            cols = slice(j * CHUNK, (j + 1) * CHUNK)
            window = POOL_WINDOWS[j]
            ubuf[t0:t0 + sub_steps, :, cols] = u.reshape(sub_steps, batch, CHUNK)

            def ext(e):
                return spool_ref[:, e, cols] if e < N_POOL else ubuf[e - N_POOL, :, cols]

            diffs = []
            for t in range(t0, t0 + sub_steps):
                tsum = ext(N_POOL + t - window + 1)
                for e in range(N_POOL + t - window + 2, N_POOL + t + 1):
                    tsum = tsum + ext(e)
                diffs.append(tsum * (1.0 / window) - ext(N_POOL + t))
            return jnp.concatenate(diffs, axis=0)

        x = jnp.concatenate([x_ref[:, t, :] for t in range(t0, t0 + sub_steps)], axis=0)
        x1_ref[off:off + sub_rows, :] = _mixer_core(x, w, conv_in, pool_diff)

    for k in range(N_CONV):
        nconv_ref[:, k, :] = cbuf[(steps + k) * batch:(steps + k + 1) * batch, :]
    for m in range(N_POOL):
        e = steps + m
        npool_ref[:, m, :] = spool_ref[:, e, :] if e < N_POOL else ubuf[e - N_POOL]


def _ffn_sample_kernel(x_ref, p_ref, sffn_ref, g_pre, w_up, conv_w, w_down, g_post, w_proj, w_gate,
                       y_ref, nffn_ref, fbuf, *, batch, steps, sub_steps):
    sub_rows = sub_steps * batch
    for k in range(N_CONV):
        fbuf[k * batch:(k + 1) * batch, :] = sffn_ref[:, k, :]

    w = dict(g_pre=g_pre, w_up=w_up, w_down=w_down, g_post=g_post, w_proj=w_proj, w_gate=w_gate)
    for sb in range(steps // sub_steps):
        t0 = sb * sub_steps
        off = t0 * batch

        def conv_in(j, v, off=off):
            cols = slice(j * CHUNK, (j + 1) * CHUNK)
            return _causal_conv(fbuf, cols, v, conv_w, N_CONV * batch + off, batch, sub_rows)

        p = jnp.concatenate([p_ref[:, t, :] for t in range(t0, t0 + sub_steps)], axis=0)
        y = _ffn_core(x_ref[off:off + sub_rows, :], p, w, conv_in)
        for t in range(sub_steps):
            y_ref[:, t0 + t, :] = y[t * batch:(t + 1) * batch, :]

    for k in range(N_CONV):
        nffn_ref[:, k, :] = fbuf[(steps + k) * batch:(steps + k + 1) * batch, :]


def _sample_layer(x, p, sconv, spool, sffn, wts):
    batch, steps, _ = x.shape
    sub_steps = max(1, SAMPLE_SUB_ROWS // batch)
    assert steps % sub_steps == 0 and batch % 8 == 0
    rows = batch * steps
    params = pltpu.CompilerParams(vmem_limit_bytes=VMEM_LIMIT_BYTES)

    x1, nconv, npool = pl.pallas_call(
        functools.partial(_mixer_sample_kernel, batch=batch, steps=steps, sub_steps=sub_steps),
        out_shape=[jax.ShapeDtypeStruct((rows, D_MODEL), F32),
                   jax.ShapeDtypeStruct(sconv.shape, F32),
                   jax.ShapeDtypeStruct(spool.shape, F32)],
        scratch_shapes=[pltpu.VMEM(((N_CONV + steps) * batch, D_MODEL), F32),
                        pltpu.VMEM((steps, batch, D_MODEL), F32)],
        compiler_params=params,
        name="mixer_sample",
    )(x, sconv, spool, wts["g_pre_mix"], wts["w_in"], wts["conv_a_w"], wts["w_a_out"], wts["pool_w"],
      wts["pool_scale"], wts["w_o"], wts["g_post_mix"])

    y, nffn = pl.pallas_call(
        functools.partial(_ffn_sample_kernel, batch=batch, steps=steps, sub_steps=sub_steps),
        out_shape=[jax.ShapeDtypeStruct(x.shape, F32), jax.ShapeDtypeStruct(sffn.shape, F32)],
        scratch_shapes=[pltpu.VMEM(((N_CONV + steps) * batch, D_FF), F32)],
        compiler_params=params,
        name="ffn_sample",
    )(x1, p, sffn, wts["g_pre_ffn"], wts["w_up"], wts["ffn_conv_w"], wts["w_down"], wts["g_post_ffn"],
      wts["w_ple_proj"], wts["w_ple_gate"])
    return y, nconv[None], npool[None], nffn[None]


def kernel(x_prompt, x_sample, p_prompt, p_sample, state_conv_a, state_pool, state_ffn_conv, g_pre_mix, w_in,
           conv_a_w, w_a_out, pool_w, pool_scale, w_o, g_post_mix, g_pre_ffn, w_up, ffn_conv_w, w_down,
           g_post_ffn, w_ple_proj, w_ple_gate):
    depth = w_in.shape[0]
    xp, xs = x_prompt, x_sample
    prompt_states, sample_states = [], []
    for i in range(depth):
        wts = dict(
            g_pre_mix=g_pre_mix[i][None], g_post_mix=g_post_mix[i][None], g_pre_ffn=g_pre_ffn[i][None],
            g_post_ffn=g_post_ffn[i][None], pool_scale=pool_scale[i][None], conv_a_w=conv_a_w[i],
            ffn_conv_w=ffn_conv_w[i],
            w_in=w_in[i].astype(BF16), w_a_out=w_a_out[i].astype(BF16), pool_w=pool_w[i].astype(BF16),
            w_o=w_o[i].astype(BF16), w_up=w_up[i].astype(BF16), w_down=w_down[i].astype(BF16),
            w_ple_proj=w_ple_proj[i].astype(BF16), w_ple_gate=w_ple_gate[i].astype(BF16))
        xp, c1, p1, f1 = _prompt_layer(xp, p_prompt[i], wts)
        xs, c2, p2, f2 = _sample_layer(xs, p_sample[i], state_conv_a[i], state_pool[i], state_ffn_conv[i], wts)
        prompt_states.append((c1, p1, f1))
        sample_states.append((c2, p2, f2))
    cat = lambda states, k: states[0][k] if depth == 1 else jnp.concatenate([s[k] for s in states], axis=0)
    return (xp, xs, cat(prompt_states, 0), cat(prompt_states, 1), cat(prompt_states, 2),
            cat(sample_states, 0), cat(sample_states, 1), cat(sample_states, 2))
```

```python
import functools

import jax
import jax.numpy as jnp
from jax import lax
from jax.experimental import pallas as pl
from jax.experimental.pallas import tpu as pltpu

D_MODEL = 1024
D_FF = 2816
PLE_DIM = 256
CONV_W = 3
POOL_WINDOWS = (2, 4, 8, 16)
POOL_MAX = 16
EPS = 1e-6

N_CONV = CONV_W - 1
N_POOL = POOL_MAX - 1
CHUNK = 256
N_MIX_CHUNKS = D_MODEL // CHUNK
N_FF_CHUNKS = D_FF // CHUNK
TIME_STEPS = 64
SUB_STEPS = 32
SAMPLE_SUB_ROWS = 256
WEIGHT_CHUNK_ROWS = 128
VMEM_LIMIT_BYTES = 60 * 1024 * 1024

F32 = jnp.float32
BF16 = jnp.bfloat16


def _rmsnorm(x, g):
    ms = jnp.mean(x * x, axis=-1, keepdims=True)
    return (x * lax.rsqrt(ms + EPS)) * g


def _dot(a, b):
    return jnp.dot(a, b, preferred_element_type=F32)


def _causal_conv(buf, cols, v, w_ref, halo, step, rows):
    buf[pl.ds(halo, rows), cols] = v
    y = buf[pl.ds(halo - 2 * step, rows), cols] * w_ref[0:1, cols]
    y = y + buf[pl.ds(halo - step, rows), cols] * w_ref[1:2, cols]
    return y + v * w_ref[2:3, cols]


def _mixer_core(x, w, conv_in, pool_diff):
    xb = _rmsnorm(x, w["g_pre"][...]).astype(BF16)

    def proj(k, j):
        lo = k * D_MODEL + j * CHUNK
        return _dot(xb, w["w_in"][:, lo:lo + CHUNK])

    def branch_proj(j):
        return tuple(proj(k, j) for k in range(4))

    def gate_proj(j):
        return proj(4, j), proj(5, j)

    nxt = branch_proj(0)
    y_a = None
    y_p = []
    for j in range(N_MIX_CHUNKS):
        cols = slice(j * CHUNK, (j + 1) * CHUNK)
        b, c, h, u = nxt
        nxt = branch_proj(j + 1) if j + 1 < N_MIX_CHUNKS else gate_proj(0)
        z = (b * conv_in(j, c * h)).astype(BF16)
        t = _dot(z, w["w_a_out"][cols, :])
        y_a = t if y_a is None else y_a + t
        d = pool_diff(j, u).astype(BF16)
        y_p.append(_dot(d, w["pool_w"][cols, :]) * w["pool_scale"][:, cols])
    mix = None
    for j in range(N_MIX_CHUNKS):
        cols = slice(j * CHUNK, (j + 1) * CHUNK)
        ga, gp = nxt
        if j + 1 < N_MIX_CHUNKS:
            nxt = gate_proj(j + 1)
        merged = jax.nn.sigmoid(ga) * y_a[:, cols] + jax.nn.sigmoid(gp) * y_p[j]
        t = _dot(merged.astype(BF16), w["w_o"][cols, :])
        mix = t if mix is None else mix + t
    return x + _rmsnorm(mix, w["g_post"][...])


def _ffn_core(x, p, w, conv_in):
    hb = _rmsnorm(x, w["g_pre"][...]).astype(BF16)

    def up(j):
        return (_dot(hb, w["w_up"][:, j * CHUNK:(j + 1) * CHUNK]),
                _dot(hb, w["w_up"][:, D_FF + j * CHUNK:D_FF + (j + 1) * CHUNK]))

    nxt = up(0)
    f = None
    for j in range(N_FF_CHUNKS):
        cols = slice(j * CHUNK, (j + 1) * CHUNK)
        a, g = nxt
        if j + 1 < N_FF_CHUNKS:
            nxt = up(j + 1)
        h = jax.nn.gelu(conv_in(j, a), approximate=True) * g
        t = _dot(h.astype(BF16), w["w_down"][cols, :])
        f = t if f is None else f + t
    x = x + _rmsnorm(f, w["g_post"][...])
    gate = jax.nn.sigmoid(_dot(x.astype(BF16), w["w_gate"][...]))
    return x + gate * _dot(p.astype(BF16), w["w_proj"][...])


def _block_copy(hbm, buf, sem, block, slot, seq, steps, *, to_hbm=False):
    hbm_view = hbm.at[seq, pl.ds(block * steps, steps), :]
    buf_view = buf.at[slot, :, seq, :]
    src, dst = (buf_view, hbm_view) if to_hbm else (hbm_view, buf_view)
    return pltpu.make_async_copy(src, dst, sem.at[slot, seq])


def _fetch_block(hbm, buf, sem, steps, n_seq):
    i = pl.program_id(0)
    slot = i % 2

    def start(block, slot):
        for s in range(n_seq):
            _block_copy(hbm, buf, sem, block, slot, s, steps).start()

    @pl.when(i == 0)
    def _():
        start(0, 0)

    @pl.when(i + 1 < pl.num_programs(0))
    def _():
        start(i + 1, 1 - slot)

    for s in range(n_seq):
        _block_copy(hbm, buf, sem, i, slot, s, steps).wait()
    return slot


def _stream_weight_as_bf16(w_hbm, w_vmem, stage, sem):
    n_rows, width = w_hbm.shape
    chunk = stage.shape[1]
    n_chunks = n_rows // chunk
    assert n_chunks * chunk == n_rows and width <= stage.shape[2]

    def chunk_copy(c, slot):
        return pltpu.make_async_copy(w_hbm.at[pl.ds(c * chunk, chunk), :], stage.at[slot, :, 0:width], sem.at[slot])

    chunk_copy(0, 0).start()

    def body(c, carry):
        slot = c % 2

        @pl.when(c + 1 < n_chunks)
        def _():
            chunk_copy(c + 1, 1 - slot).start()

        chunk_copy(c, slot).wait()
        w_vmem[pl.ds(pl.multiple_of(c * chunk, chunk), chunk), :] = stage[slot, :, 0:width].astype(BF16)
        return carry

    lax.fori_loop(0, n_chunks, body, 0)


def _load_weights(pairs, exports, stage, stage_sem, export_sem):
    for w_hbm, w_vmem in pairs:
        _stream_weight_as_bf16(w_hbm, w_vmem, stage, stage_sem)
    for k, ((_, w_vmem), w_out) in enumerate(zip(pairs, exports)):
        pltpu.make_async_copy(w_vmem, w_out, export_sem.at[k]).start()


def _finish_weight_exports(pairs, exports, export_sem):
    for k, ((_, w_vmem), w_out) in enumerate(zip(pairs, exports)):
        pltpu.make_async_copy(w_vmem, w_out, export_sem.at[k]).wait()


def _mixer_prompt_kernel(x_hbm, g_pre, w_in_hbm, conv_w, w_a_out_hbm, pool_w_hbm, pool_scale, w_o_hbm, g_post,
                         x1_ref, nconv_ref, npool_ref, w_in_out, w_a_out_out, pool_w_out, w_o_out,
                         xbuf, xsem, cbuf, ubuf, e0, e1, e2, inv_cnt,
                         w_in, w_a_out, pool_w, w_o, stage, stage_sem, export_sem,
                         *, steps, sub_steps, n_seq):
    i = pl.program_id(0)
    weight_pairs = ((w_in_hbm, w_in), (w_a_out_hbm, w_a_out), (pool_w_hbm, pool_w), (w_o_hbm, w_o))
    weight_exports = (w_in_out, w_a_out_out, pool_w_out, w_o_out)
    rows = steps * n_seq
    sub_rows = sub_steps * n_seq
    n_sub = steps // sub_steps
    conv_halo = N_CONV * n_seq
    pool_halo = N_POOL * n_seq
    head = POOL_MAX * n_seq
    ebufs = (e0, e1, e2)

    @pl.when(i == 0)
    def _():
        cbuf[0:conv_halo, :] = jnp.zeros((conv_halo, D_MODEL), F32)
        ubuf[0:pool_halo, :] = jnp.zeros((pool_halo, D_MODEL), F32)
        pos = lax.broadcasted_iota(jnp.int32, (POOL_MAX, n_seq, CHUNK), 0).reshape(head, CHUNK)
        for j, window in enumerate(POOL_WINDOWS):
            inv_cnt[j] = 1.0 / jnp.minimum(pos + 1, window).astype(F32)
        _load_weights(weight_pairs, weight_exports, stage, stage_sem, export_sem)

    @pl.when(i == 1)
    def _():
        for j, window in enumerate(POOL_WINDOWS):
            inv_cnt[j] = jnp.full((head, CHUNK), 1.0 / window, F32)

    slot = _fetch_block(x_hbm, xbuf, xsem, steps, n_seq)
    w = dict(g_pre=g_pre, w_in=w_in, w_a_out=w_a_out, pool_w=pool_w, pool_scale=pool_scale, w_o=w_o,
             g_post=g_post)

    for sb in range(n_sub):
        off = sb * sub_rows
        last_sub = sb == n_sub - 1

        def conv_in(j, v, off=off, last_sub=last_sub):
            cols = slice(j * CHUNK, (j + 1) * CHUNK)
            y = _causal_conv(cbuf, cols, v, conv_w, conv_halo + off, n_seq, sub_rows)
            if last_sub:
                cbuf[0:conv_halo, cols] = cbuf[pl.ds(rows, conv_halo), cols]
            return y

        def pool_diff(j, u, sb=sb, off=off, last_sub=last_sub):
            cols = slice(j * CHUNK, (j + 1) * CHUNK)
            window = POOL_WINDOWS[j]
            base = pool_halo + off
            ubuf[pl.ds(base, sub_rows), cols] = u
            n_stages = j + 1
            tsum = None
            for k in range(n_stages):
                lo = base if sb > 0 else pool_halo - (window - (2 << k)) * n_seq
                n = base + sub_rows - lo
                shift = (1 << k) * n_seq
                if k == 0:
                    tsum = ubuf[pl.ds(lo, n), cols] + ubuf[pl.ds(lo - shift, n), cols]
                else:
                    tsum = ebufs[k - 1][pl.ds(lo, n), cols] + ebufs[k - 1][pl.ds(lo - shift, n), cols]
                if k < n_stages - 1:
                    ebufs[k][pl.ds(lo, n), cols] = tsum
            if last_sub:
                keep = (window - 1) * n_seq
                ubuf[pl.ds(pool_halo - keep, keep), cols] = ubuf[pl.ds(pool_halo + rows - keep, keep), cols]
            if sb > 0:
                return tsum * (1.0 / window) - u
            return jnp.concatenate([tsum[0:head, :] * inv_cnt[j] - u[0:head, :],
                                    tsum[head:, :] * (1.0 / window) - u[head:, :]], axis=0)

        x = xbuf[slot, sb * sub_steps:(sb + 1) * sub_steps].reshape(sub_rows, D_MODEL)
        x1_ref[off:off + sub_rows, :] = _mixer_core(x, w, conv_in, pool_diff)

    @pl.when(i == pl.num_programs(0) - 1)
    def _():
        nconv_ref[...] = cbuf[0:conv_halo, :].reshape(N_CONV, n_seq, D_MODEL)
        npool_ref[...] = ubuf[pl.ds(rows, pool_halo), :].reshape(N_POOL, n_seq, D_MODEL)
        _finish_weight_exports(weight_pairs, weight_exports, export_sem)


def _ffn_prompt_kernel(x_ref, p_hbm, g_pre, w_up_hbm, conv_w, w_down_hbm, g_post, w_proj_hbm, w_gate_hbm,
                       y_hbm, nffn_ref, w_up_out, w_down_out, w_proj_out, w_gate_out,
                       pbuf, psem, ybuf, ysem, fbuf,
                       w_up, w_down, w_proj, w_gate, stage, stage_sem, export_sem,
                       *, steps, sub_steps, n_seq):
    i = pl.program_id(0)
    last = pl.num_programs(0) - 1
    rows = steps * n_seq
    sub_rows = sub_steps * n_seq
    n_sub = steps // sub_steps
    conv_halo = N_CONV * n_seq
    weight_pairs = ((w_up_hbm, w_up), (w_down_hbm, w_down), (w_proj_hbm, w_proj), (w_gate_hbm, w_gate))
    weight_exports = (w_up_out, w_down_out, w_proj_out, w_gate_out)

    @pl.when(i == 0)
    def _():
        fbuf[0:conv_halo, :] = jnp.zeros((conv_halo, D_FF), F32)
        _load_weights(weight_pairs, weight_exports, stage, stage_sem, export_sem)

    def wait_store(block, slot):
        for s in range(n_seq):
            _block_copy(y_hbm, ybuf, ysem, block, slot, s, steps, to_hbm=True).wait()

    slot = _fetch_block(p_hbm, pbuf, psem, steps, n_seq)

    @pl.when(i >= 2)
    def _():
        wait_store(i - 2, slot)

    w = dict(g_pre=g_pre, w_up=w_up, w_down=w_down, g_post=g_post, w_proj=w_proj, w_gate=w_gate)
    for sb in range(n_sub):
        off = sb * sub_rows
        last_sub = sb == n_sub - 1

        def conv_in(j, v, off=off, last_sub=last_sub):
            cols = slice(j * CHUNK, (j + 1) * CHUNK)
            y = _causal_conv(fbuf, cols, v, conv_w, conv_halo + off, n_seq, sub_rows)
            if last_sub:
                fbuf[0:conv_halo, cols] = fbuf[pl.ds(rows, conv_halo), cols]
            return y

        sub = slice(sb * sub_steps, (sb + 1) * sub_steps)
        p = pbuf[slot, sub].reshape(sub_rows, PLE_DIM)
        y = _ffn_core(x_ref[off:off + sub_rows, :], p, w, conv_in)
        ybuf[slot, sub] = y.reshape(sub_steps, n_seq, D_MODEL)

    for s in range(n_seq):
        _block_copy(y_hbm, ybuf, ysem, i, slot, s, steps, to_hbm=True).start()

    @pl.when(i == last)
    def _():
        nffn_ref[...] = fbuf[0:conv_halo, :].reshape(N_CONV, n_seq, D_FF)
        wait_store(i - 1, 1 - slot)
        wait_store(i, slot)
        _finish_weight_exports(weight_pairs, weight_exports, export_sem)


def _resident(shape):
    return pl.BlockSpec(shape, lambda i: (0,) * len(shape), pipeline_mode=pl.Buffered(1))


def _prompt_layer(x, p, wts):
    n_seq, seq_len, _ = x.shape
    steps, sub_steps = TIME_STEPS, SUB_STEPS
    n_blocks = seq_len // steps
    assert n_blocks * steps == seq_len and n_blocks >= 2 and n_seq % 8 == 0
    assert steps % sub_steps == 0 and sub_steps > POOL_MAX
    rows = steps * n_seq
    params = pltpu.CompilerParams(dimension_semantics=("arbitrary",), vmem_limit_bytes=VMEM_LIMIT_BYTES)
    any_space = pl.BlockSpec(memory_space=pl.ANY)
    row_block = pl.BlockSpec((rows, D_MODEL), lambda i: (i, 0))
    state_block = lambda n, width: pl.BlockSpec((n, n_seq, width), lambda i: (0, 0, 0))

    weight_spec = lambda a: any_space if a.dtype == F32 and a.shape[0] >= WEIGHT_CHUNK_ROWS else _resident(a.shape)
    bf16_like = lambda ws: [jax.ShapeDtypeStruct(a.shape, BF16) for a in ws]
    vmem_bf16 = lambda ws: [pltpu.VMEM(a.shape, BF16) for a in ws]
    stage = lambda ws: [pltpu.VMEM((2, WEIGHT_CHUNK_ROWS, max(a.shape[1] for a in ws)), F32),
                        pltpu.SemaphoreType.DMA((2,)), pltpu.SemaphoreType.DMA((len(ws),))]

    mixer_w = (wts["g_pre_mix"], wts["w_in"], wts["conv_a_w"], wts["w_a_out"], wts["pool_w"],
               wts["pool_scale"], wts["w_o"], wts["g_post_mix"])
    mixer_mats = (wts["w_in"], wts["w_a_out"], wts["pool_w"], wts["w_o"])
    stage_rows = N_POOL * n_seq + rows
    x1, nconv, npool, *mixer_bf16 = pl.pallas_call(
        functools.partial(_mixer_prompt_kernel, steps=steps, sub_steps=sub_steps, n_seq=n_seq),
        grid=(n_blocks,),
        in_specs=[any_space] + [weight_spec(a) for a in mixer_w],
        out_specs=[row_block, state_block(N_CONV, D_MODEL), state_block(N_POOL, D_MODEL)] + [any_space] * 4,
        out_shape=[jax.ShapeDtypeStruct((n_blocks * rows, D_MODEL), F32),
                   jax.ShapeDtypeStruct((N_CONV, n_seq, D_MODEL), F32),
                   jax.ShapeDtypeStruct((N_POOL, n_seq, D_MODEL), F32)] + bf16_like(mixer_mats),
        scratch_shapes=[pltpu.VMEM((2, steps, n_seq, D_MODEL), F32), pltpu.SemaphoreType.DMA((2, n_seq)),
                        pltpu.VMEM((N_CONV * n_seq + rows, D_MODEL), F32),
                        pltpu.VMEM((stage_rows, D_MODEL), F32),
                        pltpu.VMEM((stage_rows, D_MODEL), F32),
                        pltpu.VMEM((stage_rows, D_MODEL), F32),
                        pltpu.VMEM((stage_rows, D_MODEL), F32),
                        pltpu.VMEM((len(POOL_WINDOWS), POOL_MAX * n_seq, CHUNK), F32)]
                       + vmem_bf16(mixer_mats) + stage(mixer_mats),
        compiler_params=params,
        name="mixer_prompt",
    )(x, *mixer_w)

    ffn_w = (wts["g_pre_ffn"], wts["w_up"], wts["ffn_conv_w"], wts["w_down"], wts["g_post_ffn"],
             wts["w_ple_proj"], wts["w_ple_gate"])
    ffn_mats = (wts["w_up"], wts["w_down"], wts["w_ple_proj"], wts["w_ple_gate"])
    y, nffn, *ffn_bf16 = pl.pallas_call(
        functools.partial(_ffn_prompt_kernel, steps=steps, sub_steps=sub_steps, n_seq=n_seq),
        grid=(n_blocks,),
        in_specs=[row_block, any_space] + [weight_spec(a) for a in ffn_w],
        out_specs=[any_space, state_block(N_CONV, D_FF)] + [any_space] * 4,
        out_shape=[jax.ShapeDtypeStruct((n_seq, seq_len, D_MODEL), F32),
                   jax.ShapeDtypeStruct((N_CONV, n_seq, D_FF), F32)] + bf16_like(ffn_mats),
        scratch_shapes=[pltpu.VMEM((2, steps, n_seq, PLE_DIM), F32), pltpu.SemaphoreType.DMA((2, n_seq)),
                        pltpu.VMEM((2, steps, n_seq, D_MODEL), F32), pltpu.SemaphoreType.DMA((2, n_seq)),
                        pltpu.VMEM((N_CONV * n_seq + rows, D_FF), F32)]
                       + vmem_bf16(ffn_mats) + stage(ffn_mats),
        compiler_params=params,
        name="ffn_prompt",
    )(x1, p, *ffn_w)
    bf16_wts = dict(wts, **dict(zip(("w_in", "w_a_out", "pool_w", "w_o"), mixer_bf16)),
                    **dict(zip(("w_up", "w_down", "w_ple_proj", "w_ple_gate"), ffn_bf16)))
    to_batch_major = lambda s: jnp.transpose(s, (1, 0, 2))[None]
    return y, to_batch_major(nconv), to_batch_major(npool), to_batch_major(nffn), bf16_wts


def _mixer_sample_kernel(x_ref, sconv_ref, spool_ref, g_pre, w_in, conv_w, w_a_out, pool_w, pool_scale, w_o,
                         g_post, x1_ref, nconv_ref, npool_ref, cbuf, ubuf, *, batch, steps, sub_steps):
    sub_rows = sub_steps * batch
    for k in range(N_CONV):
        cbuf[k * batch:(k + 1) * batch, :] = sconv_ref[:, k, :]

    w = dict(g_pre=g_pre, w_in=w_in, w_a_out=w_a_out, pool_w=pool_w, pool_scale=pool_scale, w_o=w_o,
             g_post=g_post)
    for sb in range(steps // sub_steps):
        t0 = sb * sub_steps
        off = t0 * batch

        def conv_in(j, v, off=off):
            cols = slice(j * CHUNK, (j + 1) * CHUNK)
            return _causal_conv(cbuf, cols, v, conv_w, N_CONV * batch + off, batch, sub_rows)

        def pool_diff(j, u, t0=t0):
            cols = slice(j * CHUNK, (j + 1) * CHUNK)
            window = POOL_WINDOWS[j]
            ubuf[t0:t0 + sub_steps, :, cols] = u.reshape(sub_steps, batch, CHUNK)

            def ext(e):
                return spool_ref[:, e, cols] if e < N_POOL else ubuf[e - N_POOL, :, cols]

            diffs = []
            for t in range(t0, t0 + sub_steps):
                tsum = ext(N_POOL + t - window + 1)
                for e in range(N_POOL + t - window + 2, N_POOL + t + 1):
                    tsum = tsum + ext(e)
                diffs.append(tsum * (1.0 / window) - ext(N_POOL + t))
            return jnp.concatenate(diffs, axis=0)

        x = jnp.concatenate([x_ref[:, t, :] for t in range(t0, t0 + sub_steps)], axis=0)
        x1_ref[off:off + sub_rows, :] = _mixer_core(x, w, conv_in, pool_diff)

    for k in range(N_CONV):
        nconv_ref[:, k, :] = cbuf[(steps + k) * batch:(steps + k + 1) * batch, :]
    for m in range(N_POOL):
        e = steps + m
        npool_ref[:, m, :] = spool_ref[:, e, :] if e < N_POOL else ubuf[e - N_POOL]


def _ffn_sample_kernel(x_ref, p_ref, sffn_ref, g_pre, w_up, conv_w, w_down, g_post, w_proj, w_gate,
                       y_ref, nffn_ref, fbuf, *, batch, steps, sub_steps):
    sub_rows = sub_steps * batch
    for k in range(N_CONV):
        fbuf[k * batch:(k + 1) * batch, :] = sffn_ref[:, k, :]

    w = dict(g_pre=g_pre, w_up=w_up, w_down=w_down, g_post=g_post, w_proj=w_proj, w_gate=w_gate)
    for sb in range(steps // sub_steps):
        t0 = sb * sub_steps
        off = t0 * batch

        def conv_in(j, v, off=off):
            cols = slice(j * CHUNK, (j + 1) * CHUNK)
            return _causal_conv(fbuf, cols, v, conv_w, N_CONV * batch + off, batch, sub_rows)

        p = jnp.concatenate([p_ref[:, t, :] for t in range(t0, t0 + sub_steps)], axis=0)
        y = _ffn_core(x_ref[off:off + sub_rows, :], p, w, conv_in)
        for t in range(sub_steps):
            y_ref[:, t0 + t, :] = y[t * batch:(t + 1) * batch, :]

    for k in range(N_CONV):
        nffn_ref[:, k, :] = fbuf[(steps + k) * batch:(steps + k + 1) * batch, :]


def _sample_layer(x, p, sconv, spool, sffn, layer, wts):
    batch, steps, _ = x.shape
    sub_steps = max(1, SAMPLE_SUB_ROWS // batch)
    assert steps % sub_steps == 0 and batch % 8 == 0
    rows = batch * steps
    params = pltpu.CompilerParams(vmem_limit_bytes=VMEM_LIMIT_BYTES)
    whole = pl.BlockSpec(memory_space=pltpu.VMEM)
    of_layer = lambda a, k: pl.BlockSpec((None,) + a.shape[1:], lambda: (k, 0, 0, 0))
    one_layer = lambda a: jax.ShapeDtypeStruct((1,) + a.shape[1:], F32)

    mixer_w = (wts["g_pre_mix"], wts["w_in"], wts["conv_a_w"], wts["w_a_out"], wts["pool_w"],
               wts["pool_scale"], wts["w_o"], wts["g_post_mix"])
    x1, nconv, npool = pl.pallas_call(
        functools.partial(_mixer_sample_kernel, batch=batch, steps=steps, sub_steps=sub_steps),
        in_specs=[whole, of_layer(sconv, layer), of_layer(spool, layer)] + [whole] * len(mixer_w),
        out_specs=[whole, of_layer(sconv, 0), of_layer(spool, 0)],
        out_shape=[jax.ShapeDtypeStruct((rows, D_MODEL), F32),
                   one_layer(sconv), one_layer(spool)],
        scratch_shapes=[pltpu.VMEM(((N_CONV + steps) * batch, D_MODEL), F32),
                        pltpu.VMEM((steps, batch, D_MODEL), F32)],
        compiler_params=params,
        name="mixer_sample",
    )(x, sconv, spool, *mixer_w)

    ffn_w = (wts["g_pre_ffn"], wts["w_up"], wts["ffn_conv_w"], wts["w_down"], wts["g_post_ffn"],
             wts["w_ple_proj"], wts["w_ple_gate"])
    y, nffn = pl.pallas_call(
        functools.partial(_ffn_sample_kernel, batch=batch, steps=steps, sub_steps=sub_steps),
        in_specs=[whole, of_layer(p, layer), of_layer(sffn, layer)] + [whole] * len(ffn_w),
        out_specs=[whole, of_layer(sffn, 0)],
        out_shape=[jax.ShapeDtypeStruct(x.shape, F32), one_layer(sffn)],
        scratch_shapes=[pltpu.VMEM(((N_CONV + steps) * batch, D_FF), F32)],
        compiler_params=params,
        name="ffn_sample",
    )(x1, p, sffn, *ffn_w)
    return y, nconv, npool, nffn


def kernel(x_prompt, x_sample, p_prompt, p_sample, state_conv_a, state_pool, state_ffn_conv, g_pre_mix, w_in,
           conv_a_w, w_a_out, pool_w, pool_scale, w_o, g_post_mix, g_pre_ffn, w_up, ffn_conv_w, w_down,
           g_post_ffn, w_ple_proj, w_ple_gate):
    depth = w_in.shape[0]
    xp, xs = x_prompt, x_sample
    prompt_states, sample_states = [], []
    for i in range(depth):
        wts = dict(
            g_pre_mix=g_pre_mix[i][None], g_post_mix=g_post_mix[i][None], g_pre_ffn=g_pre_ffn[i][None],
            g_post_ffn=g_post_ffn[i][None], pool_scale=pool_scale[i][None], conv_a_w=conv_a_w[i],
            ffn_conv_w=ffn_conv_w[i],
            w_in=w_in[i], w_a_out=w_a_out[i], pool_w=pool_w[i].reshape(D_MODEL, CHUNK),
            w_o=w_o[i], w_up=w_up[i], w_down=w_down[i], w_ple_proj=w_ple_proj[i], w_ple_gate=w_ple_gate[i])
        xp, c1, p1, f1, bf16_wts = _prompt_layer(xp, p_prompt[i], wts)
        xs, c2, p2, f2 = _sample_layer(xs, p_sample, state_conv_a, state_pool, state_ffn_conv, i, bf16_wts)
        prompt_states.append((c1, p1, f1))
        sample_states.append((c2, p2, f2))
    cat = lambda states, k: states[0][k] if depth == 1 else jnp.concatenate([s[k] for s in states], axis=0)
    return (xp, xs, cat(prompt_states, 0), cat(prompt_states, 1), cat(prompt_states, 2),
            cat(sample_states, 0), cat(sample_states, 1), cat(sample_states, 2))
```

```python
import functools

import jax
import jax.numpy as jnp
from jax import lax
from jax.experimental import pallas as pl
from jax.experimental.pallas import tpu as pltpu

D_MODEL = 1024
D_FF = 2816
PLE_DIM = 256
CONV_W = 3
POOL_WINDOWS = (2, 4, 8, 16)
POOL_MAX = 16
EPS = 1e-6

N_CONV = CONV_W - 1
N_POOL = POOL_MAX - 1
CHUNK = 256
N_MIX_CHUNKS = D_MODEL // CHUNK
N_FF_CHUNKS = D_FF // CHUNK
TIME_STEPS = 64
SUB_STEPS = 32
SAMPLE_SUB_ROWS = 256
WEIGHT_CHUNK_ROWS = 128
WEIGHT_STAGE_SLOTS = 4
VMEM_LIMIT_BYTES = 60 * 1024 * 1024

F32 = jnp.float32
BF16 = jnp.bfloat16


def _rmsnorm(x, g):
    ms = jnp.mean(x * x, axis=-1, keepdims=True)
    return (x * lax.rsqrt(ms + EPS)) * g


def _dot(a, b):
    return jnp.dot(a, b, preferred_element_type=F32)


def _causal_conv(buf, cols, v, w_ref, halo, step, rows):
    buf[pl.ds(halo, rows), cols] = v
    y = buf[pl.ds(halo - 2 * step, rows), cols] * w_ref[0:1, cols]
    y = y + buf[pl.ds(halo - step, rows), cols] * w_ref[1:2, cols]
    return y + v * w_ref[2:3, cols]


def _mixer_core(x, w, conv_in, pool_diff):
    xb = _rmsnorm(x, w["g_pre"][...]).astype(BF16)

    def proj(k, j):
        lo = k * D_MODEL + j * CHUNK
        return _dot(xb, w["w_in"][:, lo:lo + CHUNK])

    def branch_proj(j):
        return tuple(proj(k, j) for k in range(4))

    def gate_proj(j):
        return proj(4, j), proj(5, j)

    nxt = branch_proj(0)
    y_a = None
    y_p = []
    for j in range(N_MIX_CHUNKS):
        cols = slice(j * CHUNK, (j + 1) * CHUNK)
        b, c, h, u = nxt
        nxt = branch_proj(j + 1) if j + 1 < N_MIX_CHUNKS else gate_proj(0)
        z = (b * conv_in(j, c * h)).astype(BF16)
        t = _dot(z, w["w_a_out"][cols, :])
        y_a = t if y_a is None else y_a + t
        d = pool_diff(j, u).astype(BF16)
        y_p.append(_dot(d, w["pool_w"][cols, :]) * w["pool_scale"][:, cols])
    mix = None
    for j in range(N_MIX_CHUNKS):
        cols = slice(j * CHUNK, (j + 1) * CHUNK)
        ga, gp = nxt
        if j + 1 < N_MIX_CHUNKS:
            nxt = gate_proj(j + 1)
        merged = jax.nn.sigmoid(ga) * y_a[:, cols] + jax.nn.sigmoid(gp) * y_p[j]
        t = _dot(merged.astype(BF16), w["w_o"][cols, :])
        mix = t if mix is None else mix + t
    return x + _rmsnorm(mix, w["g_post"][...])


def _ffn_core(x, p, w, conv_in):
    hb = _rmsnorm(x, w["g_pre"][...]).astype(BF16)

    def up(j):
        return (_dot(hb, w["w_up"][:, j * CHUNK:(j + 1) * CHUNK]),
                _dot(hb, w["w_up"][:, D_FF + j * CHUNK:D_FF + (j + 1) * CHUNK]))

    nxt = up(0)
    f = None
    for j in range(N_FF_CHUNKS):
        cols = slice(j * CHUNK, (j + 1) * CHUNK)
        a, g = nxt
        if j + 1 < N_FF_CHUNKS:
            nxt = up(j + 1)
        h = jax.nn.gelu(conv_in(j, a), approximate=True) * g
        t = _dot(h.astype(BF16), w["w_down"][cols, :])
        f = t if f is None else f + t
    x = x + _rmsnorm(f, w["g_post"][...])
    gate = jax.nn.sigmoid(_dot(x.astype(BF16), w["w_gate"][...]))
    return x + gate * _dot(p.astype(BF16), w["w_proj"][...])


def _block_copy(hbm, buf, sem, block, slot, seq, steps, *, to_hbm=False):
    hbm_view = hbm.at[seq, pl.ds(block * steps, steps), :]
    buf_view = buf.at[slot, :, seq, :]
    src, dst = (buf_view, hbm_view) if to_hbm else (hbm_view, buf_view)
    return pltpu.make_async_copy(src, dst, sem.at[slot, seq])


def _fetch_block(hbm, buf, sem, steps, n_seq):
    i = pl.program_id(0)
    slot = i % 2

    def start(block, slot):
        for s in range(n_seq):
            _block_copy(hbm, buf, sem, block, slot, s, steps).start()

    @pl.when(i == 0)
    def _():
        start(0, 0)

    @pl.when(i + 1 < pl.num_programs(0))
    def _():
        start(i + 1, 1 - slot)

    for s in range(n_seq):
        _block_copy(hbm, buf, sem, i, slot, s, steps).wait()
    return slot


def _stream_weight_as_bf16(w_hbm, w_vmem, stage, sem):
    n_rows, width = w_hbm.shape
    n_slots, chunk, _ = stage.shape
    n_chunks = n_rows // chunk
    assert n_chunks * chunk == n_rows and width <= stage.shape[2]

    def chunk_copy(c, slot):
        return pltpu.make_async_copy(w_hbm.at[pl.ds(c * chunk, chunk), :], stage.at[slot, :, 0:width], sem.at[slot])

    for c in range(min(n_slots, n_chunks)):
        chunk_copy(c, c).start()

    def body(c, carry):
        slot = c % n_slots
        chunk_copy(c, slot).wait()
        w_vmem[pl.ds(pl.multiple_of(c * chunk, chunk), chunk), :] = stage[slot, :, 0:width].astype(BF16)

        @pl.when(c + n_slots < n_chunks)
        def _():
            chunk_copy(c + n_slots, slot).start()

        return carry

    lax.fori_loop(0, n_chunks, body, 0)


def _load_weights(pairs, exports, stage, stage_sem, export_sem):
    for w_hbm, w_vmem in pairs:
        _stream_weight_as_bf16(w_hbm, w_vmem, stage, stage_sem)
    for k, ((_, w_vmem), w_out) in enumerate(zip(pairs, exports)):
        pltpu.make_async_copy(w_vmem, w_out, export_sem.at[k]).start()


def _finish_weight_exports(pairs, exports, export_sem):
    for k, ((_, w_vmem), w_out) in enumerate(zip(pairs, exports)):
        pltpu.make_async_copy(w_vmem, w_out, export_sem.at[k]).wait()


def _mixer_prompt_kernel(x_hbm, g_pre, w_in_hbm, conv_w, w_a_out_hbm, pool_w_hbm, pool_scale, w_o_hbm, g_post,
                         x1_ref, nconv_ref, npool_ref, w_in_out, w_a_out_out, pool_w_out, w_o_out,
                         xbuf, xsem, cbuf, ubuf, e0, e1, e2, inv_cnt,
                         w_in, w_a_out, pool_w, w_o, stage, stage_sem, export_sem,
                         *, steps, sub_steps, n_seq):
    i = pl.program_id(0)
    weight_pairs = ((w_in_hbm, w_in), (w_a_out_hbm, w_a_out), (pool_w_hbm, pool_w), (w_o_hbm, w_o))
    weight_exports = (w_in_out, w_a_out_out, pool_w_out, w_o_out)
    rows = steps * n_seq
    sub_rows = sub_steps * n_seq
    n_sub = steps // sub_steps
    conv_halo = N_CONV * n_seq
    pool_halo = N_POOL * n_seq
    head = POOL_MAX * n_seq
    ebufs = (e0, e1, e2)

    @pl.when(i == 0)
    def _():
        cbuf[0:conv_halo, :] = jnp.zeros((conv_halo, D_MODEL), F32)
        ubuf[0:pool_halo, :] = jnp.zeros((pool_halo, D_MODEL), F32)
        pos = lax.broadcasted_iota(jnp.int32, (POOL_MAX, n_seq, CHUNK), 0).reshape(head, CHUNK)
        for j, window in enumerate(POOL_WINDOWS):
            inv_cnt[j] = 1.0 / jnp.minimum(pos + 1, window).astype(F32)
        _load_weights(weight_pairs, weight_exports, stage, stage_sem, export_sem)

    @pl.when(i == 1)
    def _():
        for j, window in enumerate(POOL_WINDOWS):
            inv_cnt[j] = jnp.full((head, CHUNK), 1.0 / window, F32)

    slot = _fetch_block(x_hbm, xbuf, xsem, steps, n_seq)
    w = dict(g_pre=g_pre, w_in=w_in, w_a_out=w_a_out, pool_w=pool_w, pool_scale=pool_scale, w_o=w_o,
             g_post=g_post)

    for sb in range(n_sub):
        off = sb * sub_rows
        last_sub = sb == n_sub - 1

        def conv_in(j, v, off=off, last_sub=last_sub):
            cols = slice(j * CHUNK, (j + 1) * CHUNK)
            y = _causal_conv(cbuf, cols, v, conv_w, conv_halo + off, n_seq, sub_rows)
            if last_sub:
                cbuf[0:conv_halo, cols] = cbuf[pl.ds(rows, conv_halo), cols]
            return y

        def pool_diff(j, u, sb=sb, off=off, last_sub=last_sub):
            cols = slice(j * CHUNK, (j + 1) * CHUNK)
            window = POOL_WINDOWS[j]
            base = pool_halo + off
            ubuf[pl.ds(base, sub_rows), cols] = u
            n_stages = j + 1
            tsum = None
            for k in range(n_stages):
                lo = base if sb > 0 else pool_halo - (window - (2 << k)) * n_seq
                n = base + sub_rows - lo
                shift = (1 << k) * n_seq
                if k == 0:
                    tsum = ubuf[pl.ds(lo, n), cols] + ubuf[pl.ds(lo - shift, n), cols]
                else:
                    tsum = ebufs[k - 1][pl.ds(lo, n), cols] + ebufs[k - 1][pl.ds(lo - shift, n), cols]
                if k < n_stages - 1:
                    ebufs[k][pl.ds(lo, n), cols] = tsum
            if last_sub:
                keep = (window - 1) * n_seq
                ubuf[pl.ds(pool_halo - keep, keep), cols] = ubuf[pl.ds(pool_halo + rows - keep, keep), cols]
            if sb > 0:
                return tsum * (1.0 / window) - u
            return jnp.concatenate([tsum[0:head, :] * inv_cnt[j] - u[0:head, :],
                                    tsum[head:, :] * (1.0 / window) - u[head:, :]], axis=0)

        x = xbuf[slot, sb * sub_steps:(sb + 1) * sub_steps].reshape(sub_rows, D_MODEL)
        x1_ref[off:off + sub_rows, :] = _mixer_core(x, w, conv_in, pool_diff)

    @pl.when(i == pl.num_programs(0) - 1)
    def _():
        nconv_ref[...] = cbuf[0:conv_halo, :].reshape(N_CONV, n_seq, D_MODEL)
        npool_ref[...] = ubuf[pl.ds(rows, pool_halo), :].reshape(N_POOL, n_seq, D_MODEL)
        _finish_weight_exports(weight_pairs, weight_exports, export_sem)


def _ffn_prompt_kernel(x_ref, p_hbm, g_pre, w_up_hbm, conv_w, w_down_hbm, g_post, w_proj_hbm, w_gate_hbm,
                       y_hbm, nffn_ref, w_up_out, w_down_out, w_proj_out, w_gate_out,
                       pbuf, psem, ybuf, ysem, fbuf,
                       w_up, w_down, w_proj, w_gate, stage, stage_sem, export_sem,
                       *, steps, sub_steps, n_seq):
    i = pl.program_id(0)
    last = pl.num_programs(0) - 1
    rows = steps * n_seq
    sub_rows = sub_steps * n_seq
    n_sub = steps // sub_steps
    conv_halo = N_CONV * n_seq
    weight_pairs = ((w_up_hbm, w_up), (w_down_hbm, w_down), (w_proj_hbm, w_proj), (w_gate_hbm, w_gate))
    weight_exports = (w_up_out, w_down_out, w_proj_out, w_gate_out)

    @pl.when(i == 0)
    def _():
        fbuf[0:conv_halo, :] = jnp.zeros((conv_halo, D_FF), F32)
        _load_weights(weight_pairs, weight_exports, stage, stage_sem, export_sem)

    def wait_store(block, slot):
        for s in range(n_seq):
            _block_copy(y_hbm, ybuf, ysem, block, slot, s, steps, to_hbm=True).wait()

    slot = _fetch_block(p_hbm, pbuf, psem, steps, n_seq)

    @pl.when(i >= 2)
    def _():
        wait_store(i - 2, slot)

    w = dict(g_pre=g_pre, w_up=w_up, w_down=w_down, g_post=g_post, w_proj=w_proj, w_gate=w_gate)
    for sb in range(n_sub):
        off = sb * sub_rows
        last_sub = sb == n_sub - 1

        def conv_in(j, v, off=off, last_sub=last_sub):
            cols = slice(j * CHUNK, (j + 1) * CHUNK)
            y = _causal_conv(fbuf, cols, v, conv_w, conv_halo + off, n_seq, sub_rows)
            if last_sub:
                fbuf[0:conv_halo, cols] = fbuf[pl.ds(rows, conv_halo), cols]
            return y

        sub = slice(sb * sub_steps, (sb + 1) * sub_steps)
        p = pbuf[slot, sub].reshape(sub_rows, PLE_DIM)
        y = _ffn_core(x_ref[off:off + sub_rows, :], p, w, conv_in)
        ybuf[slot, sub] = y.reshape(sub_steps, n_seq, D_MODEL)

    for s in range(n_seq):
        _block_copy(y_hbm, ybuf, ysem, i, slot, s, steps, to_hbm=True).start()

    @pl.when(i == last)
    def _():
        nffn_ref[...] = fbuf[0:conv_halo, :].reshape(N_CONV, n_seq, D_FF)
        wait_store(i - 1, 1 - slot)
        wait_store(i, slot)
        _finish_weight_exports(weight_pairs, weight_exports, export_sem)


def _resident(shape):
    return pl.BlockSpec(shape, lambda i: (0,) * len(shape), pipeline_mode=pl.Buffered(1))


def _prompt_layer(x, p, wts):
    n_seq, seq_len, _ = x.shape
    steps, sub_steps = TIME_STEPS, SUB_STEPS
    n_blocks = seq_len // steps
    assert n_blocks * steps == seq_len and n_blocks >= 2 and n_seq % 8 == 0
    assert steps % sub_steps == 0 and sub_steps > POOL_MAX
    rows = steps * n_seq
    params = pltpu.CompilerParams(dimension_semantics=("arbitrary",), vmem_limit_bytes=VMEM_LIMIT_BYTES)
    any_space = pl.BlockSpec(memory_space=pl.ANY)
    row_block = pl.BlockSpec((rows, D_MODEL), lambda i: (i, 0))
    state_block = lambda n, width: pl.BlockSpec((n, n_seq, width), lambda i: (0, 0, 0))

    weight_spec = lambda a: any_space if a.dtype == F32 and a.shape[0] >= WEIGHT_CHUNK_ROWS else _resident(a.shape)
    bf16_like = lambda ws: [jax.ShapeDtypeStruct(a.shape, BF16) for a in ws]
    vmem_bf16 = lambda ws: [pltpu.VMEM(a.shape, BF16) for a in ws]
    stage = lambda ws: [pltpu.VMEM((WEIGHT_STAGE_SLOTS, WEIGHT_CHUNK_ROWS, max(a.shape[1] for a in ws)), F32),
                        pltpu.SemaphoreType.DMA((WEIGHT_STAGE_SLOTS,)), pltpu.SemaphoreType.DMA((len(ws),))]

    mixer_w = (wts["g_pre_mix"], wts["w_in"], wts["conv_a_w"], wts["w_a_out"], wts["pool_w"],
               wts["pool_scale"], wts["w_o"], wts["g_post_mix"])
    mixer_mats = (wts["w_in"], wts["w_a_out"], wts["pool_w"], wts["w_o"])
    stage_rows = N_POOL * n_seq + rows
    x1, nconv, npool, *mixer_bf16 = pl.pallas_call(
        functools.partial(_mixer_prompt_kernel, steps=steps, sub_steps=sub_steps, n_seq=n_seq),
        grid=(n_blocks,),
        in_specs=[any_space] + [weight_spec(a) for a in mixer_w],
        out_specs=[row_block, state_block(N_CONV, D_MODEL), state_block(N_POOL, D_MODEL)] + [any_space] * 4,
        out_shape=[jax.ShapeDtypeStruct((n_blocks * rows, D_MODEL), F32),
                   jax.ShapeDtypeStruct((N_CONV, n_seq, D_MODEL), F32),
                   jax.ShapeDtypeStruct((N_POOL, n_seq, D_MODEL), F32)] + bf16_like(mixer_mats),
        scratch_shapes=[pltpu.VMEM((2, steps, n_seq, D_MODEL), F32), pltpu.SemaphoreType.DMA((2, n_seq)),
                        pltpu.VMEM((N_CONV * n_seq + rows, D_MODEL), F32),
                        pltpu.VMEM((stage_rows, D_MODEL), F32),
                        pltpu.VMEM((stage_rows, D_MODEL), F32),
                        pltpu.VMEM((stage_rows, D_MODEL), F32),
                        pltpu.VMEM((stage_rows, D_MODEL), F32),
                        pltpu.VMEM((len(POOL_WINDOWS), POOL_MAX * n_seq, CHUNK), F32)]
                       + vmem_bf16(mixer_mats) + stage(mixer_mats),
        compiler_params=params,
        name="mixer_prompt",
    )(x, *mixer_w)

    ffn_w = (wts["g_pre_ffn"], wts["w_up"], wts["ffn_conv_w"], wts["w_down"], wts["g_post_ffn"],
             wts["w_ple_proj"], wts["w_ple_gate"])
    ffn_mats = (wts["w_up"], wts["w_down"], wts["w_ple_proj"], wts["w_ple_gate"])
    y, nffn, *ffn_bf16 = pl.pallas_call(
        functools.partial(_ffn_prompt_kernel, steps=steps, sub_steps=sub_steps, n_seq=n_seq),
        grid=(n_blocks,),
        in_specs=[row_block, any_space] + [weight_spec(a) for a in ffn_w],
        out_specs=[any_space, state_block(N_CONV, D_FF)] + [any_space] * 4,
        out_shape=[jax.ShapeDtypeStruct((n_seq, seq_len, D_MODEL), F32),
                   jax.ShapeDtypeStruct((N_CONV, n_seq, D_FF), F32)] + bf16_like(ffn_mats),
        scratch_shapes=[pltpu.VMEM((2, steps, n_seq, PLE_DIM), F32), pltpu.SemaphoreType.DMA((2, n_seq)),
                        pltpu.VMEM((2, steps, n_seq, D_MODEL), F32), pltpu.SemaphoreType.DMA((2, n_seq)),
                        pltpu.VMEM((N_CONV * n_seq + rows, D_FF), F32)]
                       + vmem_bf16(ffn_mats) + stage(ffn_mats),
        compiler_params=params,
        name="ffn_prompt",
    )(x1, p, *ffn_w)
    bf16_wts = dict(wts, **dict(zip(("w_in", "w_a_out", "pool_w", "w_o"), mixer_bf16)),
                    **dict(zip(("w_up", "w_down", "w_ple_proj", "w_ple_gate"), ffn_bf16)))
    to_batch_major = lambda s: jnp.transpose(s, (1, 0, 2))[None]
    return y, to_batch_major(nconv), to_batch_major(npool), to_batch_major(nffn), bf16_wts


def _mixer_sample_kernel(x_ref, sconv_ref, spool_ref, g_pre, w_in, conv_w, w_a_out, pool_w, pool_scale, w_o,
                         g_post, x1_ref, nconv_ref, npool_ref, cbuf, ubuf, *, batch, steps, sub_steps):
    sub_rows = sub_steps * batch
    for k in range(N_CONV):
        cbuf[k * batch:(k + 1) * batch, :] = sconv_ref[:, k, :]

    w = dict(g_pre=g_pre, w_in=w_in, w_a_out=w_a_out, pool_w=pool_w, pool_scale=pool_scale, w_o=w_o,
             g_post=g_post)
    for sb in range(steps // sub_steps):
        t0 = sb * sub_steps
        off = t0 * batch

        def conv_in(j, v, off=off):
            cols = slice(j * CHUNK, (j + 1) * CHUNK)
            return _causal_conv(cbuf, cols, v, conv_w, N_CONV * batch + off, batch, sub_rows)

        def pool_diff(j, u, t0=t0):
            cols = slice(j * CHUNK, (j + 1) * CHUNK)
            window = POOL_WINDOWS[j]
            ubuf[t0:t0 + sub_steps, :, cols] = u.reshape(sub_steps, batch, CHUNK)

            def ext(e):
                return spool_ref[:, e, cols] if e < N_POOL else ubuf[e - N_POOL, :, cols]

            diffs = []
            for t in range(t0, t0 + sub_steps):
                tsum = ext(N_POOL + t - window + 1)
                for e in range(N_POOL + t - window + 2, N_POOL + t + 1):
                    tsum = tsum + ext(e)
                diffs.append(tsum * (1.0 / window) - ext(N_POOL + t))
            return jnp.concatenate(diffs, axis=0)

        x = jnp.concatenate([x_ref[:, t, :] for t in range(t0, t0 + sub_steps)], axis=0)
        x1_ref[off:off + sub_rows, :] = _mixer_core(x, w, conv_in, pool_diff)

    for k in range(N_CONV):
        nconv_ref[:, k, :] = cbuf[(steps + k) * batch:(steps + k + 1) * batch, :]
    for m in range(N_POOL):
        e = steps + m
        npool_ref[:, m, :] = spool_ref[:, e, :] if e < N_POOL else ubuf[e - N_POOL]


def _ffn_sample_kernel(x_ref, p_ref, sffn_ref, g_pre, w_up, conv_w, w_down, g_post, w_proj, w_gate,
                       y_ref, nffn_ref, fbuf, *, batch, steps, sub_steps):
    sub_rows = sub_steps * batch
    for k in range(N_CONV):
        fbuf[k * batch:(k + 1) * batch, :] = sffn_ref[:, k, :]

    w = dict(g_pre=g_pre, w_up=w_up, w_down=w_down, g_post=g_post, w_proj=w_proj, w_gate=w_gate)
    for sb in range(steps // sub_steps):
        t0 = sb * sub_steps
        off = t0 * batch

        def conv_in(j, v, off=off):
            cols = slice(j * CHUNK, (j + 1) * CHUNK)
            return _causal_conv(fbuf, cols, v, conv_w, N_CONV * batch + off, batch, sub_rows)

        p = jnp.concatenate([p_ref[:, t, :] for t in range(t0, t0 + sub_steps)], axis=0)
        y = _ffn_core(x_ref[off:off + sub_rows, :], p, w, conv_in)
        for t in range(sub_steps):
            y_ref[:, t0 + t, :] = y[t * batch:(t + 1) * batch, :]

    for k in range(N_CONV):
        nffn_ref[:, k, :] = fbuf[(steps + k) * batch:(steps + k + 1) * batch, :]


def _sample_layer(x, p, sconv, spool, sffn, layer, wts):
    batch, steps, _ = x.shape
    sub_steps = max(1, SAMPLE_SUB_ROWS // batch)
    assert steps % sub_steps == 0 and batch % 8 == 0
    rows = batch * steps
    params = pltpu.CompilerParams(vmem_limit_bytes=VMEM_LIMIT_BYTES)
    whole = pl.BlockSpec(memory_space=pltpu.VMEM)
    of_layer = lambda a, k: pl.BlockSpec((None,) + a.shape[1:], lambda: (k, 0, 0, 0))
    one_layer = lambda a: jax.ShapeDtypeStruct((1,) + a.shape[1:], F32)

    mixer_w = (wts["g_pre_mix"], wts["w_in"], wts["conv_a_w"], wts["w_a_out"], wts["pool_w"],
               wts["pool_scale"], wts["w_o"], wts["g_post_mix"])
    x1, nconv, npool = pl.pallas_call(
        functools.partial(_mixer_sample_kernel, batch=batch, steps=steps, sub_steps=sub_steps),
        in_specs=[whole, of_layer(sconv, layer), of_layer(spool, layer)] + [whole] * len(mixer_w),
        out_specs=[whole, of_layer(sconv, 0), of_layer(spool, 0)],
        out_shape=[jax.ShapeDtypeStruct((rows, D_MODEL), F32),
                   one_layer(sconv), one_layer(spool)],
        scratch_shapes=[pltpu.VMEM(((N_CONV + steps) * batch, D_MODEL), F32),
                        pltpu.VMEM((steps, batch, D_MODEL), F32)],
        compiler_params=params,
        name="mixer_sample",
    )(x, sconv, spool, *mixer_w)

    ffn_w = (wts["g_pre_ffn"], wts["w_up"], wts["ffn_conv_w"], wts["w_down"], wts["g_post_ffn"],
             wts["w_ple_proj"], wts["w_ple_gate"])
    y, nffn = pl.pallas_call(
        functools.partial(_ffn_sample_kernel, batch=batch, steps=steps, sub_steps=sub_steps),
        in_specs=[whole, of_layer(p, layer), of_layer(sffn, layer)] + [whole] * len(ffn_w),
        out_specs=[whole, of_layer(sffn, 0)],
        out_shape=[jax.ShapeDtypeStruct(x.shape, F32), one_layer(sffn)],
        scratch_shapes=[pltpu.VMEM(((N_CONV + steps) * batch, D_FF), F32)],
        compiler_params=params,
        name="ffn_sample",
    )(x1, p, sffn, *ffn_w)
    return y, nconv, npool, nffn


def kernel(x_prompt, x_sample, p_prompt, p_sample, state_conv_a, state_pool, state_ffn_conv, g_pre_mix, w_in,
           conv_a_w, w_a_out, pool_w, pool_scale, w_o, g_post_mix, g_pre_ffn, w_up, ffn_conv_w, w_down,
           g_post_ffn, w_ple_proj, w_ple_gate):
    depth = w_in.shape[0]
    xp, xs = x_prompt, x_sample
    prompt_states, sample_states = [], []
    for i in range(depth):
        wts = dict(
            g_pre_mix=g_pre_mix[i][None], g_post_mix=g_post_mix[i][None], g_pre_ffn=g_pre_ffn[i][None],
            g_post_ffn=g_post_ffn[i][None], pool_scale=pool_scale[i][None], conv_a_w=conv_a_w[i],
            ffn_conv_w=ffn_conv_w[i],
            w_in=w_in[i], w_a_out=w_a_out[i], pool_w=pool_w[i].reshape(D_MODEL, CHUNK),
            w_o=w_o[i], w_up=w_up[i], w_down=w_down[i], w_ple_proj=w_ple_proj[i], w_ple_gate=w_ple_gate[i])
        xp, c1, p1, f1, bf16_wts = _prompt_layer(xp, p_prompt[i], wts)
        xs, c2, p2, f2 = _sample_layer(xs, p_sample, state_conv_a, state_pool, state_ffn_conv, i, bf16_wts)
        prompt_states.append((c1, p1, f1))
        sample_states.append((c2, p2, f2))
    cat = lambda states, k: states[0][k] if depth == 1 else jnp.concatenate([s[k] for s in states], axis=0)
    return (xp, xs, cat(prompt_states, 0), cat(prompt_states, 1), cat(prompt_states, 2),
            cat(sample_states, 0), cat(sample_states, 1), cat(sample_states, 2))
```

```python
import functools

import jax
import jax.numpy as jnp
from jax import lax
from jax.experimental import pallas as pl
from jax.experimental.pallas import tpu as pltpu

D_MODEL = 1024
D_FF = 2816
PLE_DIM = 256
CONV_W = 3
POOL_WINDOWS = (2, 4, 8, 16)
POOL_MAX = 16
EPS = 1e-6

N_CONV = CONV_W - 1
N_POOL = POOL_MAX - 1
CHUNK = 256
N_MIX_CHUNKS = D_MODEL // CHUNK
N_FF_CHUNKS = D_FF // CHUNK
TIME_STEPS = 64
SUB_STEPS = 32
SAMPLE_SUB_ROWS = 256
WEIGHT_CHUNK_ROWS = 128
WEIGHT_STAGE_SLOTS = 4
VMEM_LIMIT_BYTES = 60 * 1024 * 1024

F32 = jnp.float32
BF16 = jnp.bfloat16


def _rmsnorm(x, g):
    ms = jnp.mean(x * x, axis=-1, keepdims=True)
    return (x * lax.rsqrt(ms + EPS)) * g


def _dot(a, b):
    return jnp.dot(a, b, preferred_element_type=F32)


def _causal_conv(buf, cols, v, w_ref, halo, step, rows):
    buf[pl.ds(halo, rows), cols] = v
    y = buf[pl.ds(halo - 2 * step, rows), cols] * w_ref[0:1, cols]
    y = y + buf[pl.ds(halo - step, rows), cols] * w_ref[1:2, cols]
    return y + v * w_ref[2:3, cols]


def _mixer_core(x, w, conv_in, pool_diff):
    xb = _rmsnorm(x, w["g_pre"][...]).astype(BF16)

    def proj(k, j):
        lo = k * D_MODEL + j * CHUNK
        return _dot(xb, w["w_in"][:, lo:lo + CHUNK])

    def branch_proj(j):
        return tuple(proj(k, j) for k in range(4))

    def gate_proj(j):
        return proj(4, j), proj(5, j)

    nxt = branch_proj(0)
    y_a = None
    y_p = []
    for j in range(N_MIX_CHUNKS):
        cols = slice(j * CHUNK, (j + 1) * CHUNK)
        b, c, h, u = nxt
        nxt = branch_proj(j + 1) if j + 1 < N_MIX_CHUNKS else gate_proj(0)
        z = (b * conv_in(j, c * h)).astype(BF16)
        t = _dot(z, w["w_a_out"][cols, :])
        y_a = t if y_a is None else y_a + t
        d = pool_diff(j, u).astype(BF16)
        y_p.append(_dot(d, w["pool_w"][cols, :]) * w["pool_scale"][:, cols])
    mix = None
    for j in range(N_MIX_CHUNKS):
        cols = slice(j * CHUNK, (j + 1) * CHUNK)
        ga, gp = nxt
        if j + 1 < N_MIX_CHUNKS:
            nxt = gate_proj(j + 1)
        merged = jax.nn.sigmoid(ga) * y_a[:, cols] + jax.nn.sigmoid(gp) * y_p[j]
        t = _dot(merged.astype(BF16), w["w_o"][cols, :])
        mix = t if mix is None else mix + t
    return x + _rmsnorm(mix, w["g_post"][...])


def _ffn_core(x, p, w, conv_in):
    hb = _rmsnorm(x, w["g_pre"][...]).astype(BF16)

    def up(j):
        return (_dot(hb, w["w_up"][:, j * CHUNK:(j + 1) * CHUNK]),
                _dot(hb, w["w_up"][:, D_FF + j * CHUNK:D_FF + (j + 1) * CHUNK]))

    nxt = up(0)
    f = None
    for j in range(N_FF_CHUNKS):
        cols = slice(j * CHUNK, (j + 1) * CHUNK)
        a, g = nxt
        if j + 1 < N_FF_CHUNKS:
            nxt = up(j + 1)
        h = jax.nn.gelu(conv_in(j, a), approximate=True) * g
        t = _dot(h.astype(BF16), w["w_down"][cols, :])
        f = t if f is None else f + t
    x = x + _rmsnorm(f, w["g_post"][...])
    gate = jax.nn.sigmoid(_dot(x.astype(BF16), w["w_gate"][...]))
    return x + gate * _dot(p.astype(BF16), w["w_proj"][...])


def _block_copy(hbm, buf, sem, block, slot, seq, steps, *, to_hbm=False):
    hbm_view = hbm.at[seq, pl.ds(block * steps, steps), :]
    buf_view = buf.at[slot, :, seq, :]
    src, dst = (buf_view, hbm_view) if to_hbm else (hbm_view, buf_view)
    return pltpu.make_async_copy(src, dst, sem.at[slot, seq])


def _fetch_block(hbm, buf, sem, steps, n_seq):
    i = pl.program_id(0)
    slot = i % 2

    def start(block, slot):
        for s in range(n_seq):
            _block_copy(hbm, buf, sem, block, slot, s, steps).start()

    @pl.when(i == 0)
    def _():
        start(0, 0)

    @pl.when(i + 1 < pl.num_programs(0))
    def _():
        start(i + 1, 1 - slot)

    for s in range(n_seq):
        _block_copy(hbm, buf, sem, i, slot, s, steps).wait()
    return slot


def _stream_weight_as_bf16(w_hbm, w_vmem, stage, sem):
    n_rows, width = w_hbm.shape
    n_slots, chunk, _ = stage.shape
    n_chunks = n_rows // chunk
    assert n_chunks * chunk == n_rows and width <= stage.shape[2]

    def chunk_copy(c, slot):
        return pltpu.make_async_copy(w_hbm.at[pl.ds(c * chunk, chunk), :], stage.at[slot, :, 0:width], sem.at[slot])

    for c in range(min(n_slots, n_chunks)):
        chunk_copy(c, c).start()

    def body(c, carry):
        slot = c % n_slots
        chunk_copy(c, slot).wait()
        w_vmem[pl.ds(pl.multiple_of(c * chunk, chunk), chunk), :] = stage[slot, :, 0:width].astype(BF16)

        @pl.when(c + n_slots < n_chunks)
        def _():
            chunk_copy(c + n_slots, slot).start()

        return carry

    lax.fori_loop(0, n_chunks, body, 0)


def _load_weights(pairs, exports, stage, stage_sem, export_sem):
    for w_hbm, w_vmem in pairs:
        _stream_weight_as_bf16(w_hbm, w_vmem, stage, stage_sem)
    for k, ((_, w_vmem), w_out) in enumerate(zip(pairs, exports)):
        pltpu.make_async_copy(w_vmem, w_out, export_sem.at[k]).start()


def _finish_weight_exports(pairs, exports, export_sem):
    for k, ((_, w_vmem), w_out) in enumerate(zip(pairs, exports)):
        pltpu.make_async_copy(w_vmem, w_out, export_sem.at[k]).wait()


def _mixer_prompt_kernel(x_hbm, g_pre, w_in_hbm, conv_w, w_a_out_hbm, pool_w_hbm, pool_scale, w_o_hbm, g_post,
                         x1_ref, nconv_ref, npool_ref, w_in_out, w_a_out_out, pool_w_out, w_o_out,
                         xbuf, xsem, cbuf, ubuf, e0, e1, e2, inv_cnt,
                         w_in, w_a_out, pool_w, w_o, stage, stage_sem, export_sem,
                         *, steps, sub_steps, n_seq):
    i = pl.program_id(0)
    weight_pairs = ((w_in_hbm, w_in), (w_a_out_hbm, w_a_out), (pool_w_hbm, pool_w), (w_o_hbm, w_o))
    weight_exports = (w_in_out, w_a_out_out, pool_w_out, w_o_out)
    rows = steps * n_seq
    sub_rows = sub_steps * n_seq
    n_sub = steps // sub_steps
    conv_halo = N_CONV * n_seq
    pool_halo = N_POOL * n_seq
    head = POOL_MAX * n_seq
    ebufs = (e0, e1, e2)

    @pl.when(i == 0)
    def _():
        cbuf[0:conv_halo, :] = jnp.zeros((conv_halo, D_MODEL), F32)
        ubuf[0:pool_halo, :] = jnp.zeros((pool_halo, D_MODEL), F32)
        pos = lax.broadcasted_iota(jnp.int32, (POOL_MAX, n_seq, CHUNK), 0).reshape(head, CHUNK)
        for j, window in enumerate(POOL_WINDOWS):
            inv_cnt[j] = 1.0 / jnp.minimum(pos + 1, window).astype(F32)
        _load_weights(weight_pairs, weight_exports, stage, stage_sem, export_sem)

    @pl.when(i == 1)
    def _():
        for j, window in enumerate(POOL_WINDOWS):
            inv_cnt[j] = jnp.full((head, CHUNK), 1.0 / window, F32)

    slot = _fetch_block(x_hbm, xbuf, xsem, steps, n_seq)
    w = dict(g_pre=g_pre, w_in=w_in, w_a_out=w_a_out, pool_w=pool_w, pool_scale=pool_scale, w_o=w_o,
             g_post=g_post)

    for sb in range(n_sub):
        off = sb * sub_rows
        last_sub = sb == n_sub - 1

        def conv_in(j, v, off=off, last_sub=last_sub):
            cols = slice(j * CHUNK, (j + 1) * CHUNK)
            y = _causal_conv(cbuf, cols, v, conv_w, conv_halo + off, n_seq, sub_rows)
            if last_sub:
                cbuf[0:conv_halo, cols] = cbuf[pl.ds(rows, conv_halo), cols]
            return y

        def pool_diff(j, u, sb=sb, off=off, last_sub=last_sub):
            cols = slice(j * CHUNK, (j + 1) * CHUNK)
            window = POOL_WINDOWS[j]
            base = pool_halo + off
            ubuf[pl.ds(base, sub_rows), cols] = u
            n_stages = j + 1
            tsum = None
            for k in range(n_stages):
                lo = base if sb > 0 else pool_halo - (window - (2 << k)) * n_seq
                n = base + sub_rows - lo
                shift = (1 << k) * n_seq
                if k == 0:
                    tsum = ubuf[pl.ds(lo, n), cols] + ubuf[pl.ds(lo - shift, n), cols]
                else:
                    tsum = ebufs[k - 1][pl.ds(lo, n), cols] + ebufs[k - 1][pl.ds(lo - shift, n), cols]
                if k < n_stages - 1:
                    ebufs[k][pl.ds(lo, n), cols] = tsum
            if last_sub:
                keep = (window - 1) * n_seq
                ubuf[pl.ds(pool_halo - keep, keep), cols] = ubuf[pl.ds(pool_halo + rows - keep, keep), cols]
            if sb > 0:
                return tsum * (1.0 / window) - u
            return jnp.concatenate([tsum[0:head, :] * inv_cnt[j] - u[0:head, :],
                                    tsum[head:, :] * (1.0 / window) - u[head:, :]], axis=0)

        x = xbuf[slot, sb * sub_steps:(sb + 1) * sub_steps].reshape(sub_rows, D_MODEL)
        x1_ref[off:off + sub_rows, :] = _mixer_core(x, w, conv_in, pool_diff)

    @pl.when(i == pl.num_programs(0) - 1)
    def _():
        nconv_ref[...] = cbuf[0:conv_halo, :].reshape(N_CONV, n_seq, D_MODEL)
        npool_ref[...] = ubuf[pl.ds(rows, pool_halo), :].reshape(N_POOL, n_seq, D_MODEL)
        _finish_weight_exports(weight_pairs, weight_exports, export_sem)


def _ffn_prompt_kernel(x_ref, p_hbm, g_pre, w_up_hbm, conv_w, w_down_hbm, g_post, w_proj_hbm, w_gate_hbm,
                       y_hbm, nffn_ref, w_up_out, w_down_out, w_proj_out, w_gate_out,
                       pbuf, psem, ybuf, ysem, fbuf,
                       w_up, w_down, w_proj, w_gate, stage, stage_sem, export_sem,
                       *, steps, sub_steps, n_seq):
    i = pl.program_id(0)
    last = pl.num_programs(0) - 1
    rows = steps * n_seq
    sub_rows = sub_steps * n_seq
    n_sub = steps // sub_steps
    conv_halo = N_CONV * n_seq
    weight_pairs = ((w_up_hbm, w_up), (w_down_hbm, w_down), (w_proj_hbm, w_proj), (w_gate_hbm, w_gate))
    weight_exports = (w_up_out, w_down_out, w_proj_out, w_gate_out)

    @pl.when(i == 0)
    def _():
        fbuf[0:conv_halo, :] = jnp.zeros((conv_halo, D_FF), F32)
        _load_weights(weight_pairs, weight_exports, stage, stage_sem, export_sem)

    def wait_store(block, slot):
        for s in range(n_seq):
            _block_copy(y_hbm, ybuf, ysem, block, slot, s, steps, to_hbm=True).wait()

    slot = _fetch_block(p_hbm, pbuf, psem, steps, n_seq)

    @pl.when(i >= 2)
    def _():
        wait_store(i - 2, slot)

    w = dict(g_pre=g_pre, w_up=w_up, w_down=w_down, g_post=g_post, w_proj=w_proj, w_gate=w_gate)
    for sb in range(n_sub):
        off = sb * sub_rows
        last_sub = sb == n_sub - 1

        def conv_in(j, v, off=off, last_sub=last_sub):
            cols = slice(j * CHUNK, (j + 1) * CHUNK)
            y = _causal_conv(fbuf, cols, v, conv_w, conv_halo + off, n_seq, sub_rows)
            if last_sub:
                fbuf[0:conv_halo, cols] = fbuf[pl.ds(rows, conv_halo), cols]
            return y

        sub = slice(sb * sub_steps, (sb + 1) * sub_steps)
        p = pbuf[slot, sub].reshape(sub_rows, PLE_DIM)
        y = _ffn_core(x_ref[off:off + sub_rows, :], p, w, conv_in)
        ybuf[slot, sub] = y.reshape(sub_steps, n_seq, D_MODEL)

    for s in range(n_seq):
        _block_copy(y_hbm, ybuf, ysem, i, slot, s, steps, to_hbm=True).start()

    @pl.when(i == last)
    def _():
        nffn_ref[...] = fbuf[0:conv_halo, :].reshape(N_CONV, n_seq, D_FF)
        wait_store(i - 1, 1 - slot)
        wait_store(i, slot)
        _finish_weight_exports(weight_pairs, weight_exports, export_sem)


def _resident(shape):
    return pl.BlockSpec(shape, lambda i: (0,) * len(shape), pipeline_mode=pl.Buffered(1))


def _prompt_layer(x, p, wts):
    n_seq, seq_len, _ = x.shape
    steps, sub_steps = TIME_STEPS, SUB_STEPS
    n_blocks = seq_len // steps
    assert n_blocks * steps == seq_len and n_blocks >= 2 and n_seq % 8 == 0
    assert steps % sub_steps == 0 and sub_steps > POOL_MAX
    rows = steps * n_seq
    params = pltpu.CompilerParams(dimension_semantics=("arbitrary",), vmem_limit_bytes=VMEM_LIMIT_BYTES)
    any_space = pl.BlockSpec(memory_space=pl.ANY)
    row_block = pl.BlockSpec((rows, D_MODEL), lambda i: (i, 0))
    state_block = lambda n, width: pl.BlockSpec((n, n_seq, width), lambda i: (0, 0, 0))

    weight_spec = lambda a: any_space if a.dtype == F32 and a.shape[0] >= WEIGHT_CHUNK_ROWS else _resident(a.shape)
    bf16_like = lambda ws: [jax.ShapeDtypeStruct(a.shape, BF16) for a in ws]
    vmem_bf16 = lambda ws: [pltpu.VMEM(a.shape, BF16) for a in ws]
    stage = lambda ws: [pltpu.VMEM((WEIGHT_STAGE_SLOTS, WEIGHT_CHUNK_ROWS, max(a.shape[1] for a in ws)), F32),
                        pltpu.SemaphoreType.DMA((WEIGHT_STAGE_SLOTS,)), pltpu.SemaphoreType.DMA((len(ws),))]

    mixer_w = (wts["g_pre_mix"], wts["w_in"], wts["conv_a_w"], wts["w_a_out"], wts["pool_w"],
               wts["pool_scale"], wts["w_o"], wts["g_post_mix"])
    mixer_mats = (wts["w_in"], wts["w_a_out"], wts["pool_w"], wts["w_o"])
    stage_rows = N_POOL * n_seq + rows
    x1, nconv, npool, *mixer_bf16 = pl.pallas_call(
        functools.partial(_mixer_prompt_kernel, steps=steps, sub_steps=sub_steps, n_seq=n_seq),
        grid=(n_blocks,),
        in_specs=[any_space] + [weight_spec(a) for a in mixer_w],
        out_specs=[row_block, state_block(N_CONV, D_MODEL), state_block(N_POOL, D_MODEL)] + [any_space] * 4,
        out_shape=[jax.ShapeDtypeStruct((n_blocks * rows, D_MODEL), F32),
                   jax.ShapeDtypeStruct((N_CONV, n_seq, D_MODEL), F32),
                   jax.ShapeDtypeStruct((N_POOL, n_seq, D_MODEL), F32)] + bf16_like(mixer_mats),
        scratch_shapes=[pltpu.VMEM((2, steps, n_seq, D_MODEL), F32), pltpu.SemaphoreType.DMA((2, n_seq)),
                        pltpu.VMEM((N_CONV * n_seq + rows, D_MODEL), F32),
                        pltpu.VMEM((stage_rows, D_MODEL), F32),
                        pltpu.VMEM((stage_rows, D_MODEL), F32),
                        pltpu.VMEM((stage_rows, D_MODEL), F32),
                        pltpu.VMEM((stage_rows, D_MODEL), F32),
                        pltpu.VMEM((len(POOL_WINDOWS), POOL_MAX * n_seq, CHUNK), F32)]
                       + vmem_bf16(mixer_mats) + stage(mixer_mats),
        compiler_params=params,
        name="mixer_prompt",
    )(x, *mixer_w)

    ffn_w = (wts["g_pre_ffn"], wts["w_up"], wts["ffn_conv_w"], wts["w_down"], wts["g_post_ffn"],
             wts["w_ple_proj"], wts["w_ple_gate"])
    ffn_mats = (wts["w_up"], wts["w_down"], wts["w_ple_proj"], wts["w_ple_gate"])
    y, nffn, *ffn_bf16 = pl.pallas_call(
        functools.partial(_ffn_prompt_kernel, steps=steps, sub_steps=sub_steps, n_seq=n_seq),
        grid=(n_blocks,),
        in_specs=[row_block, any_space] + [weight_spec(a) for a in ffn_w],
        out_specs=[any_space, state_block(N_CONV, D_FF)] + [any_space] * 4,
        out_shape=[jax.ShapeDtypeStruct((n_seq, seq_len, D_MODEL), F32),
                   jax.ShapeDtypeStruct((N_CONV, n_seq, D_FF), F32)] + bf16_like(ffn_mats),
        scratch_shapes=[pltpu.VMEM((2, steps, n_seq, PLE_DIM), F32), pltpu.SemaphoreType.DMA((2, n_seq)),
                        pltpu.VMEM((2, steps, n_seq, D_MODEL), F32), pltpu.SemaphoreType.DMA((2, n_seq)),
                        pltpu.VMEM((N_CONV * n_seq + rows, D_FF), F32)]
                       + vmem_bf16(ffn_mats) + stage(ffn_mats),
        compiler_params=params,
        name="ffn_prompt",
    )(x1, p, *ffn_w)
    bf16_wts = dict(wts, **dict(zip(("w_in", "w_a_out", "pool_w", "w_o"), mixer_bf16)),
                    **dict(zip(("w_up", "w_down", "w_ple_proj", "w_ple_gate"), ffn_bf16)))
    to_batch_major = lambda s: jnp.transpose(s, (1, 0, 2))[None]
    return y, to_batch_major(nconv), to_batch_major(npool), to_batch_major(nffn), bf16_wts


def _mixer_sample_kernel(x_ref, sconv_ref, spool_ref, g_pre, w_in, conv_w, w_a_out, pool_w, pool_scale, w_o,
                         g_post, x1_ref, nconv_ref, npool_ref, cbuf, ubuf, *, batch, steps, sub_steps):
    sub_rows = sub_steps * batch
    for k in range(N_CONV):
        cbuf[k * batch:(k + 1) * batch, :] = sconv_ref[:, k, :]

    w = dict(g_pre=g_pre, w_in=w_in, w_a_out=w_a_out, pool_w=pool_w, pool_scale=pool_scale, w_o=w_o,
             g_post=g_post)
    for sb in range(steps // sub_steps):
        t0 = sb * sub_steps
        off = t0 * batch

        def conv_in(j, v, off=off):
            cols = slice(j * CHUNK, (j + 1) * CHUNK)
            return _causal_conv(cbuf, cols, v, conv_w, N_CONV * batch + off, batch, sub_rows)

        def pool_diff(j, u, t0=t0):
            cols = slice(j * CHUNK, (j + 1) * CHUNK)
            window = POOL_WINDOWS[j]
            ubuf[t0:t0 + sub_steps, :, cols] = u.reshape(sub_steps, batch, CHUNK)

            def ext(e):
                return spool_ref[e, :, cols] if e < N_POOL else ubuf[e - N_POOL, :, cols]

            diffs = []
            for t in range(t0, t0 + sub_steps):
                tsum = ext(N_POOL + t - window + 1)
                for e in range(N_POOL + t - window + 2, N_POOL + t + 1):
                    tsum = tsum + ext(e)
                diffs.append(tsum * (1.0 / window) - ext(N_POOL + t))
            return jnp.concatenate(diffs, axis=0)

        x = jnp.concatenate([x_ref[:, t, :] for t in range(t0, t0 + sub_steps)], axis=0)
        x1_ref[off:off + sub_rows, :] = _mixer_core(x, w, conv_in, pool_diff)

    for k in range(N_CONV):
        nconv_ref[:, k, :] = cbuf[(steps + k) * batch:(steps + k + 1) * batch, :]
    for m in range(N_POOL):
        e = steps + m
        npool_ref[m] = spool_ref[e] if e < N_POOL else ubuf[e - N_POOL]


def _ffn_sample_kernel(x_ref, p_ref, sffn_ref, g_pre, w_up, conv_w, w_down, g_post, w_proj, w_gate,
                       y_ref, nffn_ref, fbuf, *, batch, steps, sub_steps):
    sub_rows = sub_steps * batch
    for k in range(N_CONV):
        fbuf[k * batch:(k + 1) * batch, :] = sffn_ref[:, k, :]

    w = dict(g_pre=g_pre, w_up=w_up, w_down=w_down, g_post=g_post, w_proj=w_proj, w_gate=w_gate)
    for sb in range(steps // sub_steps):
        t0 = sb * sub_steps
        off = t0 * batch

        def conv_in(j, v, off=off):
            cols = slice(j * CHUNK, (j + 1) * CHUNK)
            return _causal_conv(fbuf, cols, v, conv_w, N_CONV * batch + off, batch, sub_rows)

        p = jnp.concatenate([p_ref[:, t, :] for t in range(t0, t0 + sub_steps)], axis=0)
        y = _ffn_core(x_ref[off:off + sub_rows, :], p, w, conv_in)
        for t in range(sub_steps):
            y_ref[:, t0 + t, :] = y[t * batch:(t + 1) * batch, :]

    for k in range(N_CONV):
        nffn_ref[:, k, :] = fbuf[(steps + k) * batch:(steps + k + 1) * batch, :]


def _sample_layer(x, p, sconv, spool, sffn, layer, wts):
    batch, steps, _ = x.shape
    sub_steps = max(1, SAMPLE_SUB_ROWS // batch)
    assert steps % sub_steps == 0 and batch % 8 == 0
    rows = batch * steps
    params = pltpu.CompilerParams(vmem_limit_bytes=VMEM_LIMIT_BYTES)
    whole = pl.BlockSpec(memory_space=pltpu.VMEM)
    of_layer = lambda a, k: pl.BlockSpec((None,) + a.shape[1:], lambda: (k, 0, 0, 0))
    one_layer = lambda a: jax.ShapeDtypeStruct((1,) + a.shape[1:], F32)

    mixer_w = (wts["g_pre_mix"], wts["w_in"], wts["conv_a_w"], wts["w_a_out"], wts["pool_w"],
               wts["pool_scale"], wts["w_o"], wts["g_post_mix"])
    x1, nconv, npool = pl.pallas_call(
        functools.partial(_mixer_sample_kernel, batch=batch, steps=steps, sub_steps=sub_steps),
        in_specs=[whole, of_layer(sconv, layer), of_layer(spool, layer)] + [whole] * len(mixer_w),
        out_specs=[whole, of_layer(sconv, 0), of_layer(spool, 0)],
        out_shape=[jax.ShapeDtypeStruct((rows, D_MODEL), F32),
                   one_layer(sconv), one_layer(spool)],
        scratch_shapes=[pltpu.VMEM(((N_CONV + steps) * batch, D_MODEL), F32),
                        pltpu.VMEM((steps, batch, D_MODEL), F32)],
        compiler_params=params,
        name="mixer_sample",
    )(x, sconv, spool, *mixer_w)

    ffn_w = (wts["g_pre_ffn"], wts["w_up"], wts["ffn_conv_w"], wts["w_down"], wts["g_post_ffn"],
             wts["w_ple_proj"], wts["w_ple_gate"])
    y, nffn = pl.pallas_call(
        functools.partial(_ffn_sample_kernel, batch=batch, steps=steps, sub_steps=sub_steps),
        in_specs=[whole, of_layer(p, layer), of_layer(sffn, layer)] + [whole] * len(ffn_w),
        out_specs=[whole, of_layer(sffn, 0)],
        out_shape=[jax.ShapeDtypeStruct(x.shape, F32), one_layer(sffn)],
        scratch_shapes=[pltpu.VMEM(((N_CONV + steps) * batch, D_FF), F32)],
        compiler_params=params,
        name="ffn_sample",
    )(x1, p, sffn, *ffn_w)
    return y, nconv, npool, nffn


def kernel(x_prompt, x_sample, p_prompt, p_sample, state_conv_a, state_pool, state_ffn_conv, g_pre_mix, w_in,
           conv_a_w, w_a_out, pool_w, pool_scale, w_o, g_post_mix, g_pre_ffn, w_up, ffn_conv_w, w_down,
           g_post_ffn, w_ple_proj, w_ple_gate):
    depth = w_in.shape[0]
    xp, xs = x_prompt, x_sample
    prompt_states, sample_states = [], []
    time_major = lambda s: jnp.transpose(s, (0, 2, 1, 3))
    state_pool_tm = time_major(state_pool)
    for i in range(depth):
        wts = dict(
            g_pre_mix=g_pre_mix[i][None], g_post_mix=g_post_mix[i][None], g_pre_ffn=g_pre_ffn[i][None],
            g_post_ffn=g_post_ffn[i][None], pool_scale=pool_scale[i][None], conv_a_w=conv_a_w[i],
            ffn_conv_w=ffn_conv_w[i],
            w_in=w_in[i], w_a_out=w_a_out[i], pool_w=pool_w[i].reshape(D_MODEL, CHUNK),
            w_o=w_o[i], w_up=w_up[i], w_down=w_down[i], w_ple_proj=w_ple_proj[i], w_ple_gate=w_ple_gate[i])
        xp, c1, p1, f1, bf16_wts = _prompt_layer(xp, p_prompt[i], wts)
        xs, c2, p2, f2 = _sample_layer(xs, p_sample, state_conv_a, state_pool_tm, state_ffn_conv, i, bf16_wts)
        prompt_states.append((c1, p1, f1))
        sample_states.append((c2, time_major(p2), f2))
    cat = lambda states, k: states[0][k] if depth == 1 else jnp.concatenate([s[k] for s in states], axis=0)
    return (xp, xs, cat(prompt_states, 0), cat(prompt_states, 1), cat(prompt_states, 2),
            cat(sample_states, 0), cat(sample_states, 1), cat(sample_states, 2))
```

```python
import functools

import jax
import jax.numpy as jnp
from jax import lax
from jax.experimental import pallas as pl
from jax.experimental.pallas import tpu as pltpu

D_MODEL = 1024
D_FF = 2816
PLE_DIM = 256
CONV_W = 3
POOL_WINDOWS = (2, 4, 8, 16)
POOL_MAX = 16
EPS = 1e-6

N_CONV = CONV_W - 1
N_POOL = POOL_MAX - 1
CHUNK = 256
N_MIX_CHUNKS = D_MODEL // CHUNK
N_FF_CHUNKS = D_FF // CHUNK
TIME_STEPS = 128
SUB_STEPS = 32
SAMPLE_SUB_ROWS = 256
WEIGHT_CHUNK_ROWS = 128
WEIGHT_STAGE_SLOTS = 4
VMEM_LIMIT_BYTES = 60 * 1024 * 1024

F32 = jnp.float32
BF16 = jnp.bfloat16


def _rmsnorm(x, g):
    ms = jnp.mean(x * x, axis=-1, keepdims=True)
    return (x * lax.rsqrt(ms + EPS)) * g


def _dot(a, b):
    return jnp.dot(a, b, preferred_element_type=F32)


def _causal_conv(buf, cols, v, w_ref, halo, step, rows):
    buf[pl.ds(halo, rows), cols] = v
    y = buf[pl.ds(halo - 2 * step, rows), cols] * w_ref[0:1, cols]
    y = y + buf[pl.ds(halo - step, rows), cols] * w_ref[1:2, cols]
    return y + v * w_ref[2:3, cols]


def _mixer_core(x, w, conv_in, pool_diff):
    xb = _rmsnorm(x, w["g_pre"][...]).astype(BF16)

    def proj(k, j):
        lo = k * D_MODEL + j * CHUNK
        return _dot(xb, w["w_in"][:, lo:lo + CHUNK])

    def branch_proj(j):
        return tuple(proj(k, j) for k in range(4))

    def gate_proj(j):
        return proj(4, j), proj(5, j)

    nxt = branch_proj(0)
    y_a = None
    y_p = []
    for j in range(N_MIX_CHUNKS):
        cols = slice(j * CHUNK, (j + 1) * CHUNK)
        b, c, h, u = nxt
        nxt = branch_proj(j + 1) if j + 1 < N_MIX_CHUNKS else gate_proj(0)
        z = (b * conv_in(j, c * h)).astype(BF16)
        t = _dot(z, w["w_a_out"][cols, :])
        y_a = t if y_a is None else y_a + t
        d = pool_diff(j, u).astype(BF16)
        y_p.append(_dot(d, w["pool_w"][cols, :]) * w["pool_scale"][:, cols])
    mix = None
    for j in range(N_MIX_CHUNKS):
        cols = slice(j * CHUNK, (j + 1) * CHUNK)
        ga, gp = nxt
        if j + 1 < N_MIX_CHUNKS:
            nxt = gate_proj(j + 1)
        merged = jax.nn.sigmoid(ga) * y_a[:, cols] + jax.nn.sigmoid(gp) * y_p[j]
        t = _dot(merged.astype(BF16), w["w_o"][cols, :])
        mix = t if mix is None else mix + t
    return x + _rmsnorm(mix, w["g_post"][...])


def _ffn_core(x, p, w, conv_in):
    hb = _rmsnorm(x, w["g_pre"][...]).astype(BF16)

    def up(j):
        return (_dot(hb, w["w_up"][:, j * CHUNK:(j + 1) * CHUNK]),
                _dot(hb, w["w_up"][:, D_FF + j * CHUNK:D_FF + (j + 1) * CHUNK]))

    nxt = up(0)
    f = None
    for j in range(N_FF_CHUNKS):
        cols = slice(j * CHUNK, (j + 1) * CHUNK)
        a, g = nxt
        if j + 1 < N_FF_CHUNKS:
            nxt = up(j + 1)
        h = jax.nn.gelu(conv_in(j, a), approximate=True) * g
        t = _dot(h.astype(BF16), w["w_down"][cols, :])
        f = t if f is None else f + t
    x = x + _rmsnorm(f, w["g_post"][...])
    gate = jax.nn.sigmoid(_dot(x.astype(BF16), w["w_gate"][...]))
    return x + gate * _dot(p.astype(BF16), w["w_proj"][...])


def _block_copy(hbm, buf, sem, block, slot, seq, steps, *, to_hbm=False):
    hbm_view = hbm.at[seq, pl.ds(block * steps, steps), :]
    buf_view = buf.at[slot, :, seq, :]
    src, dst = (buf_view, hbm_view) if to_hbm else (hbm_view, buf_view)
    return pltpu.make_async_copy(src, dst, sem.at[slot, seq])


def _fetch_block(hbm, buf, sem, steps, n_seq):
    i = pl.program_id(0)
    slot = i % 2

    def start(block, slot):
        for s in range(n_seq):
            _block_copy(hbm, buf, sem, block, slot, s, steps).start()

    @pl.when(i == 0)
    def _():
        start(0, 0)

    @pl.when(i + 1 < pl.num_programs(0))
    def _():
        start(i + 1, 1 - slot)

    for s in range(n_seq):
        _block_copy(hbm, buf, sem, i, slot, s, steps).wait()
    return slot


def _stream_weight_as_bf16(w_hbm, w_vmem, stage, sem):
    n_rows, width = w_hbm.shape
    n_slots, chunk, _ = stage.shape
    n_chunks = n_rows // chunk
    assert n_chunks * chunk == n_rows and width <= stage.shape[2]

    def chunk_copy(c, slot):
        return pltpu.make_async_copy(w_hbm.at[pl.ds(c * chunk, chunk), :], stage.at[slot, :, 0:width], sem.at[slot])

    for c in range(min(n_slots, n_chunks)):
        chunk_copy(c, c).start()

    def body(c, carry):
        slot = c % n_slots
        chunk_copy(c, slot).wait()
        w_vmem[pl.ds(pl.multiple_of(c * chunk, chunk), chunk), :] = stage[slot, :, 0:width].astype(BF16)

        @pl.when(c + n_slots < n_chunks)
        def _():
            chunk_copy(c + n_slots, slot).start()

        return carry

    lax.fori_loop(0, n_chunks, body, 0)


def _load_weights(pairs, exports, stage, stage_sem, export_sem):
    for w_hbm, w_vmem in pairs:
        _stream_weight_as_bf16(w_hbm, w_vmem, stage, stage_sem)
    for k, ((_, w_vmem), w_out) in enumerate(zip(pairs, exports)):
        pltpu.make_async_copy(w_vmem, w_out, export_sem.at[k]).start()


def _finish_weight_exports(pairs, exports, export_sem):
    for k, ((_, w_vmem), w_out) in enumerate(zip(pairs, exports)):
        pltpu.make_async_copy(w_vmem, w_out, export_sem.at[k]).wait()


def _mixer_prompt_kernel(x_hbm, g_pre, w_in_hbm, conv_w, w_a_out_hbm, pool_w_hbm, pool_scale, w_o_hbm, g_post,
                         x1_ref, nconv_ref, npool_ref, w_in_out, w_a_out_out, pool_w_out, w_o_out,
                         xbuf, xsem, cbuf, ubuf, e0, e1, e2, inv_cnt,
                         w_in, w_a_out, pool_w, w_o, stage, stage_sem, export_sem,
                         *, steps, sub_steps, n_seq):
    i = pl.program_id(0)
    weight_pairs = ((w_in_hbm, w_in), (w_a_out_hbm, w_a_out), (pool_w_hbm, pool_w), (w_o_hbm, w_o))
    weight_exports = (w_in_out, w_a_out_out, pool_w_out, w_o_out)
    sub_rows = sub_steps * n_seq
    n_sub = steps // sub_steps
    conv_halo = N_CONV * n_seq
    pool_halo = N_POOL * n_seq
    head = POOL_MAX * n_seq
    ebufs = (e0, e1, e2)

    @pl.when(i == 0)
    def _():
        cbuf[0:conv_halo, :] = jnp.zeros((conv_halo, D_MODEL), F32)
        ubuf[0:pool_halo, :] = jnp.zeros((pool_halo, D_MODEL), F32)
        pos = lax.broadcasted_iota(jnp.int32, (POOL_MAX, n_seq, CHUNK), 0).reshape(head, CHUNK)
        for j, window in enumerate(POOL_WINDOWS):
            inv_cnt[j] = 1.0 / jnp.minimum(pos + 1, window).astype(F32)
        _load_weights(weight_pairs, weight_exports, stage, stage_sem, export_sem)

    @pl.when(i == 1)
    def _():
        for j, window in enumerate(POOL_WINDOWS):
            inv_cnt[j] = jnp.full((head, CHUNK), 1.0 / window, F32)

    slot = _fetch_block(x_hbm, xbuf, xsem, steps, n_seq)
    w = dict(g_pre=g_pre, w_in=w_in, w_a_out=w_a_out, pool_w=pool_w, pool_scale=pool_scale, w_o=w_o,
             g_post=g_post)

    def conv_in(j, v):
        cols = slice(j * CHUNK, (j + 1) * CHUNK)
        y = _causal_conv(cbuf, cols, v, conv_w, conv_halo, n_seq, sub_rows)
        cbuf[0:conv_halo, cols] = cbuf[pl.ds(sub_rows, conv_halo), cols]
        return y

    def pool_diff(j, u, first_sub):
        cols = slice(j * CHUNK, (j + 1) * CHUNK)
        window = POOL_WINDOWS[j]
        ubuf[pl.ds(pool_halo, sub_rows), cols] = u
        n_stages = j + 1
        tsum = None
        for k in range(n_stages):
            lo = pool_halo - (window - (2 << k)) * n_seq
            n = pool_halo + sub_rows - lo
            shift = (1 << k) * n_seq
            if k == 0:
                tsum = ubuf[pl.ds(lo, n), cols] + ubuf[pl.ds(lo - shift, n), cols]
            else:
                tsum = ebufs[k - 1][pl.ds(lo, n), cols] + ebufs[k - 1][pl.ds(lo - shift, n), cols]
            if k < n_stages - 1:
                ebufs[k][pl.ds(lo, n), cols] = tsum
        keep = (window - 1) * n_seq
        ubuf[pl.ds(pool_halo - keep, keep), cols] = ubuf[pl.ds(pool_halo + sub_rows - keep, keep), cols]
        if not first_sub:
            return tsum * (1.0 / window) - u
        return jnp.concatenate([tsum[0:head, :] * inv_cnt[j] - u[0:head, :],
                                tsum[head:, :] * (1.0 / window) - u[head:, :]], axis=0)

    for sb in range(n_sub):
        x = xbuf[slot, sb * sub_steps:(sb + 1) * sub_steps].reshape(sub_rows, D_MODEL)
        x1_ref[sb * sub_rows:(sb + 1) * sub_rows, :] = _mixer_core(
            x, w, conv_in, functools.partial(pool_diff, first_sub=sb == 0))

    @pl.when(i == pl.num_programs(0) - 1)
    def _():
        nconv_ref[...] = cbuf[0:conv_halo, :].reshape(N_CONV, n_seq, D_MODEL)
        npool_ref[...] = ubuf[pl.ds(sub_rows, pool_halo), :].reshape(N_POOL, n_seq, D_MODEL)
        _finish_weight_exports(weight_pairs, weight_exports, export_sem)


def _ffn_prompt_kernel(x_ref, p_hbm, g_pre, w_up_hbm, conv_w, w_down_hbm, g_post, w_proj_hbm, w_gate_hbm,
                       y_hbm, nffn_ref, w_up_out, w_down_out, w_proj_out, w_gate_out,
                       pbuf, psem, ybuf, ysem, fbuf,
                       w_up, w_down, w_proj, w_gate, stage, stage_sem, export_sem,
                       *, steps, sub_steps, n_seq):
    i = pl.program_id(0)
    last = pl.num_programs(0) - 1
    sub_rows = sub_steps * n_seq
    n_sub = steps // sub_steps
    conv_halo = N_CONV * n_seq
    weight_pairs = ((w_up_hbm, w_up), (w_down_hbm, w_down), (w_proj_hbm, w_proj), (w_gate_hbm, w_gate))
    weight_exports = (w_up_out, w_down_out, w_proj_out, w_gate_out)

    @pl.when(i == 0)
    def _():
        fbuf[0:conv_halo, :] = jnp.zeros((conv_halo, D_FF), F32)
        _load_weights(weight_pairs, weight_exports, stage, stage_sem, export_sem)

    def wait_store(block, slot):
        for s in range(n_seq):
            _block_copy(y_hbm, ybuf, ysem, block, slot, s, steps, to_hbm=True).wait()

    slot = _fetch_block(p_hbm, pbuf, psem, steps, n_seq)

    @pl.when(i >= 2)
    def _():
        wait_store(i - 2, slot)

    w = dict(g_pre=g_pre, w_up=w_up, w_down=w_down, g_post=g_post, w_proj=w_proj, w_gate=w_gate)

    def conv_in(j, v):
        cols = slice(j * CHUNK, (j + 1) * CHUNK)
        y = _causal_conv(fbuf, cols, v, conv_w, conv_halo, n_seq, sub_rows)
        fbuf[0:conv_halo, cols] = fbuf[pl.ds(sub_rows, conv_halo), cols]
        return y

    for sb in range(n_sub):
        sub = slice(sb * sub_steps, (sb + 1) * sub_steps)
        p = pbuf[slot, sub].reshape(sub_rows, PLE_DIM)
        y = _ffn_core(x_ref[sb * sub_rows:(sb + 1) * sub_rows, :], p, w, conv_in)
        ybuf[slot, sub] = y.reshape(sub_steps, n_seq, D_MODEL)

    for s in range(n_seq):
        _block_copy(y_hbm, ybuf, ysem, i, slot, s, steps, to_hbm=True).start()

    @pl.when(i == last)
    def _():
        nffn_ref[...] = fbuf[0:conv_halo, :].reshape(N_CONV, n_seq, D_FF)
        wait_store(i - 1, 1 - slot)
        wait_store(i, slot)
        _finish_weight_exports(weight_pairs, weight_exports, export_sem)


def _resident(shape):
    return pl.BlockSpec(shape, lambda i: (0,) * len(shape), pipeline_mode=pl.Buffered(1))


def _prompt_layer(x, p, wts):
    n_seq, seq_len, _ = x.shape
    steps, sub_steps = TIME_STEPS, SUB_STEPS
    n_blocks = seq_len // steps
    assert n_blocks * steps == seq_len and n_blocks >= 2 and n_seq % 8 == 0
    assert steps % sub_steps == 0 and sub_steps > POOL_MAX
    rows = steps * n_seq
    params = pltpu.CompilerParams(dimension_semantics=("arbitrary",), vmem_limit_bytes=VMEM_LIMIT_BYTES)
    any_space = pl.BlockSpec(memory_space=pl.ANY)
    row_block = pl.BlockSpec((rows, D_MODEL), lambda i: (i, 0))
    state_block = lambda n, width: pl.BlockSpec((n, n_seq, width), lambda i: (0, 0, 0))

    weight_spec = lambda a: any_space if a.dtype == F32 and a.shape[0] >= WEIGHT_CHUNK_ROWS else _resident(a.shape)
    bf16_like = lambda ws: [jax.ShapeDtypeStruct(a.shape, BF16) for a in ws]
    vmem_bf16 = lambda ws: [pltpu.VMEM(a.shape, BF16) for a in ws]
    stage = lambda ws: [pltpu.VMEM((WEIGHT_STAGE_SLOTS, WEIGHT_CHUNK_ROWS, max(a.shape[1] for a in ws)), F32),
                        pltpu.SemaphoreType.DMA((WEIGHT_STAGE_SLOTS,)), pltpu.SemaphoreType.DMA((len(ws),))]

    mixer_w = (wts["g_pre_mix"], wts["w_in"], wts["conv_a_w"], wts["w_a_out"], wts["pool_w"],
               wts["pool_scale"], wts["w_o"], wts["g_post_mix"])
    mixer_mats = (wts["w_in"], wts["w_a_out"], wts["pool_w"], wts["w_o"])
    sub_rows = sub_steps * n_seq
    stage_rows = N_POOL * n_seq + sub_rows
    x1, nconv, npool, *mixer_bf16 = pl.pallas_call(
        functools.partial(_mixer_prompt_kernel, steps=steps, sub_steps=sub_steps, n_seq=n_seq),
        grid=(n_blocks,),
        in_specs=[any_space] + [weight_spec(a) for a in mixer_w],
        out_specs=[row_block, state_block(N_CONV, D_MODEL), state_block(N_POOL, D_MODEL)] + [any_space] * 4,
        out_shape=[jax.ShapeDtypeStruct((n_blocks * rows, D_MODEL), F32),
                   jax.ShapeDtypeStruct((N_CONV, n_seq, D_MODEL), F32),
                   jax.ShapeDtypeStruct((N_POOL, n_seq, D_MODEL), F32)] + bf16_like(mixer_mats),
        scratch_shapes=[pltpu.VMEM((2, steps, n_seq, D_MODEL), F32), pltpu.SemaphoreType.DMA((2, n_seq)),
                        pltpu.VMEM((N_CONV * n_seq + sub_rows, D_MODEL), F32),
                        pltpu.VMEM((stage_rows, D_MODEL), F32),
                        pltpu.VMEM((stage_rows, D_MODEL), F32),
                        pltpu.VMEM((stage_rows, D_MODEL), F32),
                        pltpu.VMEM((stage_rows, D_MODEL), F32),
                        pltpu.VMEM((len(POOL_WINDOWS), POOL_MAX * n_seq, CHUNK), F32)]
                       + vmem_bf16(mixer_mats) + stage(mixer_mats),
        compiler_params=params,
        name="mixer_prompt",
    )(x, *mixer_w)

    ffn_w = (wts["g_pre_ffn"], wts["w_up"], wts["ffn_conv_w"], wts["w_down"], wts["g_post_ffn"],
             wts["w_ple_proj"], wts["w_ple_gate"])
    ffn_mats = (wts["w_up"], wts["w_down"], wts["w_ple_proj"], wts["w_ple_gate"])
    y, nffn, *ffn_bf16 = pl.pallas_call(
        functools.partial(_ffn_prompt_kernel, steps=steps, sub_steps=sub_steps, n_seq=n_seq),
        grid=(n_blocks,),
        in_specs=[row_block, any_space] + [weight_spec(a) for a in ffn_w],
        out_specs=[any_space, state_block(N_CONV, D_FF)] + [any_space] * 4,
        out_shape=[jax.ShapeDtypeStruct((n_seq, seq_len, D_MODEL), F32),
                   jax.ShapeDtypeStruct((N_CONV, n_seq, D_FF), F32)] + bf16_like(ffn_mats),
        scratch_shapes=[pltpu.VMEM((2, steps, n_seq, PLE_DIM), F32), pltpu.SemaphoreType.DMA((2, n_seq)),
                        pltpu.VMEM((2, steps, n_seq, D_MODEL), F32), pltpu.SemaphoreType.DMA((2, n_seq)),
                        pltpu.VMEM((N_CONV * n_seq + sub_rows, D_FF), F32)]
                       + vmem_bf16(ffn_mats) + stage(ffn_mats),
        compiler_params=params,
        name="ffn_prompt",
    )(x1, p, *ffn_w)
    bf16_wts = dict(wts, **dict(zip(("w_in", "w_a_out", "pool_w", "w_o"), mixer_bf16)),
                    **dict(zip(("w_up", "w_down", "w_ple_proj", "w_ple_gate"), ffn_bf16)))
    to_batch_major = lambda s: jnp.transpose(s, (1, 0, 2))[None]
    return y, to_batch_major(nconv), to_batch_major(npool), to_batch_major(nffn), bf16_wts


def _mixer_sample_kernel(x_ref, sconv_ref, spool_ref, g_pre, w_in, conv_w, w_a_out, pool_w, pool_scale, w_o,
                         g_post, x1_ref, nconv_ref, npool_ref, cbuf, ubuf, *, batch, steps, sub_steps):
    sub_rows = sub_steps * batch
    for k in range(N_CONV):
        cbuf[k * batch:(k + 1) * batch, :] = sconv_ref[:, k, :]

    w = dict(g_pre=g_pre, w_in=w_in, w_a_out=w_a_out, pool_w=pool_w, pool_scale=pool_scale, w_o=w_o,
             g_post=g_post)
    for sb in range(steps // sub_steps):
        t0 = sb * sub_steps
        off = t0 * batch

        def conv_in(j, v, off=off):
            cols = slice(j * CHUNK, (j + 1) * CHUNK)
            return _causal_conv(cbuf, cols, v, conv_w, N_CONV * batch + off, batch, sub_rows)

        def pool_diff(j, u, t0=t0):
            cols = slice(j * CHUNK, (j + 1) * CHUNK)
            window = POOL_WINDOWS[j]
            ubuf[t0:t0 + sub_steps, :, cols] = u.reshape(sub_steps, batch, CHUNK)

            def ext(e):
                return spool_ref[e, :, cols] if e < N_POOL else ubuf[e - N_POOL, :, cols]

            diffs = []
            for t in range(t0, t0 + sub_steps):
                tsum = ext(N_POOL + t - window + 1)
                for e in range(N_POOL + t - window + 2, N_POOL + t + 1):
                    tsum = tsum + ext(e)
                diffs.append(tsum * (1.0 / window) - ext(N_POOL + t))
            return jnp.concatenate(diffs, axis=0)

        x = jnp.concatenate([x_ref[:, t, :] for t in range(t0, t0 + sub_steps)], axis=0)
        x1_ref[off:off + sub_rows, :] = _mixer_core(x, w, conv_in, pool_diff)

    for k in range(N_CONV):
        nconv_ref[:, k, :] = cbuf[(steps + k) * batch:(steps + k + 1) * batch, :]
    for m in range(N_POOL):
        e = steps + m
        npool_ref[m] = spool_ref[e] if e < N_POOL else ubuf[e - N_POOL]


def _ffn_sample_kernel(x_ref, p_ref, sffn_ref, g_pre, w_up, conv_w, w_down, g_post, w_proj, w_gate,
                       y_ref, nffn_ref, fbuf, *, batch, steps, sub_steps):
    sub_rows = sub_steps * batch
    for k in range(N_CONV):
        fbuf[k * batch:(k + 1) * batch, :] = sffn_ref[:, k, :]

    w = dict(g_pre=g_pre, w_up=w_up, w_down=w_down, g_post=g_post, w_proj=w_proj, w_gate=w_gate)
    for sb in range(steps // sub_steps):
        t0 = sb * sub_steps
        off = t0 * batch

        def conv_in(j, v, off=off):
            cols = slice(j * CHUNK, (j + 1) * CHUNK)
            return _causal_conv(fbuf, cols, v, conv_w, N_CONV * batch + off, batch, sub_rows)

        p = jnp.concatenate([p_ref[:, t, :] for t in range(t0, t0 + sub_steps)], axis=0)
        y = _ffn_core(x_ref[off:off + sub_rows, :], p, w, conv_in)
        for t in range(sub_steps):
            y_ref[:, t0 + t, :] = y[t * batch:(t + 1) * batch, :]

    for k in range(N_CONV):
        nffn_ref[:, k, :] = fbuf[(steps + k) * batch:(steps + k + 1) * batch, :]


def _sample_layer(x, p, sconv, spool, sffn, layer, wts):
    batch, steps, _ = x.shape
    sub_steps = max(1, SAMPLE_SUB_ROWS // batch)
    assert steps % sub_steps == 0 and batch % 8 == 0
    rows = batch * steps
    params = pltpu.CompilerParams(vmem_limit_bytes=VMEM_LIMIT_BYTES)
    whole = pl.BlockSpec(memory_space=pltpu.VMEM)
    of_layer = lambda a, k: pl.BlockSpec((None,) + a.shape[1:], lambda: (k, 0, 0, 0))
    one_layer = lambda a: jax.ShapeDtypeStruct((1,) + a.shape[1:], F32)

    mixer_w = (wts["g_pre_mix"], wts["w_in"], wts["conv_a_w"], wts["w_a_out"], wts["pool_w"],
               wts["pool_scale"], wts["w_o"], wts["g_post_mix"])
    x1, nconv, npool = pl.pallas_call(
        functools.partial(_mixer_sample_kernel, batch=batch, steps=steps, sub_steps=sub_steps),
        in_specs=[whole, of_layer(sconv, layer), of_layer(spool, layer)] + [whole] * len(mixer_w),
        out_specs=[whole, of_layer(sconv, 0), of_layer(spool, 0)],
        out_shape=[jax.ShapeDtypeStruct((rows, D_MODEL), F32),
                   one_layer(sconv), one_layer(spool)],
        scratch_shapes=[pltpu.VMEM(((N_CONV + steps) * batch, D_MODEL), F32),
                        pltpu.VMEM((steps, batch, D_MODEL), F32)],
        compiler_params=params,
        name="mixer_sample",
    )(x, sconv, spool, *mixer_w)

    ffn_w = (wts["g_pre_ffn"], wts["w_up"], wts["ffn_conv_w"], wts["w_down"], wts["g_post_ffn"],
             wts["w_ple_proj"], wts["w_ple_gate"])
    y, nffn = pl.pallas_call(
        functools.partial(_ffn_sample_kernel, batch=batch, steps=steps, sub_steps=sub_steps),
        in_specs=[whole, of_layer(p, layer), of_layer(sffn, layer)] + [whole] * len(ffn_w),
        out_specs=[whole, of_layer(sffn, 0)],
        out_shape=[jax.ShapeDtypeStruct(x.shape, F32), one_layer(sffn)],
        scratch_shapes=[pltpu.VMEM(((N_CONV + steps) * batch, D_FF), F32)],
        compiler_params=params,
        name="ffn_sample",
    )(x1, p, sffn, *ffn_w)
    return y, nconv, npool, nffn


def kernel(x_prompt, x_sample, p_prompt, p_sample, state_conv_a, state_pool, state_ffn_conv, g_pre_mix, w_in,
           conv_a_w, w_a_out, pool_w, pool_scale, w_o, g_post_mix, g_pre_ffn, w_up, ffn_conv_w, w_down,
           g_post_ffn, w_ple_proj, w_ple_gate):
    depth = w_in.shape[0]
    xp, xs = x_prompt, x_sample
    prompt_states, sample_states = [], []
    time_major = lambda s: jnp.transpose(s, (0, 2, 1, 3))
    state_pool_tm = time_major(state_pool)
    for i in range(depth):
        wts = dict(
            g_pre_mix=g_pre_mix[i][None], g_post_mix=g_post_mix[i][None], g_pre_ffn=g_pre_ffn[i][None],
            g_post_ffn=g_post_ffn[i][None], pool_scale=pool_scale[i][None], conv_a_w=conv_a_w[i],
            ffn_conv_w=ffn_conv_w[i],
            w_in=w_in[i], w_a_out=w_a_out[i], pool_w=pool_w[i].reshape(D_MODEL, CHUNK),
            w_o=w_o[i], w_up=w_up[i], w_down=w_down[i], w_ple_proj=w_ple_proj[i], w_ple_gate=w_ple_gate[i])
        xp, c1, p1, f1, bf16_wts = _prompt_layer(xp, p_prompt[i], wts)
        xs, c2, p2, f2 = _sample_layer(xs, p_sample, state_conv_a, state_pool_tm, state_ffn_conv, i, bf16_wts)
        prompt_states.append((c1, p1, f1))
        sample_states.append((c2, time_major(p2), f2))
    cat = lambda states, k: states[0][k] if depth == 1 else jnp.concatenate([s[k] for s in states], axis=0)
    return (xp, xs, cat(prompt_states, 0), cat(prompt_states, 1), cat(prompt_states, 2),
            cat(sample_states, 0), cat(sample_states, 1), cat(sample_states, 2))
```

```python
import functools

import jax
import jax.numpy as jnp
from jax import lax
from jax.experimental import pallas as pl
from jax.experimental.pallas import tpu as pltpu

D_MODEL = 1024
D_FF = 2816
PLE_DIM = 256
CONV_W = 3
POOL_WINDOWS = (2, 4, 8, 16)
POOL_MAX = 16
EPS = 1e-6

N_CONV = CONV_W - 1
N_POOL = POOL_MAX - 1
CHUNK = 256
N_MIX_CHUNKS = D_MODEL // CHUNK
N_FF_CHUNKS = D_FF // CHUNK
TIME_STEPS = 64
SUB_STEPS = 32
SAMPLE_SUB_ROWS = 256
WEIGHT_CHUNK_ROWS = 128
WEIGHT_STAGE_SLOTS = 4
VMEM_LIMIT_BYTES = 60 * 1024 * 1024

F32 = jnp.float32
BF16 = jnp.bfloat16


def _rmsnorm(x, g):
    ms = jnp.mean(x * x, axis=-1, keepdims=True)
    return (x * lax.rsqrt(ms + EPS)) * g


def _dot(a, b):
    return jnp.dot(a, b, preferred_element_type=F32)


def _causal_conv(buf, cols, v, w_ref, halo, step, rows):
    buf[pl.ds(halo, rows), cols] = v
    y = buf[pl.ds(halo - 2 * step, rows), cols] * w_ref[0:1, cols]
    y = y + buf[pl.ds(halo - step, rows), cols] * w_ref[1:2, cols]
    return y + v * w_ref[2:3, cols]


def _mixer_core(x, w, conv_in, pool_diff):
    xb = _rmsnorm(x, w["g_pre"][...]).astype(BF16)

    def proj(k, j):
        lo = k * D_MODEL + j * CHUNK
        return _dot(xb, w["w_in"][:, lo:lo + CHUNK])

    def branch_proj(j):
        return tuple(proj(k, j) for k in range(4))

    def gate_proj(j):
        return proj(4, j), proj(5, j)

    nxt = branch_proj(0)
    y_a = None
    y_p = []
    for j in range(N_MIX_CHUNKS):
        cols = slice(j * CHUNK, (j + 1) * CHUNK)
        b, c, h, u = nxt
        nxt = branch_proj(j + 1) if j + 1 < N_MIX_CHUNKS else gate_proj(0)
        z = (b * conv_in(j, c * h)).astype(BF16)
        t = _dot(z, w["w_a_out"][cols, :])
        y_a = t if y_a is None else y_a + t
        d = pool_diff(j, u).astype(BF16)
        y_p.append(_dot(d, w["pool_w"][cols, :]) * w["pool_scale"][:, cols])
    mix = None
    for j in range(N_MIX_CHUNKS):
        cols = slice(j * CHUNK, (j + 1) * CHUNK)
        ga, gp = nxt
        if j + 1 < N_MIX_CHUNKS:
            nxt = gate_proj(j + 1)
        merged = jax.nn.sigmoid(ga) * y_a[:, cols] + jax.nn.sigmoid(gp) * y_p[j]
        t = _dot(merged.astype(BF16), w["w_o"][cols, :])
        mix = t if mix is None else mix + t
    return x + _rmsnorm(mix, w["g_post"][...])


def _ffn_core(x, p, w, conv_in):
    hb = _rmsnorm(x, w["g_pre"][...]).astype(BF16)

    def up(j):
        return (_dot(hb, w["w_up"][:, j * CHUNK:(j + 1) * CHUNK]),
                _dot(hb, w["w_up"][:, D_FF + j * CHUNK:D_FF + (j + 1) * CHUNK]))

    nxt = up(0)
    f = None
    for j in range(N_FF_CHUNKS):
        cols = slice(j * CHUNK, (j + 1) * CHUNK)
        a, g = nxt
        if j + 1 < N_FF_CHUNKS:
            nxt = up(j + 1)
        h = jax.nn.gelu(conv_in(j, a), approximate=True) * g
        t = _dot(h.astype(BF16), w["w_down"][cols, :])
        f = t if f is None else f + t
    x = x + _rmsnorm(f, w["g_post"][...])
    gate = jax.nn.sigmoid(_dot(x.astype(BF16), w["w_gate"][...]))
    return x + gate * _dot(p.astype(BF16), w["w_proj"][...])


def _block_copy(hbm, buf, sem, block, slot, seq, steps, *, to_hbm=False):
    hbm_view = hbm.at[seq, pl.ds(block * steps, steps), :]
    buf_view = buf.at[slot, :, seq, :]
    src, dst = (buf_view, hbm_view) if to_hbm else (hbm_view, buf_view)
    return pltpu.make_async_copy(src, dst, sem.at[slot, seq])


def _fetch_block(hbm, buf, sem, steps, n_seq):
    i = pl.program_id(0)
    slot = i % 2

    def start(block, slot):
        for s in range(n_seq):
            _block_copy(hbm, buf, sem, block, slot, s, steps).start()

    @pl.when(i == 0)
    def _():
        start(0, 0)

    @pl.when(i + 1 < pl.num_programs(0))
    def _():
        start(i + 1, 1 - slot)

    for s in range(n_seq):
        _block_copy(hbm, buf, sem, i, slot, s, steps).wait()
    return slot


def _stream_weight_as_bf16(w_hbm, w_vmem, stage, sem):
    n_rows, width = w_hbm.shape
    n_slots, chunk, _ = stage.shape
    n_chunks = n_rows // chunk
    assert n_chunks * chunk == n_rows and width <= stage.shape[2]

    def chunk_copy(c, slot):
        return pltpu.make_async_copy(w_hbm.at[pl.ds(c * chunk, chunk), :], stage.at[slot, :, 0:width], sem.at[slot])

    for c in range(min(n_slots, n_chunks)):
        chunk_copy(c, c).start()

    def body(c, carry):
        slot = c % n_slots
        chunk_copy(c, slot).wait()
        w_vmem[pl.ds(pl.multiple_of(c * chunk, chunk), chunk), :] = stage[slot, :, 0:width].astype(BF16)

        @pl.when(c + n_slots < n_chunks)
        def _():
            chunk_copy(c + n_slots, slot).start()

        return carry

    lax.fori_loop(0, n_chunks, body, 0)


def _load_weights(pairs, exports, stage, stage_sem, export_sem):
    for w_hbm, w_vmem in pairs:
        _stream_weight_as_bf16(w_hbm, w_vmem, stage, stage_sem)
    for k, ((_, w_vmem), w_out) in enumerate(zip(pairs, exports)):
        pltpu.make_async_copy(w_vmem, w_out, export_sem.at[k]).start()


def _finish_weight_exports(pairs, exports, export_sem):
    for k, ((_, w_vmem), w_out) in enumerate(zip(pairs, exports)):
        pltpu.make_async_copy(w_vmem, w_out, export_sem.at[k]).wait()


def _mixer_prompt_kernel(x_hbm, g_pre, w_in_hbm, conv_w, w_a_out_hbm, pool_w_hbm, pool_scale, w_o_hbm, g_post,
                         x1_ref, nconv_ref, npool_ref, w_in_out, w_a_out_out, pool_w_out, w_o_out,
                         xbuf, xsem, cbuf, ubuf, e0, e1, e2, inv_cnt,
                         w_in, w_a_out, pool_w, w_o, stage, stage_sem, export_sem,
                         *, steps, sub_steps, n_seq):
    i = pl.program_id(0)
    weight_pairs = ((w_in_hbm, w_in), (w_a_out_hbm, w_a_out), (pool_w_hbm, pool_w), (w_o_hbm, w_o))
    weight_exports = (w_in_out, w_a_out_out, pool_w_out, w_o_out)
    sub_rows = sub_steps * n_seq
    n_sub = steps // sub_steps
    conv_halo = N_CONV * n_seq
    pool_halo = N_POOL * n_seq
    head = POOL_MAX * n_seq
    ebufs = (e0, e1, e2)

    @pl.when(i == 0)
    def _():
        cbuf[0:conv_halo, :] = jnp.zeros((conv_halo, D_MODEL), F32)
        ubuf[0:pool_halo, :] = jnp.zeros((pool_halo, D_MODEL), F32)
        pos = lax.broadcasted_iota(jnp.int32, (POOL_MAX, n_seq, CHUNK), 0).reshape(head, CHUNK)
        for j, window in enumerate(POOL_WINDOWS):
            inv_cnt[j] = 1.0 / jnp.minimum(pos + 1, window).astype(F32)
        _load_weights(weight_pairs, weight_exports, stage, stage_sem, export_sem)

    @pl.when(i == 1)
    def _():
        for j, window in enumerate(POOL_WINDOWS):
            inv_cnt[j] = jnp.full((head, CHUNK), 1.0 / window, F32)

    slot = _fetch_block(x_hbm, xbuf, xsem, steps, n_seq)
    w = dict(g_pre=g_pre, w_in=w_in, w_a_out=w_a_out, pool_w=pool_w, pool_scale=pool_scale, w_o=w_o,
             g_post=g_post)

    def conv_in(j, v):
        cols = slice(j * CHUNK, (j + 1) * CHUNK)
        y = _causal_conv(cbuf, cols, v, conv_w, conv_halo, n_seq, sub_rows)
        cbuf[0:conv_halo, cols] = cbuf[pl.ds(sub_rows, conv_halo), cols]
        return y

    def pool_diff(j, u, first_sub):
        cols = slice(j * CHUNK, (j + 1) * CHUNK)
        window = POOL_WINDOWS[j]
        ubuf[pl.ds(pool_halo, sub_rows), cols] = u
        n_stages = j + 1
        tsum = None
        for k in range(n_stages):
            lo = pool_halo - (window - (2 << k)) * n_seq
            n = pool_halo + sub_rows - lo
            shift = (1 << k) * n_seq
            if k == 0:
                tsum = ubuf[pl.ds(lo, n), cols] + ubuf[pl.ds(lo - shift, n), cols]
            else:
                tsum = ebufs[k - 1][pl.ds(lo, n), cols] + ebufs[k - 1][pl.ds(lo - shift, n), cols]
            if k < n_stages - 1:
                ebufs[k][pl.ds(lo, n), cols] = tsum
        keep = (window - 1) * n_seq
        ubuf[pl.ds(pool_halo - keep, keep), cols] = ubuf[pl.ds(pool_halo + sub_rows - keep, keep), cols]
        if not first_sub:
            return tsum * (1.0 / window) - u
        return jnp.concatenate([tsum[0:head, :] * inv_cnt[j] - u[0:head, :],
                                tsum[head:, :] * (1.0 / window) - u[head:, :]], axis=0)

    for sb in range(n_sub):
        x = xbuf[slot, sb * sub_steps:(sb + 1) * sub_steps].reshape(sub_rows, D_MODEL)
        x1_ref[sb * sub_rows:(sb + 1) * sub_rows, :] = _mixer_core(
            x, w, conv_in, functools.partial(pool_diff, first_sub=sb == 0))

    @pl.when(i == pl.num_programs(0) - 1)
    def _():
        nconv_ref[...] = cbuf[0:conv_halo, :].reshape(N_CONV, n_seq, D_MODEL)
        npool_ref[...] = ubuf[pl.ds(sub_rows, pool_halo), :].reshape(N_POOL, n_seq, D_MODEL)
        _finish_weight_exports(weight_pairs, weight_exports, export_sem)


def _ffn_prompt_kernel(x_ref, p_hbm, g_pre, w_up_hbm, conv_w, w_down_hbm, g_post, w_proj_hbm, w_gate_hbm,
                       y_hbm, nffn_ref, w_up_out, w_down_out, w_proj_out, w_gate_out,
                       pbuf, psem, ybuf, ysem, fbuf,
                       w_up, w_down, w_proj, w_gate, stage, stage_sem, export_sem,
                       *, steps, sub_steps, n_seq):
    i = pl.program_id(0)
    last = pl.num_programs(0) - 1
    sub_rows = sub_steps * n_seq
    n_sub = steps // sub_steps
    conv_halo = N_CONV * n_seq
    weight_pairs = ((w_up_hbm, w_up), (w_down_hbm, w_down), (w_proj_hbm, w_proj), (w_gate_hbm, w_gate))
    weight_exports = (w_up_out, w_down_out, w_proj_out, w_gate_out)

    @pl.when(i == 0)
    def _():
        fbuf[0:conv_halo, :] = jnp.zeros((conv_halo, D_FF), F32)
        _load_weights(weight_pairs, weight_exports, stage, stage_sem, export_sem)

    def wait_store(block, slot):
        for s in range(n_seq):
            _block_copy(y_hbm, ybuf, ysem, block, slot, s, steps, to_hbm=True).wait()

    slot = _fetch_block(p_hbm, pbuf, psem, steps, n_seq)

    @pl.when(i >= 2)
    def _():
        wait_store(i - 2, slot)

    w = dict(g_pre=g_pre, w_up=w_up, w_down=w_down, g_post=g_post, w_proj=w_proj, w_gate=w_gate)

    def conv_in(j, v):
        cols = slice(j * CHUNK, (j + 1) * CHUNK)
        y = _causal_conv(fbuf, cols, v, conv_w, conv_halo, n_seq, sub_rows)
        fbuf[0:conv_halo, cols] = fbuf[pl.ds(sub_rows, conv_halo), cols]
        return y

    for sb in range(n_sub):
        sub = slice(sb * sub_steps, (sb + 1) * sub_steps)
        p = pbuf[slot, sub].reshape(sub_rows, PLE_DIM)
        y = _ffn_core(x_ref[sb * sub_rows:(sb + 1) * sub_rows, :], p, w, conv_in)
        ybuf[slot, sub] = y.reshape(sub_steps, n_seq, D_MODEL)

    for s in range(n_seq):
        _block_copy(y_hbm, ybuf, ysem, i, slot, s, steps, to_hbm=True).start()

    @pl.when(i == last)
    def _():
        nffn_ref[...] = fbuf[0:conv_halo, :].reshape(N_CONV, n_seq, D_FF)
        wait_store(i - 1, 1 - slot)
        wait_store(i, slot)
        _finish_weight_exports(weight_pairs, weight_exports, export_sem)


def _resident(shape):
    return pl.BlockSpec(shape, lambda i: (0,) * len(shape), pipeline_mode=pl.Buffered(1))


def _prompt_layer(x, p, wts):
    n_seq, seq_len, _ = x.shape
    steps, sub_steps = TIME_STEPS, SUB_STEPS
    n_blocks = seq_len // steps
    assert n_blocks * steps == seq_len and n_blocks >= 2 and n_seq % 8 == 0
    assert steps % sub_steps == 0 and sub_steps > POOL_MAX
    rows = steps * n_seq
    params = pltpu.CompilerParams(dimension_semantics=("arbitrary",), vmem_limit_bytes=VMEM_LIMIT_BYTES)
    any_space = pl.BlockSpec(memory_space=pl.ANY)
    row_block = pl.BlockSpec((rows, D_MODEL), lambda i: (i, 0))
    state_block = lambda n, width: pl.BlockSpec((n, n_seq, width), lambda i: (0, 0, 0))

    weight_spec = lambda a: any_space if a.dtype == F32 and a.shape[0] >= WEIGHT_CHUNK_ROWS else _resident(a.shape)
    bf16_like = lambda ws: [jax.ShapeDtypeStruct(a.shape, BF16) for a in ws]
    vmem_bf16 = lambda ws: [pltpu.VMEM(a.shape, BF16) for a in ws]
    stage = lambda ws: [pltpu.VMEM((WEIGHT_STAGE_SLOTS, WEIGHT_CHUNK_ROWS, max(a.shape[1] for a in ws)), F32),
                        pltpu.SemaphoreType.DMA((WEIGHT_STAGE_SLOTS,)), pltpu.SemaphoreType.DMA((len(ws),))]

    mixer_w = (wts["g_pre_mix"], wts["w_in"], wts["conv_a_w"], wts["w_a_out"], wts["pool_w"],
               wts["pool_scale"], wts["w_o"], wts["g_post_mix"])
    mixer_mats = (wts["w_in"], wts["w_a_out"], wts["pool_w"], wts["w_o"])
    sub_rows = sub_steps * n_seq
    stage_rows = N_POOL * n_seq + sub_rows
    x1, nconv, npool, *mixer_bf16 = pl.pallas_call(
        functools.partial(_mixer_prompt_kernel, steps=steps, sub_steps=sub_steps, n_seq=n_seq),
        grid=(n_blocks,),
        in_specs=[any_space] + [weight_spec(a) for a in mixer_w],
        out_specs=[row_block, state_block(N_CONV, D_MODEL), state_block(N_POOL, D_MODEL)] + [any_space] * 4,
        out_shape=[jax.ShapeDtypeStruct((n_blocks * rows, D_MODEL), F32),
                   jax.ShapeDtypeStruct((N_CONV, n_seq, D_MODEL), F32),
                   jax.ShapeDtypeStruct((N_POOL, n_seq, D_MODEL), F32)] + bf16_like(mixer_mats),
        scratch_shapes=[pltpu.VMEM((2, steps, n_seq, D_MODEL), F32), pltpu.SemaphoreType.DMA((2, n_seq)),
                        pltpu.VMEM((N_CONV * n_seq + sub_rows, D_MODEL), F32),
                        pltpu.VMEM((stage_rows, D_MODEL), F32),
                        pltpu.VMEM((stage_rows, D_MODEL), F32),
                        pltpu.VMEM((stage_rows, D_MODEL), F32),
                        pltpu.VMEM((stage_rows, D_MODEL), F32),
                        pltpu.VMEM((len(POOL_WINDOWS), POOL_MAX * n_seq, CHUNK), F32)]
                       + vmem_bf16(mixer_mats) + stage(mixer_mats),
        compiler_params=params,
        name="mixer_prompt",
    )(x, *mixer_w)

    ffn_w = (wts["g_pre_ffn"], wts["w_up"], wts["ffn_conv_w"], wts["w_down"], wts["g_post_ffn"],
             wts["w_ple_proj"], wts["w_ple_gate"])
    ffn_mats = (wts["w_up"], wts["w_down"], wts["w_ple_proj"], wts["w_ple_gate"])
    y, nffn, *ffn_bf16 = pl.pallas_call(
        functools.partial(_ffn_prompt_kernel, steps=steps, sub_steps=sub_steps, n_seq=n_seq),
        grid=(n_blocks,),
        in_specs=[row_block, any_space] + [weight_spec(a) for a in ffn_w],
        out_specs=[any_space, state_block(N_CONV, D_FF)] + [any_space] * 4,
        out_shape=[jax.ShapeDtypeStruct((n_seq, seq_len, D_MODEL), F32),
                   jax.ShapeDtypeStruct((N_CONV, n_seq, D_FF), F32)] + bf16_like(ffn_mats),
        scratch_shapes=[pltpu.VMEM((2, steps, n_seq, PLE_DIM), F32), pltpu.SemaphoreType.DMA((2, n_seq)),
                        pltpu.VMEM((2, steps, n_seq, D_MODEL), F32), pltpu.SemaphoreType.DMA((2, n_seq)),
                        pltpu.VMEM((N_CONV * n_seq + sub_rows, D_FF), F32)]
                       + vmem_bf16(ffn_mats) + stage(ffn_mats),
        compiler_params=params,
        name="ffn_prompt",
    )(x1, p, *ffn_w)
    bf16_wts = dict(wts, **dict(zip(("w_in", "w_a_out", "pool_w", "w_o"), mixer_bf16)),
                    **dict(zip(("w_up", "w_down", "w_ple_proj", "w_ple_gate"), ffn_bf16)))
    to_batch_major = lambda s: jnp.transpose(s, (1, 0, 2))[None]
    return y, to_batch_major(nconv), to_batch_major(npool), to_batch_major(nffn), bf16_wts


def _mixer_sample_kernel(x_ref, sconv_ref, spool_ref, g_pre, w_in, conv_w, w_a_out, pool_w, pool_scale, w_o,
                         g_post, x1_ref, nconv_ref, npool_ref, cbuf, ubuf, *, batch, steps, sub_steps):
    sub_rows = sub_steps * batch
    for k in range(N_CONV):
        cbuf[k * batch:(k + 1) * batch, :] = sconv_ref[:, k, :]

    w = dict(g_pre=g_pre, w_in=w_in, w_a_out=w_a_out, pool_w=pool_w, pool_scale=pool_scale, w_o=w_o,
             g_post=g_post)
    for sb in range(steps // sub_steps):
        t0 = sb * sub_steps
        off = t0 * batch

        def conv_in(j, v, off=off):
            cols = slice(j * CHUNK, (j + 1) * CHUNK)
            return _causal_conv(cbuf, cols, v, conv_w, N_CONV * batch + off, batch, sub_rows)

        def pool_diff(j, u, t0=t0):
            cols = slice(j * CHUNK, (j + 1) * CHUNK)
            window = POOL_WINDOWS[j]
            ubuf[t0:t0 + sub_steps, :, cols] = u.reshape(sub_steps, batch, CHUNK)

            def ext(e):
                return spool_ref[e, :, cols] if e < N_POOL else ubuf[e - N_POOL, :, cols]

            diffs = []
            for t in range(t0, t0 + sub_steps):
                tsum = ext(N_POOL + t - window + 1)
                for e in range(N_POOL + t - window + 2, N_POOL + t + 1):
                    tsum = tsum + ext(e)
                diffs.append(tsum * (1.0 / window) - ext(N_POOL + t))
            return jnp.concatenate(diffs, axis=0)

        x = jnp.concatenate([x_ref[:, t, :] for t in range(t0, t0 + sub_steps)], axis=0)
        x1_ref[off:off + sub_rows, :] = _mixer_core(x, w, conv_in, pool_diff)

    for k in range(N_CONV):
        nconv_ref[:, k, :] = cbuf[(steps + k) * batch:(steps + k + 1) * batch, :]
    for m in range(N_POOL):
        e = steps + m
        npool_ref[m] = spool_ref[e] if e < N_POOL else ubuf[e - N_POOL]


def _ffn_sample_kernel(x_ref, p_ref, sffn_ref, g_pre, w_up, conv_w, w_down, g_post, w_proj, w_gate,
                       y_ref, nffn_ref, fbuf, *, batch, steps, sub_steps):
    sub_rows = sub_steps * batch
    for k in range(N_CONV):
        fbuf[k * batch:(k + 1) * batch, :] = sffn_ref[:, k, :]

    w = dict(g_pre=g_pre, w_up=w_up, w_down=w_down, g_post=g_post, w_proj=w_proj, w_gate=w_gate)
    for sb in range(steps // sub_steps):
        t0 = sb * sub_steps
        off = t0 * batch

        def conv_in(j, v, off=off):
            cols = slice(j * CHUNK, (j + 1) * CHUNK)
            return _causal_conv(fbuf, cols, v, conv_w, N_CONV * batch + off, batch, sub_rows)

        p = jnp.concatenate([p_ref[:, t, :] for t in range(t0, t0 + sub_steps)], axis=0)
        y = _ffn_core(x_ref[off:off + sub_rows, :], p, w, conv_in)
        for t in range(sub_steps):
            y_ref[:, t0 + t, :] = y[t * batch:(t + 1) * batch, :]

    for k in range(N_CONV):
        nffn_ref[:, k, :] = fbuf[(steps + k) * batch:(steps + k + 1) * batch, :]


def _sample_layer(x, p, sconv, spool, sffn, layer, wts):
    batch, steps, _ = x.shape
    sub_steps = max(1, SAMPLE_SUB_ROWS // batch)
    assert steps % sub_steps == 0 and batch % 8 == 0
    rows = batch * steps
    params = pltpu.CompilerParams(vmem_limit_bytes=VMEM_LIMIT_BYTES)
    whole = pl.BlockSpec(memory_space=pltpu.VMEM)
    of_layer = lambda a, k: pl.BlockSpec((None,) + a.shape[1:], lambda: (k, 0, 0, 0))
    one_layer = lambda a: jax.ShapeDtypeStruct((1,) + a.shape[1:], F32)

    mixer_w = (wts["g_pre_mix"], wts["w_in"], wts["conv_a_w"], wts["w_a_out"], wts["pool_w"],
               wts["pool_scale"], wts["w_o"], wts["g_post_mix"])
    x1, nconv, npool = pl.pallas_call(
        functools.partial(_mixer_sample_kernel, batch=batch, steps=steps, sub_steps=sub_steps),
        in_specs=[whole, of_layer(sconv, layer), of_layer(spool, layer)] + [whole] * len(mixer_w),
        out_specs=[whole, of_layer(sconv, 0), of_layer(spool, 0)],
        out_shape=[jax.ShapeDtypeStruct((rows, D_MODEL), F32),
                   one_layer(sconv), one_layer(spool)],
        scratch_shapes=[pltpu.VMEM(((N_CONV + steps) * batch, D_MODEL), F32),
                        pltpu.VMEM((steps, batch, D_MODEL), F32)],
        compiler_params=params,
        name="mixer_sample",
    )(x, sconv, spool, *mixer_w)

    ffn_w = (wts["g_pre_ffn"], wts["w_up"], wts["ffn_conv_w"], wts["w_down"], wts["g_post_ffn"],
             wts["w_ple_proj"], wts["w_ple_gate"])
    y, nffn = pl.pallas_call(
        functools.partial(_ffn_sample_kernel, batch=batch, steps=steps, sub_steps=sub_steps),
        in_specs=[whole, of_layer(p, layer), of_layer(sffn, layer)] + [whole] * len(ffn_w),
        out_specs=[whole, of_layer(sffn, 0)],
        out_shape=[jax.ShapeDtypeStruct(x.shape, F32), one_layer(sffn)],
        scratch_shapes=[pltpu.VMEM(((N_CONV + steps) * batch, D_FF), F32)],
        compiler_params=params,
        name="ffn_sample",
    )(x1, p, sffn, *ffn_w)
    return y, nconv, npool, nffn


def kernel(x_prompt, x_sample, p_prompt, p_sample, state_conv_a, state_pool, state_ffn_conv, g_pre_mix, w_in,
           conv_a_w, w_a_out, pool_w, pool_scale, w_o, g_post_mix, g_pre_ffn, w_up, ffn_conv_w, w_down,
           g_post_ffn, w_ple_proj, w_ple_gate):
    depth = w_in.shape[0]
    xp, xs = x_prompt, x_sample
    prompt_states, sample_states = [], []
    time_major = lambda s: jnp.transpose(s, (0, 2, 1, 3))
    state_pool_tm = time_major(state_pool)
    for i in range(depth):
        wts = dict(
            g_pre_mix=g_pre_mix[i][None], g_post_mix=g_post_mix[i][None], g_pre_ffn=g_pre_ffn[i][None],
            g_post_ffn=g_post_ffn[i][None], pool_scale=pool_scale[i][None], conv_a_w=conv_a_w[i],
            ffn_conv_w=ffn_conv_w[i],
            w_in=w_in[i], w_a_out=w_a_out[i], pool_w=pool_w[i].reshape(D_MODEL, CHUNK),
            w_o=w_o[i], w_up=w_up[i], w_down=w_down[i], w_ple_proj=w_ple_proj[i], w_ple_gate=w_ple_gate[i])
        xp, c1, p1, f1, bf16_wts = _prompt_layer(xp, p_prompt[i], wts)
        xs, c2, p2, f2 = _sample_layer(xs, p_sample, state_conv_a, state_pool_tm, state_ffn_conv, i, bf16_wts)
        prompt_states.append((c1, p1, f1))
        sample_states.append((c2, time_major(p2), f2))
    cat = lambda states, k: states[0][k] if depth == 1 else jnp.concatenate([s[k] for s in states], axis=0)
    return (xp, xs, cat(prompt_states, 0), cat(prompt_states, 1), cat(prompt_states, 2),
            cat(sample_states, 0), cat(sample_states, 1), cat(sample_states, 2))
```

```python
import functools

import jax
import jax.numpy as jnp
from jax import lax
from jax.experimental import pallas as pl
from jax.experimental.pallas import tpu as pltpu

D_MODEL = 1024
D_FF = 2816
PLE_DIM = 256
CONV_W = 3
POOL_WINDOWS = (2, 4, 8, 16)
POOL_MAX = 16
EPS = 1e-6

N_CONV = CONV_W - 1
N_POOL = POOL_MAX - 1
CHUNK = 256
N_MIX_CHUNKS = D_MODEL // CHUNK
N_FF_CHUNKS = D_FF // CHUNK
TIME_STEPS = 64
SUB_STEPS = 32
SAMPLE_SUB_ROWS = 256
MIXER_NORM_AHEAD_CHUNK = 2
FFN_NORM_AHEAD_CHUNK = 8
FFN_FINISH_CHUNK = 1
WEIGHT_CHUNK_ROWS = 128
WEIGHT_STAGE_SLOTS = 4
VMEM_LIMIT_BYTES = 60 * 1024 * 1024

F32 = jnp.float32
BF16 = jnp.bfloat16


def _rmsnorm(x, g):
    ms = jnp.mean(x * x, axis=-1, keepdims=True)
    return (x * lax.rsqrt(ms + EPS)) * g


def _dot(a, b):
    return jnp.dot(a, b, preferred_element_type=F32)


def _causal_conv(buf, cols, v, w_ref, halo, step, rows):
    buf[pl.ds(halo, rows), cols] = v
    y = buf[pl.ds(halo - 2 * step, rows), cols] * w_ref[0:1, cols]
    y = y + buf[pl.ds(halo - step, rows), cols] * w_ref[1:2, cols]
    return y + v * w_ref[2:3, cols]


def _mixer_blocks(n_sub, load_x, store_out, w, conv_in, pool_diff):
    def normed(sb):
        x = load_x(sb)
        return x, _rmsnorm(x, w["g_pre"][...]).astype(BF16)

    def finish(sb, x, mix):
        store_out(sb, x + _rmsnorm(mix, w["g_post"][...]))

    cur = normed(0)
    pending = None
    for sb in range(n_sub):
        x, xb = cur

        def proj(k, j, xb=xb):
            lo = k * D_MODEL + j * CHUNK
            return _dot(xb, w["w_in"][:, lo:lo + CHUNK])

        def branch_proj(j):
            return tuple(proj(k, j) for k in range(4))

        def gate_proj(j):
            return proj(4, j), proj(5, j)

        nxt = branch_proj(0)
        y_a = None
        y_p = []
        for j in range(N_MIX_CHUNKS):
            cols = slice(j * CHUNK, (j + 1) * CHUNK)
            b, c, h, u = nxt
            nxt = branch_proj(j + 1) if j + 1 < N_MIX_CHUNKS else gate_proj(0)
            if j == 0 and pending is not None:
                finish(*pending)
            if j == MIXER_NORM_AHEAD_CHUNK and sb + 1 < n_sub:
                cur = normed(sb + 1)
            z = (b * conv_in(sb, j, c * h)).astype(BF16)
            t = _dot(z, w["w_a_out"][cols, :])
            y_a = t if y_a is None else y_a + t
            d = pool_diff(sb, j, u).astype(BF16)
            y_p.append(_dot(d, w["pool_w"][cols, :]) * w["pool_scale"][:, cols])
        mix = None
        for j in range(N_MIX_CHUNKS):
            cols = slice(j * CHUNK, (j + 1) * CHUNK)
            ga, gp = nxt
            if j + 1 < N_MIX_CHUNKS:
                nxt = gate_proj(j + 1)
            merged = jax.nn.sigmoid(ga) * y_a[:, cols] + jax.nn.sigmoid(gp) * y_p[j]
            t = _dot(merged.astype(BF16), w["w_o"][cols, :])
            mix = t if mix is None else mix + t
        pending = (sb, x, mix)
    finish(*pending)


def _ffn_blocks(n_sub, load_x, load_p, store_out, w, conv_in):
    def normed(sb):
        x = load_x(sb)
        return x, _rmsnorm(x, w["g_pre"][...]).astype(BF16)

    def finish(sb, x, f):
        x = x + _rmsnorm(f, w["g_post"][...])
        gate = jax.nn.sigmoid(_dot(x.astype(BF16), w["w_gate"][...]))
        store_out(sb, x + gate * _dot(load_p(sb).astype(BF16), w["w_proj"][...]))

    cur = normed(0)
    pending = None
    for sb in range(n_sub):
        x, hb = cur

        def up(j, hb=hb):
            return (_dot(hb, w["w_up"][:, j * CHUNK:(j + 1) * CHUNK]),
                    _dot(hb, w["w_up"][:, D_FF + j * CHUNK:D_FF + (j + 1) * CHUNK]))

        nxt = up(0)
        f = None
        for j in range(N_FF_CHUNKS):
            cols = slice(j * CHUNK, (j + 1) * CHUNK)
            a, g = nxt
            if j + 1 < N_FF_CHUNKS:
                nxt = up(j + 1)
            if j == FFN_FINISH_CHUNK and pending is not None:
                finish(*pending)
            if j == FFN_NORM_AHEAD_CHUNK and sb + 1 < n_sub:
                cur = normed(sb + 1)
            h = jax.nn.gelu(conv_in(sb, j, a), approximate=True) * g
            t = _dot(h.astype(BF16), w["w_down"][cols, :])
            f = t if f is None else f + t
        pending = (sb, x, f)
    finish(*pending)


def _block_copy(hbm, buf, sem, block, slot, seq, steps, *, to_hbm=False):
    hbm_view = hbm.at[seq, pl.ds(block * steps, steps), :]
    buf_view = buf.at[slot, :, seq, :]
    src, dst = (buf_view, hbm_view) if to_hbm else (hbm_view, buf_view)
    return pltpu.make_async_copy(src, dst, sem.at[slot, seq])


def _fetch_block(hbm, buf, sem, steps, n_seq):
    i = pl.program_id(0)
    slot = i % 2

    def start(block, slot):
        for s in range(n_seq):
            _block_copy(hbm, buf, sem, block, slot, s, steps).start()

    @pl.when(i == 0)
    def _():
        start(0, 0)

    @pl.when(i + 1 < pl.num_programs(0))
    def _():
        start(i + 1, 1 - slot)

    for s in range(n_seq):
        _block_copy(hbm, buf, sem, i, slot, s, steps).wait()
    return slot


def _stream_weight_as_bf16(w_hbm, w_vmem, stage, sem):
    n_rows, width = w_hbm.shape
    n_slots, chunk, _ = stage.shape
    n_chunks = n_rows // chunk
    assert n_chunks * chunk == n_rows and width <= stage.shape[2]

    def chunk_copy(c, slot):
        return pltpu.make_async_copy(w_hbm.at[pl.ds(c * chunk, chunk), :], stage.at[slot, :, 0:width], sem.at[slot])

    for c in range(min(n_slots, n_chunks)):
        chunk_copy(c, c).start()

    def body(c, carry):
        slot = c % n_slots
        chunk_copy(c, slot).wait()
        w_vmem[pl.ds(pl.multiple_of(c * chunk, chunk), chunk), :] = stage[slot, :, 0:width].astype(BF16)

        @pl.when(c + n_slots < n_chunks)
        def _():
            chunk_copy(c + n_slots, slot).start()

        return carry

    lax.fori_loop(0, n_chunks, body, 0)


def _load_weights(pairs, exports, stage, stage_sem, export_sem):
    for w_hbm, w_vmem in pairs:
        _stream_weight_as_bf16(w_hbm, w_vmem, stage, stage_sem)
    for k, ((_, w_vmem), w_out) in enumerate(zip(pairs, exports)):
        pltpu.make_async_copy(w_vmem, w_out, export_sem.at[k]).start()


def _finish_weight_exports(pairs, exports, export_sem):
    for k, ((_, w_vmem), w_out) in enumerate(zip(pairs, exports)):
        pltpu.make_async_copy(w_vmem, w_out, export_sem.at[k]).wait()


def _mixer_prompt_kernel(x_hbm, g_pre, w_in_hbm, conv_w, w_a_out_hbm, pool_w_hbm, pool_scale, w_o_hbm, g_post,
                         x1_ref, nconv_ref, npool_ref, w_in_out, w_a_out_out, pool_w_out, w_o_out,
                         xbuf, xsem, cbuf, ubuf, e0, e1, e2, inv_cnt,
                         w_in, w_a_out, pool_w, w_o, stage, stage_sem, export_sem,
                         *, steps, sub_steps, n_seq):
    i = pl.program_id(0)
    weight_pairs = ((w_in_hbm, w_in), (w_a_out_hbm, w_a_out), (pool_w_hbm, pool_w), (w_o_hbm, w_o))
    weight_exports = (w_in_out, w_a_out_out, pool_w_out, w_o_out)
    sub_rows = sub_steps * n_seq
    n_sub = steps // sub_steps
    conv_halo = N_CONV * n_seq
    pool_halo = N_POOL * n_seq
    head = POOL_MAX * n_seq
    ebufs = (e0, e1, e2)

    @pl.when(i == 0)
    def _():
        cbuf[0:conv_halo, :] = jnp.zeros((conv_halo, D_MODEL), F32)
        ubuf[0:pool_halo, :] = jnp.zeros((pool_halo, D_MODEL), F32)
        pos = lax.broadcasted_iota(jnp.int32, (POOL_MAX, n_seq, CHUNK), 0).reshape(head, CHUNK)
        for j, window in enumerate(POOL_WINDOWS):
            inv_cnt[j] = 1.0 / jnp.minimum(pos + 1, window).astype(F32)
        _load_weights(weight_pairs, weight_exports, stage, stage_sem, export_sem)

    @pl.when(i == 1)
    def _():
        for j, window in enumerate(POOL_WINDOWS):
            inv_cnt[j] = jnp.full((head, CHUNK), 1.0 / window, F32)

    slot = _fetch_block(x_hbm, xbuf, xsem, steps, n_seq)
    w = dict(g_pre=g_pre, w_in=w_in, w_a_out=w_a_out, pool_w=pool_w, pool_scale=pool_scale, w_o=w_o,
             g_post=g_post)

    def conv_in(sb, j, v):
        cols = slice(j * CHUNK, (j + 1) * CHUNK)
        y = _causal_conv(cbuf, cols, v, conv_w, conv_halo, n_seq, sub_rows)
        cbuf[0:conv_halo, cols] = cbuf[pl.ds(sub_rows, conv_halo), cols]
        return y

    def pool_diff(sb, j, u):
        cols = slice(j * CHUNK, (j + 1) * CHUNK)
        window = POOL_WINDOWS[j]
        ubuf[pl.ds(pool_halo, sub_rows), cols] = u
        n_stages = j + 1
        tsum = None
        for k in range(n_stages):
            lo = pool_halo - (window - (2 << k)) * n_seq
            n = pool_halo + sub_rows - lo
            shift = (1 << k) * n_seq
            if k == 0:
                tsum = ubuf[pl.ds(lo, n), cols] + ubuf[pl.ds(lo - shift, n), cols]
            else:
                tsum = ebufs[k - 1][pl.ds(lo, n), cols] + ebufs[k - 1][pl.ds(lo - shift, n), cols]
            if k < n_stages - 1:
                ebufs[k][pl.ds(lo, n), cols] = tsum
        keep = (window - 1) * n_seq
        ubuf[pl.ds(pool_halo - keep, keep), cols] = ubuf[pl.ds(pool_halo + sub_rows - keep, keep), cols]
        if sb > 0:
            return tsum * (1.0 / window) - u
        return jnp.concatenate([tsum[0:head, :] * inv_cnt[j] - u[0:head, :],
                                tsum[head:, :] * (1.0 / window) - u[head:, :]], axis=0)

    def load_x(sb):
        return xbuf[slot, sb * sub_steps:(sb + 1) * sub_steps].reshape(sub_rows, D_MODEL)

    def store_x1(sb, value):
        x1_ref[sb * sub_rows:(sb + 1) * sub_rows, :] = value

    _mixer_blocks(n_sub, load_x, store_x1, w, conv_in, pool_diff)

    @pl.when(i == pl.num_programs(0) - 1)
    def _():
        nconv_ref[...] = cbuf[0:conv_halo, :].reshape(N_CONV, n_seq, D_MODEL)
        npool_ref[...] = ubuf[pl.ds(sub_rows, pool_halo), :].reshape(N_POOL, n_seq, D_MODEL)
        _finish_weight_exports(weight_pairs, weight_exports, export_sem)


def _ffn_prompt_kernel(x_ref, p_hbm, g_pre, w_up_hbm, conv_w, w_down_hbm, g_post, w_proj_hbm, w_gate_hbm,
                       y_hbm, nffn_ref, w_up_out, w_down_out, w_proj_out, w_gate_out,
                       pbuf, psem, ybuf, ysem, fbuf,
                       w_up, w_down, w_proj, w_gate, stage, stage_sem, export_sem,
                       *, steps, sub_steps, n_seq):
    i = pl.program_id(0)
    last = pl.num_programs(0) - 1
    sub_rows = sub_steps * n_seq
    n_sub = steps // sub_steps
    conv_halo = N_CONV * n_seq
    weight_pairs = ((w_up_hbm, w_up), (w_down_hbm, w_down), (w_proj_hbm, w_proj), (w_gate_hbm, w_gate))
    weight_exports = (w_up_out, w_down_out, w_proj_out, w_gate_out)

    @pl.when(i == 0)
    def _():
        fbuf[0:conv_halo, :] = jnp.zeros((conv_halo, D_FF), F32)
        _load_weights(weight_pairs, weight_exports, stage, stage_sem, export_sem)

    def wait_store(block, slot):
        for s in range(n_seq):
            _block_copy(y_hbm, ybuf, ysem, block, slot, s, steps, to_hbm=True).wait()

    slot = _fetch_block(p_hbm, pbuf, psem, steps, n_seq)

    @pl.when(i >= 2)
    def _():
        wait_store(i - 2, slot)

    w = dict(g_pre=g_pre, w_up=w_up, w_down=w_down, g_post=g_post, w_proj=w_proj, w_gate=w_gate)

    def conv_in(sb, j, v):
        cols = slice(j * CHUNK, (j + 1) * CHUNK)
        y = _causal_conv(fbuf, cols, v, conv_w, conv_halo, n_seq, sub_rows)
        fbuf[0:conv_halo, cols] = fbuf[pl.ds(sub_rows, conv_halo), cols]
        return y

    def load_x(sb):
        return x_ref[sb * sub_rows:(sb + 1) * sub_rows, :]

    def load_p(sb):
        return pbuf[slot, sb * sub_steps:(sb + 1) * sub_steps].reshape(sub_rows, PLE_DIM)

    def store_y(sb, value):
        ybuf[slot, sb * sub_steps:(sb + 1) * sub_steps] = value.reshape(sub_steps, n_seq, D_MODEL)

    _ffn_blocks(n_sub, load_x, load_p, store_y, w, conv_in)

    for s in range(n_seq):
        _block_copy(y_hbm, ybuf, ysem, i, slot, s, steps, to_hbm=True).start()

    @pl.when(i == last)
    def _():
        nffn_ref[...] = fbuf[0:conv_halo, :].reshape(N_CONV, n_seq, D_FF)
        wait_store(i - 1, 1 - slot)
        wait_store(i, slot)
        _finish_weight_exports(weight_pairs, weight_exports, export_sem)


def _resident(shape):
    return pl.BlockSpec(shape, lambda i: (0,) * len(shape), pipeline_mode=pl.Buffered(1))


def _prompt_layer(x, p, wts):
    n_seq, seq_len, _ = x.shape
    steps, sub_steps = TIME_STEPS, SUB_STEPS
    n_blocks = seq_len // steps
    assert n_blocks * steps == seq_len and n_blocks >= 2 and n_seq % 8 == 0
    assert steps % sub_steps == 0 and sub_steps > POOL_MAX
    rows = steps * n_seq
    params = pltpu.CompilerParams(dimension_semantics=("arbitrary",), vmem_limit_bytes=VMEM_LIMIT_BYTES)
    any_space = pl.BlockSpec(memory_space=pl.ANY)
    row_block = pl.BlockSpec((rows, D_MODEL), lambda i: (i, 0))
    state_block = lambda n, width: pl.BlockSpec((n, n_seq, width), lambda i: (0, 0, 0))

    weight_spec = lambda a: any_space if a.dtype == F32 and a.shape[0] >= WEIGHT_CHUNK_ROWS else _resident(a.shape)
    bf16_like = lambda ws: [jax.ShapeDtypeStruct(a.shape, BF16) for a in ws]
    vmem_bf16 = lambda ws: [pltpu.VMEM(a.shape, BF16) for a in ws]
    stage = lambda ws: [pltpu.VMEM((WEIGHT_STAGE_SLOTS, WEIGHT_CHUNK_ROWS, max(a.shape[1] for a in ws)), F32),
                        pltpu.SemaphoreType.DMA((WEIGHT_STAGE_SLOTS,)), pltpu.SemaphoreType.DMA((len(ws),))]

    mixer_w = (wts["g_pre_mix"], wts["w_in"], wts["conv_a_w"], wts["w_a_out"], wts["pool_w"],
               wts["pool_scale"], wts["w_o"], wts["g_post_mix"])
    mixer_mats = (wts["w_in"], wts["w_a_out"], wts["pool_w"], wts["w_o"])
    sub_rows = sub_steps * n_seq
    stage_rows = N_POOL * n_seq + sub_rows
    x1, nconv, npool, *mixer_bf16 = pl.pallas_call(
        functools.partial(_mixer_prompt_kernel, steps=steps, sub_steps=sub_steps, n_seq=n_seq),
        grid=(n_blocks,),
        in_specs=[any_space] + [weight_spec(a) for a in mixer_w],
        out_specs=[row_block, state_block(N_CONV, D_MODEL), state_block(N_POOL, D_MODEL)] + [any_space] * 4,
        out_shape=[jax.ShapeDtypeStruct((n_blocks * rows, D_MODEL), F32),
                   jax.ShapeDtypeStruct((N_CONV, n_seq, D_MODEL), F32),
                   jax.ShapeDtypeStruct((N_POOL, n_seq, D_MODEL), F32)] + bf16_like(mixer_mats),
        scratch_shapes=[pltpu.VMEM((2, steps, n_seq, D_MODEL), F32), pltpu.SemaphoreType.DMA((2, n_seq)),
                        pltpu.VMEM((N_CONV * n_seq + sub_rows, D_MODEL), F32),
                        pltpu.VMEM((stage_rows, D_MODEL), F32),
                        pltpu.VMEM((stage_rows, D_MODEL), F32),
                        pltpu.VMEM((stage_rows, D_MODEL), F32),
                        pltpu.VMEM((stage_rows, D_MODEL), F32),
                        pltpu.VMEM((len(POOL_WINDOWS), POOL_MAX * n_seq, CHUNK), F32)]
                       + vmem_bf16(mixer_mats) + stage(mixer_mats),
        compiler_params=params,
        name="mixer_prompt",
    )(x, *mixer_w)

    ffn_w = (wts["g_pre_ffn"], wts["w_up"], wts["ffn_conv_w"], wts["w_down"], wts["g_post_ffn"],
             wts["w_ple_proj"], wts["w_ple_gate"])
    ffn_mats = (wts["w_up"], wts["w_down"], wts["w_ple_proj"], wts["w_ple_gate"])
    y, nffn, *ffn_bf16 = pl.pallas_call(
        functools.partial(_ffn_prompt_kernel, steps=steps, sub_steps=sub_steps, n_seq=n_seq),
        grid=(n_blocks,),
        in_specs=[row_block, any_space] + [weight_spec(a) for a in ffn_w],
        out_specs=[any_space, state_block(N_CONV, D_FF)] + [any_space] * 4,
        out_shape=[jax.ShapeDtypeStruct((n_seq, seq_len, D_MODEL), F32),
                   jax.ShapeDtypeStruct((N_CONV, n_seq, D_FF), F32)] + bf16_like(ffn_mats),
        scratch_shapes=[pltpu.VMEM((2, steps, n_seq, PLE_DIM), F32), pltpu.SemaphoreType.DMA((2, n_seq)),
                        pltpu.VMEM((2, steps, n_seq, D_MODEL), F32), pltpu.SemaphoreType.DMA((2, n_seq)),
                        pltpu.VMEM((N_CONV * n_seq + sub_rows, D_FF), F32)]
                       + vmem_bf16(ffn_mats) + stage(ffn_mats),
        compiler_params=params,
        name="ffn_prompt",
    )(x1, p, *ffn_w)
    bf16_wts = dict(wts, **dict(zip(("w_in", "w_a_out", "pool_w", "w_o"), mixer_bf16)),
                    **dict(zip(("w_up", "w_down", "w_ple_proj", "w_ple_gate"), ffn_bf16)))
    to_batch_major = lambda s: jnp.transpose(s, (1, 0, 2))[None]
    return y, to_batch_major(nconv), to_batch_major(npool), to_batch_major(nffn), bf16_wts


def _mixer_sample_kernel(x_ref, sconv_ref, spool_ref, g_pre, w_in, conv_w, w_a_out, pool_w, pool_scale, w_o,
                         g_post, x1_ref, nconv_ref, npool_ref, cbuf, ubuf, *, batch, steps, sub_steps):
    sub_rows = sub_steps * batch
    for k in range(N_CONV):
        cbuf[k * batch:(k + 1) * batch, :] = sconv_ref[:, k, :]

    w = dict(g_pre=g_pre, w_in=w_in, w_a_out=w_a_out, pool_w=pool_w, pool_scale=pool_scale, w_o=w_o,
             g_post=g_post)
    def conv_in(sb, j, v):
        cols = slice(j * CHUNK, (j + 1) * CHUNK)
        return _causal_conv(cbuf, cols, v, conv_w, N_CONV * batch + sb * sub_rows, batch, sub_rows)

    def pool_diff(sb, j, u):
        t0 = sb * sub_steps
        cols = slice(j * CHUNK, (j + 1) * CHUNK)
        window = POOL_WINDOWS[j]
        ubuf[t0:t0 + sub_steps, :, cols] = u.reshape(sub_steps, batch, CHUNK)

        def ext(e):
            return spool_ref[e, :, cols] if e < N_POOL else ubuf[e - N_POOL, :, cols]

        diffs = []
        for t in range(t0, t0 + sub_steps):
            tsum = ext(N_POOL + t - window + 1)
            for e in range(N_POOL + t - window + 2, N_POOL + t + 1):
                tsum = tsum + ext(e)
            diffs.append(tsum * (1.0 / window) - ext(N_POOL + t))
        return jnp.concatenate(diffs, axis=0)

    def load_x(sb):
        return jnp.concatenate([x_ref[:, t, :] for t in range(sb * sub_steps, (sb + 1) * sub_steps)], axis=0)

    def store_x1(sb, value):
        x1_ref[sb * sub_rows:(sb + 1) * sub_rows, :] = value

    _mixer_blocks(steps // sub_steps, load_x, store_x1, w, conv_in, pool_diff)

    for k in range(N_CONV):
        nconv_ref[:, k, :] = cbuf[(steps + k) * batch:(steps + k + 1) * batch, :]
    for m in range(N_POOL):
        e = steps + m
        npool_ref[m] = spool_ref[e] if e < N_POOL else ubuf[e - N_POOL]


def _ffn_sample_kernel(x_ref, p_ref, sffn_ref, g_pre, w_up, conv_w, w_down, g_post, w_proj, w_gate,
                       y_ref, nffn_ref, fbuf, *, batch, steps, sub_steps):
    sub_rows = sub_steps * batch
    for k in range(N_CONV):
        fbuf[k * batch:(k + 1) * batch, :] = sffn_ref[:, k, :]

    w = dict(g_pre=g_pre, w_up=w_up, w_down=w_down, g_post=g_post, w_proj=w_proj, w_gate=w_gate)
    def conv_in(sb, j, v):
        cols = slice(j * CHUNK, (j + 1) * CHUNK)
        return _causal_conv(fbuf, cols, v, conv_w, N_CONV * batch + sb * sub_rows, batch, sub_rows)

    def load_x(sb):
        return x_ref[sb * sub_rows:(sb + 1) * sub_rows, :]

    def load_p(sb):
        return jnp.concatenate([p_ref[:, t, :] for t in range(sb * sub_steps, (sb + 1) * sub_steps)], axis=0)

    def store_y(sb, value):
        for t in range(sub_steps):
            y_ref[:, sb * sub_steps + t, :] = value[t * batch:(t + 1) * batch, :]

    _ffn_blocks(steps // sub_steps, load_x, load_p, store_y, w, conv_in)

    for k in range(N_CONV):
        nffn_ref[:, k, :] = fbuf[(steps + k) * batch:(steps + k + 1) * batch, :]


def _sample_layer(x, p, sconv, spool, sffn, layer, wts):
    batch, steps, _ = x.shape
    sub_steps = max(1, SAMPLE_SUB_ROWS // batch)
    assert steps % sub_steps == 0 and batch % 8 == 0
    rows = batch * steps
    params = pltpu.CompilerParams(vmem_limit_bytes=VMEM_LIMIT_BYTES)
    whole = pl.BlockSpec(memory_space=pltpu.VMEM)
    of_layer = lambda a, k: pl.BlockSpec((None,) + a.shape[1:], lambda: (k, 0, 0, 0))
    one_layer = lambda a: jax.ShapeDtypeStruct((1,) + a.shape[1:], F32)

    mixer_w = (wts["g_pre_mix"], wts["w_in"], wts["conv_a_w"], wts["w_a_out"], wts["pool_w"],
               wts["pool_scale"], wts["w_o"], wts["g_post_mix"])
    x1, nconv, npool = pl.pallas_call(
        functools.partial(_mixer_sample_kernel, batch=batch, steps=steps, sub_steps=sub_steps),
        in_specs=[whole, of_layer(sconv, layer), of_layer(spool, layer)] + [whole] * len(mixer_w),
        out_specs=[whole, of_layer(sconv, 0), of_layer(spool, 0)],
        out_shape=[jax.ShapeDtypeStruct((rows, D_MODEL), F32),
                   one_layer(sconv), one_layer(spool)],
        scratch_shapes=[pltpu.VMEM(((N_CONV + steps) * batch, D_MODEL), F32),
                        pltpu.VMEM((steps, batch, D_MODEL), F32)],
        compiler_params=params,
        name="mixer_sample",
    )(x, sconv, spool, *mixer_w)

    ffn_w = (wts["g_pre_ffn"], wts["w_up"], wts["ffn_conv_w"], wts["w_down"], wts["g_post_ffn"],
             wts["w_ple_proj"], wts["w_ple_gate"])
    y, nffn = pl.pallas_call(
        functools.partial(_ffn_sample_kernel, batch=batch, steps=steps, sub_steps=sub_steps),
        in_specs=[whole, of_layer(p, layer), of_layer(sffn, layer)] + [whole] * len(ffn_w),
        out_specs=[whole, of_layer(sffn, 0)],
        out_shape=[jax.ShapeDtypeStruct(x.shape, F32), one_layer(sffn)],
        scratch_shapes=[pltpu.VMEM(((N_CONV + steps) * batch, D_FF), F32)],
        compiler_params=params,
        name="ffn_sample",
    )(x1, p, sffn, *ffn_w)
    return y, nconv, npool, nffn


def kernel(x_prompt, x_sample, p_prompt, p_sample, state_conv_a, state_pool, state_ffn_conv, g_pre_mix, w_in,
           conv_a_w, w_a_out, pool_w, pool_scale, w_o, g_post_mix, g_pre_ffn, w_up, ffn_conv_w, w_down,
           g_post_ffn, w_ple_proj, w_ple_gate):
    depth = w_in.shape[0]
    xp, xs = x_prompt, x_sample
    prompt_states, sample_states = [], []
    time_major = lambda s: jnp.transpose(s, (0, 2, 1, 3))
    state_pool_tm = time_major(state_pool)
    for i in range(depth):
        wts = dict(
            g_pre_mix=g_pre_mix[i][None], g_post_mix=g_post_mix[i][None], g_pre_ffn=g_pre_ffn[i][None],
            g_post_ffn=g_post_ffn[i][None], pool_scale=pool_scale[i][None], conv_a_w=conv_a_w[i],
            ffn_conv_w=ffn_conv_w[i],
            w_in=w_in[i], w_a_out=w_a_out[i], pool_w=pool_w[i].reshape(D_MODEL, CHUNK),
            w_o=w_o[i], w_up=w_up[i], w_down=w_down[i], w_ple_proj=w_ple_proj[i], w_ple_gate=w_ple_gate[i])
        xp, c1, p1, f1, bf16_wts = _prompt_layer(xp, p_prompt[i], wts)
        xs, c2, p2, f2 = _sample_layer(xs, p_sample, state_conv_a, state_pool_tm, state_ffn_conv, i, bf16_wts)
        prompt_states.append((c1, p1, f1))
        sample_states.append((c2, time_major(p2), f2))
    cat = lambda states, k: states[0][k] if depth == 1 else jnp.concatenate([s[k] for s in states], axis=0)
    return (xp, xs, cat(prompt_states, 0), cat(prompt_states, 1), cat(prompt_states, 2),
            cat(sample_states, 0), cat(sample_states, 1), cat(sample_states, 2))
```

```python
import functools

import jax
import jax.numpy as jnp
from jax import lax
from jax.experimental import pallas as pl
from jax.experimental.pallas import tpu as pltpu

D_MODEL = 1024
D_FF = 2816
PLE_DIM = 256
CONV_W = 3
POOL_WINDOWS = (2, 4, 8, 16)
POOL_MAX = 16
EPS = 1e-6

N_CONV = CONV_W - 1
N_POOL = POOL_MAX - 1
CHUNK = 256
N_MIX_CHUNKS = D_MODEL // CHUNK
N_FF_CHUNKS = D_FF // CHUNK
TIME_STEPS = 128
SUB_STEPS = 32
SAMPLE_SUB_ROWS = 256
MIXER_NORM_AHEAD_CHUNK = 2
FFN_NORM_AHEAD_CHUNK = 8
FFN_FINISH_CHUNK = 1
WEIGHT_CHUNK_ROWS = 128
WEIGHT_STAGE_SLOTS = 4
VMEM_LIMIT_BYTES = 60 * 1024 * 1024

F32 = jnp.float32
BF16 = jnp.bfloat16


def _rmsnorm(x, g):
    ms = jnp.mean(x * x, axis=-1, keepdims=True)
    return (x * lax.rsqrt(ms + EPS)) * g


def _dot(a, b):
    return jnp.dot(a, b, preferred_element_type=F32)


def _causal_conv(buf, cols, v, w_ref, halo, step, rows):
    buf[pl.ds(halo, rows), cols] = v
    y = buf[pl.ds(halo - 2 * step, rows), cols] * w_ref[0:1, cols]
    y = y + buf[pl.ds(halo - step, rows), cols] * w_ref[1:2, cols]
    return y + v * w_ref[2:3, cols]


def _mixer_blocks(n_sub, load_x, store_out, w, conv_in, pool_diff):
    def normed(sb):
        x = load_x(sb)
        return x, _rmsnorm(x, w["g_pre"][...]).astype(BF16)

    def finish(sb, x, mix):
        store_out(sb, x + _rmsnorm(mix, w["g_post"][...]))

    cur = normed(0)
    pending = None
    for sb in range(n_sub):
        x, xb = cur

        def proj(k, j, xb=xb):
            lo = k * D_MODEL + j * CHUNK
            return _dot(xb, w["w_in"][:, lo:lo + CHUNK])

        def branch_proj(j):
            return tuple(proj(k, j) for k in range(4))

        def gate_proj(j):
            return proj(4, j), proj(5, j)

        nxt = branch_proj(0)
        y_a = None
        y_p = []
        for j in range(N_MIX_CHUNKS):
            cols = slice(j * CHUNK, (j + 1) * CHUNK)
            b, c, h, u = nxt
            nxt = branch_proj(j + 1) if j + 1 < N_MIX_CHUNKS else gate_proj(0)
            if j == 0 and pending is not None:
                finish(*pending)
            if j == MIXER_NORM_AHEAD_CHUNK and sb + 1 < n_sub:
                cur = normed(sb + 1)
            z = (b * conv_in(sb, j, c * h)).astype(BF16)
            t = _dot(z, w["w_a_out"][cols, :])
            y_a = t if y_a is None else y_a + t
            d = pool_diff(sb, j, u).astype(BF16)
            y_p.append(_dot(d, w["pool_w"][cols, :]) * w["pool_scale"][:, cols])
        mix = None
        for j in range(N_MIX_CHUNKS):
            cols = slice(j * CHUNK, (j + 1) * CHUNK)
            ga, gp = nxt
            if j + 1 < N_MIX_CHUNKS:
                nxt = gate_proj(j + 1)
            merged = jax.nn.sigmoid(ga) * y_a[:, cols] + jax.nn.sigmoid(gp) * y_p[j]
            t = _dot(merged.astype(BF16), w["w_o"][cols, :])
            mix = t if mix is None else mix + t
        pending = (sb, x, mix)
    finish(*pending)


def _ffn_blocks(n_sub, load_x, load_p, store_out, w, conv_in):
    def normed(sb):
        x = load_x(sb)
        return x, _rmsnorm(x, w["g_pre"][...]).astype(BF16)

    def finish(sb, x, f):
        x = x + _rmsnorm(f, w["g_post"][...])
        gate = jax.nn.sigmoid(_dot(x.astype(BF16), w["w_gate"][...]))
        store_out(sb, x + gate * _dot(load_p(sb).astype(BF16), w["w_proj"][...]))

    cur = normed(0)
    pending = None
    for sb in range(n_sub):
        x, hb = cur

        def up(j, hb=hb):
            return (_dot(hb, w["w_up"][:, j * CHUNK:(j + 1) * CHUNK]),
                    _dot(hb, w["w_up"][:, D_FF + j * CHUNK:D_FF + (j + 1) * CHUNK]))

        nxt = up(0)
        f = None
        for j in range(N_FF_CHUNKS):
            cols = slice(j * CHUNK, (j + 1) * CHUNK)
            a, g = nxt
            if j + 1 < N_FF_CHUNKS:
                nxt = up(j + 1)
            if j == FFN_FINISH_CHUNK and pending is not None:
                finish(*pending)
            if j == FFN_NORM_AHEAD_CHUNK and sb + 1 < n_sub:
                cur = normed(sb + 1)
            h = jax.nn.gelu(conv_in(sb, j, a), approximate=True) * g
            t = _dot(h.astype(BF16), w["w_down"][cols, :])
            f = t if f is None else f + t
        pending = (sb, x, f)
    finish(*pending)


def _block_copy(hbm, buf, sem, block, slot, seq, steps, *, to_hbm=False):
    hbm_view = hbm.at[seq, pl.ds(block * steps, steps), :]
    buf_view = buf.at[slot, :, seq, :]
    src, dst = (buf_view, hbm_view) if to_hbm else (hbm_view, buf_view)
    return pltpu.make_async_copy(src, dst, sem.at[slot, seq])


def _fetch_block(hbm, buf, sem, steps, n_seq):
    i = pl.program_id(0)
    slot = i % 2

    def start(block, slot):
        for s in range(n_seq):
            _block_copy(hbm, buf, sem, block, slot, s, steps).start()

    @pl.when(i == 0)
    def _():
        start(0, 0)

    @pl.when(i + 1 < pl.num_programs(0))
    def _():
        start(i + 1, 1 - slot)

    for s in range(n_seq):
        _block_copy(hbm, buf, sem, i, slot, s, steps).wait()
    return slot


def _stream_weight_as_bf16(w_hbm, w_vmem, stage, sem):
    n_rows, width = w_hbm.shape
    n_slots, chunk, _ = stage.shape
    n_chunks = n_rows // chunk
    assert n_chunks * chunk == n_rows and width <= stage.shape[2]

    def chunk_copy(c, slot):
        return pltpu.make_async_copy(w_hbm.at[pl.ds(c * chunk, chunk), :], stage.at[slot, :, 0:width], sem.at[slot])

    for c in range(min(n_slots, n_chunks)):
        chunk_copy(c, c).start()

    def body(c, carry):
        slot = c % n_slots
        chunk_copy(c, slot).wait()
        w_vmem[pl.ds(pl.multiple_of(c * chunk, chunk), chunk), :] = stage[slot, :, 0:width].astype(BF16)

        @pl.when(c + n_slots < n_chunks)
        def _():
            chunk_copy(c + n_slots, slot).start()

        return carry

    lax.fori_loop(0, n_chunks, body, 0)


def _load_weights(pairs, exports, stage, stage_sem, export_sem):
    for w_hbm, w_vmem in pairs:
        _stream_weight_as_bf16(w_hbm, w_vmem, stage, stage_sem)
    for k, ((_, w_vmem), w_out) in enumerate(zip(pairs, exports)):
        pltpu.make_async_copy(w_vmem, w_out, export_sem.at[k]).start()


def _finish_weight_exports(pairs, exports, export_sem):
    for k, ((_, w_vmem), w_out) in enumerate(zip(pairs, exports)):
        pltpu.make_async_copy(w_vmem, w_out, export_sem.at[k]).wait()


def _mixer_prompt_kernel(x_hbm, g_pre, w_in_hbm, conv_w, w_a_out_hbm, pool_w_hbm, pool_scale, w_o_hbm, g_post,
                         x1_ref, nconv_ref, npool_ref, w_in_out, w_a_out_out, pool_w_out, w_o_out,
                         xbuf, xsem, cbuf, ubuf, e0, e1, e2, inv_cnt,
                         w_in, w_a_out, pool_w, w_o, stage, stage_sem, export_sem,
                         *, steps, sub_steps, n_seq):
    i = pl.program_id(0)
    weight_pairs = ((w_in_hbm, w_in), (w_a_out_hbm, w_a_out), (pool_w_hbm, pool_w), (w_o_hbm, w_o))
    weight_exports = (w_in_out, w_a_out_out, pool_w_out, w_o_out)
    sub_rows = sub_steps * n_seq
    n_sub = steps // sub_steps
    conv_halo = N_CONV * n_seq
    pool_halo = N_POOL * n_seq
    head = POOL_MAX * n_seq
    ebufs = (e0, e1, e2)

    @pl.when(i == 0)
    def _():
        cbuf[0:conv_halo, :] = jnp.zeros((conv_halo, D_MODEL), F32)
        ubuf[0:pool_halo, :] = jnp.zeros((pool_halo, D_MODEL), F32)
        pos = lax.broadcasted_iota(jnp.int32, (POOL_MAX, n_seq, CHUNK), 0).reshape(head, CHUNK)
        for j, window in enumerate(POOL_WINDOWS):
            inv_cnt[j] = 1.0 / jnp.minimum(pos + 1, window).astype(F32)
        _load_weights(weight_pairs, weight_exports, stage, stage_sem, export_sem)

    @pl.when(i == 1)
    def _():
        for j, window in enumerate(POOL_WINDOWS):
            inv_cnt[j] = jnp.full((head, CHUNK), 1.0 / window, F32)

    slot = _fetch_block(x_hbm, xbuf, xsem, steps, n_seq)
    w = dict(g_pre=g_pre, w_in=w_in, w_a_out=w_a_out, pool_w=pool_w, pool_scale=pool_scale, w_o=w_o,
             g_post=g_post)

    def conv_in(sb, j, v):
        cols = slice(j * CHUNK, (j + 1) * CHUNK)
        y = _causal_conv(cbuf, cols, v, conv_w, conv_halo, n_seq, sub_rows)
        cbuf[0:conv_halo, cols] = cbuf[pl.ds(sub_rows, conv_halo), cols]
        return y

    def pool_diff(sb, j, u):
        cols = slice(j * CHUNK, (j + 1) * CHUNK)
        window = POOL_WINDOWS[j]
        ubuf[pl.ds(pool_halo, sub_rows), cols] = u
        n_stages = j + 1
        tsum = None
        for k in range(n_stages):
            lo = pool_halo - (window - (2 << k)) * n_seq
            n = pool_halo + sub_rows - lo
            shift = (1 << k) * n_seq
            if k == 0:
                tsum = ubuf[pl.ds(lo, n), cols] + ubuf[pl.ds(lo - shift, n), cols]
            else:
                tsum = ebufs[k - 1][pl.ds(lo, n), cols] + ebufs[k - 1][pl.ds(lo - shift, n), cols]
            if k < n_stages - 1:
                ebufs[k][pl.ds(lo, n), cols] = tsum
        keep = (window - 1) * n_seq
        ubuf[pl.ds(pool_halo - keep, keep), cols] = ubuf[pl.ds(pool_halo + sub_rows - keep, keep), cols]
        if sb > 0:
            return tsum * (1.0 / window) - u
        return jnp.concatenate([tsum[0:head, :] * inv_cnt[j] - u[0:head, :],
                                tsum[head:, :] * (1.0 / window) - u[head:, :]], axis=0)

    def load_x(sb):
        return xbuf[slot, sb * sub_steps:(sb + 1) * sub_steps].reshape(sub_rows, D_MODEL)

    def store_x1(sb, value):
        x1_ref[sb * sub_rows:(sb + 1) * sub_rows, :] = value

    _mixer_blocks(n_sub, load_x, store_x1, w, conv_in, pool_diff)

    @pl.when(i == pl.num_programs(0) - 1)
    def _():
        for k in range(N_CONV):
            nconv_ref[:, k, :] = cbuf[k * n_seq:(k + 1) * n_seq, :]
        npool_ref[...] = ubuf[pl.ds(sub_rows, pool_halo), :].reshape(N_POOL, n_seq, D_MODEL)
        _finish_weight_exports(weight_pairs, weight_exports, export_sem)


def _ffn_prompt_kernel(x_ref, p_hbm, g_pre, w_up_hbm, conv_w, w_down_hbm, g_post, w_proj_hbm, w_gate_hbm,
                       y_hbm, nffn_ref, w_up_out, w_down_out, w_proj_out, w_gate_out,
                       pbuf, psem, ybuf, ysem, fbuf,
                       w_up, w_down, w_proj, w_gate, stage, stage_sem, export_sem,
                       *, steps, sub_steps, n_seq):
    i = pl.program_id(0)
    last = pl.num_programs(0) - 1
    sub_rows = sub_steps * n_seq
    n_sub = steps // sub_steps
    conv_halo = N_CONV * n_seq
    weight_pairs = ((w_up_hbm, w_up), (w_down_hbm, w_down), (w_proj_hbm, w_proj), (w_gate_hbm, w_gate))
    weight_exports = (w_up_out, w_down_out, w_proj_out, w_gate_out)

    @pl.when(i == 0)
    def _():
        fbuf[0:conv_halo, :] = jnp.zeros((conv_halo, D_FF), F32)
        _load_weights(weight_pairs, weight_exports, stage, stage_sem, export_sem)

    def wait_store(block, slot):
        for s in range(n_seq):
            _block_copy(y_hbm, ybuf, ysem, block, slot, s, steps, to_hbm=True).wait()

    slot = _fetch_block(p_hbm, pbuf, psem, steps, n_seq)

    @pl.when(i >= 2)
    def _():
        wait_store(i - 2, slot)

    w = dict(g_pre=g_pre, w_up=w_up, w_down=w_down, g_post=g_post, w_proj=w_proj, w_gate=w_gate)

    def conv_in(sb, j, v):
        cols = slice(j * CHUNK, (j + 1) * CHUNK)
        y = _causal_conv(fbuf, cols, v, conv_w, conv_halo, n_seq, sub_rows)
        fbuf[0:conv_halo, cols] = fbuf[pl.ds(sub_rows, conv_halo), cols]
        return y

    def load_x(sb):
        return x_ref[sb * sub_rows:(sb + 1) * sub_rows, :]

    def load_p(sb):
        return pbuf[slot, sb * sub_steps:(sb + 1) * sub_steps].reshape(sub_rows, PLE_DIM)

    def store_y(sb, value):
        ybuf[slot, sb * sub_steps:(sb + 1) * sub_steps] = value.reshape(sub_steps, n_seq, D_MODEL)

    _ffn_blocks(n_sub, load_x, load_p, store_y, w, conv_in)

    for s in range(n_seq):
        _block_copy(y_hbm, ybuf, ysem, i, slot, s, steps, to_hbm=True).start()

    @pl.when(i == last)
    def _():
        for k in range(N_CONV):
            nffn_ref[:, k, :] = fbuf[k * n_seq:(k + 1) * n_seq, :]
        wait_store(i - 1, 1 - slot)
        wait_store(i, slot)
        _finish_weight_exports(weight_pairs, weight_exports, export_sem)


def _resident(shape):
    return pl.BlockSpec(shape, lambda i: (0,) * len(shape), pipeline_mode=pl.Buffered(1))


def _prompt_layer(x, p, wts):
    n_seq, seq_len, _ = x.shape
    steps, sub_steps = TIME_STEPS, SUB_STEPS
    n_blocks = seq_len // steps
    assert n_blocks * steps == seq_len and n_blocks >= 2 and n_seq % 8 == 0
    assert steps % sub_steps == 0 and sub_steps > POOL_MAX
    rows = steps * n_seq
    params = pltpu.CompilerParams(dimension_semantics=("arbitrary",), vmem_limit_bytes=VMEM_LIMIT_BYTES)
    any_space = pl.BlockSpec(memory_space=pl.ANY)
    row_block = pl.BlockSpec((rows, D_MODEL), lambda i: (i, 0))
    whole_block = lambda shape: pl.BlockSpec(shape, lambda i: (0,) * len(shape))
    conv_state = lambda width: (n_seq, N_CONV, width)
    pool_state = (N_POOL, n_seq, D_MODEL)

    weight_spec = lambda a: any_space if a.dtype == F32 and a.shape[0] >= WEIGHT_CHUNK_ROWS else _resident(a.shape)
    bf16_like = lambda ws: [jax.ShapeDtypeStruct(a.shape, BF16) for a in ws]
    vmem_bf16 = lambda ws: [pltpu.VMEM(a.shape, BF16) for a in ws]
    stage = lambda ws: [pltpu.VMEM((WEIGHT_STAGE_SLOTS, WEIGHT_CHUNK_ROWS, max(a.shape[1] for a in ws)), F32),
                        pltpu.SemaphoreType.DMA((WEIGHT_STAGE_SLOTS,)), pltpu.SemaphoreType.DMA((len(ws),))]

    mixer_w = (wts["g_pre_mix"], wts["w_in"], wts["conv_a_w"], wts["w_a_out"], wts["pool_w"],
               wts["pool_scale"], wts["w_o"], wts["g_post_mix"])
    mixer_mats = (wts["w_in"], wts["w_a_out"], wts["pool_w"], wts["w_o"])
    sub_rows = sub_steps * n_seq
    stage_rows = N_POOL * n_seq + sub_rows
    x1, nconv, npool, *mixer_bf16 = pl.pallas_call(
        functools.partial(_mixer_prompt_kernel, steps=steps, sub_steps=sub_steps, n_seq=n_seq),
        grid=(n_blocks,),
        in_specs=[any_space] + [weight_spec(a) for a in mixer_w],
        out_specs=[row_block, whole_block(conv_state(D_MODEL)), whole_block(pool_state)] + [any_space] * 4,
        out_shape=[jax.ShapeDtypeStruct((n_blocks * rows, D_MODEL), F32),
                   jax.ShapeDtypeStruct(conv_state(D_MODEL), F32),
                   jax.ShapeDtypeStruct(pool_state, F32)] + bf16_like(mixer_mats),
        scratch_shapes=[pltpu.VMEM((2, steps, n_seq, D_MODEL), F32), pltpu.SemaphoreType.DMA((2, n_seq)),
                        pltpu.VMEM((N_CONV * n_seq + sub_rows, D_MODEL), F32),
                        pltpu.VMEM((stage_rows, D_MODEL), F32),
                        pltpu.VMEM((stage_rows, D_MODEL), F32),
                        pltpu.VMEM((stage_rows, D_MODEL), F32),
                        pltpu.VMEM((stage_rows, D_MODEL), F32),
                        pltpu.VMEM((len(POOL_WINDOWS), POOL_MAX * n_seq, CHUNK), F32)]
                       + vmem_bf16(mixer_mats) + stage(mixer_mats),
        compiler_params=params,
        name="mixer_prompt",
    )(x, *mixer_w)

    ffn_w = (wts["g_pre_ffn"], wts["w_up"], wts["ffn_conv_w"], wts["w_down"], wts["g_post_ffn"],
             wts["w_ple_proj"], wts["w_ple_gate"])
    ffn_mats = (wts["w_up"], wts["w_down"], wts["w_ple_proj"], wts["w_ple_gate"])
    y, nffn, *ffn_bf16 = pl.pallas_call(
        functools.partial(_ffn_prompt_kernel, steps=steps, sub_steps=sub_steps, n_seq=n_seq),
        grid=(n_blocks,),
        in_specs=[row_block, any_space] + [weight_spec(a) for a in ffn_w],
        out_specs=[any_space, whole_block(conv_state(D_FF))] + [any_space] * 4,
        out_shape=[jax.ShapeDtypeStruct((n_seq, seq_len, D_MODEL), F32),
                   jax.ShapeDtypeStruct(conv_state(D_FF), F32)] + bf16_like(ffn_mats),
        scratch_shapes=[pltpu.VMEM((2, steps, n_seq, PLE_DIM), F32), pltpu.SemaphoreType.DMA((2, n_seq)),
                        pltpu.VMEM((2, steps, n_seq, D_MODEL), F32), pltpu.SemaphoreType.DMA((2, n_seq)),
                        pltpu.VMEM((N_CONV * n_seq + sub_rows, D_FF), F32)]
                       + vmem_bf16(ffn_mats) + stage(ffn_mats),
        compiler_params=params,
        name="ffn_prompt",
    )(x1, p, *ffn_w)
    bf16_wts = dict(wts, **dict(zip(("w_in", "w_a_out", "pool_w", "w_o"), mixer_bf16)),
                    **dict(zip(("w_up", "w_down", "w_ple_proj", "w_ple_gate"), ffn_bf16)))
    return y, nconv[None], jnp.transpose(npool, (1, 0, 2))[None], nffn[None], bf16_wts


def _mixer_sample_kernel(x_ref, sconv_ref, spool_ref, g_pre, w_in, conv_w, w_a_out, pool_w, pool_scale, w_o,
                         g_post, x1_ref, nconv_ref, npool_ref, cbuf, ubuf, *, batch, steps, sub_steps):
    sub_rows = sub_steps * batch
    for k in range(N_CONV):
        cbuf[k * batch:(k + 1) * batch, :] = sconv_ref[:, k, :]

    w = dict(g_pre=g_pre, w_in=w_in, w_a_out=w_a_out, pool_w=pool_w, pool_scale=pool_scale, w_o=w_o,
             g_post=g_post)
    def conv_in(sb, j, v):
        cols = slice(j * CHUNK, (j + 1) * CHUNK)
        return _causal_conv(cbuf, cols, v, conv_w, N_CONV * batch + sb * sub_rows, batch, sub_rows)

    def pool_diff(sb, j, u):
        t0 = sb * sub_steps
        cols = slice(j * CHUNK, (j + 1) * CHUNK)
        window = POOL_WINDOWS[j]
        ubuf[t0:t0 + sub_steps, :, cols] = u.reshape(sub_steps, batch, CHUNK)

        def ext(e):
            return spool_ref[e, :, cols] if e < N_POOL else ubuf[e - N_POOL, :, cols]

        diffs = []
        for t in range(t0, t0 + sub_steps):
            tsum = ext(N_POOL + t - window + 1)
            for e in range(N_POOL + t - window + 2, N_POOL + t + 1):
                tsum = tsum + ext(e)
            diffs.append(tsum * (1.0 / window) - ext(N_POOL + t))
        return jnp.concatenate(diffs, axis=0)

    def load_x(sb):
        return jnp.concatenate([x_ref[:, t, :] for t in range(sb * sub_steps, (sb + 1) * sub_steps)], axis=0)

    def store_x1(sb, value):
        x1_ref[sb * sub_rows:(sb + 1) * sub_rows, :] = value

    _mixer_blocks(steps // sub_steps, load_x, store_x1, w, conv_in, pool_diff)

    for k in range(N_CONV):
        nconv_ref[:, k, :] = cbuf[(steps + k) * batch:(steps + k + 1) * batch, :]
    for m in range(N_POOL):
        e = steps + m
        npool_ref[m] = spool_ref[e] if e < N_POOL else ubuf[e - N_POOL]


def _ffn_sample_kernel(x_ref, p_ref, sffn_ref, g_pre, w_up, conv_w, w_down, g_post, w_proj, w_gate,
                       y_ref, nffn_ref, fbuf, *, batch, steps, sub_steps):
    sub_rows = sub_steps * batch
    for k in range(N_CONV):
        fbuf[k * batch:(k + 1) * batch, :] = sffn_ref[:, k, :]

    w = dict(g_pre=g_pre, w_up=w_up, w_down=w_down, g_post=g_post, w_proj=w_proj, w_gate=w_gate)
    def conv_in(sb, j, v):
        cols = slice(j * CHUNK, (j + 1) * CHUNK)
        return _causal_conv(fbuf, cols, v, conv_w, N_CONV * batch + sb * sub_rows, batch, sub_rows)

    def load_x(sb):
        return x_ref[sb * sub_rows:(sb + 1) * sub_rows, :]

    def load_p(sb):
        return jnp.concatenate([p_ref[:, t, :] for t in range(sb * sub_steps, (sb + 1) * sub_steps)], axis=0)

    def store_y(sb, value):
        for t in range(sub_steps):
            y_ref[:, sb * sub_steps + t, :] = value[t * batch:(t + 1) * batch, :]

    _ffn_blocks(steps // sub_steps, load_x, load_p, store_y, w, conv_in)

    for k in range(N_CONV):
        nffn_ref[:, k, :] = fbuf[(steps + k) * batch:(steps + k + 1) * batch, :]


def _sample_layer(x, p, sconv, spool, sffn, layer, wts):
    batch, steps, _ = x.shape
    sub_steps = max(1, SAMPLE_SUB_ROWS // batch)
    assert steps % sub_steps == 0 and batch % 8 == 0
    rows = batch * steps
    params = pltpu.CompilerParams(vmem_limit_bytes=VMEM_LIMIT_BYTES)
    whole = pl.BlockSpec(memory_space=pltpu.VMEM)
    of_layer = lambda a, k: pl.BlockSpec((None,) + a.shape[1:], lambda: (k, 0, 0, 0))
    one_layer = lambda a: jax.ShapeDtypeStruct((1,) + a.shape[1:], F32)

    mixer_w = (wts["g_pre_mix"], wts["w_in"], wts["conv_a_w"], wts["w_a_out"], wts["pool_w"],
               wts["pool_scale"], wts["w_o"], wts["g_post_mix"])
    x1, nconv, npool = pl.pallas_call(
        functools.partial(_mixer_sample_kernel, batch=batch, steps=steps, sub_steps=sub_steps),
        in_specs=[whole, of_layer(sconv, layer), of_layer(spool, layer)] + [whole] * len(mixer_w),
        out_specs=[whole, of_layer(sconv, 0), of_layer(spool, 0)],
        out_shape=[jax.ShapeDtypeStruct((rows, D_MODEL), F32),
                   one_layer(sconv), one_layer(spool)],
        scratch_shapes=[pltpu.VMEM(((N_CONV + steps) * batch, D_MODEL), F32),
                        pltpu.VMEM((steps, batch, D_MODEL), F32)],
        compiler_params=params,
        name="mixer_sample",
    )(x, sconv, spool, *mixer_w)

    ffn_w = (wts["g_pre_ffn"], wts["w_up"], wts["ffn_conv_w"], wts["w_down"], wts["g_post_ffn"],
             wts["w_ple_proj"], wts["w_ple_gate"])
    y, nffn = pl.pallas_call(
        functools.partial(_ffn_sample_kernel, batch=batch, steps=steps, sub_steps=sub_steps),
        in_specs=[whole, of_layer(p, layer), of_layer(sffn, layer)] + [whole] * len(ffn_w),
        out_specs=[whole, of_layer(sffn, 0)],
        out_shape=[jax.ShapeDtypeStruct(x.shape, F32), one_layer(sffn)],
        scratch_shapes=[pltpu.VMEM(((N_CONV + steps) * batch, D_FF), F32)],
        compiler_params=params,
        name="ffn_sample",
    )(x1, p, sffn, *ffn_w)
    return y, nconv, npool, nffn


def kernel(x_prompt, x_sample, p_prompt, p_sample, state_conv_a, state_pool, state_ffn_conv, g_pre_mix, w_in,
           conv_a_w, w_a_out, pool_w, pool_scale, w_o, g_post_mix, g_pre_ffn, w_up, ffn_conv_w, w_down,
           g_post_ffn, w_ple_proj, w_ple_gate):
    depth = w_in.shape[0]
    xp, xs = x_prompt, x_sample
    prompt_states, sample_states = [], []
    time_major = lambda s: jnp.transpose(s, (0, 2, 1, 3))
    state_pool_tm = time_major(state_pool)
    for i in range(depth):
        wts = dict(
            g_pre_mix=g_pre_mix[i][None], g_post_mix=g_post_mix[i][None], g_pre_ffn=g_pre_ffn[i][None],
            g_post_ffn=g_post_ffn[i][None], pool_scale=pool_scale[i][None], conv_a_w=conv_a_w[i],
            ffn_conv_w=ffn_conv_w[i],
            w_in=w_in[i], w_a_out=w_a_out[i], pool_w=pool_w[i].reshape(D_MODEL, CHUNK),
            w_o=w_o[i], w_up=w_up[i], w_down=w_down[i], w_ple_proj=w_ple_proj[i], w_ple_gate=w_ple_gate[i])
        xp, c1, p1, f1, bf16_wts = _prompt_layer(xp, p_prompt[i], wts)
        xs, c2, p2, f2 = _sample_layer(xs, p_sample, state_conv_a, state_pool_tm, state_ffn_conv, i, bf16_wts)
        prompt_states.append((c1, p1, f1))
        sample_states.append((c2, time_major(p2), f2))
    cat = lambda states, k: states[0][k] if depth == 1 else jnp.concatenate([s[k] for s in states], axis=0)
    return (xp, xs, cat(prompt_states, 0), cat(prompt_states, 1), cat(prompt_states, 2),
            cat(sample_states, 0), cat(sample_states, 1), cat(sample_states, 2))
```

```python
import functools
import math

import jax
import jax.numpy as jnp
from jax import lax
from jax.experimental import pallas as pl
from jax.experimental.pallas import tpu as pltpu

D_MODEL = 1024
D_FF = 2816
PLE_DIM = 256
CONV_W = 3
POOL_WINDOWS = (2, 4, 8, 16)
POOL_MAX = 16
EPS = 1e-6

N_CONV = CONV_W - 1
N_POOL = POOL_MAX - 1
CHUNK = 256
N_MIX_CHUNKS = D_MODEL // CHUNK
N_FF_CHUNKS = D_FF // CHUNK
TIME_STEPS = 64
SUB_STEPS = 32
SAMPLE_SUB_ROWS = 256
MIXER_NORM_AHEAD_CHUNK = 2
FFN_NORM_AHEAD_CHUNK = 8
FFN_FINISH_CHUNK = 1
WEIGHT_CHUNK_ROWS = 128
WEIGHT_STAGE_SLOTS = 4
VMEM_CAPACITY_BYTES = 64 * 1024 * 1024
VMEM_COMPILER_SCRATCH_BYTES = 4 * 1024 * 1024

F32 = jnp.float32
BF16 = jnp.bfloat16


def _nbytes(a):
    return math.prod(a.shape) * jnp.dtype(a.dtype).itemsize


def _vmem_limit(scratch, single_buffered=(), double_buffered=()):
    total = sum(_nbytes(s) for s in scratch if s.memory_space == pltpu.VMEM)
    total += sum(_nbytes(a) for a in single_buffered) + 2 * sum(_nbytes(a) for a in double_buffered)
    return min(total + VMEM_COMPILER_SCRATCH_BYTES, VMEM_CAPACITY_BYTES - 4 * 1024 * 1024)


def _rmsnorm(x, g):
    ms = jnp.mean(x * x, axis=-1, keepdims=True)
    return (x * lax.rsqrt(ms + EPS)) * g


def _dot(a, b):
    return jnp.dot(a, b, preferred_element_type=F32)


def _causal_conv(buf, cols, v, w_ref, halo, step, rows):
    buf[pl.ds(halo, rows), cols] = v
    y = buf[pl.ds(halo - 2 * step, rows), cols] * w_ref[0:1, cols]
    y = y + buf[pl.ds(halo - step, rows), cols] * w_ref[1:2, cols]
    return y + v * w_ref[2:3, cols]


def _mixer_blocks(n_sub, load_x, store_out, w, conv_in, pool_diff):
    def normed(sb):
        x = load_x(sb)
        return x, _rmsnorm(x, w["g_pre"][...]).astype(BF16)

    def finish(sb, x, mix):
        store_out(sb, x + _rmsnorm(mix, w["g_post"][...]))

    cur = normed(0)
    pending = None
    for sb in range(n_sub):
        x, xb = cur

        def proj(k, j, xb=xb):
            lo = k * D_MODEL + j * CHUNK
            return _dot(xb, w["w_in"][:, lo:lo + CHUNK])

        def branch_proj(j):
            return tuple(proj(k, j) for k in range(4))

        def gate_proj(j):
            return proj(4, j), proj(5, j)

        nxt = branch_proj(0)
        y_a = None
        y_p = []
        for j in range(N_MIX_CHUNKS):
            cols = slice(j * CHUNK, (j + 1) * CHUNK)
            b, c, h, u = nxt
            nxt = branch_proj(j + 1) if j + 1 < N_MIX_CHUNKS else gate_proj(0)
            if j == 0 and pending is not None:
                finish(*pending)
            if j == MIXER_NORM_AHEAD_CHUNK and sb + 1 < n_sub:
                cur = normed(sb + 1)
            z = (b * conv_in(sb, j, c * h)).astype(BF16)
            t = _dot(z, w["w_a_out"][cols, :])
            y_a = t if y_a is None else y_a + t
            d = pool_diff(sb, j, u).astype(BF16)
            y_p.append(_dot(d, w["pool_w"][cols, :]) * w["pool_scale"][:, cols])
        mix = None
        for j in range(N_MIX_CHUNKS):
            cols = slice(j * CHUNK, (j + 1) * CHUNK)
            ga, gp = nxt
            if j + 1 < N_MIX_CHUNKS:
                nxt = gate_proj(j + 1)
            merged = jax.nn.sigmoid(ga) * y_a[:, cols] + jax.nn.sigmoid(gp) * y_p[j]
            t = _dot(merged.astype(BF16), w["w_o"][cols, :])
            mix = t if mix is None else mix + t
        pending = (sb, x, mix)
    finish(*pending)


def _ffn_blocks(n_sub, load_x, load_p, store_out, w, conv_in):
    def normed(sb):
        x = load_x(sb)
        return x, _rmsnorm(x, w["g_pre"][...]).astype(BF16)

    def finish(sb, x, f):
        x = x + _rmsnorm(f, w["g_post"][...])
        gate = jax.nn.sigmoid(_dot(x.astype(BF16), w["w_gate"][...]))
        store_out(sb, x + gate * _dot(load_p(sb).astype(BF16), w["w_proj"][...]))

    cur = normed(0)
    pending = None
    for sb in range(n_sub):
        x, hb = cur

        def up(j, hb=hb):
            return (_dot(hb, w["w_up"][:, j * CHUNK:(j + 1) * CHUNK]),
                    _dot(hb, w["w_up"][:, D_FF + j * CHUNK:D_FF + (j + 1) * CHUNK]))

        nxt = up(0)
        f = None
        for j in range(N_FF_CHUNKS):
            cols = slice(j * CHUNK, (j + 1) * CHUNK)
            a, g = nxt
            if j + 1 < N_FF_CHUNKS:
                nxt = up(j + 1)
            if j == FFN_FINISH_CHUNK and pending is not None:
                finish(*pending)
            if j == FFN_NORM_AHEAD_CHUNK and sb + 1 < n_sub:
                cur = normed(sb + 1)
            h = jax.nn.gelu(conv_in(sb, j, a), approximate=True) * g
            t = _dot(h.astype(BF16), w["w_down"][cols, :])
            f = t if f is None else f + t
        pending = (sb, x, f)
    finish(*pending)


def _block_copy(hbm, buf, sem, block, slot, seq, steps, *, to_hbm=False):
    hbm_view = hbm.at[seq, pl.ds(block * steps, steps), :]
    buf_view = buf.at[slot, :, seq, :]
    src, dst = (buf_view, hbm_view) if to_hbm else (hbm_view, buf_view)
    return pltpu.make_async_copy(src, dst, sem.at[slot, seq])


def _fetch_block(hbm, buf, sem, steps, n_seq):
    i = pl.program_id(0)
    slot = i % 2

    def start(block, slot):
        for s in range(n_seq):
            _block_copy(hbm, buf, sem, block, slot, s, steps).start()

    @pl.when(i == 0)
    def _():
        start(0, 0)

    @pl.when(i + 1 < pl.num_programs(0))
    def _():
        start(i + 1, 1 - slot)

    for s in range(n_seq):
        _block_copy(hbm, buf, sem, i, slot, s, steps).wait()
    return slot


def _stream_weight_as_bf16(w_hbm, w_vmem, stage, sem):
    n_rows, width = w_hbm.shape
    n_slots, chunk, _ = stage.shape
    n_chunks = n_rows // chunk
    assert n_chunks * chunk == n_rows and width <= stage.shape[2]

    def chunk_copy(c, slot):
        return pltpu.make_async_copy(w_hbm.at[pl.ds(c * chunk, chunk), :], stage.at[slot, :, 0:width], sem.at[slot])

    for c in range(min(n_slots, n_chunks)):
        chunk_copy(c, c).start()

    def body(c, carry):
        slot = c % n_slots
        chunk_copy(c, slot).wait()
        w_vmem[pl.ds(pl.multiple_of(c * chunk, chunk), chunk), :] = stage[slot, :, 0:width].astype(BF16)

        @pl.when(c + n_slots < n_chunks)
        def _():
            chunk_copy(c + n_slots, slot).start()

        return carry

    lax.fori_loop(0, n_chunks, body, 0)


def _load_weights(pairs, exports, stage, stage_sem, export_sem):
    for w_hbm, w_vmem in pairs:
        _stream_weight_as_bf16(w_hbm, w_vmem, stage, stage_sem)
    for k, ((_, w_vmem), w_out) in enumerate(zip(pairs, exports)):
        pltpu.make_async_copy(w_vmem, w_out, export_sem.at[k]).start()


def _finish_weight_exports(pairs, exports, export_sem):
    for k, ((_, w_vmem), w_out) in enumerate(zip(pairs, exports)):
        pltpu.make_async_copy(w_vmem, w_out, export_sem.at[k]).wait()


def _mixer_prompt_kernel(x_hbm, g_pre, w_in_hbm, conv_w, w_a_out_hbm, pool_w_hbm, pool_scale, w_o_hbm, g_post,
                         x1_ref, nconv_ref, npool_ref, w_in_out, w_a_out_out, pool_w_out, w_o_out,
                         xbuf, xsem, cbuf, ubuf, e0, e1, e2, inv_cnt,
                         w_in, w_a_out, pool_w, w_o, stage, stage_sem, export_sem,
                         *, steps, sub_steps, n_seq):
    i = pl.program_id(0)
    weight_pairs = ((w_in_hbm, w_in), (w_a_out_hbm, w_a_out), (pool_w_hbm, pool_w), (w_o_hbm, w_o))
    weight_exports = (w_in_out, w_a_out_out, pool_w_out, w_o_out)
    sub_rows = sub_steps * n_seq
    n_sub = steps // sub_steps
    conv_halo = N_CONV * n_seq
    pool_halo = N_POOL * n_seq
    head = POOL_MAX * n_seq
    ebufs = (e0, e1, e2)

    @pl.when(i == 0)
    def _():
        cbuf[0:conv_halo, :] = jnp.zeros((conv_halo, D_MODEL), F32)
        ubuf[0:pool_halo, :] = jnp.zeros((pool_halo, D_MODEL), F32)
        pos = lax.broadcasted_iota(jnp.int32, (POOL_MAX, n_seq, CHUNK), 0).reshape(head, CHUNK)
        for j, window in enumerate(POOL_WINDOWS):
            inv_cnt[j] = 1.0 / jnp.minimum(pos + 1, window).astype(F32)
        _load_weights(weight_pairs, weight_exports, stage, stage_sem, export_sem)

    @pl.when(i == 1)
    def _():
        for j, window in enumerate(POOL_WINDOWS):
            inv_cnt[j] = jnp.full((head, CHUNK), 1.0 / window, F32)

    slot = _fetch_block(x_hbm, xbuf, xsem, steps, n_seq)
    w = dict(g_pre=g_pre, w_in=w_in, w_a_out=w_a_out, pool_w=pool_w, pool_scale=pool_scale, w_o=w_o,
             g_post=g_post)

    def conv_in(sb, j, v):
        cols = slice(j * CHUNK, (j + 1) * CHUNK)
        y = _causal_conv(cbuf, cols, v, conv_w, conv_halo, n_seq, sub_rows)
        cbuf[0:conv_halo, cols] = cbuf[pl.ds(sub_rows, conv_halo), cols]
        return y

    def pool_diff(sb, j, u):
        cols = slice(j * CHUNK, (j + 1) * CHUNK)
        window = POOL_WINDOWS[j]
        ubuf[pl.ds(pool_halo, sub_rows), cols] = u
        n_stages = j + 1
        tsum = None
        for k in range(n_stages):
            lo = pool_halo - (window - (2 << k)) * n_seq
            n = pool_halo + sub_rows - lo
            shift = (1 << k) * n_seq
            if k == 0:
                tsum = ubuf[pl.ds(lo, n), cols] + ubuf[pl.ds(lo - shift, n), cols]
            else:
                tsum = ebufs[k - 1][pl.ds(lo, n), cols] + ebufs[k - 1][pl.ds(lo - shift, n), cols]
            if k < n_stages - 1:
                ebufs[k][pl.ds(lo, n), cols] = tsum
        keep = (window - 1) * n_seq
        ubuf[pl.ds(pool_halo - keep, keep), cols] = ubuf[pl.ds(pool_halo + sub_rows - keep, keep), cols]
        if sb > 0:
            return tsum * (1.0 / window) - u
        return jnp.concatenate([tsum[0:head, :] * inv_cnt[j] - u[0:head, :],
                                tsum[head:, :] * (1.0 / window) - u[head:, :]], axis=0)

    def load_x(sb):
        return xbuf[slot, sb * sub_steps:(sb + 1) * sub_steps].reshape(sub_rows, D_MODEL)

    def store_x1(sb, value):
        x1_ref[sb * sub_rows:(sb + 1) * sub_rows, :] = value

    _mixer_blocks(n_sub, load_x, store_x1, w, conv_in, pool_diff)

    @pl.when(i == pl.num_programs(0) - 1)
    def _():
        for k in range(N_CONV):
            nconv_ref[:, k, :] = cbuf[k * n_seq:(k + 1) * n_seq, :]
        npool_ref[...] = ubuf[pl.ds(sub_rows, pool_halo), :].reshape(N_POOL, n_seq, D_MODEL)
        _finish_weight_exports(weight_pairs, weight_exports, export_sem)


def _ffn_prompt_kernel(x_ref, p_hbm, g_pre, w_up_hbm, conv_w, w_down_hbm, g_post, w_proj_hbm, w_gate_hbm,
                       y_hbm, nffn_ref, w_up_out, w_down_out, w_proj_out, w_gate_out,
                       pbuf, psem, ybuf, ysem, fbuf,
                       w_up, w_down, w_proj, w_gate, stage, stage_sem, export_sem,
                       *, steps, sub_steps, n_seq):
    i = pl.program_id(0)
    last = pl.num_programs(0) - 1
    sub_rows = sub_steps * n_seq
    n_sub = steps // sub_steps
    conv_halo = N_CONV * n_seq
    weight_pairs = ((w_up_hbm, w_up), (w_down_hbm, w_down), (w_proj_hbm, w_proj), (w_gate_hbm, w_gate))
    weight_exports = (w_up_out, w_down_out, w_proj_out, w_gate_out)

    @pl.when(i == 0)
    def _():
        fbuf[0:conv_halo, :] = jnp.zeros((conv_halo, D_FF), F32)
        _load_weights(weight_pairs, weight_exports, stage, stage_sem, export_sem)

    def wait_store(block, slot):
        for s in range(n_seq):
            _block_copy(y_hbm, ybuf, ysem, block, slot, s, steps, to_hbm=True).wait()

    slot = _fetch_block(p_hbm, pbuf, psem, steps, n_seq)

    @pl.when(i >= 2)
    def _():
        wait_store(i - 2, slot)

    w = dict(g_pre=g_pre, w_up=w_up, w_down=w_down, g_post=g_post, w_proj=w_proj, w_gate=w_gate)

    def conv_in(sb, j, v):
        cols = slice(j * CHUNK, (j + 1) * CHUNK)
        y = _causal_conv(fbuf, cols, v, conv_w, conv_halo, n_seq, sub_rows)
        fbuf[0:conv_halo, cols] = fbuf[pl.ds(sub_rows, conv_halo), cols]
        return y

    def load_x(sb):
        return x_ref[sb * sub_rows:(sb + 1) * sub_rows, :]

    def load_p(sb):
        return pbuf[slot, sb * sub_steps:(sb + 1) * sub_steps].reshape(sub_rows, PLE_DIM)

    def store_y(sb, value):
        ybuf[slot, sb * sub_steps:(sb + 1) * sub_steps] = value.reshape(sub_steps, n_seq, D_MODEL)

    _ffn_blocks(n_sub, load_x, load_p, store_y, w, conv_in)

    for s in range(n_seq):
        _block_copy(y_hbm, ybuf, ysem, i, slot, s, steps, to_hbm=True).start()

    @pl.when(i == last)
    def _():
        for k in range(N_CONV):
            nffn_ref[:, k, :] = fbuf[k * n_seq:(k + 1) * n_seq, :]
        wait_store(i - 1, 1 - slot)
        wait_store(i, slot)
        _finish_weight_exports(weight_pairs, weight_exports, export_sem)


def _resident(shape):
    return pl.BlockSpec(shape, lambda i: (0,) * len(shape), pipeline_mode=pl.Buffered(1))


def _prompt_layer(x, p, wts):
    n_seq, seq_len, _ = x.shape
    steps, sub_steps = TIME_STEPS, SUB_STEPS
    n_blocks = seq_len // steps
    assert n_blocks * steps == seq_len and n_blocks >= 2 and n_seq % 8 == 0
    assert steps % sub_steps == 0 and sub_steps > POOL_MAX
    rows = steps * n_seq
    params = lambda limit: pltpu.CompilerParams(dimension_semantics=("arbitrary",), vmem_limit_bytes=limit)
    any_space = pl.BlockSpec(memory_space=pl.ANY)
    row_block = pl.BlockSpec((rows, D_MODEL), lambda i: (i, 0))
    row_block_shape = jax.ShapeDtypeStruct((rows, D_MODEL), F32)
    whole_block = lambda shape: pl.BlockSpec(shape, lambda i: (0,) * len(shape))
    conv_state = lambda width: (n_seq, N_CONV, width)
    pool_state = (N_POOL, n_seq, D_MODEL)

    weight_spec = lambda a: any_space if a.dtype == F32 and a.shape[0] >= WEIGHT_CHUNK_ROWS else _resident(a.shape)
    bf16_like = lambda ws: [jax.ShapeDtypeStruct(a.shape, BF16) for a in ws]
    vmem_bf16 = lambda ws: [pltpu.VMEM(a.shape, BF16) for a in ws]
    stage = lambda ws: [pltpu.VMEM((WEIGHT_STAGE_SLOTS, WEIGHT_CHUNK_ROWS, max(a.shape[1] for a in ws)), F32),
                        pltpu.SemaphoreType.DMA((WEIGHT_STAGE_SLOTS,)), pltpu.SemaphoreType.DMA((len(ws),))]

    mixer_w = (wts["g_pre_mix"], wts["w_in"], wts["conv_a_w"], wts["w_a_out"], wts["pool_w"],
               wts["pool_scale"], wts["w_o"], wts["g_post_mix"])
    mixer_mats = (wts["w_in"], wts["w_a_out"], wts["pool_w"], wts["w_o"])
    sub_rows = sub_steps * n_seq
    stage_rows = N_POOL * n_seq + sub_rows
    mixer_scratch = ([pltpu.VMEM((2, steps, n_seq, D_MODEL), F32), pltpu.SemaphoreType.DMA((2, n_seq)),
                      pltpu.VMEM((N_CONV * n_seq + sub_rows, D_MODEL), F32),
                      pltpu.VMEM((stage_rows, D_MODEL), F32),
                      pltpu.VMEM((stage_rows, D_MODEL), F32),
                      pltpu.VMEM((stage_rows, D_MODEL), F32),
                      pltpu.VMEM((stage_rows, D_MODEL), F32),
                      pltpu.VMEM((len(POOL_WINDOWS), POOL_MAX * n_seq, CHUNK), F32)]
                     + vmem_bf16(mixer_mats) + stage(mixer_mats))
    x1, nconv, npool, *mixer_bf16 = pl.pallas_call(
        functools.partial(_mixer_prompt_kernel, steps=steps, sub_steps=sub_steps, n_seq=n_seq),
        grid=(n_blocks,),
        in_specs=[any_space] + [weight_spec(a) for a in mixer_w],
        out_specs=[row_block, whole_block(conv_state(D_MODEL)), whole_block(pool_state)] + [any_space] * 4,
        out_shape=[jax.ShapeDtypeStruct((n_blocks * rows, D_MODEL), F32),
                   jax.ShapeDtypeStruct(conv_state(D_MODEL), F32),
                   jax.ShapeDtypeStruct(pool_state, F32)] + bf16_like(mixer_mats),
        scratch_shapes=mixer_scratch,
        compiler_params=params(_vmem_limit(mixer_scratch, double_buffered=[row_block_shape])),
        name="mixer_prompt",
    )(x, *mixer_w)

    ffn_w = (wts["g_pre_ffn"], wts["w_up"], wts["ffn_conv_w"], wts["w_down"], wts["g_post_ffn"],
             wts["w_ple_proj"], wts["w_ple_gate"])
    ffn_mats = (wts["w_up"], wts["w_down"], wts["w_ple_proj"], wts["w_ple_gate"])
    ffn_scratch = ([pltpu.VMEM((2, steps, n_seq, PLE_DIM), F32), pltpu.SemaphoreType.DMA((2, n_seq)),
                    pltpu.VMEM((2, steps, n_seq, D_MODEL), F32), pltpu.SemaphoreType.DMA((2, n_seq)),
                    pltpu.VMEM((N_CONV * n_seq + sub_rows, D_FF), F32)]
                   + vmem_bf16(ffn_mats) + stage(ffn_mats))
    y, nffn, *ffn_bf16 = pl.pallas_call(
        functools.partial(_ffn_prompt_kernel, steps=steps, sub_steps=sub_steps, n_seq=n_seq),
        grid=(n_blocks,),
        in_specs=[row_block, any_space] + [weight_spec(a) for a in ffn_w],
        out_specs=[any_space, whole_block(conv_state(D_FF))] + [any_space] * 4,
        out_shape=[jax.ShapeDtypeStruct((n_seq, seq_len, D_MODEL), F32),
                   jax.ShapeDtypeStruct(conv_state(D_FF), F32)] + bf16_like(ffn_mats),
        scratch_shapes=ffn_scratch,
        compiler_params=params(_vmem_limit(ffn_scratch, double_buffered=[row_block_shape])),
        name="ffn_prompt",
    )(x1, p, *ffn_w)
    bf16_wts = dict(wts, **dict(zip(("w_in", "w_a_out", "pool_w", "w_o"), mixer_bf16)),
                    **dict(zip(("w_up", "w_down", "w_ple_proj", "w_ple_gate"), ffn_bf16)))
    return y, nconv[None], jnp.transpose(npool, (1, 0, 2))[None], nffn[None], bf16_wts


def _mixer_sample_kernel(x_ref, sconv_ref, spool_ref, g_pre, w_in, conv_w, w_a_out, pool_w, pool_scale, w_o,
                         g_post, x1_ref, nconv_ref, npool_ref, cbuf, ubuf, *, batch, steps, sub_steps):
    sub_rows = sub_steps * batch
    for k in range(N_CONV):
        cbuf[k * batch:(k + 1) * batch, :] = sconv_ref[:, k, :]

    w = dict(g_pre=g_pre, w_in=w_in, w_a_out=w_a_out, pool_w=pool_w, pool_scale=pool_scale, w_o=w_o,
             g_post=g_post)
    def conv_in(sb, j, v):
        cols = slice(j * CHUNK, (j + 1) * CHUNK)
        return _causal_conv(cbuf, cols, v, conv_w, N_CONV * batch + sb * sub_rows, batch, sub_rows)

    def pool_diff(sb, j, u):
        t0 = sb * sub_steps
        cols = slice(j * CHUNK, (j + 1) * CHUNK)
        window = POOL_WINDOWS[j]
        ubuf[t0:t0 + sub_steps, :, cols] = u.reshape(sub_steps, batch, CHUNK)

        def ext(e):
            return spool_ref[e, :, cols] if e < N_POOL else ubuf[e - N_POOL, :, cols]

        diffs = []
        for t in range(t0, t0 + sub_steps):
            tsum = ext(N_POOL + t - window + 1)
            for e in range(N_POOL + t - window + 2, N_POOL + t + 1):
                tsum = tsum + ext(e)
            diffs.append(tsum * (1.0 / window) - ext(N_POOL + t))
        return jnp.concatenate(diffs, axis=0)

    def load_x(sb):
        return jnp.concatenate([x_ref[:, t, :] for t in range(sb * sub_steps, (sb + 1) * sub_steps)], axis=0)

    def store_x1(sb, value):
        x1_ref[sb * sub_rows:(sb + 1) * sub_rows, :] = value

    _mixer_blocks(steps // sub_steps, load_x, store_x1, w, conv_in, pool_diff)

    for k in range(N_CONV):
        nconv_ref[:, k, :] = cbuf[(steps + k) * batch:(steps + k + 1) * batch, :]
    for m in range(N_POOL):
        e = steps + m
        npool_ref[m] = spool_ref[e] if e < N_POOL else ubuf[e - N_POOL]


def _ffn_sample_kernel(x_ref, p_ref, sffn_ref, g_pre, w_up, conv_w, w_down, g_post, w_proj, w_gate,
                       y_ref, nffn_ref, fbuf, *, batch, steps, sub_steps):
    sub_rows = sub_steps * batch
    for k in range(N_CONV):
        fbuf[k * batch:(k + 1) * batch, :] = sffn_ref[:, k, :]

    w = dict(g_pre=g_pre, w_up=w_up, w_down=w_down, g_post=g_post, w_proj=w_proj, w_gate=w_gate)
    def conv_in(sb, j, v):
        cols = slice(j * CHUNK, (j + 1) * CHUNK)
        return _causal_conv(fbuf, cols, v, conv_w, N_CONV * batch + sb * sub_rows, batch, sub_rows)

    def load_x(sb):
        return x_ref[sb * sub_rows:(sb + 1) * sub_rows, :]

    def load_p(sb):
        return jnp.concatenate([p_ref[:, t, :] for t in range(sb * sub_steps, (sb + 1) * sub_steps)], axis=0)

    def store_y(sb, value):
        for t in range(sub_steps):
            y_ref[:, sb * sub_steps + t, :] = value[t * batch:(t + 1) * batch, :]

    _ffn_blocks(steps // sub_steps, load_x, load_p, store_y, w, conv_in)

    for k in range(N_CONV):
        nffn_ref[:, k, :] = fbuf[(steps + k) * batch:(steps + k + 1) * batch, :]


def _sample_layer(x, p, sconv, spool, sffn, layer, wts):
    batch, steps, _ = x.shape
    sub_steps = max(1, SAMPLE_SUB_ROWS // batch)
    assert steps % sub_steps == 0 and batch % 8 == 0
    rows = batch * steps
    params = lambda limit: pltpu.CompilerParams(vmem_limit_bytes=limit)
    whole = pl.BlockSpec(memory_space=pltpu.VMEM)
    of_layer = lambda a, k: pl.BlockSpec((None,) + a.shape[1:], lambda: (k, 0, 0, 0))
    one_layer = lambda a: jax.ShapeDtypeStruct((1,) + a.shape[1:], F32)
    x1_shape = jax.ShapeDtypeStruct((rows, D_MODEL), F32)

    mixer_w = (wts["g_pre_mix"], wts["w_in"], wts["conv_a_w"], wts["w_a_out"], wts["pool_w"],
               wts["pool_scale"], wts["w_o"], wts["g_post_mix"])
    mixer_scratch = [pltpu.VMEM(((N_CONV + steps) * batch, D_MODEL), F32), pltpu.VMEM((steps, batch, D_MODEL), F32)]
    mixer_out = [x1_shape, one_layer(sconv), one_layer(spool)]
    x1, nconv, npool = pl.pallas_call(
        functools.partial(_mixer_sample_kernel, batch=batch, steps=steps, sub_steps=sub_steps),
        in_specs=[whole, of_layer(sconv, layer), of_layer(spool, layer)] + [whole] * len(mixer_w),
        out_specs=[whole, of_layer(sconv, 0), of_layer(spool, 0)],
        out_shape=mixer_out,
        scratch_shapes=mixer_scratch,
        compiler_params=params(_vmem_limit(mixer_scratch, single_buffered=[x, *mixer_out, *mixer_out[1:], *mixer_w])),
        name="mixer_sample",
    )(x, sconv, spool, *mixer_w)

    ffn_w = (wts["g_pre_ffn"], wts["w_up"], wts["ffn_conv_w"], wts["w_down"], wts["g_post_ffn"],
             wts["w_ple_proj"], wts["w_ple_gate"])
    ffn_scratch = [pltpu.VMEM(((N_CONV + steps) * batch, D_FF), F32)]
    ffn_out = [jax.ShapeDtypeStruct(x.shape, F32), one_layer(sffn)]
    y, nffn = pl.pallas_call(
        functools.partial(_ffn_sample_kernel, batch=batch, steps=steps, sub_steps=sub_steps),
        in_specs=[whole, of_layer(p, layer), of_layer(sffn, layer)] + [whole] * len(ffn_w),
        out_specs=[whole, of_layer(sffn, 0)],
        out_shape=ffn_out,
        scratch_shapes=ffn_scratch,
        compiler_params=params(_vmem_limit(
            ffn_scratch, single_buffered=[x1_shape, one_layer(p), *ffn_out, ffn_out[1], *ffn_w])),
        name="ffn_sample",
    )(x1, p, sffn, *ffn_w)
    return y, nconv, npool, nffn


def kernel(x_prompt, x_sample, p_prompt, p_sample, state_conv_a, state_pool, state_ffn_conv, g_pre_mix, w_in,
           conv_a_w, w_a_out, pool_w, pool_scale, w_o, g_post_mix, g_pre_ffn, w_up, ffn_conv_w, w_down,
           g_post_ffn, w_ple_proj, w_ple_gate):
    depth = w_in.shape[0]
    xp, xs = x_prompt, x_sample
    prompt_states, sample_states = [], []
    time_major = lambda s: jnp.transpose(s, (0, 2, 1, 3))
    state_pool_tm = time_major(state_pool)
    for i in range(depth):
        wts = dict(
            g_pre_mix=g_pre_mix[i][None], g_post_mix=g_post_mix[i][None], g_pre_ffn=g_pre_ffn[i][None],
            g_post_ffn=g_post_ffn[i][None], pool_scale=pool_scale[i][None], conv_a_w=conv_a_w[i],
            ffn_conv_w=ffn_conv_w[i],
            w_in=w_in[i], w_a_out=w_a_out[i], pool_w=pool_w[i].reshape(D_MODEL, CHUNK),
            w_o=w_o[i], w_up=w_up[i], w_down=w_down[i], w_ple_proj=w_ple_proj[i], w_ple_gate=w_ple_gate[i])
        xp, c1, p1, f1, bf16_wts = _prompt_layer(xp, p_prompt[i], wts)
        xs, c2, p2, f2 = _sample_layer(xs, p_sample, state_conv_a, state_pool_tm, state_ffn_conv, i, bf16_wts)
        prompt_states.append((c1, p1, f1))
        sample_states.append((c2, time_major(p2), f2))
    cat = lambda states, k: states[0][k] if depth == 1 else jnp.concatenate([s[k] for s in states], axis=0)
    return (xp, xs, cat(prompt_states, 0), cat(prompt_states, 1), cat(prompt_states, 2),
            cat(sample_states, 0), cat(sample_states, 1), cat(sample_states, 2))
```

```python
import functools
import math

import jax
import jax.numpy as jnp
from jax import lax
from jax.experimental import pallas as pl
from jax.experimental.pallas import tpu as pltpu

D_MODEL = 1024
D_FF = 2816
PLE_DIM = 256
CONV_W = 3
POOL_WINDOWS = (2, 4, 8, 16)
POOL_MAX = 16
EPS = 1e-6

N_CONV = CONV_W - 1
N_POOL = POOL_MAX - 1
CHUNK = 256
N_MIX_CHUNKS = D_MODEL // CHUNK
N_FF_CHUNKS = D_FF // CHUNK
TIME_STEPS = 64
SUB_STEPS = 32
SAMPLE_SUB_ROWS = 256
MIXER_NORM_AHEAD_CHUNK = 2
FFN_NORM_AHEAD_CHUNK = 8
FFN_FINISH_CHUNK = 1
WEIGHT_CHUNK_ROWS = 128
WEIGHT_STAGE_SLOTS = 4
VMEM_CAPACITY_BYTES = 64 * 1024 * 1024
VMEM_COMPILER_SCRATCH_BYTES = 6 * 1024 * 1024

F32 = jnp.float32
BF16 = jnp.bfloat16


def _nbytes(a):
    return math.prod(a.shape) * jnp.dtype(a.dtype).itemsize


def _vmem_limit(scratch, single_buffered=(), double_buffered=()):
    total = sum(_nbytes(s) for s in scratch if s.memory_space == pltpu.VMEM)
    total += sum(_nbytes(a) for a in single_buffered) + 2 * sum(_nbytes(a) for a in double_buffered)
    return min(total + VMEM_COMPILER_SCRATCH_BYTES, VMEM_CAPACITY_BYTES - 4 * 1024 * 1024)


def _rmsnorm(x, g):
    ms = jnp.mean(x * x, axis=-1, keepdims=True)
    return (x * lax.rsqrt(ms + EPS)) * g


def _dot(a, b):
    return jnp.dot(a, b, preferred_element_type=F32)


def _causal_conv(buf, cols, v, w_ref, halo, step, rows):
    buf[pl.ds(halo, rows), cols] = v
    y = buf[pl.ds(halo - 2 * step, rows), cols] * w_ref[0:1, cols]
    y = y + buf[pl.ds(halo - step, rows), cols] * w_ref[1:2, cols]
    return y + v * w_ref[2:3, cols]


def _mixer_blocks(n_sub, load_x, store_out, w, conv_in, pool_diff):
    def normed(sb):
        x = load_x(sb)
        return x, _rmsnorm(x, w["g_pre"][...]).astype(BF16)

    def finish(sb, x, mix):
        store_out(sb, x + _rmsnorm(mix, w["g_post"][...]))

    cur = normed(0)
    pending = None
    for sb in range(n_sub):
        x, xb = cur

        def proj(k, j, xb=xb):
            lo = k * D_MODEL + j * CHUNK
            return _dot(xb, w["w_in"][:, lo:lo + CHUNK])

        def branch_proj(j):
            return tuple(proj(k, j) for k in range(4))

        def gate_proj(j):
            return proj(4, j), proj(5, j)

        nxt = branch_proj(0)
        y_a = None
        y_p = []
        for j in range(N_MIX_CHUNKS):
            cols = slice(j * CHUNK, (j + 1) * CHUNK)
            b, c, h, u = nxt
            nxt = branch_proj(j + 1) if j + 1 < N_MIX_CHUNKS else gate_proj(0)
            if j == 0 and pending is not None:
                finish(*pending)
            if j == MIXER_NORM_AHEAD_CHUNK and sb + 1 < n_sub:
                cur = normed(sb + 1)
            z = (b * conv_in(sb, j, c * h)).astype(BF16)
            t = _dot(z, w["w_a_out"][cols, :])
            y_a = t if y_a is None else y_a + t
            d = pool_diff(sb, j, u).astype(BF16)
            y_p.append(_dot(d, w["pool_w"][cols, :]) * w["pool_scale"][:, cols])
        mix = None
        for j in range(N_MIX_CHUNKS):
            cols = slice(j * CHUNK, (j + 1) * CHUNK)
            ga, gp = nxt
            if j + 1 < N_MIX_CHUNKS:
                nxt = gate_proj(j + 1)
            merged = jax.nn.sigmoid(ga) * y_a[:, cols] + jax.nn.sigmoid(gp) * y_p[j]
            t = _dot(merged.astype(BF16), w["w_o"][cols, :])
            mix = t if mix is None else mix + t
        pending = (sb, x, mix)
    finish(*pending)


def _ffn_blocks(n_sub, load_x, load_p, store_out, w, conv_in):
    def normed(sb):
        x = load_x(sb)
        return x, _rmsnorm(x, w["g_pre"][...]).astype(BF16)

    def finish(sb, x, f, ple):
        x = x + _rmsnorm(f, w["g_post"][...])
        gate = jax.nn.sigmoid(_dot(x.astype(BF16), w["w_gate"][...]))
        store_out(sb, x + gate * ple)

    cur = normed(0)
    pending = None
    for sb in range(n_sub):
        x, hb = cur

        def up(j, hb=hb):
            return (_dot(hb, w["w_up"][:, j * CHUNK:(j + 1) * CHUNK]),
                    _dot(hb, w["w_up"][:, D_FF + j * CHUNK:D_FF + (j + 1) * CHUNK]))

        ple = _dot(load_p(sb).astype(BF16), w["w_proj"][...])
        nxt = up(0)
        f = None
        for j in range(N_FF_CHUNKS):
            cols = slice(j * CHUNK, (j + 1) * CHUNK)
            a, g = nxt
            if j + 1 < N_FF_CHUNKS:
                nxt = up(j + 1)
            if j == FFN_FINISH_CHUNK and pending is not None:
                finish(*pending)
            if j == FFN_NORM_AHEAD_CHUNK and sb + 1 < n_sub:
                cur = normed(sb + 1)
            h = jax.nn.gelu(conv_in(sb, j, a), approximate=True) * g
            t = _dot(h.astype(BF16), w["w_down"][cols, :])
            f = t if f is None else f + t
        pending = (sb, x, f, ple)
    finish(*pending)


def _block_copy(hbm, buf, sem, block, slot, seq, steps, *, to_hbm=False):
    hbm_view = hbm.at[seq, pl.ds(block * steps, steps), :]
    buf_view = buf.at[slot, :, seq, :]
    src, dst = (buf_view, hbm_view) if to_hbm else (hbm_view, buf_view)
    return pltpu.make_async_copy(src, dst, sem.at[slot, seq])


def _fetch_block(hbm, buf, sem, steps, n_seq):
    i = pl.program_id(0)
    slot = i % 2

    def start(block, slot):
        for s in range(n_seq):
            _block_copy(hbm, buf, sem, block, slot, s, steps).start()

    @pl.when(i == 0)
    def _():
        start(0, 0)

    @pl.when(i + 1 < pl.num_programs(0))
    def _():
        start(i + 1, 1 - slot)

    for s in range(n_seq):
        _block_copy(hbm, buf, sem, i, slot, s, steps).wait()
    return slot


def _stream_weight_as_bf16(w_hbm, w_vmem, stage, sem):
    n_rows, width = w_hbm.shape
    n_slots, chunk, _ = stage.shape
    n_chunks = n_rows // chunk
    assert n_chunks * chunk == n_rows and width <= stage.shape[2]

    def chunk_copy(c, slot):
        return pltpu.make_async_copy(w_hbm.at[pl.ds(c * chunk, chunk), :], stage.at[slot, :, 0:width], sem.at[slot])

    for c in range(min(n_slots, n_chunks)):
        chunk_copy(c, c).start()

    def body(c, carry):
        slot = c % n_slots
        chunk_copy(c, slot).wait()
        w_vmem[pl.ds(pl.multiple_of(c * chunk, chunk), chunk), :] = stage[slot, :, 0:width].astype(BF16)

        @pl.when(c + n_slots < n_chunks)
        def _():
            chunk_copy(c + n_slots, slot).start()

        return carry

    lax.fori_loop(0, n_chunks, body, 0)


def _load_weights(pairs, exports, stage, stage_sem, export_sem):
    for w_hbm, w_vmem in pairs:
        _stream_weight_as_bf16(w_hbm, w_vmem, stage, stage_sem)
    for k, ((_, w_vmem), w_out) in enumerate(zip(pairs, exports)):
        pltpu.make_async_copy(w_vmem, w_out, export_sem.at[k]).start()


def _finish_weight_exports(pairs, exports, export_sem):
    for k, ((_, w_vmem), w_out) in enumerate(zip(pairs, exports)):
        pltpu.make_async_copy(w_vmem, w_out, export_sem.at[k]).wait()


def _mixer_prompt_kernel(x_hbm, g_pre, w_in_hbm, conv_w, w_a_out_hbm, pool_w_hbm, pool_scale, w_o_hbm, g_post,
                         x1_ref, nconv_ref, npool_ref, w_in_out, w_a_out_out, pool_w_out, w_o_out,
                         xbuf, xsem, cbuf, ubuf, e0, e1, e2, inv_cnt,
                         w_in, w_a_out, pool_w, w_o, stage, stage_sem, export_sem,
                         *, steps, sub_steps, n_seq):
    i = pl.program_id(0)
    weight_pairs = ((w_in_hbm, w_in), (w_a_out_hbm, w_a_out), (pool_w_hbm, pool_w), (w_o_hbm, w_o))
    weight_exports = (w_in_out, w_a_out_out, pool_w_out, w_o_out)
    sub_rows = sub_steps * n_seq
    n_sub = steps // sub_steps
    conv_halo = N_CONV * n_seq
    pool_halo = N_POOL * n_seq
    head = POOL_MAX * n_seq
    ebufs = (e0, e1, e2)

    @pl.when(i == 0)
    def _():
        cbuf[0:conv_halo, :] = jnp.zeros((conv_halo, D_MODEL), F32)
        ubuf[0:pool_halo, :] = jnp.zeros((pool_halo, D_MODEL), F32)
        pos = lax.broadcasted_iota(jnp.int32, (POOL_MAX, n_seq, CHUNK), 0).reshape(head, CHUNK)
        for j, window in enumerate(POOL_WINDOWS):
            inv_cnt[j] = 1.0 / jnp.minimum(pos + 1, window).astype(F32)
        _load_weights(weight_pairs, weight_exports, stage, stage_sem, export_sem)

    @pl.when(i == 1)
    def _():
        for j, window in enumerate(POOL_WINDOWS):
            inv_cnt[j] = jnp.full((head, CHUNK), 1.0 / window, F32)

    slot = _fetch_block(x_hbm, xbuf, xsem, steps, n_seq)
    w = dict(g_pre=g_pre, w_in=w_in, w_a_out=w_a_out, pool_w=pool_w, pool_scale=pool_scale, w_o=w_o,
             g_post=g_post)

    def conv_in(sb, j, v):
        cols = slice(j * CHUNK, (j + 1) * CHUNK)
        y = _causal_conv(cbuf, cols, v, conv_w, conv_halo, n_seq, sub_rows)
        cbuf[0:conv_halo, cols] = cbuf[pl.ds(sub_rows, conv_halo), cols]
        return y

    def pool_diff(sb, j, u):
        cols = slice(j * CHUNK, (j + 1) * CHUNK)
        window = POOL_WINDOWS[j]
        ubuf[pl.ds(pool_halo, sub_rows), cols] = u
        n_stages = j + 1
        tsum = None
        for k in range(n_stages):
            lo = pool_halo - (window - (2 << k)) * n_seq
            n = pool_halo + sub_rows - lo
            shift = (1 << k) * n_seq
            if k == 0:
                tsum = ubuf[pl.ds(lo, n), cols] + ubuf[pl.ds(lo - shift, n), cols]
            else:
                tsum = ebufs[k - 1][pl.ds(lo, n), cols] + ebufs[k - 1][pl.ds(lo - shift, n), cols]
            if k < n_stages - 1:
                ebufs[k][pl.ds(lo, n), cols] = tsum
        keep = (window - 1) * n_seq
        ubuf[pl.ds(pool_halo - keep, keep), cols] = ubuf[pl.ds(pool_halo + sub_rows - keep, keep), cols]
        if sb > 0:
            return tsum * (1.0 / window) - u
        return jnp.concatenate([tsum[0:head, :] * inv_cnt[j] - u[0:head, :],
                                tsum[head:, :] * (1.0 / window) - u[head:, :]], axis=0)

    def load_x(sb):
        return xbuf[slot, sb * sub_steps:(sb + 1) * sub_steps].reshape(sub_rows, D_MODEL)

    def store_x1(sb, value):
        x1_ref[sb * sub_rows:(sb + 1) * sub_rows, :] = value

    _mixer_blocks(n_sub, load_x, store_x1, w, conv_in, pool_diff)

    @pl.when(i == pl.num_programs(0) - 1)
    def _():
        for k in range(N_CONV):
            nconv_ref[:, k, :] = cbuf[k * n_seq:(k + 1) * n_seq, :]
        npool_ref[...] = ubuf[pl.ds(sub_rows, pool_halo), :].reshape(N_POOL, n_seq, D_MODEL)
        _finish_weight_exports(weight_pairs, weight_exports, export_sem)


def _ffn_prompt_kernel(x_ref, p_hbm, g_pre, w_up_hbm, conv_w, w_down_hbm, g_post, w_proj_hbm, w_gate_hbm,
                       y_hbm, nffn_ref, w_up_out, w_down_out, w_proj_out, w_gate_out,
                       pbuf, psem, ybuf, ysem, fbuf,
                       w_up, w_down, w_proj, w_gate, stage, stage_sem, export_sem,
                       *, steps, sub_steps, n_seq):
    i = pl.program_id(0)
    last = pl.num_programs(0) - 1
    sub_rows = sub_steps * n_seq
    n_sub = steps // sub_steps
    conv_halo = N_CONV * n_seq
    weight_pairs = ((w_up_hbm, w_up), (w_down_hbm, w_down), (w_proj_hbm, w_proj), (w_gate_hbm, w_gate))
    weight_exports = (w_up_out, w_down_out, w_proj_out, w_gate_out)

    @pl.when(i == 0)
    def _():
        fbuf[0:conv_halo, :] = jnp.zeros((conv_halo, D_FF), F32)
        _load_weights(weight_pairs, weight_exports, stage, stage_sem, export_sem)

    def wait_store(block, slot):
        for s in range(n_seq):
            _block_copy(y_hbm, ybuf, ysem, block, slot, s, steps, to_hbm=True).wait()

    slot = _fetch_block(p_hbm, pbuf, psem, steps, n_seq)

    @pl.when(i >= 2)
    def _():
        wait_store(i - 2, slot)

    w = dict(g_pre=g_pre, w_up=w_up, w_down=w_down, g_post=g_post, w_proj=w_proj, w_gate=w_gate)

    def conv_in(sb, j, v):
        cols = slice(j * CHUNK, (j + 1) * CHUNK)
        y = _causal_conv(fbuf, cols, v, conv_w, conv_halo, n_seq, sub_rows)
        fbuf[0:conv_halo, cols] = fbuf[pl.ds(sub_rows, conv_halo), cols]
        return y

    def load_x(sb):
        return x_ref[sb * sub_rows:(sb + 1) * sub_rows, :]

    def load_p(sb):
        return pbuf[slot, sb * sub_steps:(sb + 1) * sub_steps].reshape(sub_rows, PLE_DIM)

    def store_y(sb, value):
        ybuf[slot, sb * sub_steps:(sb + 1) * sub_steps] = value.reshape(sub_steps, n_seq, D_MODEL)

    _ffn_blocks(n_sub, load_x, load_p, store_y, w, conv_in)

    for s in range(n_seq):
        _block_copy(y_hbm, ybuf, ysem, i, slot, s, steps, to_hbm=True).start()

    @pl.when(i == last)
    def _():
        for k in range(N_CONV):
            nffn_ref[:, k, :] = fbuf[k * n_seq:(k + 1) * n_seq, :]
        wait_store(i - 1, 1 - slot)
        wait_store(i, slot)
        _finish_weight_exports(weight_pairs, weight_exports, export_sem)


def _resident(shape):
    return pl.BlockSpec(shape, lambda i: (0,) * len(shape), pipeline_mode=pl.Buffered(1))


def _prompt_layer(x, p, wts):
    n_seq, seq_len, _ = x.shape
    steps, sub_steps = TIME_STEPS, SUB_STEPS
    n_blocks = seq_len // steps
    assert n_blocks * steps == seq_len and n_blocks >= 2 and n_seq % 8 == 0
    assert steps % sub_steps == 0 and sub_steps > POOL_MAX
    rows = steps * n_seq
    params = lambda limit: pltpu.CompilerParams(dimension_semantics=("arbitrary",), vmem_limit_bytes=limit)
    any_space = pl.BlockSpec(memory_space=pl.ANY)
    row_block = pl.BlockSpec((rows, D_MODEL), lambda i: (i, 0))
    row_block_shape = jax.ShapeDtypeStruct((rows, D_MODEL), F32)
    whole_block = lambda shape: pl.BlockSpec(shape, lambda i: (0,) * len(shape))
    conv_state = lambda width: (n_seq, N_CONV, width)
    pool_state = (N_POOL, n_seq, D_MODEL)

    weight_spec = lambda a: any_space if a.dtype == F32 and a.shape[0] >= WEIGHT_CHUNK_ROWS else _resident(a.shape)
    bf16_like = lambda ws: [jax.ShapeDtypeStruct(a.shape, BF16) for a in ws]
    vmem_bf16 = lambda ws: [pltpu.VMEM(a.shape, BF16) for a in ws]
    stage = lambda ws: [pltpu.VMEM((WEIGHT_STAGE_SLOTS, WEIGHT_CHUNK_ROWS, max(a.shape[1] for a in ws)), F32),
                        pltpu.SemaphoreType.DMA((WEIGHT_STAGE_SLOTS,)), pltpu.SemaphoreType.DMA((len(ws),))]

    mixer_w = (wts["g_pre_mix"], wts["w_in"], wts["conv_a_w"], wts["w_a_out"], wts["pool_w"],
               wts["pool_scale"], wts["w_o"], wts["g_post_mix"])
    mixer_mats = (wts["w_in"], wts["w_a_out"], wts["pool_w"], wts["w_o"])
    sub_rows = sub_steps * n_seq
    stage_rows = N_POOL * n_seq + sub_rows
    mixer_scratch = ([pltpu.VMEM((2, steps, n_seq, D_MODEL), F32), pltpu.SemaphoreType.DMA((2, n_seq)),
                      pltpu.VMEM((N_CONV * n_seq + sub_rows, D_MODEL), F32),
                      pltpu.VMEM((stage_rows, D_MODEL), F32),
                      pltpu.VMEM((stage_rows, D_MODEL), F32),
                      pltpu.VMEM((stage_rows, D_MODEL), F32),
                      pltpu.VMEM((stage_rows, D_MODEL), F32),
                      pltpu.VMEM((len(POOL_WINDOWS), POOL_MAX * n_seq, CHUNK), F32)]
                     + vmem_bf16(mixer_mats) + stage(mixer_mats))
    x1, nconv, npool, *mixer_bf16 = pl.pallas_call(
        functools.partial(_mixer_prompt_kernel, steps=steps, sub_steps=sub_steps, n_seq=n_seq),
        grid=(n_blocks,),
        in_specs=[any_space] + [weight_spec(a) for a in mixer_w],
        out_specs=[row_block, whole_block(conv_state(D_MODEL)), whole_block(pool_state)] + [any_space] * 4,
        out_shape=[jax.ShapeDtypeStruct((n_blocks * rows, D_MODEL), F32),
                   jax.ShapeDtypeStruct(conv_state(D_MODEL), F32),
                   jax.ShapeDtypeStruct(pool_state, F32)] + bf16_like(mixer_mats),
        scratch_shapes=mixer_scratch,
        compiler_params=params(_vmem_limit(mixer_scratch, double_buffered=[row_block_shape])),
        name="mixer_prompt",
    )(x, *mixer_w)

    ffn_w = (wts["g_pre_ffn"], wts["w_up"], wts["ffn_conv_w"], wts["w_down"], wts["g_post_ffn"],
             wts["w_ple_proj"], wts["w_ple_gate"])
    ffn_mats = (wts["w_up"], wts["w_down"], wts["w_ple_proj"], wts["w_ple_gate"])
    ffn_scratch = ([pltpu.VMEM((2, steps, n_seq, PLE_DIM), F32), pltpu.SemaphoreType.DMA((2, n_seq)),
                    pltpu.VMEM((2, steps, n_seq, D_MODEL), F32), pltpu.SemaphoreType.DMA((2, n_seq)),
                    pltpu.VMEM((N_CONV * n_seq + sub_rows, D_FF), F32)]
                   + vmem_bf16(ffn_mats) + stage(ffn_mats))
    y, nffn, *ffn_bf16 = pl.pallas_call(
        functools.partial(_ffn_prompt_kernel, steps=steps, sub_steps=sub_steps, n_seq=n_seq),
        grid=(n_blocks,),
        in_specs=[row_block, any_space] + [weight_spec(a) for a in ffn_w],
        out_specs=[any_space, whole_block(conv_state(D_FF))] + [any_space] * 4,
        out_shape=[jax.ShapeDtypeStruct((n_seq, seq_len, D_MODEL), F32),
                   jax.ShapeDtypeStruct(conv_state(D_FF), F32)] + bf16_like(ffn_mats),
        scratch_shapes=ffn_scratch,
        compiler_params=params(_vmem_limit(ffn_scratch, double_buffered=[row_block_shape])),
        name="ffn_prompt",
    )(x1, p, *ffn_w)
    bf16_wts = dict(wts, **dict(zip(("w_in", "w_a_out", "pool_w", "w_o"), mixer_bf16)),
                    **dict(zip(("w_up", "w_down", "w_ple_proj", "w_ple_gate"), ffn_bf16)))
    return y, nconv[None], jnp.transpose(npool, (1, 0, 2))[None], nffn[None], bf16_wts


def _mixer_sample_kernel(x_ref, sconv_ref, spool_ref, g_pre, w_in, conv_w, w_a_out, pool_w, pool_scale, w_o,
                         g_post, x1_ref, nconv_ref, npool_ref, cbuf, ubuf, *, batch, steps, sub_steps):
    sub_rows = sub_steps * batch
    for k in range(N_CONV):
        cbuf[k * batch:(k + 1) * batch, :] = sconv_ref[:, k, :]

    w = dict(g_pre=g_pre, w_in=w_in, w_a_out=w_a_out, pool_w=pool_w, pool_scale=pool_scale, w_o=w_o,
             g_post=g_post)
    def conv_in(sb, j, v):
        cols = slice(j * CHUNK, (j + 1) * CHUNK)
        return _causal_conv(cbuf, cols, v, conv_w, N_CONV * batch + sb * sub_rows, batch, sub_rows)

    def pool_diff(sb, j, u):
        t0 = sb * sub_steps
        cols = slice(j * CHUNK, (j + 1) * CHUNK)
        window = POOL_WINDOWS[j]
        ubuf[t0:t0 + sub_steps, :, cols] = u.reshape(sub_steps, batch, CHUNK)

        def ext(e):
            return spool_ref[e, :, cols] if e < N_POOL else ubuf[e - N_POOL, :, cols]

        diffs = []
        for t in range(t0, t0 + sub_steps):
            tsum = ext(N_POOL + t - window + 1)
            for e in range(N_POOL + t - window + 2, N_POOL + t + 1):
                tsum = tsum + ext(e)
            diffs.append(tsum * (1.0 / window) - ext(N_POOL + t))
        return jnp.concatenate(diffs, axis=0)

    def load_x(sb):
        return jnp.concatenate([x_ref[:, t, :] for t in range(sb * sub_steps, (sb + 1) * sub_steps)], axis=0)

    def store_x1(sb, value):
        x1_ref[sb * sub_rows:(sb + 1) * sub_rows, :] = value

    _mixer_blocks(steps // sub_steps, load_x, store_x1, w, conv_in, pool_diff)

    for k in range(N_CONV):
        nconv_ref[:, k, :] = cbuf[(steps + k) * batch:(steps + k + 1) * batch, :]
    for m in range(N_POOL):
        e = steps + m
        npool_ref[m] = spool_ref[e] if e < N_POOL else ubuf[e - N_POOL]


def _ffn_sample_kernel(x_ref, p_ref, sffn_ref, g_pre, w_up, conv_w, w_down, g_post, w_proj, w_gate,
                       y_ref, nffn_ref, fbuf, *, batch, steps, sub_steps):
    sub_rows = sub_steps * batch
    for k in range(N_CONV):
        fbuf[k * batch:(k + 1) * batch, :] = sffn_ref[:, k, :]

    w = dict(g_pre=g_pre, w_up=w_up, w_down=w_down, g_post=g_post, w_proj=w_proj, w_gate=w_gate)
    def conv_in(sb, j, v):
        cols = slice(j * CHUNK, (j + 1) * CHUNK)
        return _causal_conv(fbuf, cols, v, conv_w, N_CONV * batch + sb * sub_rows, batch, sub_rows)

    def load_x(sb):
        return x_ref[sb * sub_rows:(sb + 1) * sub_rows, :]

    def load_p(sb):
        return jnp.concatenate([p_ref[:, t, :] for t in range(sb * sub_steps, (sb + 1) * sub_steps)], axis=0)

    def store_y(sb, value):
        for t in range(sub_steps):
            y_ref[:, sb * sub_steps + t, :] = value[t * batch:(t + 1) * batch, :]

    _ffn_blocks(steps // sub_steps, load_x, load_p, store_y, w, conv_in)

    for k in range(N_CONV):
        nffn_ref[:, k, :] = fbuf[(steps + k) * batch:(steps + k + 1) * batch, :]


def _sample_layer(x, p, sconv, spool, sffn, layer, wts):
    batch, steps, _ = x.shape
    sub_steps = max(1, SAMPLE_SUB_ROWS // batch)
    assert steps % sub_steps == 0 and batch % 8 == 0
    rows = batch * steps
    params = lambda limit: pltpu.CompilerParams(vmem_limit_bytes=limit)
    whole = pl.BlockSpec(memory_space=pltpu.VMEM)
    of_layer = lambda a, k: pl.BlockSpec((None,) + a.shape[1:], lambda: (k, 0, 0, 0))
    one_layer = lambda a: jax.ShapeDtypeStruct((1,) + a.shape[1:], F32)
    x1_shape = jax.ShapeDtypeStruct((rows, D_MODEL), F32)

    mixer_w = (wts["g_pre_mix"], wts["w_in"], wts["conv_a_w"], wts["w_a_out"], wts["pool_w"],
               wts["pool_scale"], wts["w_o"], wts["g_post_mix"])
    mixer_scratch = [pltpu.VMEM(((N_CONV + steps) * batch, D_MODEL), F32), pltpu.VMEM((steps, batch, D_MODEL), F32)]
    mixer_out = [x1_shape, one_layer(sconv), one_layer(spool)]
    x1, nconv, npool = pl.pallas_call(
        functools.partial(_mixer_sample_kernel, batch=batch, steps=steps, sub_steps=sub_steps),
        in_specs=[whole, of_layer(sconv, layer), of_layer(spool, layer)] + [whole] * len(mixer_w),
        out_specs=[whole, of_layer(sconv, 0), of_layer(spool, 0)],
        out_shape=mixer_out,
        scratch_shapes=mixer_scratch,
        compiler_params=params(_vmem_limit(mixer_scratch, single_buffered=[x, *mixer_out, *mixer_out[1:], *mixer_w])),
        name="mixer_sample",
    )(x, sconv, spool, *mixer_w)

    ffn_w = (wts["g_pre_ffn"], wts["w_up"], wts["ffn_conv_w"], wts["w_down"], wts["g_post_ffn"],
             wts["w_ple_proj"], wts["w_ple_gate"])
    ffn_scratch = [pltpu.VMEM(((N_CONV + steps) * batch, D_FF), F32)]
    ffn_out = [jax.ShapeDtypeStruct(x.shape, F32), one_layer(sffn)]
    y, nffn = pl.pallas_call(
        functools.partial(_ffn_sample_kernel, batch=batch, steps=steps, sub_steps=sub_steps),
        in_specs=[whole, of_layer(p, layer), of_layer(sffn, layer)] + [whole] * len(ffn_w),
        out_specs=[whole, of_layer(sffn, 0)],
        out_shape=ffn_out,
        scratch_shapes=ffn_scratch,
        compiler_params=params(_vmem_limit(
            ffn_scratch, single_buffered=[x1_shape, one_layer(p), *ffn_out, ffn_out[1], *ffn_w])),
        name="ffn_sample",
    )(x1, p, sffn, *ffn_w)
    return y, nconv, npool, nffn


def kernel(x_prompt, x_sample, p_prompt, p_sample, state_conv_a, state_pool, state_ffn_conv, g_pre_mix, w_in,
           conv_a_w, w_a_out, pool_w, pool_scale, w_o, g_post_mix, g_pre_ffn, w_up, ffn_conv_w, w_down,
           g_post_ffn, w_ple_proj, w_ple_gate):
    depth = w_in.shape[0]
    xp, xs = x_prompt, x_sample
    prompt_states, sample_states = [], []
    time_major = lambda s: jnp.transpose(s, (0, 2, 1, 3))
    state_pool_tm = time_major(state_pool)
    for i in range(depth):
        wts = dict(
            g_pre_mix=g_pre_mix[i][None], g_post_mix=g_post_mix[i][None], g_pre_ffn=g_pre_ffn[i][None],
            g_post_ffn=g_post_ffn[i][None], pool_scale=pool_scale[i][None], conv_a_w=conv_a_w[i],
            ffn_conv_w=ffn_conv_w[i],
            w_in=w_in[i], w_a_out=w_a_out[i], pool_w=pool_w[i].reshape(D_MODEL, CHUNK),
            w_o=w_o[i], w_up=w_up[i], w_down=w_down[i], w_ple_proj=w_ple_proj[i], w_ple_gate=w_ple_gate[i])
        xp, c1, p1, f1, bf16_wts = _prompt_layer(xp, p_prompt[i], wts)
        xs, c2, p2, f2 = _sample_layer(xs, p_sample, state_conv_a, state_pool_tm, state_ffn_conv, i, bf16_wts)
        prompt_states.append((c1, p1, f1))
        sample_states.append((c2, time_major(p2), f2))
    cat = lambda states, k: states[0][k] if depth == 1 else jnp.concatenate([s[k] for s in states], axis=0)
    return (xp, xs, cat(prompt_states, 0), cat(prompt_states, 1), cat(prompt_states, 2),
            cat(sample_states, 0), cat(sample_states, 1), cat(sample_states, 2))
```

```python
import functools
import math

import jax
import jax.numpy as jnp
from jax import lax
from jax.experimental import pallas as pl
from jax.experimental.pallas import tpu as pltpu

D_MODEL = 1024
D_FF = 2816
PLE_DIM = 256
CONV_W = 3
POOL_WINDOWS = (2, 4, 8, 16)
POOL_MAX = 16
EPS = 1e-6

N_CONV = CONV_W - 1
N_POOL = POOL_MAX - 1
CHUNK = 256
N_MIX_CHUNKS = D_MODEL // CHUNK
N_FF_CHUNKS = D_FF // CHUNK
TIME_STEPS = 64
SUB_STEPS = 32
SAMPLE_SUB_ROWS = 256
MIXER_NORM_AHEAD_CHUNK = 2
FFN_NORM_AHEAD_CHUNK = 8
FFN_FINISH_CHUNK = 1
WEIGHT_CHUNK_ROWS = 128
WEIGHT_STAGE_SLOTS = 4
VMEM_CAPACITY_BYTES = 64 * 1024 * 1024
VMEM_COMPILER_SCRATCH_BYTES = 6 * 1024 * 1024
VMEM_UNREQUESTABLE_BYTES = 4 * 1024 * 1024

F32 = jnp.float32
BF16 = jnp.bfloat16


def _nbytes(a):
    return math.prod(a.shape) * jnp.dtype(a.dtype).itemsize


def _vmem_limit(scratch, single_buffered=(), double_buffered=()):
    total = sum(_nbytes(s) for s in scratch if s.memory_space == pltpu.VMEM)
    total += sum(_nbytes(a) for a in single_buffered) + 2 * sum(_nbytes(a) for a in double_buffered)
    return min(total + VMEM_COMPILER_SCRATCH_BYTES, VMEM_CAPACITY_BYTES - VMEM_UNREQUESTABLE_BYTES)


def _rmsnorm(x, g):
    ms = jnp.mean(x * x, axis=-1, keepdims=True)
    return (x * lax.rsqrt(ms + EPS)) * g


def _dot(a, b):
    return jnp.dot(a, b, preferred_element_type=F32)


def _causal_conv(buf, cols, v, w_ref, halo, step, rows):
    buf[pl.ds(halo, rows), cols] = v
    y = buf[pl.ds(halo - 2 * step, rows), cols] * w_ref[0:1, cols]
    y = y + buf[pl.ds(halo - step, rows), cols] * w_ref[1:2, cols]
    return y + v * w_ref[2:3, cols]


def _mixer_blocks(n_sub, load_x, store_out, w, conv_in, pool_diff):
    def normed(sb):
        x = load_x(sb)
        return x, _rmsnorm(x, w["g_pre"][...]).astype(BF16)

    def finish(sb, x, mix):
        store_out(sb, x + _rmsnorm(mix, w["g_post"][...]))

    cur = normed(0)
    pending = None
    for sb in range(n_sub):
        x, xb = cur

        def proj(k, j, xb=xb):
            lo = k * D_MODEL + j * CHUNK
            return _dot(xb, w["w_in"][:, lo:lo + CHUNK])

        def branch_proj(j):
            return tuple(proj(k, j) for k in range(4))

        def gate_proj(j):
            return proj(4, j), proj(5, j)

        nxt = branch_proj(0)
        y_a = None
        y_p = []
        for j in range(N_MIX_CHUNKS):
            cols = slice(j * CHUNK, (j + 1) * CHUNK)
            b, c, h, u = nxt
            nxt = branch_proj(j + 1) if j + 1 < N_MIX_CHUNKS else gate_proj(0)
            if j == 0 and pending is not None:
                finish(*pending)
            if j == MIXER_NORM_AHEAD_CHUNK and sb + 1 < n_sub:
                cur = normed(sb + 1)
            z = (b * conv_in(sb, j, c * h)).astype(BF16)
            t = _dot(z, w["w_a_out"][cols, :])
            y_a = t if y_a is None else y_a + t
            d = pool_diff(sb, j, u).astype(BF16)
            y_p.append(_dot(d, w["pool_w"][cols, :]) * w["pool_scale"][:, cols])
        mix = None
        for j in range(N_MIX_CHUNKS):
            cols = slice(j * CHUNK, (j + 1) * CHUNK)
            ga, gp = nxt
            if j + 1 < N_MIX_CHUNKS:
                nxt = gate_proj(j + 1)
            merged = jax.nn.sigmoid(ga) * y_a[:, cols] + jax.nn.sigmoid(gp) * y_p[j]
            t = _dot(merged.astype(BF16), w["w_o"][cols, :])
            mix = t if mix is None else mix + t
        pending = (sb, x, mix)
    finish(*pending)


def _ffn_blocks(n_sub, load_x, load_p, store_out, w, conv_in):
    def normed(sb):
        x = load_x(sb)
        return x, _rmsnorm(x, w["g_pre"][...]).astype(BF16)

    def finish(sb, x, f, ple):
        x = x + _rmsnorm(f, w["g_post"][...])
        gate = jax.nn.sigmoid(_dot(x.astype(BF16), w["w_gate"][...]))
        store_out(sb, x + gate * ple)

    cur = normed(0)
    pending = None
    for sb in range(n_sub):
        x, hb = cur

        def up(j, hb=hb):
            return (_dot(hb, w["w_up"][:, j * CHUNK:(j + 1) * CHUNK]),
                    _dot(hb, w["w_up"][:, D_FF + j * CHUNK:D_FF + (j + 1) * CHUNK]))

        ple = _dot(load_p(sb).astype(BF16), w["w_proj"][...])
        nxt = up(0)
        f = None
        for j in range(N_FF_CHUNKS):
            cols = slice(j * CHUNK, (j + 1) * CHUNK)
            a, g = nxt
            if j + 1 < N_FF_CHUNKS:
                nxt = up(j + 1)
            if j == FFN_FINISH_CHUNK and pending is not None:
                finish(*pending)
            if j == FFN_NORM_AHEAD_CHUNK and sb + 1 < n_sub:
                cur = normed(sb + 1)
            h = jax.nn.gelu(conv_in(sb, j, a), approximate=True) * g
            t = _dot(h.astype(BF16), w["w_down"][cols, :])
            f = t if f is None else f + t
        pending = (sb, x, f, ple)
    finish(*pending)


def _block_copy(hbm, buf, sem, block, slot, seq, steps, *, to_hbm=False):
    hbm_view = hbm.at[seq, pl.ds(block * steps, steps), :]
    buf_view = buf.at[slot, :, seq, :]
    src, dst = (buf_view, hbm_view) if to_hbm else (hbm_view, buf_view)
    return pltpu.make_async_copy(src, dst, sem.at[slot, seq])


def _fetch_block(hbm, buf, sem, steps, n_seq):
    i = pl.program_id(0)
    slot = i % 2

    def start(block, slot):
        for s in range(n_seq):
            _block_copy(hbm, buf, sem, block, slot, s, steps).start()

    @pl.when(i == 0)
    def _():
        start(0, 0)

    @pl.when(i + 1 < pl.num_programs(0))
    def _():
        start(i + 1, 1 - slot)

    for s in range(n_seq):
        _block_copy(hbm, buf, sem, i, slot, s, steps).wait()
    return slot


def _stream_weight_as_bf16(w_hbm, w_vmem, stage, sem):
    n_rows, width = w_hbm.shape
    n_slots, chunk, _ = stage.shape
    n_chunks = n_rows // chunk
    assert n_chunks * chunk == n_rows and width <= stage.shape[2]

    def chunk_copy(c, slot):
        return pltpu.make_async_copy(w_hbm.at[pl.ds(c * chunk, chunk), :], stage.at[slot, :, 0:width], sem.at[slot])

    for c in range(min(n_slots, n_chunks)):
        chunk_copy(c, c).start()

    def body(c, carry):
        slot = c % n_slots
        chunk_copy(c, slot).wait()
        w_vmem[pl.ds(pl.multiple_of(c * chunk, chunk), chunk), :] = stage[slot, :, 0:width].astype(BF16)

        @pl.when(c + n_slots < n_chunks)
        def _():
            chunk_copy(c + n_slots, slot).start()

        return carry

    lax.fori_loop(0, n_chunks, body, 0)


def _mixer_prompt_kernel(x_hbm, g_pre, w_in_hbm, conv_w, w_a_out_hbm, pool_w_hbm, pool_scale, w_o_hbm, g_post,
                         x1_ref, nconv_ref, npool_ref, w_in, w_a_out, pool_w, w_o,
                         xbuf, xsem, cbuf, ubuf, e0, e1, e2, inv_cnt, stage, stage_sem,
                         *, steps, sub_steps, n_seq):
    i = pl.program_id(0)
    weight_pairs = ((w_in_hbm, w_in), (w_a_out_hbm, w_a_out), (pool_w_hbm, pool_w), (w_o_hbm, w_o))
    sub_rows = sub_steps * n_seq
    n_sub = steps // sub_steps
    conv_halo = N_CONV * n_seq
    pool_halo = N_POOL * n_seq
    head = POOL_MAX * n_seq
    ebufs = (e0, e1, e2)

    @pl.when(i == 0)
    def _():
        cbuf[0:conv_halo, :] = jnp.zeros((conv_halo, D_MODEL), F32)
        ubuf[0:pool_halo, :] = jnp.zeros((pool_halo, D_MODEL), F32)
        pos = lax.broadcasted_iota(jnp.int32, (POOL_MAX, n_seq, CHUNK), 0).reshape(head, CHUNK)
        for j, window in enumerate(POOL_WINDOWS):
            inv_cnt[j] = 1.0 / jnp.minimum(pos + 1, window).astype(F32)
        for w_hbm, w_vmem in weight_pairs:
            _stream_weight_as_bf16(w_hbm, w_vmem, stage, stage_sem)

    @pl.when(i == 1)
    def _():
        for j, window in enumerate(POOL_WINDOWS):
            inv_cnt[j] = jnp.full((head, CHUNK), 1.0 / window, F32)

    slot = _fetch_block(x_hbm, xbuf, xsem, steps, n_seq)
    w = dict(g_pre=g_pre, w_in=w_in, w_a_out=w_a_out, pool_w=pool_w, pool_scale=pool_scale, w_o=w_o,
             g_post=g_post)

    def conv_in(sb, j, v):
        cols = slice(j * CHUNK, (j + 1) * CHUNK)
        y = _causal_conv(cbuf, cols, v, conv_w, conv_halo, n_seq, sub_rows)
        cbuf[0:conv_halo, cols] = cbuf[pl.ds(sub_rows, conv_halo), cols]
        return y

    def pool_diff(sb, j, u):
        cols = slice(j * CHUNK, (j + 1) * CHUNK)
        window = POOL_WINDOWS[j]
        ubuf[pl.ds(pool_halo, sub_rows), cols] = u
        n_stages = j + 1
        tsum = None
        for k in range(n_stages):
            lo = pool_halo - (window - (2 << k)) * n_seq
            n = pool_halo + sub_rows - lo
            shift = (1 << k) * n_seq
            if k == 0:
                tsum = ubuf[pl.ds(lo, n), cols] + ubuf[pl.ds(lo - shift, n), cols]
            else:
                tsum = ebufs[k - 1][pl.ds(lo, n), cols] + ebufs[k - 1][pl.ds(lo - shift, n), cols]
            if k < n_stages - 1:
                ebufs[k][pl.ds(lo, n), cols] = tsum
        keep = (window - 1) * n_seq
        ubuf[pl.ds(pool_halo - keep, keep), cols] = ubuf[pl.ds(pool_halo + sub_rows - keep, keep), cols]
        if sb > 0:
            return tsum * (1.0 / window) - u
        return jnp.concatenate([tsum[0:head, :] * inv_cnt[j] - u[0:head, :],
                                tsum[head:, :] * (1.0 / window) - u[head:, :]], axis=0)

    def load_x(sb):
        return xbuf[slot, sb * sub_steps:(sb + 1) * sub_steps].reshape(sub_rows, D_MODEL)

    def store_x1(sb, value):
        x1_ref[sb * sub_rows:(sb + 1) * sub_rows, :] = value

    _mixer_blocks(n_sub, load_x, store_x1, w, conv_in, pool_diff)

    @pl.when(i == pl.num_programs(0) - 1)
    def _():
        for k in range(N_CONV):
            nconv_ref[:, k, :] = cbuf[k * n_seq:(k + 1) * n_seq, :]
        npool_ref[...] = ubuf[pl.ds(sub_rows, pool_halo), :].reshape(N_POOL, n_seq, D_MODEL)


def _ffn_prompt_kernel(x_ref, p_hbm, g_pre, w_up_hbm, conv_w, w_down_hbm, g_post, w_proj_hbm, w_gate_hbm,
                       y_hbm, nffn_ref, w_up, w_down, w_proj, w_gate,
                       pbuf, psem, ybuf, ysem, fbuf, stage, stage_sem,
                       *, steps, sub_steps, n_seq):
    i = pl.program_id(0)
    last = pl.num_programs(0) - 1
    sub_rows = sub_steps * n_seq
    n_sub = steps // sub_steps
    conv_halo = N_CONV * n_seq
    weight_pairs = ((w_up_hbm, w_up), (w_down_hbm, w_down), (w_proj_hbm, w_proj), (w_gate_hbm, w_gate))

    @pl.when(i == 0)
    def _():
        fbuf[0:conv_halo, :] = jnp.zeros((conv_halo, D_FF), F32)
        for w_hbm, w_vmem in weight_pairs:
            _stream_weight_as_bf16(w_hbm, w_vmem, stage, stage_sem)

    def wait_store(block, slot):
        for s in range(n_seq):
            _block_copy(y_hbm, ybuf, ysem, block, slot, s, steps, to_hbm=True).wait()

    slot = _fetch_block(p_hbm, pbuf, psem, steps, n_seq)

    @pl.when(i >= 2)
    def _():
        wait_store(i - 2, slot)

    w = dict(g_pre=g_pre, w_up=w_up, w_down=w_down, g_post=g_post, w_proj=w_proj, w_gate=w_gate)

    def conv_in(sb, j, v):
        cols = slice(j * CHUNK, (j + 1) * CHUNK)
        y = _causal_conv(fbuf, cols, v, conv_w, conv_halo, n_seq, sub_rows)
        fbuf[0:conv_halo, cols] = fbuf[pl.ds(sub_rows, conv_halo), cols]
        return y

    def load_x(sb):
        return x_ref[sb * sub_rows:(sb + 1) * sub_rows, :]

    def load_p(sb):
        return pbuf[slot, sb * sub_steps:(sb + 1) * sub_steps].reshape(sub_rows, PLE_DIM)

    def store_y(sb, value):
        ybuf[slot, sb * sub_steps:(sb + 1) * sub_steps] = value.reshape(sub_steps, n_seq, D_MODEL)

    _ffn_blocks(n_sub, load_x, load_p, store_y, w, conv_in)

    for s in range(n_seq):
        _block_copy(y_hbm, ybuf, ysem, i, slot, s, steps, to_hbm=True).start()

    @pl.when(i == last)
    def _():
        for k in range(N_CONV):
            nffn_ref[:, k, :] = fbuf[k * n_seq:(k + 1) * n_seq, :]
        wait_store(i - 1, 1 - slot)
        wait_store(i, slot)


def _resident(shape):
    return pl.BlockSpec(shape, lambda i: (0,) * len(shape), pipeline_mode=pl.Buffered(1))


def _prompt_layer(x, p, wts):
    n_seq, seq_len, _ = x.shape
    steps, sub_steps = TIME_STEPS, SUB_STEPS
    n_blocks = seq_len // steps
    assert n_blocks * steps == seq_len and n_blocks >= 2 and n_seq % 8 == 0
    assert steps % sub_steps == 0 and sub_steps > POOL_MAX
    rows = steps * n_seq
    params = lambda limit: pltpu.CompilerParams(dimension_semantics=("arbitrary",), vmem_limit_bytes=limit)
    any_space = pl.BlockSpec(memory_space=pl.ANY)
    row_block = pl.BlockSpec((rows, D_MODEL), lambda i: (i, 0))
    row_block_shape = jax.ShapeDtypeStruct((rows, D_MODEL), F32)
    whole_block = lambda shape: pl.BlockSpec(shape, lambda i: (0,) * len(shape))
    conv_state = lambda width: (n_seq, N_CONV, width)
    pool_state = (N_POOL, n_seq, D_MODEL)

    weight_spec = lambda a: any_space if a.dtype == F32 and a.shape[0] >= WEIGHT_CHUNK_ROWS else _resident(a.shape)
    bf16_like = lambda ws: [jax.ShapeDtypeStruct(a.shape, BF16) for a in ws]
    in_vmem = pl.BlockSpec(memory_space=pltpu.VMEM)
    stage = lambda ws: [pltpu.VMEM((WEIGHT_STAGE_SLOTS, WEIGHT_CHUNK_ROWS, max(a.shape[1] for a in ws)), F32),
                        pltpu.SemaphoreType.DMA((WEIGHT_STAGE_SLOTS,))]

    mixer_w = (wts["g_pre_mix"], wts["w_in"], wts["conv_a_w"], wts["w_a_out"], wts["pool_w"],
               wts["pool_scale"], wts["w_o"], wts["g_post_mix"])
    mixer_mats = (wts["w_in"], wts["w_a_out"], wts["pool_w"], wts["w_o"])
    sub_rows = sub_steps * n_seq
    stage_rows = N_POOL * n_seq + sub_rows
    mixer_scratch = ([pltpu.VMEM((2, steps, n_seq, D_MODEL), F32), pltpu.SemaphoreType.DMA((2, n_seq)),
                      pltpu.VMEM((N_CONV * n_seq + sub_rows, D_MODEL), F32),
                      pltpu.VMEM((stage_rows, D_MODEL), F32),
                      pltpu.VMEM((stage_rows, D_MODEL), F32),
                      pltpu.VMEM((stage_rows, D_MODEL), F32),
                      pltpu.VMEM((stage_rows, D_MODEL), F32),
                      pltpu.VMEM((len(POOL_WINDOWS), POOL_MAX * n_seq, CHUNK), F32)]
                     + stage(mixer_mats))
    x1, nconv, npool, *mixer_bf16 = pl.pallas_call(
        functools.partial(_mixer_prompt_kernel, steps=steps, sub_steps=sub_steps, n_seq=n_seq),
        grid=(n_blocks,),
        in_specs=[any_space] + [weight_spec(a) for a in mixer_w],
        out_specs=[row_block, whole_block(conv_state(D_MODEL)), whole_block(pool_state)] + [in_vmem] * 4,
        out_shape=[jax.ShapeDtypeStruct((n_blocks * rows, D_MODEL), F32),
                   jax.ShapeDtypeStruct(conv_state(D_MODEL), F32),
                   jax.ShapeDtypeStruct(pool_state, F32)] + bf16_like(mixer_mats),
        scratch_shapes=mixer_scratch,
        compiler_params=params(_vmem_limit(mixer_scratch, single_buffered=bf16_like(mixer_mats),
                                           double_buffered=[row_block_shape])),
        name="mixer_prompt",
    )(x, *mixer_w)

    ffn_w = (wts["g_pre_ffn"], wts["w_up"], wts["ffn_conv_w"], wts["w_down"], wts["g_post_ffn"],
             wts["w_ple_proj"], wts["w_ple_gate"])
    ffn_mats = (wts["w_up"], wts["w_down"], wts["w_ple_proj"], wts["w_ple_gate"])
    ffn_scratch = ([pltpu.VMEM((2, steps, n_seq, PLE_DIM), F32), pltpu.SemaphoreType.DMA((2, n_seq)),
                    pltpu.VMEM((2, steps, n_seq, D_MODEL), F32), pltpu.SemaphoreType.DMA((2, n_seq)),
                    pltpu.VMEM((N_CONV * n_seq + sub_rows, D_FF), F32)]
                   + stage(ffn_mats))
    y, nffn, *ffn_bf16 = pl.pallas_call(
        functools.partial(_ffn_prompt_kernel, steps=steps, sub_steps=sub_steps, n_seq=n_seq),
        grid=(n_blocks,),
        in_specs=[row_block, any_space] + [weight_spec(a) for a in ffn_w],
        out_specs=[any_space, whole_block(conv_state(D_FF))] + [in_vmem] * 4,
        out_shape=[jax.ShapeDtypeStruct((n_seq, seq_len, D_MODEL), F32),
                   jax.ShapeDtypeStruct(conv_state(D_FF), F32)] + bf16_like(ffn_mats),
        scratch_shapes=ffn_scratch,
        compiler_params=params(_vmem_limit(ffn_scratch, single_buffered=bf16_like(ffn_mats),
                                           double_buffered=[row_block_shape])),
        name="ffn_prompt",
    )(x1, p, *ffn_w)
    bf16_wts = dict(wts, **dict(zip(("w_in", "w_a_out", "pool_w", "w_o"), mixer_bf16)),
                    **dict(zip(("w_up", "w_down", "w_ple_proj", "w_ple_gate"), ffn_bf16)))
    return y, nconv[None], jnp.transpose(npool, (1, 0, 2))[None], nffn[None], bf16_wts


def _mixer_sample_kernel(x_ref, sconv_ref, spool_ref, g_pre, w_in, conv_w, w_a_out, pool_w, pool_scale, w_o,
                         g_post, x1_ref, nconv_ref, npool_ref, cbuf, ubuf, *, batch, steps, sub_steps):
    sub_rows = sub_steps * batch
    for k in range(N_CONV):
        cbuf[k * batch:(k + 1) * batch, :] = sconv_ref[:, k, :]

    w = dict(g_pre=g_pre, w_in=w_in, w_a_out=w_a_out, pool_w=pool_w, pool_scale=pool_scale, w_o=w_o,
             g_post=g_post)
    def conv_in(sb, j, v):
        cols = slice(j * CHUNK, (j + 1) * CHUNK)
        return _causal_conv(cbuf, cols, v, conv_w, N_CONV * batch + sb * sub_rows, batch, sub_rows)

    def pool_diff(sb, j, u):
        t0 = sb * sub_steps
        cols = slice(j * CHUNK, (j + 1) * CHUNK)
        window = POOL_WINDOWS[j]
        ubuf[t0:t0 + sub_steps, :, cols] = u.reshape(sub_steps, batch, CHUNK)

        def ext(e):
            return spool_ref[e, :, cols] if e < N_POOL else ubuf[e - N_POOL, :, cols]

        diffs = []
        for t in range(t0, t0 + sub_steps):
            tsum = ext(N_POOL + t - window + 1)
            for e in range(N_POOL + t - window + 2, N_POOL + t + 1):
                tsum = tsum + ext(e)
            diffs.append(tsum * (1.0 / window) - ext(N_POOL + t))
        return jnp.concatenate(diffs, axis=0)

    def load_x(sb):
        return jnp.concatenate([x_ref[:, t, :] for t in range(sb * sub_steps, (sb + 1) * sub_steps)], axis=0)

    def store_x1(sb, value):
        x1_ref[sb * sub_rows:(sb + 1) * sub_rows, :] = value

    _mixer_blocks(steps // sub_steps, load_x, store_x1, w, conv_in, pool_diff)

    for k in range(N_CONV):
        nconv_ref[:, k, :] = cbuf[(steps + k) * batch:(steps + k + 1) * batch, :]
    for m in range(N_POOL):
        e = steps + m
        npool_ref[m] = spool_ref[e] if e < N_POOL else ubuf[e - N_POOL]


def _ffn_sample_kernel(x_ref, p_ref, sffn_ref, g_pre, w_up, conv_w, w_down, g_post, w_proj, w_gate,
                       y_ref, nffn_ref, fbuf, *, batch, steps, sub_steps):
    sub_rows = sub_steps * batch
    for k in range(N_CONV):
        fbuf[k * batch:(k + 1) * batch, :] = sffn_ref[:, k, :]

    w = dict(g_pre=g_pre, w_up=w_up, w_down=w_down, g_post=g_post, w_proj=w_proj, w_gate=w_gate)
    def conv_in(sb, j, v):
        cols = slice(j * CHUNK, (j + 1) * CHUNK)
        return _causal_conv(fbuf, cols, v, conv_w, N_CONV * batch + sb * sub_rows, batch, sub_rows)

    def load_x(sb):
        return x_ref[sb * sub_rows:(sb + 1) * sub_rows, :]

    def load_p(sb):
        return jnp.concatenate([p_ref[:, t, :] for t in range(sb * sub_steps, (sb + 1) * sub_steps)], axis=0)

    def store_y(sb, value):
        for t in range(sub_steps):
            y_ref[:, sb * sub_steps + t, :] = value[t * batch:(t + 1) * batch, :]

    _ffn_blocks(steps // sub_steps, load_x, load_p, store_y, w, conv_in)

    for k in range(N_CONV):
        nffn_ref[:, k, :] = fbuf[(steps + k) * batch:(steps + k + 1) * batch, :]


def _sample_layer(x, p, sconv, spool, sffn, layer, wts):
    batch, steps, _ = x.shape
    sub_steps = max(1, SAMPLE_SUB_ROWS // batch)
    assert steps % sub_steps == 0 and batch % 8 == 0
    rows = batch * steps
    params = lambda limit: pltpu.CompilerParams(vmem_limit_bytes=limit)
    whole = pl.BlockSpec(memory_space=pltpu.VMEM)
    of_layer = lambda a, k: pl.BlockSpec((None,) + a.shape[1:], lambda: (k, 0, 0, 0))
    one_layer = lambda a: jax.ShapeDtypeStruct((1,) + a.shape[1:], F32)
    x1_shape = jax.ShapeDtypeStruct((rows, D_MODEL), F32)

    mixer_w = (wts["g_pre_mix"], wts["w_in"], wts["conv_a_w"], wts["w_a_out"], wts["pool_w"],
               wts["pool_scale"], wts["w_o"], wts["g_post_mix"])
    mixer_scratch = [pltpu.VMEM(((N_CONV + steps) * batch, D_MODEL), F32), pltpu.VMEM((steps, batch, D_MODEL), F32)]
    mixer_out = [x1_shape, one_layer(sconv), one_layer(spool)]
    x1, nconv, npool = pl.pallas_call(
        functools.partial(_mixer_sample_kernel, batch=batch, steps=steps, sub_steps=sub_steps),
        in_specs=[whole, of_layer(sconv, layer), of_layer(spool, layer)] + [whole] * len(mixer_w),
        out_specs=[whole, of_layer(sconv, 0), of_layer(spool, 0)],
        out_shape=mixer_out,
        scratch_shapes=mixer_scratch,
        compiler_params=params(_vmem_limit(mixer_scratch, single_buffered=[x, *mixer_out, *mixer_out[1:], *mixer_w])),
        name="mixer_sample",
    )(x, sconv, spool, *mixer_w)

    ffn_w = (wts["g_pre_ffn"], wts["w_up"], wts["ffn_conv_w"], wts["w_down"], wts["g_post_ffn"],
             wts["w_ple_proj"], wts["w_ple_gate"])
    ffn_scratch = [pltpu.VMEM(((N_CONV + steps) * batch, D_FF), F32)]
    ffn_out = [jax.ShapeDtypeStruct(x.shape, F32), one_layer(sffn)]
    y, nffn = pl.pallas_call(
        functools.partial(_ffn_sample_kernel, batch=batch, steps=steps, sub_steps=sub_steps),
        in_specs=[whole, of_layer(p, layer), of_layer(sffn, layer)] + [whole] * len(ffn_w),
        out_specs=[whole, of_layer(sffn, 0)],
        out_shape=ffn_out,
        scratch_shapes=ffn_scratch,
        compiler_params=params(_vmem_limit(
            ffn_scratch, single_buffered=[x1_shape, one_layer(p), *ffn_out, ffn_out[1], *ffn_w])),
        name="ffn_sample",
    )(x1, p, sffn, *ffn_w)
    return y, nconv, npool, nffn


def kernel(x_prompt, x_sample, p_prompt, p_sample, state_conv_a, state_pool, state_ffn_conv, g_pre_mix, w_in,
           conv_a_w, w_a_out, pool_w, pool_scale, w_o, g_post_mix, g_pre_ffn, w_up, ffn_conv_w, w_down,
           g_post_ffn, w_ple_proj, w_ple_gate):
    depth = w_in.shape[0]
    xp, xs = x_prompt, x_sample
    prompt_states, sample_states = [], []
    time_major = lambda s: jnp.transpose(s, (0, 2, 1, 3))
    state_pool_tm = time_major(state_pool)
    for i in range(depth):
        wts = dict(
            g_pre_mix=g_pre_mix[i][None], g_post_mix=g_post_mix[i][None], g_pre_ffn=g_pre_ffn[i][None],
            g_post_ffn=g_post_ffn[i][None], pool_scale=pool_scale[i][None], conv_a_w=conv_a_w[i],
            ffn_conv_w=ffn_conv_w[i],
            w_in=w_in[i], w_a_out=w_a_out[i], pool_w=pool_w[i].reshape(D_MODEL, CHUNK),
            w_o=w_o[i], w_up=w_up[i], w_down=w_down[i], w_ple_proj=w_ple_proj[i], w_ple_gate=w_ple_gate[i])
        xp, c1, p1, f1, bf16_wts = _prompt_layer(xp, p_prompt[i], wts)
        xs, c2, p2, f2 = _sample_layer(xs, p_sample, state_conv_a, state_pool_tm, state_ffn_conv, i, bf16_wts)
        prompt_states.append((c1, p1, f1))
        sample_states.append((c2, time_major(p2), f2))
    cat = lambda states, k: states[0][k] if depth == 1 else jnp.concatenate([s[k] for s in states], axis=0)
    return (xp, xs, cat(prompt_states, 0), cat(prompt_states, 1), cat(prompt_states, 2),
            cat(sample_states, 0), cat(sample_states, 1), cat(sample_states, 2))
```

```python
import functools
import math

import jax
import jax.numpy as jnp
from jax import lax
from jax.experimental import pallas as pl
from jax.experimental.pallas import tpu as pltpu

D_MODEL = 1024
D_FF = 2816
PLE_DIM = 256
CONV_W = 3
POOL_WINDOWS = (2, 4, 8, 16)
POOL_MAX = 16
EPS = 1e-6

N_CONV = CONV_W - 1
N_POOL = POOL_MAX - 1
CHUNK = 256
N_MIX_CHUNKS = D_MODEL // CHUNK
N_FF_CHUNKS = D_FF // CHUNK
TIME_STEPS = 64
SUB_STEPS = 32
SAMPLE_SUB_ROWS = 256
MIXER_NORM_AHEAD_CHUNK = 2
FFN_NORM_AHEAD_CHUNK = 8
FFN_FINISH_CHUNK = 1
WEIGHT_CHUNK_ROWS = 128
WEIGHT_STAGE_SLOTS = 4
VMEM_CAPACITY_BYTES = 64 * 1024 * 1024
VMEM_COMPILER_SCRATCH_BYTES = 6 * 1024 * 1024
VMEM_UNREQUESTABLE_BYTES = 4 * 1024 * 1024

F32 = jnp.float32
BF16 = jnp.bfloat16


def _nbytes(a):
    return math.prod(a.shape) * jnp.dtype(a.dtype).itemsize


def _vmem_limit(scratch, single_buffered=(), double_buffered=()):
    total = sum(_nbytes(s) for s in scratch if s.memory_space == pltpu.VMEM)
    total += sum(_nbytes(a) for a in single_buffered) + 2 * sum(_nbytes(a) for a in double_buffered)
    return min(total + VMEM_COMPILER_SCRATCH_BYTES, VMEM_CAPACITY_BYTES - VMEM_UNREQUESTABLE_BYTES)


def _rmsnorm(x, g):
    ms = jnp.mean(x * x, axis=-1, keepdims=True)
    return (x * lax.rsqrt(ms + EPS)) * g


def _dot(a, b):
    return jnp.dot(a, b, preferred_element_type=F32)


def _causal_conv(buf, cols, v, w_ref, halo, step, rows):
    buf[pl.ds(halo, rows), cols] = v
    y = buf[pl.ds(halo - 2 * step, rows), cols] * w_ref[0:1, cols]
    y = y + buf[pl.ds(halo - step, rows), cols] * w_ref[1:2, cols]
    return y + v * w_ref[2:3, cols]


def _mixer_blocks(n_sub, load_x, store_out, w, conv_in, pool_diff):
    def normed(sb):
        x = load_x(sb)
        return x, _rmsnorm(x, w["g_pre"][...]).astype(BF16)

    def finish(sb, x, mix):
        store_out(sb, x + _rmsnorm(mix, w["g_post"][...]))

    cur = normed(0)
    pending = None
    for sb in range(n_sub):
        x, xb = cur

        def proj(k, j, xb=xb):
            lo = k * D_MODEL + j * CHUNK
            return _dot(xb, w["w_in"][:, lo:lo + CHUNK])

        def branch_proj(j):
            return tuple(proj(k, j) for k in range(4))

        def gate_proj(j):
            return proj(4, j), proj(5, j)

        nxt = branch_proj(0)
        y_a = None
        y_p = []
        for j in range(N_MIX_CHUNKS):
            cols = slice(j * CHUNK, (j + 1) * CHUNK)
            b, c, h, u = nxt
            nxt = branch_proj(j + 1) if j + 1 < N_MIX_CHUNKS else gate_proj(0)
            if j == 0 and pending is not None:
                finish(*pending)
            if j == MIXER_NORM_AHEAD_CHUNK and sb + 1 < n_sub:
                cur = normed(sb + 1)
            z = (b * conv_in(sb, j, c * h)).astype(BF16)
            t = _dot(z, w["w_a_out"][cols, :])
            y_a = t if y_a is None else y_a + t
            d = pool_diff(sb, j, u).astype(BF16)
            y_p.append(_dot(d, w["pool_w"][cols, :]) * w["pool_scale"][:, cols])
        mix = None
        for j in range(N_MIX_CHUNKS):
            cols = slice(j * CHUNK, (j + 1) * CHUNK)
            ga, gp = nxt
            if j + 1 < N_MIX_CHUNKS:
                nxt = gate_proj(j + 1)
            merged = jax.nn.sigmoid(ga) * y_a[:, cols] + jax.nn.sigmoid(gp) * y_p[j]
            t = _dot(merged.astype(BF16), w["w_o"][cols, :])
            mix = t if mix is None else mix + t
        pending = (sb, x, mix)
    finish(*pending)


def _ffn_blocks(n_sub, load_x, load_p, store_out, w, conv_in):
    def normed(sb):
        x = load_x(sb)
        return x, _rmsnorm(x, w["g_pre"][...]).astype(BF16)

    def finish(sb, x, f, ple):
        x = x + _rmsnorm(f, w["g_post"][...])
        gate = jax.nn.sigmoid(_dot(x.astype(BF16), w["w_gate"][...]))
        store_out(sb, x + gate * ple)

    cur = normed(0)
    pending = None
    for sb in range(n_sub):
        x, hb = cur

        def up(j, hb=hb):
            return (_dot(hb, w["w_up"][:, j * CHUNK:(j + 1) * CHUNK]),
                    _dot(hb, w["w_up"][:, D_FF + j * CHUNK:D_FF + (j + 1) * CHUNK]))

        ple = _dot(load_p(sb).astype(BF16), w["w_proj"][...])
        nxt = up(0)
        f = None
        for j in range(N_FF_CHUNKS):
            cols = slice(j * CHUNK, (j + 1) * CHUNK)
            a, g = nxt
            if j + 1 < N_FF_CHUNKS:
                nxt = up(j + 1)
            if j == FFN_FINISH_CHUNK and pending is not None:
                finish(*pending)
            if j == FFN_NORM_AHEAD_CHUNK and sb + 1 < n_sub:
                cur = normed(sb + 1)
            h = jax.nn.gelu(conv_in(sb, j, a), approximate=True) * g
            t = _dot(h.astype(BF16), w["w_down"][cols, :])
            f = t if f is None else f + t
        pending = (sb, x, f, ple)
    finish(*pending)


def _block_copy(hbm, buf, sem, block, slot, seq, steps, *, to_hbm=False):
    hbm_view = hbm.at[seq, pl.ds(block * steps, steps), :]
    buf_view = buf.at[slot, :, seq, :]
    src, dst = (buf_view, hbm_view) if to_hbm else (hbm_view, buf_view)
    return pltpu.make_async_copy(src, dst, sem.at[slot, seq])


def _fetch_block(hbm, buf, sem, steps, n_seq):
    i = pl.program_id(0)
    slot = i % 2

    def start(block, slot):
        for s in range(n_seq):
            _block_copy(hbm, buf, sem, block, slot, s, steps).start()

    @pl.when(i == 0)
    def _():
        start(0, 0)

    @pl.when(i + 1 < pl.num_programs(0))
    def _():
        start(i + 1, 1 - slot)

    for s in range(n_seq):
        _block_copy(hbm, buf, sem, i, slot, s, steps).wait()
    return slot


def _stream_weight_as_bf16(w_hbm, w_vmem, stage, sem):
    n_rows, width = w_hbm.shape
    n_slots, chunk, _ = stage.shape
    n_chunks = n_rows // chunk
    assert n_chunks * chunk == n_rows and width <= stage.shape[2]

    def chunk_copy(c, slot):
        return pltpu.make_async_copy(w_hbm.at[pl.ds(c * chunk, chunk), :], stage.at[slot, :, 0:width], sem.at[slot])

    for c in range(min(n_slots, n_chunks)):
        chunk_copy(c, c).start()

    def body(c, carry):
        slot = c % n_slots
        chunk_copy(c, slot).wait()
        w_vmem[pl.ds(pl.multiple_of(c * chunk, chunk), chunk), :] = stage[slot, :, 0:width].astype(BF16)

        @pl.when(c + n_slots < n_chunks)
        def _():
            chunk_copy(c + n_slots, slot).start()

        return carry

    lax.fori_loop(0, n_chunks, body, 0)


def _load_weights(pairs, exports, stage, stage_sem, export_sem):
    for w_hbm, w_vmem in pairs:
        _stream_weight_as_bf16(w_hbm, w_vmem, stage, stage_sem)
    for k, ((_, w_vmem), w_out) in enumerate(zip(pairs, exports)):
        pltpu.make_async_copy(w_vmem, w_out, export_sem.at[k]).start()


def _finish_weight_exports(pairs, exports, export_sem):
    for k, ((_, w_vmem), w_out) in enumerate(zip(pairs, exports)):
        pltpu.make_async_copy(w_vmem, w_out, export_sem.at[k]).wait()


def _mixer_prompt_kernel(x_hbm, g_pre, w_in_hbm, conv_w, w_a_out_hbm, pool_w_hbm, pool_scale, w_o_hbm, g_post,
                         x1_ref, nconv_ref, npool_ref, w_in_out, w_a_out_out, pool_w_out, w_o_out,
                         xbuf, xsem, cbuf, ubuf, e0, e1, e2, inv_cnt,
                         w_in, w_a_out, pool_w, w_o, stage, stage_sem, export_sem,
                         *, steps, sub_steps, n_seq):
    i = pl.program_id(0)
    weight_pairs = ((w_in_hbm, w_in), (w_a_out_hbm, w_a_out), (pool_w_hbm, pool_w), (w_o_hbm, w_o))
    weight_exports = (w_in_out, w_a_out_out, pool_w_out, w_o_out)
    sub_rows = sub_steps * n_seq
    n_sub = steps // sub_steps
    conv_halo = N_CONV * n_seq
    pool_halo = N_POOL * n_seq
    head = POOL_MAX * n_seq
    ebufs = (e0, e1, e2)

    @pl.when(i == 0)
    def _():
        cbuf[0:conv_halo, :] = jnp.zeros((conv_halo, D_MODEL), F32)
        ubuf[0:pool_halo, :] = jnp.zeros((pool_halo, D_MODEL), F32)
        pos = lax.broadcasted_iota(jnp.int32, (POOL_MAX, n_seq, CHUNK), 0).reshape(head, CHUNK)
        for j, window in enumerate(POOL_WINDOWS):
            inv_cnt[j] = 1.0 / jnp.minimum(pos + 1, window).astype(F32)
        _load_weights(weight_pairs, weight_exports, stage, stage_sem, export_sem)

    @pl.when(i == 1)
    def _():
        for j, window in enumerate(POOL_WINDOWS):
            inv_cnt[j] = jnp.full((head, CHUNK), 1.0 / window, F32)

    slot = _fetch_block(x_hbm, xbuf, xsem, steps, n_seq)
    w = dict(g_pre=g_pre, w_in=w_in, w_a_out=w_a_out, pool_w=pool_w, pool_scale=pool_scale, w_o=w_o,
             g_post=g_post)

    def conv_in(sb, j, v):
        cols = slice(j * CHUNK, (j + 1) * CHUNK)
        y = _causal_conv(cbuf, cols, v, conv_w, conv_halo, n_seq, sub_rows)
        cbuf[0:conv_halo, cols] = cbuf[pl.ds(sub_rows, conv_halo), cols]
        return y

    def pool_diff(sb, j, u):
        cols = slice(j * CHUNK, (j + 1) * CHUNK)
        window = POOL_WINDOWS[j]
        ubuf[pl.ds(pool_halo, sub_rows), cols] = u
        n_stages = j + 1
        tsum = None
        for k in range(n_stages):
            lo = pool_halo - (window - (2 << k)) * n_seq
            n = pool_halo + sub_rows - lo
            shift = (1 << k) * n_seq
            if k == 0:
                tsum = ubuf[pl.ds(lo, n), cols] + ubuf[pl.ds(lo - shift, n), cols]
            else:
                tsum = ebufs[k - 1][pl.ds(lo, n), cols] + ebufs[k - 1][pl.ds(lo - shift, n), cols]
            if k < n_stages - 1:
                ebufs[k][pl.ds(lo, n), cols] = tsum
        keep = (window - 1) * n_seq
        ubuf[pl.ds(pool_halo - keep, keep), cols] = ubuf[pl.ds(pool_halo + sub_rows - keep, keep), cols]
        if sb > 0:
            return tsum * (1.0 / window) - u
        return jnp.concatenate([tsum[0:head, :] * inv_cnt[j] - u[0:head, :],
                                tsum[head:, :] * (1.0 / window) - u[head:, :]], axis=0)

    def load_x(sb):
        return xbuf[slot, sb * sub_steps:(sb + 1) * sub_steps].reshape(sub_rows, D_MODEL)

    def store_x1(sb, value):
        x1_ref[sb * sub_rows:(sb + 1) * sub_rows, :] = value

    _mixer_blocks(n_sub, load_x, store_x1, w, conv_in, pool_diff)

    @pl.when(i == pl.num_programs(0) - 1)
    def _():
        for k in range(N_CONV):
            nconv_ref[:, k, :] = cbuf[k * n_seq:(k + 1) * n_seq, :]
        npool_ref[...] = ubuf[pl.ds(sub_rows, pool_halo), :].reshape(N_POOL, n_seq, D_MODEL)
        _finish_weight_exports(weight_pairs, weight_exports, export_sem)


def _ffn_prompt_kernel(x_ref, p_hbm, g_pre, w_up_hbm, conv_w, w_down_hbm, g_post, w_proj_hbm, w_gate_hbm,
                       y_hbm, nffn_ref, w_up_out, w_down_out, w_proj_out, w_gate_out,
                       pbuf, psem, ybuf, ysem, fbuf,
                       w_up, w_down, w_proj, w_gate, stage, stage_sem, export_sem,
                       *, steps, sub_steps, n_seq):
    i = pl.program_id(0)
    last = pl.num_programs(0) - 1
    sub_rows = sub_steps * n_seq
    n_sub = steps // sub_steps
    conv_halo = N_CONV * n_seq
    weight_pairs = ((w_up_hbm, w_up), (w_down_hbm, w_down), (w_proj_hbm, w_proj), (w_gate_hbm, w_gate))
    weight_exports = (w_up_out, w_down_out, w_proj_out, w_gate_out)

    @pl.when(i == 0)
    def _():
        fbuf[0:conv_halo, :] = jnp.zeros((conv_halo, D_FF), F32)
        _load_weights(weight_pairs, weight_exports, stage, stage_sem, export_sem)

    def wait_store(block, slot):
        for s in range(n_seq):
            _block_copy(y_hbm, ybuf, ysem, block, slot, s, steps, to_hbm=True).wait()

    slot = _fetch_block(p_hbm, pbuf, psem, steps, n_seq)

    @pl.when(i >= 2)
    def _():
        wait_store(i - 2, slot)

    w = dict(g_pre=g_pre, w_up=w_up, w_down=w_down, g_post=g_post, w_proj=w_proj, w_gate=w_gate)

    def conv_in(sb, j, v):
        cols = slice(j * CHUNK, (j + 1) * CHUNK)
        y = _causal_conv(fbuf, cols, v, conv_w, conv_halo, n_seq, sub_rows)
        fbuf[0:conv_halo, cols] = fbuf[pl.ds(sub_rows, conv_halo), cols]
        return y

    def load_x(sb):
        return x_ref[sb * sub_rows:(sb + 1) * sub_rows, :]

    def load_p(sb):
        return pbuf[slot, sb * sub_steps:(sb + 1) * sub_steps].reshape(sub_rows, PLE_DIM)

    def store_y(sb, value):
        ybuf[slot, sb * sub_steps:(sb + 1) * sub_steps] = value.reshape(sub_steps, n_seq, D_MODEL)

    _ffn_blocks(n_sub, load_x, load_p, store_y, w, conv_in)

    for s in range(n_seq):
        _block_copy(y_hbm, ybuf, ysem, i, slot, s, steps, to_hbm=True).start()

    @pl.when(i == last)
    def _():
        for k in range(N_CONV):
            nffn_ref[:, k, :] = fbuf[k * n_seq:(k + 1) * n_seq, :]
        wait_store(i - 1, 1 - slot)
        wait_store(i, slot)
        _finish_weight_exports(weight_pairs, weight_exports, export_sem)


def _resident(shape):
    return pl.BlockSpec(shape, lambda i: (0,) * len(shape), pipeline_mode=pl.Buffered(1))


def _prompt_layer(x, p, wts):
    n_seq, seq_len, _ = x.shape
    steps, sub_steps = TIME_STEPS, SUB_STEPS
    n_blocks = seq_len // steps
    assert n_blocks * steps == seq_len and n_blocks >= 2 and n_seq % 8 == 0
    assert steps % sub_steps == 0 and sub_steps > POOL_MAX
    rows = steps * n_seq
    params = lambda limit: pltpu.CompilerParams(dimension_semantics=("arbitrary",), vmem_limit_bytes=limit)
    any_space = pl.BlockSpec(memory_space=pl.ANY)
    row_block = pl.BlockSpec((rows, D_MODEL), lambda i: (i, 0))
    row_block_shape = jax.ShapeDtypeStruct((rows, D_MODEL), F32)
    whole_block = lambda shape: pl.BlockSpec(shape, lambda i: (0,) * len(shape))
    conv_state = lambda width: (n_seq, N_CONV, width)
    pool_state = (N_POOL, n_seq, D_MODEL)

    weight_spec = lambda a: any_space if a.dtype == F32 and a.shape[0] >= WEIGHT_CHUNK_ROWS else _resident(a.shape)
    bf16_like = lambda ws: [jax.ShapeDtypeStruct(a.shape, BF16) for a in ws]
    vmem_bf16 = lambda ws: [pltpu.VMEM(a.shape, BF16) for a in ws]
    stage = lambda ws: [pltpu.VMEM((WEIGHT_STAGE_SLOTS, WEIGHT_CHUNK_ROWS, max(a.shape[1] for a in ws)), F32),
                        pltpu.SemaphoreType.DMA((WEIGHT_STAGE_SLOTS,)), pltpu.SemaphoreType.DMA((len(ws),))]

    mixer_w = (wts["g_pre_mix"], wts["w_in"], wts["conv_a_w"], wts["w_a_out"], wts["pool_w"],
               wts["pool_scale"], wts["w_o"], wts["g_post_mix"])
    mixer_mats = (wts["w_in"], wts["w_a_out"], wts["pool_w"], wts["w_o"])
    sub_rows = sub_steps * n_seq
    stage_rows = N_POOL * n_seq + sub_rows
    mixer_scratch = ([pltpu.VMEM((2, steps, n_seq, D_MODEL), F32), pltpu.SemaphoreType.DMA((2, n_seq)),
                      pltpu.VMEM((N_CONV * n_seq + sub_rows, D_MODEL), F32),
                      pltpu.VMEM((stage_rows, D_MODEL), F32),
                      pltpu.VMEM((stage_rows, D_MODEL), F32),
                      pltpu.VMEM((stage_rows, D_MODEL), F32),
                      pltpu.VMEM((stage_rows, D_MODEL), F32),
                      pltpu.VMEM((len(POOL_WINDOWS), POOL_MAX * n_seq, CHUNK), F32)]
                     + vmem_bf16(mixer_mats) + stage(mixer_mats))
    x1, nconv, npool, *mixer_bf16 = pl.pallas_call(
        functools.partial(_mixer_prompt_kernel, steps=steps, sub_steps=sub_steps, n_seq=n_seq),
        grid=(n_blocks,),
        in_specs=[any_space] + [weight_spec(a) for a in mixer_w],
        out_specs=[row_block, whole_block(conv_state(D_MODEL)), whole_block(pool_state)] + [any_space] * 4,
        out_shape=[jax.ShapeDtypeStruct((n_blocks * rows, D_MODEL), F32),
                   jax.ShapeDtypeStruct(conv_state(D_MODEL), F32),
                   jax.ShapeDtypeStruct(pool_state, F32)] + bf16_like(mixer_mats),
        scratch_shapes=mixer_scratch,
        compiler_params=params(_vmem_limit(mixer_scratch, double_buffered=[row_block_shape])),
        name="mixer_prompt",
    )(x, *mixer_w)

    ffn_w = (wts["g_pre_ffn"], wts["w_up"], wts["ffn_conv_w"], wts["w_down"], wts["g_post_ffn"],
             wts["w_ple_proj"], wts["w_ple_gate"])
    ffn_mats = (wts["w_up"], wts["w_down"], wts["w_ple_proj"], wts["w_ple_gate"])
    ffn_scratch = ([pltpu.VMEM((2, steps, n_seq, PLE_DIM), F32), pltpu.SemaphoreType.DMA((2, n_seq)),
                    pltpu.VMEM((2, steps, n_seq, D_MODEL), F32), pltpu.SemaphoreType.DMA((2, n_seq)),
                    pltpu.VMEM((N_CONV * n_seq + sub_rows, D_FF), F32)]
                   + vmem_bf16(ffn_mats) + stage(ffn_mats))
    y, nffn, *ffn_bf16 = pl.pallas_call(
        functools.partial(_ffn_prompt_kernel, steps=steps, sub_steps=sub_steps, n_seq=n_seq),
        grid=(n_blocks,),
        in_specs=[row_block, any_space] + [weight_spec(a) for a in ffn_w],
        out_specs=[any_space, whole_block(conv_state(D_FF))] + [any_space] * 4,
        out_shape=[jax.ShapeDtypeStruct((n_seq, seq_len, D_MODEL), F32),
                   jax.ShapeDtypeStruct(conv_state(D_FF), F32)] + bf16_like(ffn_mats),
        scratch_shapes=ffn_scratch,
        compiler_params=params(_vmem_limit(ffn_scratch, double_buffered=[row_block_shape])),
        name="ffn_prompt",
    )(x1, p, *ffn_w)
    bf16_wts = dict(wts, **dict(zip(("w_in", "w_a_out", "pool_w", "w_o"), mixer_bf16)),
                    **dict(zip(("w_up", "w_down", "w_ple_proj", "w_ple_gate"), ffn_bf16)))
    return y, nconv[None], jnp.transpose(npool, (1, 0, 2))[None], nffn[None], bf16_wts


def _mixer_sample_kernel(x_ref, sconv_ref, spool_ref, g_pre, w_in, conv_w, w_a_out, pool_w, pool_scale, w_o,
                         g_post, x1_ref, nconv_ref, npool_ref, cbuf, ubuf, *, batch, steps, sub_steps):
    sub_rows = sub_steps * batch
    for k in range(N_CONV):
        cbuf[k * batch:(k + 1) * batch, :] = sconv_ref[:, k, :]

    w = dict(g_pre=g_pre, w_in=w_in, w_a_out=w_a_out, pool_w=pool_w, pool_scale=pool_scale, w_o=w_o,
             g_post=g_post)
    def conv_in(sb, j, v):
        cols = slice(j * CHUNK, (j + 1) * CHUNK)
        return _causal_conv(cbuf, cols, v, conv_w, N_CONV * batch + sb * sub_rows, batch, sub_rows)

    def pool_diff(sb, j, u):
        t0 = sb * sub_steps
        cols = slice(j * CHUNK, (j + 1) * CHUNK)
        window = POOL_WINDOWS[j]
        ubuf[t0:t0 + sub_steps, :, cols] = u.reshape(sub_steps, batch, CHUNK)

        def ext(e):
            return spool_ref[e, :, cols] if e < N_POOL else ubuf[e - N_POOL, :, cols]

        diffs = []
        for t in range(t0, t0 + sub_steps):
            tsum = ext(N_POOL + t - window + 1)
            for e in range(N_POOL + t - window + 2, N_POOL + t + 1):
                tsum = tsum + ext(e)
            diffs.append(tsum * (1.0 / window) - ext(N_POOL + t))
        return jnp.concatenate(diffs, axis=0)

    def load_x(sb):
        return jnp.concatenate([x_ref[:, t, :] for t in range(sb * sub_steps, (sb + 1) * sub_steps)], axis=0)

    def store_x1(sb, value):
        x1_ref[sb * sub_rows:(sb + 1) * sub_rows, :] = value

    _mixer_blocks(steps // sub_steps, load_x, store_x1, w, conv_in, pool_diff)

    for k in range(N_CONV):
        nconv_ref[:, k, :] = cbuf[(steps + k) * batch:(steps + k + 1) * batch, :]
    for m in range(N_POOL):
        e = steps + m
        npool_ref[m] = spool_ref[e] if e < N_POOL else ubuf[e - N_POOL]


def _ffn_sample_kernel(x_ref, p_ref, sffn_ref, g_pre, w_up, conv_w, w_down, g_post, w_proj, w_gate,
                       y_ref, nffn_ref, fbuf, *, batch, steps, sub_steps):
    sub_rows = sub_steps * batch
    for k in range(N_CONV):
        fbuf[k * batch:(k + 1) * batch, :] = sffn_ref[:, k, :]

    w = dict(g_pre=g_pre, w_up=w_up, w_down=w_down, g_post=g_post, w_proj=w_proj, w_gate=w_gate)
    def conv_in(sb, j, v):
        cols = slice(j * CHUNK, (j + 1) * CHUNK)
        return _causal_conv(fbuf, cols, v, conv_w, N_CONV * batch + sb * sub_rows, batch, sub_rows)

    def load_x(sb):
        return x_ref[sb * sub_rows:(sb + 1) * sub_rows, :]

    def load_p(sb):
        return jnp.concatenate([p_ref[:, t, :] for t in range(sb * sub_steps, (sb + 1) * sub_steps)], axis=0)

    def store_y(sb, value):
        for t in range(sub_steps):
            y_ref[:, sb * sub_steps + t, :] = value[t * batch:(t + 1) * batch, :]

    _ffn_blocks(steps // sub_steps, load_x, load_p, store_y, w, conv_in)

    for k in range(N_CONV):
        nffn_ref[:, k, :] = fbuf[(steps + k) * batch:(steps + k + 1) * batch, :]


def _sample_layer(x, p, sconv, spool, sffn, layer, wts):
    batch, steps, _ = x.shape
    sub_steps = max(1, SAMPLE_SUB_ROWS // batch)
    assert steps % sub_steps == 0 and batch % 8 == 0
    rows = batch * steps
    params = lambda limit: pltpu.CompilerParams(vmem_limit_bytes=limit)
    whole = pl.BlockSpec(memory_space=pltpu.VMEM)
    of_layer = lambda a, k: pl.BlockSpec((None,) + a.shape[1:], lambda: (k, 0, 0, 0))
    one_layer = lambda a: jax.ShapeDtypeStruct((1,) + a.shape[1:], F32)
    x1_shape = jax.ShapeDtypeStruct((rows, D_MODEL), F32)

    mixer_w = (wts["g_pre_mix"], wts["w_in"], wts["conv_a_w"], wts["w_a_out"], wts["pool_w"],
               wts["pool_scale"], wts["w_o"], wts["g_post_mix"])
    mixer_scratch = [pltpu.VMEM(((N_CONV + steps) * batch, D_MODEL), F32), pltpu.VMEM((steps, batch, D_MODEL), F32)]
    mixer_out = [x1_shape, one_layer(sconv), one_layer(spool)]
    x1, nconv, npool = pl.pallas_call(
        functools.partial(_mixer_sample_kernel, batch=batch, steps=steps, sub_steps=sub_steps),
        in_specs=[whole, of_layer(sconv, layer), of_layer(spool, layer)] + [whole] * len(mixer_w),
        out_specs=[whole, of_layer(sconv, 0), of_layer(spool, 0)],
        out_shape=mixer_out,
        scratch_shapes=mixer_scratch,
        compiler_params=params(_vmem_limit(mixer_scratch, single_buffered=[x, *mixer_out, *mixer_out[1:], *mixer_w])),
        name="mixer_sample",
    )(x, sconv, spool, *mixer_w)

    ffn_w = (wts["g_pre_ffn"], wts["w_up"], wts["ffn_conv_w"], wts["w_down"], wts["g_post_ffn"],
             wts["w_ple_proj"], wts["w_ple_gate"])
    ffn_scratch = [pltpu.VMEM(((N_CONV + steps) * batch, D_FF), F32)]
    ffn_out = [jax.ShapeDtypeStruct(x.shape, F32), one_layer(sffn)]
    y, nffn = pl.pallas_call(
        functools.partial(_ffn_sample_kernel, batch=batch, steps=steps, sub_steps=sub_steps),
        in_specs=[whole, of_layer(p, layer), of_layer(sffn, layer)] + [whole] * len(ffn_w),
        out_specs=[whole, of_layer(sffn, 0)],
        out_shape=ffn_out,
        scratch_shapes=ffn_scratch,
        compiler_params=params(_vmem_limit(
            ffn_scratch, single_buffered=[x1_shape, one_layer(p), *ffn_out, ffn_out[1], *ffn_w])),
        name="ffn_sample",
    )(x1, p, sffn, *ffn_w)
    return y, nconv, npool, nffn


def kernel(x_prompt, x_sample, p_prompt, p_sample, state_conv_a, state_pool, state_ffn_conv, g_pre_mix, w_in,
           conv_a_w, w_a_out, pool_w, pool_scale, w_o, g_post_mix, g_pre_ffn, w_up, ffn_conv_w, w_down,
           g_post_ffn, w_ple_proj, w_ple_gate):
    depth = w_in.shape[0]
    xp, xs = x_prompt, x_sample
    prompt_states, sample_states = [], []
    time_major = lambda s: jnp.transpose(s, (0, 2, 1, 3))
    state_pool_tm = time_major(state_pool)
    for i in range(depth):
        wts = dict(
            g_pre_mix=g_pre_mix[i][None], g_post_mix=g_post_mix[i][None], g_pre_ffn=g_pre_ffn[i][None],
            g_post_ffn=g_post_ffn[i][None], pool_scale=pool_scale[i][None], conv_a_w=conv_a_w[i],
            ffn_conv_w=ffn_conv_w[i],
            w_in=w_in[i], w_a_out=w_a_out[i], pool_w=pool_w[i].reshape(D_MODEL, CHUNK),
            w_o=w_o[i], w_up=w_up[i], w_down=w_down[i], w_ple_proj=w_ple_proj[i], w_ple_gate=w_ple_gate[i])
        xp, c1, p1, f1, bf16_wts = _prompt_layer(xp, p_prompt[i], wts)
        xs, c2, p2, f2 = _sample_layer(xs, p_sample, state_conv_a, state_pool_tm, state_ffn_conv, i, bf16_wts)
        prompt_states.append((c1, p1, f1))
        sample_states.append((c2, time_major(p2), f2))
    cat = lambda states, k: states[0][k] if depth == 1 else jnp.concatenate([s[k] for s in states], axis=0)
    return (xp, xs, cat(prompt_states, 0), cat(prompt_states, 1), cat(prompt_states, 2),
            cat(sample_states, 0), cat(sample_states, 1), cat(sample_states, 2))
```

```python
import functools
import math

import jax
import jax.numpy as jnp
from jax import lax
from jax.experimental import pallas as pl
from jax.experimental.pallas import tpu as pltpu

D_MODEL = 1024
D_FF = 2816
PLE_DIM = 256
CONV_W = 3
POOL_WINDOWS = (2, 4, 8, 16)
POOL_MAX = 16
EPS = 1e-6

N_CONV = CONV_W - 1
N_POOL = POOL_MAX - 1
CHUNK = 256
N_MIX_CHUNKS = D_MODEL // CHUNK
N_FF_CHUNKS = D_FF // CHUNK
TIME_STEPS = 64
SUB_STEPS = 32
SAMPLE_SUB_ROWS = 256
MIXER_NORM_AHEAD_CHUNK = 2
FFN_NORM_AHEAD_CHUNK = 8
FFN_FINISH_CHUNK = 1
WEIGHT_CHUNK_ROWS = 128
WEIGHT_STAGE_SLOTS = 4
VMEM_CAPACITY_BYTES = 64 * 1024 * 1024
VMEM_COMPILER_SCRATCH_BYTES = 6 * 1024 * 1024
VMEM_UNREQUESTABLE_BYTES = 4 * 1024 * 1024

F32 = jnp.float32
BF16 = jnp.bfloat16


def _nbytes(a):
    return math.prod(a.shape) * jnp.dtype(a.dtype).itemsize


def _vmem_limit(scratch, single_buffered=(), double_buffered=()):
    total = sum(_nbytes(s) for s in scratch if s.memory_space == pltpu.VMEM)
    total += sum(_nbytes(a) for a in single_buffered) + 2 * sum(_nbytes(a) for a in double_buffered)
    return min(total + VMEM_COMPILER_SCRATCH_BYTES, VMEM_CAPACITY_BYTES - VMEM_UNREQUESTABLE_BYTES)


def _rmsnorm(x, g):
    ms = jnp.mean(x * x, axis=-1, keepdims=True)
    return (x * lax.rsqrt(ms + EPS)) * g


def _dot(a, b):
    return jnp.dot(a, b, preferred_element_type=F32)


def _causal_conv(buf, cols, v, w_ref, halo, step, rows):
    buf[pl.ds(halo, rows), cols] = v
    y = buf[pl.ds(halo - 2 * step, rows), cols] * w_ref[0:1, cols]
    y = y + buf[pl.ds(halo - step, rows), cols] * w_ref[1:2, cols]
    return y + v * w_ref[2:3, cols]


def _mixer_blocks(n_sub, load_x, store_out, w, conv_in, pool_diff):
    def normed(sb):
        x = load_x(sb)
        return x, _rmsnorm(x, w["g_pre"][...]).astype(BF16)

    def finish(sb, x, mix):
        store_out(sb, x + _rmsnorm(mix, w["g_post"][...]))

    cur = normed(0)
    pending = None
    for sb in range(n_sub):
        x, xb = cur

        def proj(k, j, xb=xb):
            lo = k * D_MODEL + j * CHUNK
            return _dot(xb, w["w_in"][:, lo:lo + CHUNK])

        def branch_proj(j):
            return tuple(proj(k, j) for k in range(4))

        def gate_proj(j):
            return proj(4, j), proj(5, j)

        nxt = branch_proj(0)
        y_a = None
        y_p = []
        for j in range(N_MIX_CHUNKS):
            cols = slice(j * CHUNK, (j + 1) * CHUNK)
            b, c, h, u = nxt
            nxt = branch_proj(j + 1) if j + 1 < N_MIX_CHUNKS else gate_proj(0)
            if j == 0 and pending is not None:
                finish(*pending)
            if j == MIXER_NORM_AHEAD_CHUNK and sb + 1 < n_sub:
                cur = normed(sb + 1)
            z = (b * conv_in(sb, j, c * h)).astype(BF16)
            t = _dot(z, w["w_a_out"][cols, :])
            y_a = t if y_a is None else y_a + t
            d = pool_diff(sb, j, u).astype(BF16)
            y_p.append(_dot(d, w["pool_w"][cols, :]) * w["pool_scale"][:, cols])
        mix = None
        for j in range(N_MIX_CHUNKS):
            cols = slice(j * CHUNK, (j + 1) * CHUNK)
            ga, gp = nxt
            if j + 1 < N_MIX_CHUNKS:
                nxt = gate_proj(j + 1)
            merged = jax.nn.sigmoid(ga) * y_a[:, cols] + jax.nn.sigmoid(gp) * y_p[j]
            t = _dot(merged.astype(BF16), w["w_o"][cols, :])
            mix = t if mix is None else mix + t
        pending = (sb, x, mix)
    finish(*pending)


def _ffn_blocks(n_sub, load_x, load_p, store_out, w, conv_in):
    def normed(sb):
        x = load_x(sb)
        return x, _rmsnorm(x, w["g_pre"][...]).astype(BF16)

    def finish(sb, x, f, ple):
        x = x + _rmsnorm(f, w["g_post"][...])
        gate = jax.nn.sigmoid(_dot(x.astype(BF16), w["w_gate"][...]))
        store_out(sb, x + gate * ple)

    cur = normed(0)
    pending = None
    for sb in range(n_sub):
        x, hb = cur

        def up(j, hb=hb):
            return (_dot(hb, w["w_up"][:, j * CHUNK:(j + 1) * CHUNK]),
                    _dot(hb, w["w_up"][:, D_FF + j * CHUNK:D_FF + (j + 1) * CHUNK]))

        ple = _dot(load_p(sb).astype(BF16), w["w_proj"][...])
        nxt = up(0)
        f = None
        for j in range(N_FF_CHUNKS):
            cols = slice(j * CHUNK, (j + 1) * CHUNK)
            a, g = nxt
            if j + 1 < N_FF_CHUNKS:
                nxt = up(j + 1)
            if j == FFN_FINISH_CHUNK and pending is not None:
                finish(*pending)
            if j == FFN_NORM_AHEAD_CHUNK and sb + 1 < n_sub:
                cur = normed(sb + 1)
            h = jax.nn.gelu(conv_in(sb, j, a), approximate=True) * g
            t = _dot(h.astype(BF16), w["w_down"][cols, :])
            f = t if f is None else f + t
        pending = (sb, x, f, ple)
    finish(*pending)


def _block_copy(hbm, buf, sem, block, slot, seq, steps, *, to_hbm=False):
    hbm_view = hbm.at[seq, pl.ds(block * steps, steps), :]
    buf_view = buf.at[slot, :, seq, :]
    src, dst = (buf_view, hbm_view) if to_hbm else (hbm_view, buf_view)
    return pltpu.make_async_copy(src, dst, sem.at[slot, seq])


def _start_block_fetch(hbm, buf, sem, block, slot, steps, n_seq):
    for s in range(n_seq):
        _block_copy(hbm, buf, sem, block, slot, s, steps).start()


def _fetch_block(hbm, buf, sem, steps, n_seq):
    i = pl.program_id(0)
    slot = i % 2

    @pl.when(i + 1 < pl.num_programs(0))
    def _():
        _start_block_fetch(hbm, buf, sem, i + 1, 1 - slot, steps, n_seq)

    for s in range(n_seq):
        _block_copy(hbm, buf, sem, i, slot, s, steps).wait()
    return slot


def _stream_weight_as_bf16(w_hbm, w_vmem, stage, sem):
    n_rows, width = w_hbm.shape
    n_slots, chunk, _ = stage.shape
    n_chunks = n_rows // chunk
    assert n_chunks * chunk == n_rows and width <= stage.shape[2]

    def chunk_copy(c, slot):
        return pltpu.make_async_copy(w_hbm.at[pl.ds(c * chunk, chunk), :], stage.at[slot, :, 0:width], sem.at[slot])

    for c in range(min(n_slots, n_chunks)):
        chunk_copy(c, c).start()

    def body(c, carry):
        slot = c % n_slots
        chunk_copy(c, slot).wait()
        w_vmem[pl.ds(pl.multiple_of(c * chunk, chunk), chunk), :] = stage[slot, :, 0:width].astype(BF16)

        @pl.when(c + n_slots < n_chunks)
        def _():
            chunk_copy(c + n_slots, slot).start()

        return carry

    lax.fori_loop(0, n_chunks, body, 0)


def _load_weights(pairs, exports, stage, stage_sem, export_sem):
    for w_hbm, w_vmem in pairs:
        _stream_weight_as_bf16(w_hbm, w_vmem, stage, stage_sem)
    for k, ((_, w_vmem), w_out) in enumerate(zip(pairs, exports)):
        pltpu.make_async_copy(w_vmem, w_out, export_sem.at[k]).start()


def _finish_weight_exports(pairs, exports, export_sem):
    for k, ((_, w_vmem), w_out) in enumerate(zip(pairs, exports)):
        pltpu.make_async_copy(w_vmem, w_out, export_sem.at[k]).wait()


def _mixer_prompt_kernel(x_hbm, g_pre, w_in_hbm, conv_w, w_a_out_hbm, pool_w_hbm, pool_scale, w_o_hbm, g_post,
                         x1_ref, nconv_ref, npool_ref, w_in_out, w_a_out_out, pool_w_out, w_o_out,
                         xbuf, xsem, cbuf, ubuf, e0, e1, e2, inv_cnt,
                         w_in, w_a_out, pool_w, w_o, stage, stage_sem, export_sem,
                         *, steps, sub_steps, n_seq):
    i = pl.program_id(0)
    weight_pairs = ((w_in_hbm, w_in), (w_a_out_hbm, w_a_out), (pool_w_hbm, pool_w), (w_o_hbm, w_o))
    weight_exports = (w_in_out, w_a_out_out, pool_w_out, w_o_out)
    sub_rows = sub_steps * n_seq
    n_sub = steps // sub_steps
    conv_halo = N_CONV * n_seq
    pool_halo = N_POOL * n_seq
    head = POOL_MAX * n_seq
    ebufs = (e0, e1, e2)

    @pl.when(i == 0)
    def _():
        _start_block_fetch(x_hbm, xbuf, xsem, 0, 0, steps, n_seq)
        cbuf[0:conv_halo, :] = jnp.zeros((conv_halo, D_MODEL), F32)
        ubuf[0:pool_halo, :] = jnp.zeros((pool_halo, D_MODEL), F32)
        pos = lax.broadcasted_iota(jnp.int32, (POOL_MAX, n_seq, CHUNK), 0).reshape(head, CHUNK)
        for j, window in enumerate(POOL_WINDOWS):
            inv_cnt[j] = 1.0 / jnp.minimum(pos + 1, window).astype(F32)
        _load_weights(weight_pairs, weight_exports, stage, stage_sem, export_sem)

    @pl.when(i == 1)
    def _():
        for j, window in enumerate(POOL_WINDOWS):
            inv_cnt[j] = jnp.full((head, CHUNK), 1.0 / window, F32)

    slot = _fetch_block(x_hbm, xbuf, xsem, steps, n_seq)
    w = dict(g_pre=g_pre, w_in=w_in, w_a_out=w_a_out, pool_w=pool_w, pool_scale=pool_scale, w_o=w_o,
             g_post=g_post)

    def conv_in(sb, j, v):
        cols = slice(j * CHUNK, (j + 1) * CHUNK)
        y = _causal_conv(cbuf, cols, v, conv_w, conv_halo, n_seq, sub_rows)
        cbuf[0:conv_halo, cols] = cbuf[pl.ds(sub_rows, conv_halo), cols]
        return y

    def pool_diff(sb, j, u):
        cols = slice(j * CHUNK, (j + 1) * CHUNK)
        window = POOL_WINDOWS[j]
        ubuf[pl.ds(pool_halo, sub_rows), cols] = u
        n_stages = j + 1
        tsum = None
        for k in range(n_stages):
            lo = pool_halo - (window - (2 << k)) * n_seq
            n = pool_halo + sub_rows - lo
            shift = (1 << k) * n_seq
            if k == 0:
                tsum = ubuf[pl.ds(lo, n), cols] + ubuf[pl.ds(lo - shift, n), cols]
            else:
                tsum = ebufs[k - 1][pl.ds(lo, n), cols] + ebufs[k - 1][pl.ds(lo - shift, n), cols]
            if k < n_stages - 1:
                ebufs[k][pl.ds(lo, n), cols] = tsum
        keep = (window - 1) * n_seq
        ubuf[pl.ds(pool_halo - keep, keep), cols] = ubuf[pl.ds(pool_halo + sub_rows - keep, keep), cols]
        if sb > 0:
            return tsum * (1.0 / window) - u
        return jnp.concatenate([tsum[0:head, :] * inv_cnt[j] - u[0:head, :],
                                tsum[head:, :] * (1.0 / window) - u[head:, :]], axis=0)

    def load_x(sb):
        return xbuf[slot, sb * sub_steps:(sb + 1) * sub_steps].reshape(sub_rows, D_MODEL)

    def store_x1(sb, value):
        x1_ref[sb * sub_rows:(sb + 1) * sub_rows, :] = value

    _mixer_blocks(n_sub, load_x, store_x1, w, conv_in, pool_diff)

    @pl.when(i == pl.num_programs(0) - 1)
    def _():
        for k in range(N_CONV):
            nconv_ref[:, k, :] = cbuf[k * n_seq:(k + 1) * n_seq, :]
        npool_ref[...] = ubuf[pl.ds(sub_rows, pool_halo), :].reshape(N_POOL, n_seq, D_MODEL)
        _finish_weight_exports(weight_pairs, weight_exports, export_sem)


def _ffn_prompt_kernel(x_ref, p_hbm, g_pre, w_up_hbm, conv_w, w_down_hbm, g_post, w_proj_hbm, w_gate_hbm,
                       y_hbm, nffn_ref, w_up_out, w_down_out, w_proj_out, w_gate_out,
                       pbuf, psem, ybuf, ysem, fbuf,
                       w_up, w_down, w_proj, w_gate, stage, stage_sem, export_sem,
                       *, steps, sub_steps, n_seq):
    i = pl.program_id(0)
    last = pl.num_programs(0) - 1
    sub_rows = sub_steps * n_seq
    n_sub = steps // sub_steps
    conv_halo = N_CONV * n_seq
    weight_pairs = ((w_up_hbm, w_up), (w_down_hbm, w_down), (w_proj_hbm, w_proj), (w_gate_hbm, w_gate))
    weight_exports = (w_up_out, w_down_out, w_proj_out, w_gate_out)

    @pl.when(i == 0)
    def _():
        _start_block_fetch(p_hbm, pbuf, psem, 0, 0, steps, n_seq)
        fbuf[0:conv_halo, :] = jnp.zeros((conv_halo, D_FF), F32)
        _load_weights(weight_pairs, weight_exports, stage, stage_sem, export_sem)

    def wait_store(block, slot):
        for s in range(n_seq):
            _block_copy(y_hbm, ybuf, ysem, block, slot, s, steps, to_hbm=True).wait()

    slot = _fetch_block(p_hbm, pbuf, psem, steps, n_seq)

    @pl.when(i >= 2)
    def _():
        wait_store(i - 2, slot)

    w = dict(g_pre=g_pre, w_up=w_up, w_down=w_down, g_post=g_post, w_proj=w_proj, w_gate=w_gate)

    def conv_in(sb, j, v):
        cols = slice(j * CHUNK, (j + 1) * CHUNK)
        y = _causal_conv(fbuf, cols, v, conv_w, conv_halo, n_seq, sub_rows)
        fbuf[0:conv_halo, cols] = fbuf[pl.ds(sub_rows, conv_halo), cols]
        return y

    def load_x(sb):
        return x_ref[sb * sub_rows:(sb + 1) * sub_rows, :]

    def load_p(sb):
        return pbuf[slot, sb * sub_steps:(sb + 1) * sub_steps].reshape(sub_rows, PLE_DIM)

    def store_y(sb, value):
        ybuf[slot, sb * sub_steps:(sb + 1) * sub_steps] = value.reshape(sub_steps, n_seq, D_MODEL)

    _ffn_blocks(n_sub, load_x, load_p, store_y, w, conv_in)

    for s in range(n_seq):
        _block_copy(y_hbm, ybuf, ysem, i, slot, s, steps, to_hbm=True).start()

    @pl.when(i == last)
    def _():
        for k in range(N_CONV):
            nffn_ref[:, k, :] = fbuf[k * n_seq:(k + 1) * n_seq, :]
        wait_store(i - 1, 1 - slot)
        wait_store(i, slot)
        _finish_weight_exports(weight_pairs, weight_exports, export_sem)


def _resident(shape):
    return pl.BlockSpec(shape, lambda i: (0,) * len(shape), pipeline_mode=pl.Buffered(1))


def _prompt_layer(x, p, wts):
    n_seq, seq_len, _ = x.shape
    steps, sub_steps = TIME_STEPS, SUB_STEPS
    n_blocks = seq_len // steps
    assert n_blocks * steps == seq_len and n_blocks >= 2 and n_seq % 8 == 0
    assert steps % sub_steps == 0 and sub_steps > POOL_MAX
    rows = steps * n_seq
    params = lambda limit: pltpu.CompilerParams(dimension_semantics=("arbitrary",), vmem_limit_bytes=limit)
    any_space = pl.BlockSpec(memory_space=pl.ANY)
    row_block = pl.BlockSpec((rows, D_MODEL), lambda i: (i, 0))
    row_block_shape = jax.ShapeDtypeStruct((rows, D_MODEL), F32)
    whole_block = lambda shape: pl.BlockSpec(shape, lambda i: (0,) * len(shape))
    conv_state = lambda width: (n_seq, N_CONV, width)
    pool_state = (N_POOL, n_seq, D_MODEL)

    weight_spec = lambda a: any_space if a.dtype == F32 and a.shape[0] >= WEIGHT_CHUNK_ROWS else _resident(a.shape)
    bf16_like = lambda ws: [jax.ShapeDtypeStruct(a.shape, BF16) for a in ws]
    vmem_bf16 = lambda ws: [pltpu.VMEM(a.shape, BF16) for a in ws]
    stage = lambda ws: [pltpu.VMEM((WEIGHT_STAGE_SLOTS, WEIGHT_CHUNK_ROWS, max(a.shape[1] for a in ws)), F32),
                        pltpu.SemaphoreType.DMA((WEIGHT_STAGE_SLOTS,)), pltpu.SemaphoreType.DMA((len(ws),))]

    mixer_w = (wts["g_pre_mix"], wts["w_in"], wts["conv_a_w"], wts["w_a_out"], wts["pool_w"],
               wts["pool_scale"], wts["w_o"], wts["g_post_mix"])
    mixer_mats = (wts["w_in"], wts["w_a_out"], wts["pool_w"], wts["w_o"])
    sub_rows = sub_steps * n_seq
    stage_rows = N_POOL * n_seq + sub_rows
    mixer_scratch = ([pltpu.VMEM((2, steps, n_seq, D_MODEL), F32), pltpu.SemaphoreType.DMA((2, n_seq)),
                      pltpu.VMEM((N_CONV * n_seq + sub_rows, D_MODEL), F32),
                      pltpu.VMEM((stage_rows, D_MODEL), F32),
                      pltpu.VMEM((stage_rows, D_MODEL), F32),
                      pltpu.VMEM((stage_rows, D_MODEL), F32),
                      pltpu.VMEM((stage_rows, D_MODEL), F32),
                      pltpu.VMEM((len(POOL_WINDOWS), POOL_MAX * n_seq, CHUNK), F32)]
                     + vmem_bf16(mixer_mats) + stage(mixer_mats))
    x1, nconv, npool, *mixer_bf16 = pl.pallas_call(
        functools.partial(_mixer_prompt_kernel, steps=steps, sub_steps=sub_steps, n_seq=n_seq),
        grid=(n_blocks,),
        in_specs=[any_space] + [weight_spec(a) for a in mixer_w],
        out_specs=[row_block, whole_block(conv_state(D_MODEL)), whole_block(pool_state)] + [any_space] * 4,
        out_shape=[jax.ShapeDtypeStruct((n_blocks * rows, D_MODEL), F32),
                   jax.ShapeDtypeStruct(conv_state(D_MODEL), F32),
                   jax.ShapeDtypeStruct(pool_state, F32)] + bf16_like(mixer_mats),
        scratch_shapes=mixer_scratch,
        compiler_params=params(_vmem_limit(mixer_scratch, double_buffered=[row_block_shape])),
        name="mixer_prompt",
    )(x, *mixer_w)

    ffn_w = (wts["g_pre_ffn"], wts["w_up"], wts["ffn_conv_w"], wts["w_down"], wts["g_post_ffn"],
             wts["w_ple_proj"], wts["w_ple_gate"])
    ffn_mats = (wts["w_up"], wts["w_down"], wts["w_ple_proj"], wts["w_ple_gate"])
    ffn_scratch = ([pltpu.VMEM((2, steps, n_seq, PLE_DIM), F32), pltpu.SemaphoreType.DMA((2, n_seq)),
                    pltpu.VMEM((2, steps, n_seq, D_MODEL), F32), pltpu.SemaphoreType.DMA((2, n_seq)),
                    pltpu.VMEM((N_CONV * n_seq + sub_rows, D_FF), F32)]
                   + vmem_bf16(ffn_mats) + stage(ffn_mats))
    y, nffn, *ffn_bf16 = pl.pallas_call(
        functools.partial(_ffn_prompt_kernel, steps=steps, sub_steps=sub_steps, n_seq=n_seq),
        grid=(n_blocks,),
        in_specs=[row_block, any_space] + [weight_spec(a) for a in ffn_w],
        out_specs=[any_space, whole_block(conv_state(D_FF))] + [any_space] * 4,
        out_shape=[jax.ShapeDtypeStruct((n_seq, seq_len, D_MODEL), F32),
                   jax.ShapeDtypeStruct(conv_state(D_FF), F32)] + bf16_like(ffn_mats),
        scratch_shapes=ffn_scratch,
        compiler_params=params(_vmem_limit(ffn_scratch, double_buffered=[row_block_shape])),
        name="ffn_prompt",
    )(x1, p, *ffn_w)
    bf16_wts = dict(wts, **dict(zip(("w_in", "w_a_out", "pool_w", "w_o"), mixer_bf16)),
                    **dict(zip(("w_up", "w_down", "w_ple_proj", "w_ple_gate"), ffn_bf16)))
    return y, nconv[None], jnp.transpose(npool, (1, 0, 2))[None], nffn[None], bf16_wts


def _mixer_sample_kernel(x_ref, sconv_ref, spool_ref, g_pre, w_in, conv_w, w_a_out, pool_w, pool_scale, w_o,
                         g_post, x1_ref, nconv_ref, npool_ref, cbuf, ubuf, *, batch, steps, sub_steps):
    sub_rows = sub_steps * batch
    for k in range(N_CONV):
        cbuf[k * batch:(k + 1) * batch, :] = sconv_ref[:, k, :]

    w = dict(g_pre=g_pre, w_in=w_in, w_a_out=w_a_out, pool_w=pool_w, pool_scale=pool_scale, w_o=w_o,
             g_post=g_post)
    def conv_in(sb, j, v):
        cols = slice(j * CHUNK, (j + 1) * CHUNK)
        return _causal_conv(cbuf, cols, v, conv_w, N_CONV * batch + sb * sub_rows, batch, sub_rows)

    def pool_diff(sb, j, u):
        t0 = sb * sub_steps
        cols = slice(j * CHUNK, (j + 1) * CHUNK)
        window = POOL_WINDOWS[j]
        ubuf[t0:t0 + sub_steps, :, cols] = u.reshape(sub_steps, batch, CHUNK)

        def ext(e):
            return spool_ref[e, :, cols] if e < N_POOL else ubuf[e - N_POOL, :, cols]

        diffs = []
        for t in range(t0, t0 + sub_steps):
            tsum = ext(N_POOL + t - window + 1)
            for e in range(N_POOL + t - window + 2, N_POOL + t + 1):
                tsum = tsum + ext(e)
            diffs.append(tsum * (1.0 / window) - ext(N_POOL + t))
        return jnp.concatenate(diffs, axis=0)

    def load_x(sb):
        return jnp.concatenate([x_ref[:, t, :] for t in range(sb * sub_steps, (sb + 1) * sub_steps)], axis=0)

    def store_x1(sb, value):
        x1_ref[sb * sub_rows:(sb + 1) * sub_rows, :] = value

    _mixer_blocks(steps // sub_steps, load_x, store_x1, w, conv_in, pool_diff)

    for k in range(N_CONV):
        nconv_ref[:, k, :] = cbuf[(steps + k) * batch:(steps + k + 1) * batch, :]
    for m in range(N_POOL):
        e = steps + m
        npool_ref[m] = spool_ref[e] if e < N_POOL else ubuf[e - N_POOL]


def _ffn_sample_kernel(x_ref, p_ref, sffn_ref, g_pre, w_up, conv_w, w_down, g_post, w_proj, w_gate,
                       y_ref, nffn_ref, fbuf, *, batch, steps, sub_steps):
    sub_rows = sub_steps * batch
    for k in range(N_CONV):
        fbuf[k * batch:(k + 1) * batch, :] = sffn_ref[:, k, :]

    w = dict(g_pre=g_pre, w_up=w_up, w_down=w_down, g_post=g_post, w_proj=w_proj, w_gate=w_gate)
    def conv_in(sb, j, v):
        cols = slice(j * CHUNK, (j + 1) * CHUNK)
        return _causal_conv(fbuf, cols, v, conv_w, N_CONV * batch + sb * sub_rows, batch, sub_rows)

    def load_x(sb):
        return x_ref[sb * sub_rows:(sb + 1) * sub_rows, :]

    def load_p(sb):
        return jnp.concatenate([p_ref[:, t, :] for t in range(sb * sub_steps, (sb + 1) * sub_steps)], axis=0)

    def store_y(sb, value):
        for t in range(sub_steps):
            y_ref[:, sb * sub_steps + t, :] = value[t * batch:(t + 1) * batch, :]

    _ffn_blocks(steps // sub_steps, load_x, load_p, store_y, w, conv_in)

    for k in range(N_CONV):
        nffn_ref[:, k, :] = fbuf[(steps + k) * batch:(steps + k + 1) * batch, :]


def _sample_layer(x, p, sconv, spool, sffn, layer, wts):
    batch, steps, _ = x.shape
    sub_steps = max(1, SAMPLE_SUB_ROWS // batch)
    assert steps % sub_steps == 0 and batch % 8 == 0
    rows = batch * steps
    params = lambda limit: pltpu.CompilerParams(vmem_limit_bytes=limit)
    whole = pl.BlockSpec(memory_space=pltpu.VMEM)
    of_layer = lambda a, k: pl.BlockSpec((None,) + a.shape[1:], lambda: (k, 0, 0, 0))
    one_layer = lambda a: jax.ShapeDtypeStruct((1,) + a.shape[1:], F32)
    x1_shape = jax.ShapeDtypeStruct((rows, D_MODEL), F32)

    mixer_w = (wts["g_pre_mix"], wts["w_in"], wts["conv_a_w"], wts["w_a_out"], wts["pool_w"],
               wts["pool_scale"], wts["w_o"], wts["g_post_mix"])
    mixer_scratch = [pltpu.VMEM(((N_CONV + steps) * batch, D_MODEL), F32), pltpu.VMEM((steps, batch, D_MODEL), F32)]
    mixer_out = [x1_shape, one_layer(sconv), one_layer(spool)]
    x1, nconv, npool = pl.pallas_call(
        functools.partial(_mixer_sample_kernel, batch=batch, steps=steps, sub_steps=sub_steps),
        in_specs=[whole, of_layer(sconv, layer), of_layer(spool, layer)] + [whole] * len(mixer_w),
        out_specs=[whole, of_layer(sconv, 0), of_layer(spool, 0)],
        out_shape=mixer_out,
        scratch_shapes=mixer_scratch,
        compiler_params=params(_vmem_limit(mixer_scratch, single_buffered=[x, *mixer_out, *mixer_out[1:], *mixer_w])),
        name="mixer_sample",
    )(x, sconv, spool, *mixer_w)

    ffn_w = (wts["g_pre_ffn"], wts["w_up"], wts["ffn_conv_w"], wts["w_down"], wts["g_post_ffn"],
             wts["w_ple_proj"], wts["w_ple_gate"])
    ffn_scratch = [pltpu.VMEM(((N_CONV + steps) * batch, D_FF), F32)]
    ffn_out = [jax.ShapeDtypeStruct(x.shape, F32), one_layer(sffn)]
    y, nffn = pl.pallas_call(
        functools.partial(_ffn_sample_kernel, batch=batch, steps=steps, sub_steps=sub_steps),
        in_specs=[whole, of_layer(p, layer), of_layer(sffn, layer)] + [whole] * len(ffn_w),
        out_specs=[whole, of_layer(sffn, 0)],
        out_shape=ffn_out,
        scratch_shapes=ffn_scratch,
        compiler_params=params(_vmem_limit(
            ffn_scratch, single_buffered=[x1_shape, one_layer(p), *ffn_out, ffn_out[1], *ffn_w])),
        name="ffn_sample",
    )(x1, p, sffn, *ffn_w)
    return y, nconv, npool, nffn


def kernel(x_prompt, x_sample, p_prompt, p_sample, state_conv_a, state_pool, state_ffn_conv, g_pre_mix, w_in,
           conv_a_w, w_a_out, pool_w, pool_scale, w_o, g_post_mix, g_pre_ffn, w_up, ffn_conv_w, w_down,
           g_post_ffn, w_ple_proj, w_ple_gate):
    depth = w_in.shape[0]
    xp, xs = x_prompt, x_sample
    prompt_states, sample_states = [], []
    time_major = lambda s: jnp.transpose(s, (0, 2, 1, 3))
    state_pool_tm = time_major(state_pool)
    for i in range(depth):
        wts = dict(
            g_pre_mix=g_pre_mix[i][None], g_post_mix=g_post_mix[i][None], g_pre_ffn=g_pre_ffn[i][None],
            g_post_ffn=g_post_ffn[i][None], pool_scale=pool_scale[i][None], conv_a_w=conv_a_w[i],
            ffn_conv_w=ffn_conv_w[i],
            w_in=w_in[i], w_a_out=w_a_out[i], pool_w=pool_w[i].reshape(D_MODEL, CHUNK),
            w_o=w_o[i], w_up=w_up[i], w_down=w_down[i], w_ple_proj=w_ple_proj[i], w_ple_gate=w_ple_gate[i])
        xp, c1, p1, f1, bf16_wts = _prompt_layer(xp, p_prompt[i], wts)
        xs, c2, p2, f2 = _sample_layer(xs, p_sample, state_conv_a, state_pool_tm, state_ffn_conv, i, bf16_wts)
        prompt_states.append((c1, p1, f1))
        sample_states.append((c2, time_major(p2), f2))
    cat = lambda states, k: states[0][k] if depth == 1 else jnp.concatenate([s[k] for s in states], axis=0)
    return (xp, xs, cat(prompt_states, 0), cat(prompt_states, 1), cat(prompt_states, 2),
            cat(sample_states, 0), cat(sample_states, 1), cat(sample_states, 2))
```

```python
import functools
import math

import jax
import jax.numpy as jnp
from jax import lax
from jax.experimental import pallas as pl
from jax.experimental.pallas import tpu as pltpu

D_MODEL = 1024
D_FF = 2816
PLE_DIM = 256
CONV_W = 3
POOL_WINDOWS = (2, 4, 8, 16)
POOL_MAX = 16
EPS = 1e-6

N_CONV = CONV_W - 1
N_POOL = POOL_MAX - 1
CHUNK = 256
N_MIX_CHUNKS = D_MODEL // CHUNK
N_FF_CHUNKS = D_FF // CHUNK
TIME_STEPS = 128
SUB_STEPS = 32
SAMPLE_SUB_ROWS = 256
MIXER_NORM_AHEAD_CHUNK = 2
FFN_NORM_AHEAD_CHUNK = 8
FFN_FINISH_CHUNK = 1
WEIGHT_CHUNK_ROWS = 128
WEIGHT_STAGE_SLOTS = 4
VMEM_CAPACITY_BYTES = 64 * 1024 * 1024
VMEM_COMPILER_SCRATCH_BYTES = 6 * 1024 * 1024
VMEM_UNREQUESTABLE_BYTES = 4 * 1024 * 1024

F32 = jnp.float32
BF16 = jnp.bfloat16


def _nbytes(a):
    return math.prod(a.shape) * jnp.dtype(a.dtype).itemsize


def _vmem_limit(scratch, single_buffered=(), double_buffered=()):
    total = sum(_nbytes(s) for s in scratch if s.memory_space == pltpu.VMEM)
    total += sum(_nbytes(a) for a in single_buffered) + 2 * sum(_nbytes(a) for a in double_buffered)
    return min(total + VMEM_COMPILER_SCRATCH_BYTES, VMEM_CAPACITY_BYTES - VMEM_UNREQUESTABLE_BYTES)


def _rmsnorm(x, g):
    ms = jnp.mean(x * x, axis=-1, keepdims=True)
    return (x * lax.rsqrt(ms + EPS)) * g


def _dot(a, b):
    return jnp.dot(a, b, preferred_element_type=F32)


def _causal_conv(buf, cols, v, w_ref, halo, step, rows):
    buf[pl.ds(halo, rows), cols] = v
    y = buf[pl.ds(halo - 2 * step, rows), cols] * w_ref[0:1, cols]
    y = y + buf[pl.ds(halo - step, rows), cols] * w_ref[1:2, cols]
    return y + v * w_ref[2:3, cols]


def _mixer_blocks(n_sub, load_x, store_out, w, conv_in, pool_diff):
    def normed(sb):
        x = load_x(sb)
        return x, _rmsnorm(x, w["g_pre"][...]).astype(BF16)

    def finish(sb, x, mix):
        store_out(sb, x + _rmsnorm(mix, w["g_post"][...]))

    cur = normed(0)
    pending = None
    for sb in range(n_sub):
        x, xb = cur

        def proj(k, j, xb=xb):
            lo = k * D_MODEL + j * CHUNK
            return _dot(xb, w["w_in"][:, lo:lo + CHUNK])

        def branch_proj(j):
            return tuple(proj(k, j) for k in range(4))

        def gate_proj(j):
            return proj(4, j), proj(5, j)

        nxt = branch_proj(0)
        y_a = None
        y_p = []
        for j in range(N_MIX_CHUNKS):
            cols = slice(j * CHUNK, (j + 1) * CHUNK)
            b, c, h, u = nxt
            nxt = branch_proj(j + 1) if j + 1 < N_MIX_CHUNKS else gate_proj(0)
            if j == 0 and pending is not None:
                finish(*pending)
            if j == MIXER_NORM_AHEAD_CHUNK and sb + 1 < n_sub:
                cur = normed(sb + 1)
            z = (b * conv_in(sb, j, c * h)).astype(BF16)
            t = _dot(z, w["w_a_out"][cols, :])
            y_a = t if y_a is None else y_a + t
            d = pool_diff(sb, j, u).astype(BF16)
            y_p.append(_dot(d, w["pool_w"][cols, :]) * w["pool_scale"][:, cols])
        mix = None
        for j in range(N_MIX_CHUNKS):
            cols = slice(j * CHUNK, (j + 1) * CHUNK)
            ga, gp = nxt
            if j + 1 < N_MIX_CHUNKS:
                nxt = gate_proj(j + 1)
            merged = jax.nn.sigmoid(ga) * y_a[:, cols] + jax.nn.sigmoid(gp) * y_p[j]
            t = _dot(merged.astype(BF16), w["w_o"][cols, :])
            mix = t if mix is None else mix + t
        pending = (sb, x, mix)
    finish(*pending)


def _ffn_blocks(n_sub, load_x, load_p, store_out, w, conv_in):
    def normed(sb):
        x = load_x(sb)
        return x, _rmsnorm(x, w["g_pre"][...]).astype(BF16)

    def finish(sb, x, f, ple):
        x = x + _rmsnorm(f, w["g_post"][...])
        gate = jax.nn.sigmoid(_dot(x.astype(BF16), w["w_gate"][...]))
        store_out(sb, x + gate * ple)

    cur = normed(0)
    pending = None
    for sb in range(n_sub):
        x, hb = cur

        def up(j, hb=hb):
            return (_dot(hb, w["w_up"][:, j * CHUNK:(j + 1) * CHUNK]),
                    _dot(hb, w["w_up"][:, D_FF + j * CHUNK:D_FF + (j + 1) * CHUNK]))

        ple = _dot(load_p(sb).astype(BF16), w["w_proj"][...])
        nxt = up(0)
        f = None
        for j in range(N_FF_CHUNKS):
            cols = slice(j * CHUNK, (j + 1) * CHUNK)
            a, g = nxt
            if j + 1 < N_FF_CHUNKS:
                nxt = up(j + 1)
            if j == FFN_FINISH_CHUNK and pending is not None:
                finish(*pending)
            if j == FFN_NORM_AHEAD_CHUNK and sb + 1 < n_sub:
                cur = normed(sb + 1)
            h = jax.nn.gelu(conv_in(sb, j, a), approximate=True) * g
            t = _dot(h.astype(BF16), w["w_down"][cols, :])
            f = t if f is None else f + t
        pending = (sb, x, f, ple)
    finish(*pending)


def _block_copy(hbm, buf, sem, block, slot, seq, steps, *, to_hbm=False):
    hbm_view = hbm.at[seq, pl.ds(block * steps, steps), :]
    buf_view = buf.at[slot, :, seq, :]
    src, dst = (buf_view, hbm_view) if to_hbm else (hbm_view, buf_view)
    return pltpu.make_async_copy(src, dst, sem.at[slot, seq])


def _start_block_fetch(hbm, buf, sem, block, slot, steps, n_seq):
    for s in range(n_seq):
        _block_copy(hbm, buf, sem, block, slot, s, steps).start()


def _fetch_block(hbm, buf, sem, steps, n_seq):
    i = pl.program_id(0)
    slot = i % 2

    @pl.when(i + 1 < pl.num_programs(0))
    def _():
        _start_block_fetch(hbm, buf, sem, i + 1, 1 - slot, steps, n_seq)

    for s in range(n_seq):
        _block_copy(hbm, buf, sem, i, slot, s, steps).wait()
    return slot


def _stream_weight_as_bf16(w_hbm, w_vmem, stage, sem):
    n_rows, width = w_hbm.shape
    n_slots, chunk, _ = stage.shape
    n_chunks = n_rows // chunk
    assert n_chunks * chunk == n_rows and width <= stage.shape[2]

    def chunk_copy(c, slot):
        return pltpu.make_async_copy(w_hbm.at[pl.ds(c * chunk, chunk), :], stage.at[slot, :, 0:width], sem.at[slot])

    for c in range(min(n_slots, n_chunks)):
        chunk_copy(c, c).start()

    def body(c, carry):
        slot = c % n_slots
        chunk_copy(c, slot).wait()
        w_vmem[pl.ds(pl.multiple_of(c * chunk, chunk), chunk), :] = stage[slot, :, 0:width].astype(BF16)

        @pl.when(c + n_slots < n_chunks)
        def _():
            chunk_copy(c + n_slots, slot).start()

        return carry

    lax.fori_loop(0, n_chunks, body, 0)


def _load_weights(pairs, exports, stage, stage_sem, export_sem):
    for w_hbm, w_vmem in pairs:
        _stream_weight_as_bf16(w_hbm, w_vmem, stage, stage_sem)
    for k, ((_, w_vmem), w_out) in enumerate(zip(pairs, exports)):
        pltpu.make_async_copy(w_vmem, w_out, export_sem.at[k]).start()


def _finish_weight_exports(pairs, exports, export_sem):
    for k, ((_, w_vmem), w_out) in enumerate(zip(pairs, exports)):
        pltpu.make_async_copy(w_vmem, w_out, export_sem.at[k]).wait()


def _mixer_prompt_kernel(x_hbm, g_pre, w_in_hbm, conv_w, w_a_out_hbm, pool_w_hbm, pool_scale, w_o_hbm, g_post,
                         x1_ref, nconv_ref, npool_ref, w_in_out, w_a_out_out, pool_w_out, w_o_out,
                         xbuf, xsem, cbuf, ubuf, e0, e1, e2, inv_cnt,
                         w_in, w_a_out, pool_w, w_o, stage, stage_sem, export_sem,
                         *, steps, sub_steps, n_seq):
    i = pl.program_id(0)
    weight_pairs = ((w_in_hbm, w_in), (w_a_out_hbm, w_a_out), (pool_w_hbm, pool_w), (w_o_hbm, w_o))
    weight_exports = (w_in_out, w_a_out_out, pool_w_out, w_o_out)
    sub_rows = sub_steps * n_seq
    n_sub = steps // sub_steps
    conv_halo = N_CONV * n_seq
    pool_halo = N_POOL * n_seq
    head = POOL_MAX * n_seq
    ebufs = (e0, e1, e2)

    @pl.when(i == 0)
    def _():
        _start_block_fetch(x_hbm, xbuf, xsem, 0, 0, steps, n_seq)
        cbuf[0:conv_halo, :] = jnp.zeros((conv_halo, D_MODEL), F32)
        ubuf[0:pool_halo, :] = jnp.zeros((pool_halo, D_MODEL), F32)
        pos = lax.broadcasted_iota(jnp.int32, (POOL_MAX, n_seq, CHUNK), 0).reshape(head, CHUNK)
        for j, window in enumerate(POOL_WINDOWS):
            inv_cnt[j] = 1.0 / jnp.minimum(pos + 1, window).astype(F32)
        _load_weights(weight_pairs, weight_exports, stage, stage_sem, export_sem)

    @pl.when(i == 1)
    def _():
        for j, window in enumerate(POOL_WINDOWS):
            inv_cnt[j] = jnp.full((head, CHUNK), 1.0 / window, F32)

    slot = _fetch_block(x_hbm, xbuf, xsem, steps, n_seq)
    w = dict(g_pre=g_pre, w_in=w_in, w_a_out=w_a_out, pool_w=pool_w, pool_scale=pool_scale, w_o=w_o,
             g_post=g_post)

    def conv_in(sb, j, v):
        cols = slice(j * CHUNK, (j + 1) * CHUNK)
        y = _causal_conv(cbuf, cols, v, conv_w, conv_halo, n_seq, sub_rows)
        cbuf[0:conv_halo, cols] = cbuf[pl.ds(sub_rows, conv_halo), cols]
        return y

    def pool_diff(sb, j, u):
        cols = slice(j * CHUNK, (j + 1) * CHUNK)
        window = POOL_WINDOWS[j]
        ubuf[pl.ds(pool_halo, sub_rows), cols] = u
        n_stages = j + 1
        tsum = None
        for k in range(n_stages):
            lo = pool_halo - (window - (2 << k)) * n_seq
            n = pool_halo + sub_rows - lo
            shift = (1 << k) * n_seq
            if k == 0:
                tsum = ubuf[pl.ds(lo, n), cols] + ubuf[pl.ds(lo - shift, n), cols]
            else:
                tsum = ebufs[k - 1][pl.ds(lo, n), cols] + ebufs[k - 1][pl.ds(lo - shift, n), cols]
            if k < n_stages - 1:
                ebufs[k][pl.ds(lo, n), cols] = tsum
        keep = (window - 1) * n_seq
        ubuf[pl.ds(pool_halo - keep, keep), cols] = ubuf[pl.ds(pool_halo + sub_rows - keep, keep), cols]
        if sb > 0:
            return tsum * (1.0 / window) - u
        return jnp.concatenate([tsum[0:head, :] * inv_cnt[j] - u[0:head, :],
                                tsum[head:, :] * (1.0 / window) - u[head:, :]], axis=0)

    def load_x(sb):
        return xbuf[slot, sb * sub_steps:(sb + 1) * sub_steps].reshape(sub_rows, D_MODEL)

    def store_x1(sb, value):
        x1_ref[sb * sub_rows:(sb + 1) * sub_rows, :] = value

    _mixer_blocks(n_sub, load_x, store_x1, w, conv_in, pool_diff)

    @pl.when(i == pl.num_programs(0) - 1)
    def _():
        for k in range(N_CONV):
            nconv_ref[:, k, :] = cbuf[k * n_seq:(k + 1) * n_seq, :]
        npool_ref[...] = ubuf[pl.ds(sub_rows, pool_halo), :].reshape(N_POOL, n_seq, D_MODEL)
        _finish_weight_exports(weight_pairs, weight_exports, export_sem)


def _ffn_prompt_kernel(x_ref, p_hbm, g_pre, w_up_hbm, conv_w, w_down_hbm, g_post, w_proj_hbm, w_gate_hbm,
                       y_hbm, nffn_ref, w_up_out, w_down_out, w_proj_out, w_gate_out,
                       pbuf, psem, ybuf, ysem, fbuf,
                       w_up, w_down, w_proj, w_gate, stage, stage_sem, export_sem,
                       *, steps, sub_steps, n_seq):
    i = pl.program_id(0)
    last = pl.num_programs(0) - 1
    sub_rows = sub_steps * n_seq
    n_sub = steps // sub_steps
    conv_halo = N_CONV * n_seq
    weight_pairs = ((w_up_hbm, w_up), (w_down_hbm, w_down), (w_proj_hbm, w_proj), (w_gate_hbm, w_gate))
    weight_exports = (w_up_out, w_down_out, w_proj_out, w_gate_out)

    @pl.when(i == 0)
    def _():
        _start_block_fetch(p_hbm, pbuf, psem, 0, 0, steps, n_seq)
        fbuf[0:conv_halo, :] = jnp.zeros((conv_halo, D_FF), F32)
        _load_weights(weight_pairs, weight_exports, stage, stage_sem, export_sem)

    def wait_store(block, slot):
        for s in range(n_seq):
            _block_copy(y_hbm, ybuf, ysem, block, slot, s, steps, to_hbm=True).wait()

    slot = _fetch_block(p_hbm, pbuf, psem, steps, n_seq)

    @pl.when(i >= 2)
    def _():
        wait_store(i - 2, slot)

    w = dict(g_pre=g_pre, w_up=w_up, w_down=w_down, g_post=g_post, w_proj=w_proj, w_gate=w_gate)

    def conv_in(sb, j, v):
        cols = slice(j * CHUNK, (j + 1) * CHUNK)
        y = _causal_conv(fbuf, cols, v, conv_w, conv_halo, n_seq, sub_rows)
        fbuf[0:conv_halo, cols] = fbuf[pl.ds(sub_rows, conv_halo), cols]
        return y

    def load_x(sb):
        return x_ref[sb * sub_rows:(sb + 1) * sub_rows, :]

    def load_p(sb):
        return pbuf[slot, sb * sub_steps:(sb + 1) * sub_steps].reshape(sub_rows, PLE_DIM)

    def store_y(sb, value):
        ybuf[slot, sb * sub_steps:(sb + 1) * sub_steps] = value.reshape(sub_steps, n_seq, D_MODEL)

    _ffn_blocks(n_sub, load_x, load_p, store_y, w, conv_in)

    for s in range(n_seq):
        _block_copy(y_hbm, ybuf, ysem, i, slot, s, steps, to_hbm=True).start()

    @pl.when(i == last)
    def _():
        for k in range(N_CONV):
            nffn_ref[:, k, :] = fbuf[k * n_seq:(k + 1) * n_seq, :]
        wait_store(i - 1, 1 - slot)
        wait_store(i, slot)
        _finish_weight_exports(weight_pairs, weight_exports, export_sem)


def _resident(shape):
    return pl.BlockSpec(shape, lambda i: (0,) * len(shape), pipeline_mode=pl.Buffered(1))


def _prompt_layer(x, p, wts):
    n_seq, seq_len, _ = x.shape
    steps, sub_steps = TIME_STEPS, SUB_STEPS
    n_blocks = seq_len // steps
    assert n_blocks * steps == seq_len and n_blocks >= 2 and n_seq % 8 == 0
    assert steps % sub_steps == 0 and sub_steps > POOL_MAX
    rows = steps * n_seq
    params = lambda limit: pltpu.CompilerParams(dimension_semantics=("arbitrary",), vmem_limit_bytes=limit)
    any_space = pl.BlockSpec(memory_space=pl.ANY)
    row_block = pl.BlockSpec((rows, D_MODEL), lambda i: (i, 0))
    row_block_shape = jax.ShapeDtypeStruct((rows, D_MODEL), F32)
    whole_block = lambda shape: pl.BlockSpec(shape, lambda i: (0,) * len(shape))
    conv_state = lambda width: (n_seq, N_CONV, width)
    pool_state = (N_POOL, n_seq, D_MODEL)

    weight_spec = lambda a: any_space if a.dtype == F32 and a.shape[0] >= WEIGHT_CHUNK_ROWS else _resident(a.shape)
    bf16_like = lambda ws: [jax.ShapeDtypeStruct(a.shape, BF16) for a in ws]
    vmem_bf16 = lambda ws: [pltpu.VMEM(a.shape, BF16) for a in ws]
    stage = lambda ws: [pltpu.VMEM((WEIGHT_STAGE_SLOTS, WEIGHT_CHUNK_ROWS, max(a.shape[1] for a in ws)), F32),
                        pltpu.SemaphoreType.DMA((WEIGHT_STAGE_SLOTS,)), pltpu.SemaphoreType.DMA((len(ws),))]

    mixer_w = (wts["g_pre_mix"], wts["w_in"], wts["conv_a_w"], wts["w_a_out"], wts["pool_w"],
               wts["pool_scale"], wts["w_o"], wts["g_post_mix"])
    mixer_mats = (wts["w_in"], wts["w_a_out"], wts["pool_w"], wts["w_o"])
    sub_rows = sub_steps * n_seq
    stage_rows = N_POOL * n_seq + sub_rows
    mixer_scratch = ([pltpu.VMEM((2, steps, n_seq, D_MODEL), F32), pltpu.SemaphoreType.DMA((2, n_seq)),
                      pltpu.VMEM((N_CONV * n_seq + sub_rows, D_MODEL), F32),
                      pltpu.VMEM((stage_rows, D_MODEL), F32),
                      pltpu.VMEM((stage_rows, D_MODEL), F32),
                      pltpu.VMEM((stage_rows, D_MODEL), F32),
                      pltpu.VMEM((stage_rows, D_MODEL), F32),
                      pltpu.VMEM((len(POOL_WINDOWS), POOL_MAX * n_seq, CHUNK), F32)]
                     + vmem_bf16(mixer_mats) + stage(mixer_mats))
    x1, nconv, npool, *mixer_bf16 = pl.pallas_call(
        functools.partial(_mixer_prompt_kernel, steps=steps, sub_steps=sub_steps, n_seq=n_seq),
        grid=(n_blocks,),
        in_specs=[any_space] + [weight_spec(a) for a in mixer_w],
        out_specs=[row_block, whole_block(conv_state(D_MODEL)), whole_block(pool_state)] + [any_space] * 4,
        out_shape=[jax.ShapeDtypeStruct((n_blocks * rows, D_MODEL), F32),
                   jax.ShapeDtypeStruct(conv_state(D_MODEL), F32),
                   jax.ShapeDtypeStruct(pool_state, F32)] + bf16_like(mixer_mats),
        scratch_shapes=mixer_scratch,
        compiler_params=params(_vmem_limit(mixer_scratch, double_buffered=[row_block_shape])),
        name="mixer_prompt",
    )(x, *mixer_w)

    ffn_w = (wts["g_pre_ffn"], wts["w_up"], wts["ffn_conv_w"], wts["w_down"], wts["g_post_ffn"],
             wts["w_ple_proj"], wts["w_ple_gate"])
    ffn_mats = (wts["w_up"], wts["w_down"], wts["w_ple_proj"], wts["w_ple_gate"])
    ffn_scratch = ([pltpu.VMEM((2, steps, n_seq, PLE_DIM), F32), pltpu.SemaphoreType.DMA((2, n_seq)),
                    pltpu.VMEM((2, steps, n_seq, D_MODEL), F32), pltpu.SemaphoreType.DMA((2, n_seq)),
                    pltpu.VMEM((N_CONV * n_seq + sub_rows, D_FF), F32)]
                   + vmem_bf16(ffn_mats) + stage(ffn_mats))
    y, nffn, *ffn_bf16 = pl.pallas_call(
        functools.partial(_ffn_prompt_kernel, steps=steps, sub_steps=sub_steps, n_seq=n_seq),
        grid=(n_blocks,),
        in_specs=[row_block, any_space] + [weight_spec(a) for a in ffn_w],
        out_specs=[any_space, whole_block(conv_state(D_FF))] + [any_space] * 4,
        out_shape=[jax.ShapeDtypeStruct((n_seq, seq_len, D_MODEL), F32),
                   jax.ShapeDtypeStruct(conv_state(D_FF), F32)] + bf16_like(ffn_mats),
        scratch_shapes=ffn_scratch,
        compiler_params=params(_vmem_limit(ffn_scratch, double_buffered=[row_block_shape])),
        name="ffn_prompt",
    )(x1, p, *ffn_w)
    bf16_wts = dict(wts, **dict(zip(("w_in", "w_a_out", "pool_w", "w_o"), mixer_bf16)),
                    **dict(zip(("w_up", "w_down", "w_ple_proj", "w_ple_gate"), ffn_bf16)))
    return y, nconv[None], jnp.transpose(npool, (1, 0, 2))[None], nffn[None], bf16_wts


def _mixer_sample_kernel(x_ref, sconv_ref, spool_ref, g_pre, w_in, conv_w, w_a_out, pool_w, pool_scale, w_o,
                         g_post, x1_ref, nconv_ref, npool_ref, cbuf, ubuf, *, batch, steps, sub_steps):
    sub_rows = sub_steps * batch
    for k in range(N_CONV):
        cbuf[k * batch:(k + 1) * batch, :] = sconv_ref[:, k, :]

    w = dict(g_pre=g_pre, w_in=w_in, w_a_out=w_a_out, pool_w=pool_w, pool_scale=pool_scale, w_o=w_o,
             g_post=g_post)
    def conv_in(sb, j, v):
        cols = slice(j * CHUNK, (j + 1) * CHUNK)
        return _causal_conv(cbuf, cols, v, conv_w, N_CONV * batch + sb * sub_rows, batch, sub_rows)

    def pool_diff(sb, j, u):
        t0 = sb * sub_steps
        cols = slice(j * CHUNK, (j + 1) * CHUNK)
        window = POOL_WINDOWS[j]
        ubuf[t0:t0 + sub_steps, :, cols] = u.reshape(sub_steps, batch, CHUNK)

        def ext(e):
            return spool_ref[e, :, cols] if e < N_POOL else ubuf[e - N_POOL, :, cols]

        diffs = []
        for t in range(t0, t0 + sub_steps):
            tsum = ext(N_POOL + t - window + 1)
            for e in range(N_POOL + t - window + 2, N_POOL + t + 1):
                tsum = tsum + ext(e)
            diffs.append(tsum * (1.0 / window) - ext(N_POOL + t))
        return jnp.concatenate(diffs, axis=0)

    def load_x(sb):
        return jnp.concatenate([x_ref[:, t, :] for t in range(sb * sub_steps, (sb + 1) * sub_steps)], axis=0)

    def store_x1(sb, value):
        x1_ref[sb * sub_rows:(sb + 1) * sub_rows, :] = value

    _mixer_blocks(steps // sub_steps, load_x, store_x1, w, conv_in, pool_diff)

    for k in range(N_CONV):
        nconv_ref[:, k, :] = cbuf[(steps + k) * batch:(steps + k + 1) * batch, :]
    for m in range(N_POOL):
        e = steps + m
        npool_ref[m] = spool_ref[e] if e < N_POOL else ubuf[e - N_POOL]


def _ffn_sample_kernel(x_ref, p_ref, sffn_ref, g_pre, w_up, conv_w, w_down, g_post, w_proj, w_gate,
                       y_ref, nffn_ref, fbuf, *, batch, steps, sub_steps):
    sub_rows = sub_steps * batch
    for k in range(N_CONV):
        fbuf[k * batch:(k + 1) * batch, :] = sffn_ref[:, k, :]

    w = dict(g_pre=g_pre, w_up=w_up, w_down=w_down, g_post=g_post, w_proj=w_proj, w_gate=w_gate)
    def conv_in(sb, j, v):
        cols = slice(j * CHUNK, (j + 1) * CHUNK)
        return _causal_conv(fbuf, cols, v, conv_w, N_CONV * batch + sb * sub_rows, batch, sub_rows)

    def load_x(sb):
        return x_ref[sb * sub_rows:(sb + 1) * sub_rows, :]

    def load_p(sb):
        return jnp.concatenate([p_ref[:, t, :] for t in range(sb * sub_steps, (sb + 1) * sub_steps)], axis=0)

    def store_y(sb, value):
        for t in range(sub_steps):
            y_ref[:, sb * sub_steps + t, :] = value[t * batch:(t + 1) * batch, :]

    _ffn_blocks(steps // sub_steps, load_x, load_p, store_y, w, conv_in)

    for k in range(N_CONV):
        nffn_ref[:, k, :] = fbuf[(steps + k) * batch:(steps + k + 1) * batch, :]


def _sample_layer(x, p, sconv, spool, sffn, layer, wts):
    batch, steps, _ = x.shape
    sub_steps = max(1, SAMPLE_SUB_ROWS // batch)
    assert steps % sub_steps == 0 and batch % 8 == 0
    rows = batch * steps
    params = lambda limit: pltpu.CompilerParams(vmem_limit_bytes=limit)
    whole = pl.BlockSpec(memory_space=pltpu.VMEM)
    of_layer = lambda a, k: pl.BlockSpec((None,) + a.shape[1:], lambda: (k, 0, 0, 0))
    one_layer = lambda a: jax.ShapeDtypeStruct((1,) + a.shape[1:], F32)
    x1_shape = jax.ShapeDtypeStruct((rows, D_MODEL), F32)

    mixer_w = (wts["g_pre_mix"], wts["w_in"], wts["conv_a_w"], wts["w_a_out"], wts["pool_w"],
               wts["pool_scale"], wts["w_o"], wts["g_post_mix"])
    mixer_scratch = [pltpu.VMEM(((N_CONV + steps) * batch, D_MODEL), F32), pltpu.VMEM((steps, batch, D_MODEL), F32)]
    mixer_out = [x1_shape, one_layer(sconv), one_layer(spool)]
    x1, nconv, npool = pl.pallas_call(
        functools.partial(_mixer_sample_kernel, batch=batch, steps=steps, sub_steps=sub_steps),
        in_specs=[whole, of_layer(sconv, layer), of_layer(spool, layer)] + [whole] * len(mixer_w),
        out_specs=[whole, of_layer(sconv, 0), of_layer(spool, 0)],
        out_shape=mixer_out,
        scratch_shapes=mixer_scratch,
        compiler_params=params(_vmem_limit(mixer_scratch, single_buffered=[x, *mixer_out, *mixer_out[1:], *mixer_w])),
        name="mixer_sample",
    )(x, sconv, spool, *mixer_w)

    ffn_w = (wts["g_pre_ffn"], wts["w_up"], wts["ffn_conv_w"], wts["w_down"], wts["g_post_ffn"],
             wts["w_ple_proj"], wts["w_ple_gate"])
    ffn_scratch = [pltpu.VMEM(((N_CONV + steps) * batch, D_FF), F32)]
    ffn_out = [jax.ShapeDtypeStruct(x.shape, F32), one_layer(sffn)]
    y, nffn = pl.pallas_call(
        functools.partial(_ffn_sample_kernel, batch=batch, steps=steps, sub_steps=sub_steps),
        in_specs=[whole, of_layer(p, layer), of_layer(sffn, layer)] + [whole] * len(ffn_w),
        out_specs=[whole, of_layer(sffn, 0)],
        out_shape=ffn_out,
        scratch_shapes=ffn_scratch,
        compiler_params=params(_vmem_limit(
            ffn_scratch, single_buffered=[x1_shape, one_layer(p), *ffn_out, ffn_out[1], *ffn_w])),
        name="ffn_sample",
    )(x1, p, sffn, *ffn_w)
    return y, nconv, npool, nffn


def kernel(x_prompt, x_sample, p_prompt, p_sample, state_conv_a, state_pool, state_ffn_conv, g_pre_mix, w_in,
           conv_a_w, w_a_out, pool_w, pool_scale, w_o, g_post_mix, g_pre_ffn, w_up, ffn_conv_w, w_down,
           g_post_ffn, w_ple_proj, w_ple_gate):
    depth = w_in.shape[0]
    xp, xs = x_prompt, x_sample
    prompt_states, sample_states = [], []
    time_major = lambda s: jnp.transpose(s, (0, 2, 1, 3))
    state_pool_tm = time_major(state_pool)
    for i in range(depth):
        wts = dict(
            g_pre_mix=g_pre_mix[i][None], g_post_mix=g_post_mix[i][None], g_pre_ffn=g_pre_ffn[i][None],
            g_post_ffn=g_post_ffn[i][None], pool_scale=pool_scale[i][None], conv_a_w=conv_a_w[i],
            ffn_conv_w=ffn_conv_w[i],
            w_in=w_in[i], w_a_out=w_a_out[i], pool_w=pool_w[i].reshape(D_MODEL, CHUNK),
            w_o=w_o[i], w_up=w_up[i], w_down=w_down[i], w_ple_proj=w_ple_proj[i], w_ple_gate=w_ple_gate[i])
        xp, c1, p1, f1, bf16_wts = _prompt_layer(xp, p_prompt[i], wts)
        xs, c2, p2, f2 = _sample_layer(xs, p_sample, state_conv_a, state_pool_tm, state_ffn_conv, i, bf16_wts)
        prompt_states.append((c1, p1, f1))
        sample_states.append((c2, time_major(p2), f2))
    cat = lambda states, k: states[0][k] if depth == 1 else jnp.concatenate([s[k] for s in states], axis=0)
    return (xp, xs, cat(prompt_states, 0), cat(prompt_states, 1), cat(prompt_states, 2),
            cat(sample_states, 0), cat(sample_states, 1), cat(sample_states, 2))
```

```python
import functools
import math

import jax
import jax.numpy as jnp
from jax import lax
from jax.experimental import pallas as pl
from jax.experimental.pallas import tpu as pltpu

D_MODEL = 1024
D_FF = 2816
PLE_DIM = 256
CONV_W = 3
POOL_WINDOWS = (2, 4, 8, 16)
POOL_MAX = 16
EPS = 1e-6

N_CONV = CONV_W - 1
N_POOL = POOL_MAX - 1
CHUNK = 256
N_MIX_CHUNKS = D_MODEL // CHUNK
N_FF_CHUNKS = D_FF // CHUNK
TIME_STEPS = 64
SUB_STEPS = 32
SAMPLE_SUB_ROWS = 256
MIXER_NORM_AHEAD_CHUNK = 2
FFN_NORM_AHEAD_CHUNK = 8
FFN_FINISH_CHUNK = 1
WEIGHT_CHUNK_ROWS = 128
WEIGHT_STAGE_SLOTS = 4
VMEM_CAPACITY_BYTES = 64 * 1024 * 1024
VMEM_COMPILER_SCRATCH_BYTES = 6 * 1024 * 1024
VMEM_UNREQUESTABLE_BYTES = 4 * 1024 * 1024

F32 = jnp.float32
BF16 = jnp.bfloat16


def _nbytes(a):
    return math.prod(a.shape) * jnp.dtype(a.dtype).itemsize


def _vmem_limit(scratch, single_buffered=(), double_buffered=()):
    total = sum(_nbytes(s) for s in scratch if s.memory_space == pltpu.VMEM)
    total += sum(_nbytes(a) for a in single_buffered) + 2 * sum(_nbytes(a) for a in double_buffered)
    return min(total + VMEM_COMPILER_SCRATCH_BYTES, VMEM_CAPACITY_BYTES - VMEM_UNREQUESTABLE_BYTES)


def _rmsnorm(x, g):
    ms = jnp.mean(x * x, axis=-1, keepdims=True)
    return (x * lax.rsqrt(ms + EPS)) * g


def _dot(a, b):
    return jnp.dot(a, b, preferred_element_type=F32)


def _causal_conv(buf, cols, v, w_ref, halo, step, rows):
    buf[pl.ds(halo, rows), cols] = v
    y = buf[pl.ds(halo - 2 * step, rows), cols] * w_ref[0:1, cols]
    y = y + buf[pl.ds(halo - step, rows), cols] * w_ref[1:2, cols]
    return y + v * w_ref[2:3, cols]


def _mixer_blocks(n_sub, load_x, store_out, w, conv_in, pool_diff):
    def normed(sb):
        x = load_x(sb)
        return x, _rmsnorm(x, w["g_pre"][...]).astype(BF16)

    def finish(sb, x, mix):
        store_out(sb, x + _rmsnorm(mix, w["g_post"][...]))

    cur = normed(0)
    pending = None
    for sb in range(n_sub):
        x, xb = cur

        def proj(k, j, xb=xb):
            lo = k * D_MODEL + j * CHUNK
            return _dot(xb, w["w_in"][:, lo:lo + CHUNK])

        def branch_proj(j):
            return tuple(proj(k, j) for k in range(4))

        def gate_proj(j):
            return proj(4, j), proj(5, j)

        nxt = branch_proj(0)
        y_a = None
        y_p = []
        for j in range(N_MIX_CHUNKS):
            cols = slice(j * CHUNK, (j + 1) * CHUNK)
            b, c, h, u = nxt
            nxt = branch_proj(j + 1) if j + 1 < N_MIX_CHUNKS else gate_proj(0)
            if j == 0 and pending is not None:
                finish(*pending)
            if j == MIXER_NORM_AHEAD_CHUNK and sb + 1 < n_sub:
                cur = normed(sb + 1)
            z = (b * conv_in(sb, j, c * h)).astype(BF16)
            t = _dot(z, w["w_a_out"][cols, :])
            y_a = t if y_a is None else y_a + t
            d = pool_diff(sb, j, u).astype(BF16)
            y_p.append(_dot(d, w["pool_w"][cols, :]) * w["pool_scale"][:, cols])
        mix = None
        for j in range(N_MIX_CHUNKS):
            cols = slice(j * CHUNK, (j + 1) * CHUNK)
            ga, gp = nxt
            if j + 1 < N_MIX_CHUNKS:
                nxt = gate_proj(j + 1)
            merged = jax.nn.sigmoid(ga) * y_a[:, cols] + jax.nn.sigmoid(gp) * y_p[j]
            t = _dot(merged.astype(BF16), w["w_o"][cols, :])
            mix = t if mix is None else mix + t
        pending = (sb, x, mix)
    finish(*pending)


def _ffn_blocks(n_sub, load_x, load_p, store_out, w, conv_in):
    def normed(sb):
        x = load_x(sb)
        return x, _rmsnorm(x, w["g_pre"][...]).astype(BF16)

    def finish(sb, x, f, ple):
        x = x + _rmsnorm(f, w["g_post"][...])
        gate = jax.nn.sigmoid(_dot(x.astype(BF16), w["w_gate"][...]))
        store_out(sb, x + gate * ple)

    cur = normed(0)
    pending = None
    for sb in range(n_sub):
        x, hb = cur

        def up(j, hb=hb):
            return (_dot(hb, w["w_up"][:, j * CHUNK:(j + 1) * CHUNK]),
                    _dot(hb, w["w_up"][:, D_FF + j * CHUNK:D_FF + (j + 1) * CHUNK]))

        ple = _dot(load_p(sb).astype(BF16), w["w_proj"][...])
        nxt = up(0)
        f = None
        for j in range(N_FF_CHUNKS):
            cols = slice(j * CHUNK, (j + 1) * CHUNK)
            a, g = nxt
            if j + 1 < N_FF_CHUNKS:
                nxt = up(j + 1)
            if j == FFN_FINISH_CHUNK and pending is not None:
                finish(*pending)
            if j == FFN_NORM_AHEAD_CHUNK and sb + 1 < n_sub:
                cur = normed(sb + 1)
            h = jax.nn.gelu(conv_in(sb, j, a), approximate=True) * g
            t = _dot(h.astype(BF16), w["w_down"][cols, :])
            f = t if f is None else f + t
        pending = (sb, x, f, ple)
    finish(*pending)


def _block_copy(hbm, buf, sem, block, slot, seq, steps, *, to_hbm=False):
    hbm_view = hbm.at[seq, pl.ds(block * steps, steps), :]
    buf_view = buf.at[slot, :, seq, :]
    src, dst = (buf_view, hbm_view) if to_hbm else (hbm_view, buf_view)
    return pltpu.make_async_copy(src, dst, sem.at[slot, seq])


def _start_block_copies(hbm, buf, sem, block, slot, steps, n_seq, *, to_hbm=False):
    for s in range(n_seq):
        _block_copy(hbm, buf, sem, block, slot, s, steps, to_hbm=to_hbm).start(priority=s % 2)


def _fetch_block(hbm, buf, sem, steps, n_seq):
    i = pl.program_id(0)
    slot = i % 2

    @pl.when(i + 1 < pl.num_programs(0))
    def _():
        _start_block_copies(hbm, buf, sem, i + 1, 1 - slot, steps, n_seq)

    for s in range(n_seq):
        _block_copy(hbm, buf, sem, i, slot, s, steps).wait()
    return slot


def _stream_weight_as_bf16(w_hbm, w_vmem, stage, sem):
    n_rows, width = w_hbm.shape
    n_slots, chunk, _ = stage.shape
    n_chunks = n_rows // chunk
    assert n_chunks * chunk == n_rows and width <= stage.shape[2]

    def chunk_copy(c, slot):
        return pltpu.make_async_copy(w_hbm.at[pl.ds(c * chunk, chunk), :], stage.at[slot, :, 0:width], sem.at[slot])

    for c in range(min(n_slots, n_chunks)):
        chunk_copy(c, c).start()

    def body(c, carry):
        slot = c % n_slots
        chunk_copy(c, slot).wait()
        w_vmem[pl.ds(pl.multiple_of(c * chunk, chunk), chunk), :] = stage[slot, :, 0:width].astype(BF16)

        @pl.when(c + n_slots < n_chunks)
        def _():
            chunk_copy(c + n_slots, slot).start()

        return carry

    lax.fori_loop(0, n_chunks, body, 0)


def _load_weights(pairs, exports, stage, stage_sem, export_sem):
    for w_hbm, w_vmem in pairs:
        _stream_weight_as_bf16(w_hbm, w_vmem, stage, stage_sem)
    for k, ((_, w_vmem), w_out) in enumerate(zip(pairs, exports)):
        pltpu.make_async_copy(w_vmem, w_out, export_sem.at[k]).start()


def _finish_weight_exports(pairs, exports, export_sem):
    for k, ((_, w_vmem), w_out) in enumerate(zip(pairs, exports)):
        pltpu.make_async_copy(w_vmem, w_out, export_sem.at[k]).wait()


def _mixer_prompt_kernel(x_hbm, g_pre, w_in_hbm, conv_w, w_a_out_hbm, pool_w_hbm, pool_scale, w_o_hbm, g_post,
                         x1_ref, nconv_ref, npool_ref, w_in_out, w_a_out_out, pool_w_out, w_o_out,
                         xbuf, xsem, cbuf, ubuf, e0, e1, e2, inv_cnt,
                         w_in, w_a_out, pool_w, w_o, stage, stage_sem, export_sem,
                         *, steps, sub_steps, n_seq):
    i = pl.program_id(0)
    weight_pairs = ((w_in_hbm, w_in), (w_a_out_hbm, w_a_out), (pool_w_hbm, pool_w), (w_o_hbm, w_o))
    weight_exports = (w_in_out, w_a_out_out, pool_w_out, w_o_out)
    sub_rows = sub_steps * n_seq
    n_sub = steps // sub_steps
    conv_halo = N_CONV * n_seq
    pool_halo = N_POOL * n_seq
    head = POOL_MAX * n_seq
    ebufs = (e0, e1, e2)

    @pl.when(i == 0)
    def _():
        _start_block_copies(x_hbm, xbuf, xsem, 0, 0, steps, n_seq)
        cbuf[0:conv_halo, :] = jnp.zeros((conv_halo, D_MODEL), F32)
        ubuf[0:pool_halo, :] = jnp.zeros((pool_halo, D_MODEL), F32)
        pos = lax.broadcasted_iota(jnp.int32, (POOL_MAX, n_seq, CHUNK), 0).reshape(head, CHUNK)
        for j, window in enumerate(POOL_WINDOWS):
            inv_cnt[j] = 1.0 / jnp.minimum(pos + 1, window).astype(F32)
        _load_weights(weight_pairs, weight_exports, stage, stage_sem, export_sem)

    @pl.when(i == 1)
    def _():
        for j, window in enumerate(POOL_WINDOWS):
            inv_cnt[j] = jnp.full((head, CHUNK), 1.0 / window, F32)

    slot = _fetch_block(x_hbm, xbuf, xsem, steps, n_seq)
    w = dict(g_pre=g_pre, w_in=w_in, w_a_out=w_a_out, pool_w=pool_w, pool_scale=pool_scale, w_o=w_o,
             g_post=g_post)

    def conv_in(sb, j, v):
        cols = slice(j * CHUNK, (j + 1) * CHUNK)
        y = _causal_conv(cbuf, cols, v, conv_w, conv_halo, n_seq, sub_rows)
        cbuf[0:conv_halo, cols] = cbuf[pl.ds(sub_rows, conv_halo), cols]
        return y

    def pool_diff(sb, j, u):
        cols = slice(j * CHUNK, (j + 1) * CHUNK)
        window = POOL_WINDOWS[j]
        ubuf[pl.ds(pool_halo, sub_rows), cols] = u
        n_stages = j + 1
        tsum = None
        for k in range(n_stages):
            lo = pool_halo - (window - (2 << k)) * n_seq
            n = pool_halo + sub_rows - lo
            shift = (1 << k) * n_seq
            if k == 0:
                tsum = ubuf[pl.ds(lo, n), cols] + ubuf[pl.ds(lo - shift, n), cols]
            else:
                tsum = ebufs[k - 1][pl.ds(lo, n), cols] + ebufs[k - 1][pl.ds(lo - shift, n), cols]
            if k < n_stages - 1:
                ebufs[k][pl.ds(lo, n), cols] = tsum
        keep = (window - 1) * n_seq
        ubuf[pl.ds(pool_halo - keep, keep), cols] = ubuf[pl.ds(pool_halo + sub_rows - keep, keep), cols]
        if sb > 0:
            return tsum * (1.0 / window) - u
        return jnp.concatenate([tsum[0:head, :] * inv_cnt[j] - u[0:head, :],
                                tsum[head:, :] * (1.0 / window) - u[head:, :]], axis=0)

    def load_x(sb):
        return xbuf[slot, sb * sub_steps:(sb + 1) * sub_steps].reshape(sub_rows, D_MODEL)

    def store_x1(sb, value):
        x1_ref[sb * sub_rows:(sb + 1) * sub_rows, :] = value

    _mixer_blocks(n_sub, load_x, store_x1, w, conv_in, pool_diff)

    @pl.when(i == pl.num_programs(0) - 1)
    def _():
        for k in range(N_CONV):
            nconv_ref[:, k, :] = cbuf[k * n_seq:(k + 1) * n_seq, :]
        npool_ref[...] = ubuf[pl.ds(sub_rows, pool_halo), :].reshape(N_POOL, n_seq, D_MODEL)
        _finish_weight_exports(weight_pairs, weight_exports, export_sem)


def _ffn_prompt_kernel(x_ref, p_hbm, g_pre, w_up_hbm, conv_w, w_down_hbm, g_post, w_proj_hbm, w_gate_hbm,
                       y_hbm, nffn_ref, w_up_out, w_down_out, w_proj_out, w_gate_out,
                       pbuf, psem, ybuf, ysem, fbuf,
                       w_up, w_down, w_proj, w_gate, stage, stage_sem, export_sem,
                       *, steps, sub_steps, n_seq):
    i = pl.program_id(0)
    last = pl.num_programs(0) - 1
    sub_rows = sub_steps * n_seq
    n_sub = steps // sub_steps
    conv_halo = N_CONV * n_seq
    weight_pairs = ((w_up_hbm, w_up), (w_down_hbm, w_down), (w_proj_hbm, w_proj), (w_gate_hbm, w_gate))
    weight_exports = (w_up_out, w_down_out, w_proj_out, w_gate_out)

    @pl.when(i == 0)
    def _():
        _start_block_copies(p_hbm, pbuf, psem, 0, 0, steps, n_seq)
        fbuf[0:conv_halo, :] = jnp.zeros((conv_halo, D_FF), F32)
        _load_weights(weight_pairs, weight_exports, stage, stage_sem, export_sem)

    def wait_store(block, slot):
        for s in range(n_seq):
            _block_copy(y_hbm, ybuf, ysem, block, slot, s, steps, to_hbm=True).wait()

    slot = _fetch_block(p_hbm, pbuf, psem, steps, n_seq)

    @pl.when(i >= 2)
    def _():
        wait_store(i - 2, slot)

    w = dict(g_pre=g_pre, w_up=w_up, w_down=w_down, g_post=g_post, w_proj=w_proj, w_gate=w_gate)

    def conv_in(sb, j, v):
        cols = slice(j * CHUNK, (j + 1) * CHUNK)
        y = _causal_conv(fbuf, cols, v, conv_w, conv_halo, n_seq, sub_rows)
        fbuf[0:conv_halo, cols] = fbuf[pl.ds(sub_rows, conv_halo), cols]
        return y

    def load_x(sb):
        return x_ref[sb * sub_rows:(sb + 1) * sub_rows, :]

    def load_p(sb):
        return pbuf[slot, sb * sub_steps:(sb + 1) * sub_steps].reshape(sub_rows, PLE_DIM)

    def store_y(sb, value):
        ybuf[slot, sb * sub_steps:(sb + 1) * sub_steps] = value.reshape(sub_steps, n_seq, D_MODEL)

    _ffn_blocks(n_sub, load_x, load_p, store_y, w, conv_in)

    _start_block_copies(y_hbm, ybuf, ysem, i, slot, steps, n_seq, to_hbm=True)

    @pl.when(i == last)
    def _():
        for k in range(N_CONV):
            nffn_ref[:, k, :] = fbuf[k * n_seq:(k + 1) * n_seq, :]
        wait_store(i - 1, 1 - slot)
        wait_store(i, slot)
        _finish_weight_exports(weight_pairs, weight_exports, export_sem)


def _resident(shape):
    return pl.BlockSpec(shape, lambda i: (0,) * len(shape), pipeline_mode=pl.Buffered(1))


def _prompt_layer(x, p, wts):
    n_seq, seq_len, _ = x.shape
    steps, sub_steps = TIME_STEPS, SUB_STEPS
    n_blocks = seq_len // steps
    assert n_blocks * steps == seq_len and n_blocks >= 2 and n_seq % 8 == 0
    assert steps % sub_steps == 0 and sub_steps > POOL_MAX
    rows = steps * n_seq
    params = lambda limit: pltpu.CompilerParams(dimension_semantics=("arbitrary",), vmem_limit_bytes=limit)
    any_space = pl.BlockSpec(memory_space=pl.ANY)
    row_block = pl.BlockSpec((rows, D_MODEL), lambda i: (i, 0))
    row_block_shape = jax.ShapeDtypeStruct((rows, D_MODEL), F32)
    whole_block = lambda shape: pl.BlockSpec(shape, lambda i: (0,) * len(shape))
    conv_state = lambda width: (n_seq, N_CONV, width)
    pool_state = (N_POOL, n_seq, D_MODEL)

    weight_spec = lambda a: any_space if a.dtype == F32 and a.shape[0] >= WEIGHT_CHUNK_ROWS else _resident(a.shape)
    bf16_like = lambda ws: [jax.ShapeDtypeStruct(a.shape, BF16) for a in ws]
    vmem_bf16 = lambda ws: [pltpu.VMEM(a.shape, BF16) for a in ws]
    stage = lambda ws: [pltpu.VMEM((WEIGHT_STAGE_SLOTS, WEIGHT_CHUNK_ROWS, max(a.shape[1] for a in ws)), F32),
                        pltpu.SemaphoreType.DMA((WEIGHT_STAGE_SLOTS,)), pltpu.SemaphoreType.DMA((len(ws),))]

    mixer_w = (wts["g_pre_mix"], wts["w_in"], wts["conv_a_w"], wts["w_a_out"], wts["pool_w"],
               wts["pool_scale"], wts["w_o"], wts["g_post_mix"])
    mixer_mats = (wts["w_in"], wts["w_a_out"], wts["pool_w"], wts["w_o"])
    sub_rows = sub_steps * n_seq
    stage_rows = N_POOL * n_seq + sub_rows
    mixer_scratch = ([pltpu.VMEM((2, steps, n_seq, D_MODEL), F32), pltpu.SemaphoreType.DMA((2, n_seq)),
                      pltpu.VMEM((N_CONV * n_seq + sub_rows, D_MODEL), F32),
                      pltpu.VMEM((stage_rows, D_MODEL), F32),
                      pltpu.VMEM((stage_rows, D_MODEL), F32),
                      pltpu.VMEM((stage_rows, D_MODEL), F32),
                      pltpu.VMEM((stage_rows, D_MODEL), F32),
                      pltpu.VMEM((len(POOL_WINDOWS), POOL_MAX * n_seq, CHUNK), F32)]
                     + vmem_bf16(mixer_mats) + stage(mixer_mats))
    x1, nconv, npool, *mixer_bf16 = pl.pallas_call(
        functools.partial(_mixer_prompt_kernel, steps=steps, sub_steps=sub_steps, n_seq=n_seq),
        grid=(n_blocks,),
        in_specs=[any_space] + [weight_spec(a) for a in mixer_w],
        out_specs=[row_block, whole_block(conv_state(D_MODEL)), whole_block(pool_state)] + [any_space] * 4,
        out_shape=[jax.ShapeDtypeStruct((n_blocks * rows, D_MODEL), F32),
                   jax.ShapeDtypeStruct(conv_state(D_MODEL), F32),
                   jax.ShapeDtypeStruct(pool_state, F32)] + bf16_like(mixer_mats),
        scratch_shapes=mixer_scratch,
        compiler_params=params(_vmem_limit(mixer_scratch, double_buffered=[row_block_shape])),
        name="mixer_prompt",
    )(x, *mixer_w)

    ffn_w = (wts["g_pre_ffn"], wts["w_up"], wts["ffn_conv_w"], wts["w_down"], wts["g_post_ffn"],
             wts["w_ple_proj"], wts["w_ple_gate"])
    ffn_mats = (wts["w_up"], wts["w_down"], wts["w_ple_proj"], wts["w_ple_gate"])
    ffn_scratch = ([pltpu.VMEM((2, steps, n_seq, PLE_DIM), F32), pltpu.SemaphoreType.DMA((2, n_seq)),
                    pltpu.VMEM((2, steps, n_seq, D_MODEL), F32), pltpu.SemaphoreType.DMA((2, n_seq)),
                    pltpu.VMEM((N_CONV * n_seq + sub_rows, D_FF), F32)]
                   + vmem_bf16(ffn_mats) + stage(ffn_mats))
    y, nffn, *ffn_bf16 = pl.pallas_call(
        functools.partial(_ffn_prompt_kernel, steps=steps, sub_steps=sub_steps, n_seq=n_seq),
        grid=(n_blocks,),
        in_specs=[row_block, any_space] + [weight_spec(a) for a in ffn_w],
        out_specs=[any_space, whole_block(conv_state(D_FF))] + [any_space] * 4,
        out_shape=[jax.ShapeDtypeStruct((n_seq, seq_len, D_MODEL), F32),
                   jax.ShapeDtypeStruct(conv_state(D_FF), F32)] + bf16_like(ffn_mats),
        scratch_shapes=ffn_scratch,
        compiler_params=params(_vmem_limit(ffn_scratch, double_buffered=[row_block_shape])),
        name="ffn_prompt",
    )(x1, p, *ffn_w)
    bf16_wts = dict(wts, **dict(zip(("w_in", "w_a_out", "pool_w", "w_o"), mixer_bf16)),
                    **dict(zip(("w_up", "w_down", "w_ple_proj", "w_ple_gate"), ffn_bf16)))
    return y, nconv[None], jnp.transpose(npool, (1, 0, 2))[None], nffn[None], bf16_wts


def _mixer_sample_kernel(x_ref, sconv_ref, spool_ref, g_pre, w_in, conv_w, w_a_out, pool_w, pool_scale, w_o,
                         g_post, x1_ref, nconv_ref, npool_ref, cbuf, ubuf, *, batch, steps, sub_steps):
    sub_rows = sub_steps * batch
    for k in range(N_CONV):
        cbuf[k * batch:(k + 1) * batch, :] = sconv_ref[:, k, :]

    w = dict(g_pre=g_pre, w_in=w_in, w_a_out=w_a_out, pool_w=pool_w, pool_scale=pool_scale, w_o=w_o,
             g_post=g_post)
    def conv_in(sb, j, v):
        cols = slice(j * CHUNK, (j + 1) * CHUNK)
        return _causal_conv(cbuf, cols, v, conv_w, N_CONV * batch + sb * sub_rows, batch, sub_rows)

    def pool_diff(sb, j, u):
        t0 = sb * sub_steps
        cols = slice(j * CHUNK, (j + 1) * CHUNK)
        window = POOL_WINDOWS[j]
        ubuf[t0:t0 + sub_steps, :, cols] = u.reshape(sub_steps, batch, CHUNK)

        def ext(e):
            return spool_ref[e, :, cols] if e < N_POOL else ubuf[e - N_POOL, :, cols]

        diffs = []
        for t in range(t0, t0 + sub_steps):
            tsum = ext(N_POOL + t - window + 1)
            for e in range(N_POOL + t - window + 2, N_POOL + t + 1):
                tsum = tsum + ext(e)
            diffs.append(tsum * (1.0 / window) - ext(N_POOL + t))
        return jnp.concatenate(diffs, axis=0)

    def load_x(sb):
        return jnp.concatenate([x_ref[:, t, :] for t in range(sb * sub_steps, (sb + 1) * sub_steps)], axis=0)

    def store_x1(sb, value):
        x1_ref[sb * sub_rows:(sb + 1) * sub_rows, :] = value

    _mixer_blocks(steps // sub_steps, load_x, store_x1, w, conv_in, pool_diff)

    for k in range(N_CONV):
        nconv_ref[:, k, :] = cbuf[(steps + k) * batch:(steps + k + 1) * batch, :]
    for m in range(N_POOL):
        e = steps + m
        npool_ref[m] = spool_ref[e] if e < N_POOL else ubuf[e - N_POOL]


def _ffn_sample_kernel(x_ref, p_ref, sffn_ref, g_pre, w_up, conv_w, w_down, g_post, w_proj, w_gate,
                       y_ref, nffn_ref, fbuf, *, batch, steps, sub_steps):
    sub_rows = sub_steps * batch
    for k in range(N_CONV):
        fbuf[k * batch:(k + 1) * batch, :] = sffn_ref[:, k, :]

    w = dict(g_pre=g_pre, w_up=w_up, w_down=w_down, g_post=g_post, w_proj=w_proj, w_gate=w_gate)
    def conv_in(sb, j, v):
        cols = slice(j * CHUNK, (j + 1) * CHUNK)
        return _causal_conv(fbuf, cols, v, conv_w, N_CONV * batch + sb * sub_rows, batch, sub_rows)

    def load_x(sb):
        return x_ref[sb * sub_rows:(sb + 1) * sub_rows, :]

    def load_p(sb):
        return jnp.concatenate([p_ref[:, t, :] for t in range(sb * sub_steps, (sb + 1) * sub_steps)], axis=0)

    def store_y(sb, value):
        for t in range(sub_steps):
            y_ref[:, sb * sub_steps + t, :] = value[t * batch:(t + 1) * batch, :]

    _ffn_blocks(steps // sub_steps, load_x, load_p, store_y, w, conv_in)

    for k in range(N_CONV):
        nffn_ref[:, k, :] = fbuf[(steps + k) * batch:(steps + k + 1) * batch, :]


def _sample_layer(x, p, sconv, spool, sffn, layer, wts):
    batch, steps, _ = x.shape
    sub_steps = max(1, SAMPLE_SUB_ROWS // batch)
    assert steps % sub_steps == 0 and batch % 8 == 0
    rows = batch * steps
    params = lambda limit: pltpu.CompilerParams(vmem_limit_bytes=limit)
    whole = pl.BlockSpec(memory_space=pltpu.VMEM)
    of_layer = lambda a, k: pl.BlockSpec((None,) + a.shape[1:], lambda: (k, 0, 0, 0))
    one_layer = lambda a: jax.ShapeDtypeStruct((1,) + a.shape[1:], F32)
    x1_shape = jax.ShapeDtypeStruct((rows, D_MODEL), F32)

    mixer_w = (wts["g_pre_mix"], wts["w_in"], wts["conv_a_w"], wts["w_a_out"], wts["pool_w"],
               wts["pool_scale"], wts["w_o"], wts["g_post_mix"])
    mixer_scratch = [pltpu.VMEM(((N_CONV + steps) * batch, D_MODEL), F32), pltpu.VMEM((steps, batch, D_MODEL), F32)]
    mixer_out = [x1_shape, one_layer(sconv), one_layer(spool)]
    x1, nconv, npool = pl.pallas_call(
        functools.partial(_mixer_sample_kernel, batch=batch, steps=steps, sub_steps=sub_steps),
        in_specs=[whole, of_layer(sconv, layer), of_layer(spool, layer)] + [whole] * len(mixer_w),
        out_specs=[whole, of_layer(sconv, 0), of_layer(spool, 0)],
        out_shape=mixer_out,
        scratch_shapes=mixer_scratch,
        compiler_params=params(_vmem_limit(mixer_scratch, single_buffered=[x, *mixer_out, *mixer_out[1:], *mixer_w])),
        name="mixer_sample",
    )(x, sconv, spool, *mixer_w)

    ffn_w = (wts["g_pre_ffn"], wts["w_up"], wts["ffn_conv_w"], wts["w_down"], wts["g_post_ffn"],
             wts["w_ple_proj"], wts["w_ple_gate"])
    ffn_scratch = [pltpu.VMEM(((N_CONV + steps) * batch, D_FF), F32)]
    ffn_out = [jax.ShapeDtypeStruct(x.shape, F32), one_layer(sffn)]
    y, nffn = pl.pallas_call(
        functools.partial(_ffn_sample_kernel, batch=batch, steps=steps, sub_steps=sub_steps),
        in_specs=[whole, of_layer(p, layer), of_layer(sffn, layer)] + [whole] * len(ffn_w),
        out_specs=[whole, of_layer(sffn, 0)],
        out_shape=ffn_out,
        scratch_shapes=ffn_scratch,
        compiler_params=params(_vmem_limit(
            ffn_scratch, single_buffered=[x1_shape, one_layer(p), *ffn_out, ffn_out[1], *ffn_w])),
        name="ffn_sample",
    )(x1, p, sffn, *ffn_w)
    return y, nconv, npool, nffn


def kernel(x_prompt, x_sample, p_prompt, p_sample, state_conv_a, state_pool, state_ffn_conv, g_pre_mix, w_in,
           conv_a_w, w_a_out, pool_w, pool_scale, w_o, g_post_mix, g_pre_ffn, w_up, ffn_conv_w, w_down,
           g_post_ffn, w_ple_proj, w_ple_gate):
    depth = w_in.shape[0]
    xp, xs = x_prompt, x_sample
    prompt_states, sample_states = [], []
    time_major = lambda s: jnp.transpose(s, (0, 2, 1, 3))
    state_pool_tm = time_major(state_pool)
    for i in range(depth):
        wts = dict(
            g_pre_mix=g_pre_mix[i][None], g_post_mix=g_post_mix[i][None], g_pre_ffn=g_pre_ffn[i][None],
            g_post_ffn=g_post_ffn[i][None], pool_scale=pool_scale[i][None], conv_a_w=conv_a_w[i],
            ffn_conv_w=ffn_conv_w[i],
            w_in=w_in[i], w_a_out=w_a_out[i], pool_w=pool_w[i].reshape(D_MODEL, CHUNK),
            w_o=w_o[i], w_up=w_up[i], w_down=w_down[i], w_ple_proj=w_ple_proj[i], w_ple_gate=w_ple_gate[i])
        xp, c1, p1, f1, bf16_wts = _prompt_layer(xp, p_prompt[i], wts)
        xs, c2, p2, f2 = _sample_layer(xs, p_sample, state_conv_a, state_pool_tm, state_ffn_conv, i, bf16_wts)
        prompt_states.append((c1, p1, f1))
        sample_states.append((c2, time_major(p2), f2))
    cat = lambda states, k: states[0][k] if depth == 1 else jnp.concatenate([s[k] for s in states], axis=0)
    return (xp, xs, cat(prompt_states, 0), cat(prompt_states, 1), cat(prompt_states, 2),
            cat(sample_states, 0), cat(sample_states, 1), cat(sample_states, 2))
```

```python
import functools
import math

import jax
import jax.numpy as jnp
from jax import lax
from jax.experimental import pallas as pl
from jax.experimental.pallas import tpu as pltpu

D_MODEL = 1024
D_FF = 2816
PLE_DIM = 256
CONV_W = 3
POOL_WINDOWS = (2, 4, 8, 16)
POOL_MAX = 16
EPS = 1e-6

N_CONV = CONV_W - 1
N_POOL = POOL_MAX - 1
CHUNK = 256
N_MIX_CHUNKS = D_MODEL // CHUNK
N_FF_CHUNKS = D_FF // CHUNK
TIME_STEPS = 32
SUB_STEPS = 32
SAMPLE_SUB_ROWS = 256
PROMPT_START_POS = 0
SAMPLE_START_POS = 16384
MIXER_NORM_AHEAD_CHUNK = 2
FFN_NORM_AHEAD_CHUNK = 8
FFN_FINISH_CHUNK = 1
WEIGHT_CHUNK_ROWS = 128
WEIGHT_STAGE_SLOTS = 4
VMEM_CAPACITY_BYTES = 64 * 1024 * 1024
VMEM_COMPILER_SCRATCH_BYTES = 6 * 1024 * 1024
VMEM_UNREQUESTABLE_BYTES = 4 * 1024 * 1024

F32 = jnp.float32
BF16 = jnp.bfloat16


def _nbytes(a):
    return math.prod(a.shape) * jnp.dtype(a.dtype).itemsize


def _vmem_limit(scratch, single_buffered=(), double_buffered=()):
    total = sum(_nbytes(s) for s in scratch if s.memory_space == pltpu.VMEM)
    total += sum(_nbytes(a) for a in single_buffered) + 2 * sum(_nbytes(a) for a in double_buffered)
    return min(total + VMEM_COMPILER_SCRATCH_BYTES, VMEM_CAPACITY_BYTES - VMEM_UNREQUESTABLE_BYTES)


def _rmsnorm(x, g):
    ms = jnp.mean(x * x, axis=-1, keepdims=True)
    return (x * lax.rsqrt(ms + EPS)) * g


def _dot(a, b):
    return jnp.dot(a, b, preferred_element_type=F32)


def _causal_conv(buf, cols, v, w_ref, halo, step, rows):
    buf[pl.ds(halo, rows), cols] = v
    y = buf[pl.ds(halo - 2 * step, rows), cols] * w_ref[0:1, cols]
    y = y + buf[pl.ds(halo - step, rows), cols] * w_ref[1:2, cols]
    return y + v * w_ref[2:3, cols]


def _mixer_blocks(n_sub, load_x, store_out, w, conv_in, pool_diff):
    def normed(sb):
        x = load_x(sb)
        return x, _rmsnorm(x, w["g_pre"][...]).astype(BF16)

    def finish(sb, x, mix):
        store_out(sb, x + _rmsnorm(mix, w["g_post"][...]))

    cur = normed(0)
    pending = None
    for sb in range(n_sub):
        x, xb = cur

        def proj(k, j, xb=xb):
            lo = k * D_MODEL + j * CHUNK
            return _dot(xb, w["w_in"][:, lo:lo + CHUNK])

        def branch_proj(j):
            return tuple(proj(k, j) for k in range(4))

        def gate_proj(j):
            return proj(4, j), proj(5, j)

        nxt = branch_proj(0)
        y_a = None
        y_p = []
        for j in range(N_MIX_CHUNKS):
            cols = slice(j * CHUNK, (j + 1) * CHUNK)
            b, c, h, u = nxt
            nxt = branch_proj(j + 1) if j + 1 < N_MIX_CHUNKS else gate_proj(0)
            if j == 0 and pending is not None:
                finish(*pending)
            if j == MIXER_NORM_AHEAD_CHUNK and sb + 1 < n_sub:
                cur = normed(sb + 1)
            z = (b * conv_in(sb, j, c * h)).astype(BF16)
            t = _dot(z, w["w_a_out"][cols, :])
            y_a = t if y_a is None else y_a + t
            d = pool_diff(sb, j, u).astype(BF16)
            y_p.append(_dot(d, w["pool_w"][cols, :]) * w["pool_scale"][:, cols])
        mix = None
        for j in range(N_MIX_CHUNKS):
            cols = slice(j * CHUNK, (j + 1) * CHUNK)
            ga, gp = nxt
            if j + 1 < N_MIX_CHUNKS:
                nxt = gate_proj(j + 1)
            merged = jax.nn.sigmoid(ga) * y_a[:, cols] + jax.nn.sigmoid(gp) * y_p[j]
            t = _dot(merged.astype(BF16), w["w_o"][cols, :])
            mix = t if mix is None else mix + t
        pending = (sb, x, mix)
    finish(*pending)


def _ffn_blocks(n_sub, load_x, load_p, store_out, w, conv_in):
    def normed(sb):
        x = load_x(sb)
        return x, _rmsnorm(x, w["g_pre"][...]).astype(BF16)

    def finish(sb, x, f, ple):
        x = x + _rmsnorm(f, w["g_post"][...])
        gate = jax.nn.sigmoid(_dot(x.astype(BF16), w["w_gate"][...]))
        store_out(sb, x + gate * ple)

    cur = normed(0)
    pending = None
    for sb in range(n_sub):
        x, hb = cur

        def up(j, hb=hb):
            return (_dot(hb, w["w_up"][:, j * CHUNK:(j + 1) * CHUNK]),
                    _dot(hb, w["w_up"][:, D_FF + j * CHUNK:D_FF + (j + 1) * CHUNK]))

        ple = _dot(load_p(sb).astype(BF16), w["w_proj"][...])
        nxt = up(0)
        f = None
        for j in range(N_FF_CHUNKS):
            cols = slice(j * CHUNK, (j + 1) * CHUNK)
            a, g = nxt
            if j + 1 < N_FF_CHUNKS:
                nxt = up(j + 1)
            if j == FFN_FINISH_CHUNK and pending is not None:
                finish(*pending)
            if j == FFN_NORM_AHEAD_CHUNK and sb + 1 < n_sub:
                cur = normed(sb + 1)
            h = jax.nn.gelu(conv_in(sb, j, a), approximate=True) * g
            t = _dot(h.astype(BF16), w["w_down"][cols, :])
            f = t if f is None else f + t
        pending = (sb, x, f, ple)
    finish(*pending)


def _block_copy(hbm, buf, sem, block, slot, seq, steps, *, to_hbm=False):
    hbm_view = hbm.at[seq, pl.ds(block * steps, steps), :]
    buf_view = buf.at[slot, :, seq, :]
    src, dst = (buf_view, hbm_view) if to_hbm else (hbm_view, buf_view)
    return pltpu.make_async_copy(src, dst, sem.at[slot, seq])


def _start_block_fetch(hbm, buf, sem, block, slot, steps, n_seq):
    for s in range(n_seq):
        _block_copy(hbm, buf, sem, block, slot, s, steps).start()


def _fetch_block(hbm, buf, sem, steps, n_seq):
    i = pl.program_id(0)
    slot = i % 2

    @pl.when(i + 1 < pl.num_programs(0))
    def _():
        _start_block_fetch(hbm, buf, sem, i + 1, 1 - slot, steps, n_seq)

    for s in range(n_seq):
        _block_copy(hbm, buf, sem, i, slot, s, steps).wait()
    return slot


def _stream_weight_as_bf16(w_hbm, w_vmem, stage, sem):
    n_rows, width = w_hbm.shape
    n_slots, chunk, _ = stage.shape
    n_chunks = n_rows // chunk
    assert n_chunks * chunk == n_rows and width <= stage.shape[2]

    def chunk_copy(c, slot):
        return pltpu.make_async_copy(w_hbm.at[pl.ds(c * chunk, chunk), :], stage.at[slot, :, 0:width], sem.at[slot])

    for c in range(min(n_slots, n_chunks)):
        chunk_copy(c, c).start()

    def body(c, carry):
        slot = c % n_slots
        chunk_copy(c, slot).wait()
        w_vmem[pl.ds(pl.multiple_of(c * chunk, chunk), chunk), :] = stage[slot, :, 0:width].astype(BF16)

        @pl.when(c + n_slots < n_chunks)
        def _():
            chunk_copy(c + n_slots, slot).start()

        return carry

    lax.fori_loop(0, n_chunks, body, 0)


def _load_weights(pairs, exports, stage, stage_sem, export_sem):
    for w_hbm, w_vmem in pairs:
        _stream_weight_as_bf16(w_hbm, w_vmem, stage, stage_sem)
    for k, ((_, w_vmem), w_out) in enumerate(zip(pairs, exports)):
        pltpu.make_async_copy(w_vmem, w_out, export_sem.at[k]).start()


def _finish_weight_exports(pairs, exports, export_sem):
    for k, ((_, w_vmem), w_out) in enumerate(zip(pairs, exports)):
        pltpu.make_async_copy(w_vmem, w_out, export_sem.at[k]).wait()


def _mixer_prompt_kernel(x_hbm, g_pre, w_in_hbm, conv_w, w_a_out_hbm, pool_w_hbm, pool_scale, w_o_hbm, g_post,
                         x1_ref, nconv_ref, npool_ref, w_in_out, w_a_out_out, pool_w_out, w_o_out,
                         xbuf, xsem, cbuf, ubuf, e0, e1, e2, inv_cnt,
                         w_in, w_a_out, pool_w, w_o, stage, stage_sem, export_sem,
                         *, steps, sub_steps, n_seq):
    i = pl.program_id(0)
    weight_pairs = ((w_in_hbm, w_in), (w_a_out_hbm, w_a_out), (pool_w_hbm, pool_w), (w_o_hbm, w_o))
    weight_exports = (w_in_out, w_a_out_out, pool_w_out, w_o_out)
    sub_rows = sub_steps * n_seq
    n_sub = steps // sub_steps
    conv_halo = N_CONV * n_seq
    pool_halo = N_POOL * n_seq
    head = POOL_MAX * n_seq
    ebufs = (e0, e1, e2)

    @pl.when(i == 0)
    def _():
        _start_block_fetch(x_hbm, xbuf, xsem, 0, 0, steps, n_seq)
        cbuf[0:conv_halo, :] = jnp.zeros((conv_halo, D_MODEL), F32)
        ubuf[0:pool_halo, :] = jnp.zeros((pool_halo, D_MODEL), F32)
        pos = lax.broadcasted_iota(jnp.int32, (POOL_MAX, n_seq, CHUNK), 0).reshape(head, CHUNK)
        for j, window in enumerate(POOL_WINDOWS):
            inv_cnt[j] = 1.0 / jnp.minimum(pos + 1, window).astype(F32)
        _load_weights(weight_pairs, weight_exports, stage, stage_sem, export_sem)

    @pl.when(i == 1)
    def _():
        for j, window in enumerate(POOL_WINDOWS):
            inv_cnt[j] = jnp.full((head, CHUNK), 1.0 / window, F32)

    slot = _fetch_block(x_hbm, xbuf, xsem, steps, n_seq)
    w = dict(g_pre=g_pre, w_in=w_in, w_a_out=w_a_out, pool_w=pool_w, pool_scale=pool_scale, w_o=w_o,
             g_post=g_post)

    def conv_in(sb, j, v):
        cols = slice(j * CHUNK, (j + 1) * CHUNK)
        y = _causal_conv(cbuf, cols, v, conv_w, conv_halo, n_seq, sub_rows)
        cbuf[0:conv_halo, cols] = cbuf[pl.ds(sub_rows, conv_halo), cols]
        return y

    def pool_diff(sb, j, u):
        cols = slice(j * CHUNK, (j + 1) * CHUNK)
        window = POOL_WINDOWS[j]
        ubuf[pl.ds(pool_halo, sub_rows), cols] = u
        n_stages = j + 1
        tsum = None
        for k in range(n_stages):
            lo = pool_halo - (window - (2 << k)) * n_seq
            n = pool_halo + sub_rows - lo
            shift = (1 << k) * n_seq
            if k == 0:
                tsum = ubuf[pl.ds(lo, n), cols] + ubuf[pl.ds(lo - shift, n), cols]
            else:
                tsum = ebufs[k - 1][pl.ds(lo, n), cols] + ebufs[k - 1][pl.ds(lo - shift, n), cols]
            if k < n_stages - 1:
                ebufs[k][pl.ds(lo, n), cols] = tsum
        keep = (window - 1) * n_seq
        ubuf[pl.ds(pool_halo - keep, keep), cols] = ubuf[pl.ds(pool_halo + sub_rows - keep, keep), cols]
        if sb > 0:
            return tsum * (1.0 / window) - u
        return jnp.concatenate([tsum[0:head, :] * inv_cnt[j] - u[0:head, :],
                                tsum[head:, :] * (1.0 / window) - u[head:, :]], axis=0)

    def load_x(sb):
        return xbuf[slot, sb * sub_steps:(sb + 1) * sub_steps].reshape(sub_rows, D_MODEL)

    def store_x1(sb, value):
        x1_ref[sb * sub_rows:(sb + 1) * sub_rows, :] = value

    _mixer_blocks(n_sub, load_x, store_x1, w, conv_in, pool_diff)

    @pl.when(i == pl.num_programs(0) - 1)
    def _():
        for k in range(N_CONV):
            nconv_ref[:, k, :] = cbuf[k * n_seq:(k + 1) * n_seq, :]
        npool_ref[...] = ubuf[pl.ds(sub_rows, pool_halo), :].reshape(N_POOL, n_seq, D_MODEL)
        _finish_weight_exports(weight_pairs, weight_exports, export_sem)


def _ffn_prompt_kernel(x_ref, p_hbm, g_pre, w_up_hbm, conv_w, w_down_hbm, g_post, w_proj_hbm, w_gate_hbm,
                       y_hbm, nffn_ref, w_up_out, w_down_out, w_proj_out, w_gate_out,
                       pbuf, psem, ybuf, ysem, fbuf,
                       w_up, w_down, w_proj, w_gate, stage, stage_sem, export_sem,
                       *, steps, sub_steps, n_seq):
    i = pl.program_id(0)
    last = pl.num_programs(0) - 1
    sub_rows = sub_steps * n_seq
    n_sub = steps // sub_steps
    conv_halo = N_CONV * n_seq
    weight_pairs = ((w_up_hbm, w_up), (w_down_hbm, w_down), (w_proj_hbm, w_proj), (w_gate_hbm, w_gate))
    weight_exports = (w_up_out, w_down_out, w_proj_out, w_gate_out)

    @pl.when(i == 0)
    def _():
        _start_block_fetch(p_hbm, pbuf, psem, 0, 0, steps, n_seq)
        fbuf[0:conv_halo, :] = jnp.zeros((conv_halo, D_FF), F32)
        _load_weights(weight_pairs, weight_exports, stage, stage_sem, export_sem)

    def wait_store(block, slot):
        for s in range(n_seq):
            _block_copy(y_hbm, ybuf, ysem, block, slot, s, steps, to_hbm=True).wait()

    slot = _fetch_block(p_hbm, pbuf, psem, steps, n_seq)

    @pl.when(i >= 2)
    def _():
        wait_store(i - 2, slot)

    w = dict(g_pre=g_pre, w_up=w_up, w_down=w_down, g_post=g_post, w_proj=w_proj, w_gate=w_gate)

    def conv_in(sb, j, v):
        cols = slice(j * CHUNK, (j + 1) * CHUNK)
        y = _causal_conv(fbuf, cols, v, conv_w, conv_halo, n_seq, sub_rows)
        fbuf[0:conv_halo, cols] = fbuf[pl.ds(sub_rows, conv_halo), cols]
        return y

    def load_x(sb):
        return x_ref[sb * sub_rows:(sb + 1) * sub_rows, :]

    def load_p(sb):
        return pbuf[slot, sb * sub_steps:(sb + 1) * sub_steps].reshape(sub_rows, PLE_DIM)

    def store_y(sb, value):
        ybuf[slot, sb * sub_steps:(sb + 1) * sub_steps] = value.reshape(sub_steps, n_seq, D_MODEL)

    _ffn_blocks(n_sub, load_x, load_p, store_y, w, conv_in)

    for s in range(n_seq):
        _block_copy(y_hbm, ybuf, ysem, i, slot, s, steps, to_hbm=True).start()

    @pl.when(i == last)
    def _():
        for k in range(N_CONV):
            nffn_ref[:, k, :] = fbuf[k * n_seq:(k + 1) * n_seq, :]
        wait_store(i - 1, 1 - slot)
        wait_store(i, slot)
        _finish_weight_exports(weight_pairs, weight_exports, export_sem)


def _resident(shape):
    return pl.BlockSpec(shape, lambda i: (0,) * len(shape), pipeline_mode=pl.Buffered(1))


def _prompt_layer(x, p, wts):
    n_seq, seq_len, _ = x.shape
    steps, sub_steps = TIME_STEPS, SUB_STEPS
    n_blocks = seq_len // steps
    assert n_blocks * steps == seq_len and n_blocks >= 2 and n_seq % 8 == 0
    assert steps % sub_steps == 0 and sub_steps > POOL_MAX
    assert PROMPT_START_POS == 0
    rows = steps * n_seq
    params = lambda limit: pltpu.CompilerParams(dimension_semantics=("arbitrary",), vmem_limit_bytes=limit)
    any_space = pl.BlockSpec(memory_space=pl.ANY)
    row_block = pl.BlockSpec((rows, D_MODEL), lambda i: (i, 0))
    row_block_shape = jax.ShapeDtypeStruct((rows, D_MODEL), F32)
    whole_block = lambda shape: pl.BlockSpec(shape, lambda i: (0,) * len(shape))
    conv_state = lambda width: (n_seq, N_CONV, width)
    pool_state = (N_POOL, n_seq, D_MODEL)

    weight_spec = lambda a: any_space if a.dtype == F32 and a.shape[0] >= WEIGHT_CHUNK_ROWS else _resident(a.shape)
    bf16_like = lambda ws: [jax.ShapeDtypeStruct(a.shape, BF16) for a in ws]
    vmem_bf16 = lambda ws: [pltpu.VMEM(a.shape, BF16) for a in ws]
    stage = lambda ws: [pltpu.VMEM((WEIGHT_STAGE_SLOTS, WEIGHT_CHUNK_ROWS, max(a.shape[1] for a in ws)), F32),
                        pltpu.SemaphoreType.DMA((WEIGHT_STAGE_SLOTS,)), pltpu.SemaphoreType.DMA((len(ws),))]

    mixer_w = (wts["g_pre_mix"], wts["w_in"], wts["conv_a_w"], wts["w_a_out"], wts["pool_w"],
               wts["pool_scale"], wts["w_o"], wts["g_post_mix"])
    mixer_mats = (wts["w_in"], wts["w_a_out"], wts["pool_w"], wts["w_o"])
    sub_rows = sub_steps * n_seq
    stage_rows = N_POOL * n_seq + sub_rows
    mixer_scratch = ([pltpu.VMEM((2, steps, n_seq, D_MODEL), F32), pltpu.SemaphoreType.DMA((2, n_seq)),
                      pltpu.VMEM((N_CONV * n_seq + sub_rows, D_MODEL), F32),
                      pltpu.VMEM((stage_rows, D_MODEL), F32),
                      pltpu.VMEM((stage_rows, D_MODEL), F32),
                      pltpu.VMEM((stage_rows, D_MODEL), F32),
                      pltpu.VMEM((stage_rows, D_MODEL), F32),
                      pltpu.VMEM((len(POOL_WINDOWS), POOL_MAX * n_seq, CHUNK), F32)]
                     + vmem_bf16(mixer_mats) + stage(mixer_mats))
    x1, nconv, npool, *mixer_bf16 = pl.pallas_call(
        functools.partial(_mixer_prompt_kernel, steps=steps, sub_steps=sub_steps, n_seq=n_seq),
        grid=(n_blocks,),
        in_specs=[any_space] + [weight_spec(a) for a in mixer_w],
        out_specs=[row_block, whole_block(conv_state(D_MODEL)), whole_block(pool_state)] + [any_space] * 4,
        out_shape=[jax.ShapeDtypeStruct((n_blocks * rows, D_MODEL), F32),
                   jax.ShapeDtypeStruct(conv_state(D_MODEL), F32),
                   jax.ShapeDtypeStruct(pool_state, F32)] + bf16_like(mixer_mats),
        scratch_shapes=mixer_scratch,
        compiler_params=params(_vmem_limit(mixer_scratch, double_buffered=[row_block_shape])),
        name="mixer_prompt",
    )(x, *mixer_w)

    ffn_w = (wts["g_pre_ffn"], wts["w_up"], wts["ffn_conv_w"], wts["w_down"], wts["g_post_ffn"],
             wts["w_ple_proj"], wts["w_ple_gate"])
    ffn_mats = (wts["w_up"], wts["w_down"], wts["w_ple_proj"], wts["w_ple_gate"])
    ffn_scratch = ([pltpu.VMEM((2, steps, n_seq, PLE_DIM), F32), pltpu.SemaphoreType.DMA((2, n_seq)),
                    pltpu.VMEM((2, steps, n_seq, D_MODEL), F32), pltpu.SemaphoreType.DMA((2, n_seq)),
                    pltpu.VMEM((N_CONV * n_seq + sub_rows, D_FF), F32)]
                   + vmem_bf16(ffn_mats) + stage(ffn_mats))
    y, nffn, *ffn_bf16 = pl.pallas_call(
        functools.partial(_ffn_prompt_kernel, steps=steps, sub_steps=sub_steps, n_seq=n_seq),
        grid=(n_blocks,),
        in_specs=[row_block, any_space] + [weight_spec(a) for a in ffn_w],
        out_specs=[any_space, whole_block(conv_state(D_FF))] + [any_space] * 4,
        out_shape=[jax.ShapeDtypeStruct((n_seq, seq_len, D_MODEL), F32),
                   jax.ShapeDtypeStruct(conv_state(D_FF), F32)] + bf16_like(ffn_mats),
        scratch_shapes=ffn_scratch,
        compiler_params=params(_vmem_limit(ffn_scratch, double_buffered=[row_block_shape])),
        name="ffn_prompt",
    )(x1, p, *ffn_w)
    bf16_wts = dict(wts, **dict(zip(("w_in", "w_a_out", "pool_w", "w_o"), mixer_bf16)),
                    **dict(zip(("w_up", "w_down", "w_ple_proj", "w_ple_gate"), ffn_bf16)))
    return y, nconv[None], jnp.transpose(npool, (1, 0, 2))[None], nffn[None], bf16_wts


def _mixer_sample_kernel(x_ref, sconv_ref, spool_ref, g_pre, w_in, conv_w, w_a_out, pool_w, pool_scale, w_o,
                         g_post, x1_ref, nconv_ref, npool_ref, cbuf, ubuf, *, batch, steps, sub_steps):
    sub_rows = sub_steps * batch
    for k in range(N_CONV):
        cbuf[k * batch:(k + 1) * batch, :] = sconv_ref[:, k, :]

    w = dict(g_pre=g_pre, w_in=w_in, w_a_out=w_a_out, pool_w=pool_w, pool_scale=pool_scale, w_o=w_o,
             g_post=g_post)
    def conv_in(sb, j, v):
        cols = slice(j * CHUNK, (j + 1) * CHUNK)
        return _causal_conv(cbuf, cols, v, conv_w, N_CONV * batch + sb * sub_rows, batch, sub_rows)

    def pool_diff(sb, j, u):
        t0 = sb * sub_steps
        cols = slice(j * CHUNK, (j + 1) * CHUNK)
        window = POOL_WINDOWS[j]
        ubuf[t0:t0 + sub_steps, :, cols] = u.reshape(sub_steps, batch, CHUNK)

        def ext(e):
            return spool_ref[e, :, cols] if e < N_POOL else ubuf[e - N_POOL, :, cols]

        diffs = []
        for t in range(t0, t0 + sub_steps):
            tsum = ext(N_POOL + t - window + 1)
            for e in range(N_POOL + t - window + 2, N_POOL + t + 1):
                tsum = tsum + ext(e)
            diffs.append(tsum * (1.0 / window) - ext(N_POOL + t))
        return jnp.concatenate(diffs, axis=0)

    def load_x(sb):
        return jnp.concatenate([x_ref[:, t, :] for t in range(sb * sub_steps, (sb + 1) * sub_steps)], axis=0)

    def store_x1(sb, value):
        x1_ref[sb * sub_rows:(sb + 1) * sub_rows, :] = value

    _mixer_blocks(steps // sub_steps, load_x, store_x1, w, conv_in, pool_diff)

    for k in range(N_CONV):
        nconv_ref[:, k, :] = cbuf[(steps + k) * batch:(steps + k + 1) * batch, :]
    for m in range(N_POOL):
        e = steps + m
        npool_ref[m] = spool_ref[e] if e < N_POOL else ubuf[e - N_POOL]


def _ffn_sample_kernel(x_ref, p_ref, sffn_ref, g_pre, w_up, conv_w, w_down, g_post, w_proj, w_gate,
                       y_ref, nffn_ref, fbuf, *, batch, steps, sub_steps):
    sub_rows = sub_steps * batch
    for k in range(N_CONV):
        fbuf[k * batch:(k + 1) * batch, :] = sffn_ref[:, k, :]

    w = dict(g_pre=g_pre, w_up=w_up, w_down=w_down, g_post=g_post, w_proj=w_proj, w_gate=w_gate)
    def conv_in(sb, j, v):
        cols = slice(j * CHUNK, (j + 1) * CHUNK)
        return _causal_conv(fbuf, cols, v, conv_w, N_CONV * batch + sb * sub_rows, batch, sub_rows)

    def load_x(sb):
        return x_ref[sb * sub_rows:(sb + 1) * sub_rows, :]

    def load_p(sb):
        return jnp.concatenate([p_ref[:, t, :] for t in range(sb * sub_steps, (sb + 1) * sub_steps)], axis=0)

    def store_y(sb, value):
        for t in range(sub_steps):
            y_ref[:, sb * sub_steps + t, :] = value[t * batch:(t + 1) * batch, :]

    _ffn_blocks(steps // sub_steps, load_x, load_p, store_y, w, conv_in)

    for k in range(N_CONV):
        nffn_ref[:, k, :] = fbuf[(steps + k) * batch:(steps + k + 1) * batch, :]


def _sample_layer(x, p, sconv, spool, sffn, layer, wts):
    batch, steps, _ = x.shape
    sub_steps = max(1, SAMPLE_SUB_ROWS // batch)
    assert steps % sub_steps == 0 and batch % 8 == 0
    assert SAMPLE_START_POS >= N_POOL
    rows = batch * steps
    params = lambda limit: pltpu.CompilerParams(vmem_limit_bytes=limit)
    whole = pl.BlockSpec(memory_space=pltpu.VMEM)
    of_layer = lambda a, k: pl.BlockSpec((None,) + a.shape[1:], lambda: (k, 0, 0, 0))
    one_layer = lambda a: jax.ShapeDtypeStruct((1,) + a.shape[1:], F32)
    x1_shape = jax.ShapeDtypeStruct((rows, D_MODEL), F32)

    mixer_w = (wts["g_pre_mix"], wts["w_in"], wts["conv_a_w"], wts["w_a_out"], wts["pool_w"],
               wts["pool_scale"], wts["w_o"], wts["g_post_mix"])
    mixer_scratch = [pltpu.VMEM(((N_CONV + steps) * batch, D_MODEL), F32), pltpu.VMEM((steps, batch, D_MODEL), F32)]
    mixer_out = [x1_shape, one_layer(sconv), one_layer(spool)]
    x1, nconv, npool = pl.pallas_call(
        functools.partial(_mixer_sample_kernel, batch=batch, steps=steps, sub_steps=sub_steps),
        in_specs=[whole, of_layer(sconv, layer), of_layer(spool, layer)] + [whole] * len(mixer_w),
        out_specs=[whole, of_layer(sconv, 0), of_layer(spool, 0)],
        out_shape=mixer_out,
        scratch_shapes=mixer_scratch,
        compiler_params=params(_vmem_limit(mixer_scratch, single_buffered=[x, *mixer_out, *mixer_out[1:], *mixer_w])),
        name="mixer_sample",
    )(x, sconv, spool, *mixer_w)

    ffn_w = (wts["g_pre_ffn"], wts["w_up"], wts["ffn_conv_w"], wts["w_down"], wts["g_post_ffn"],
             wts["w_ple_proj"], wts["w_ple_gate"])
    ffn_scratch = [pltpu.VMEM(((N_CONV + steps) * batch, D_FF), F32)]
    ffn_out = [jax.ShapeDtypeStruct(x.shape, F32), one_layer(sffn)]
    y, nffn = pl.pallas_call(
        functools.partial(_ffn_sample_kernel, batch=batch, steps=steps, sub_steps=sub_steps),
        in_specs=[whole, of_layer(p, layer), of_layer(sffn, layer)] + [whole] * len(ffn_w),
        out_specs=[whole, of_layer(sffn, 0)],
        out_shape=ffn_out,
        scratch_shapes=ffn_scratch,
        compiler_params=params(_vmem_limit(
            ffn_scratch, single_buffered=[x1_shape, one_layer(p), *ffn_out, ffn_out[1], *ffn_w])),
        name="ffn_sample",
    )(x1, p, sffn, *ffn_w)
    return y, nconv, npool, nffn


def kernel(x_prompt, x_sample, p_prompt, p_sample, state_conv_a, state_pool, state_ffn_conv, g_pre_mix, w_in,
           conv_a_w, w_a_out, pool_w, pool_scale, w_o, g_post_mix, g_pre_ffn, w_up, ffn_conv_w, w_down,
           g_post_ffn, w_ple_proj, w_ple_gate):
    depth = w_in.shape[0]
    xp, xs = x_prompt, x_sample
    prompt_states, sample_states = [], []
    time_major = lambda s: jnp.transpose(s, (0, 2, 1, 3))
    state_pool_tm = time_major(state_pool)
    for i in range(depth):
        wts = dict(
            g_pre_mix=g_pre_mix[i][None], g_post_mix=g_post_mix[i][None], g_pre_ffn=g_pre_ffn[i][None],
            g_post_ffn=g_post_ffn[i][None], pool_scale=pool_scale[i][None], conv_a_w=conv_a_w[i],
            ffn_conv_w=ffn_conv_w[i],
            w_in=w_in[i], w_a_out=w_a_out[i], pool_w=pool_w[i].reshape(D_MODEL, CHUNK),
            w_o=w_o[i], w_up=w_up[i], w_down=w_down[i], w_ple_proj=w_ple_proj[i], w_ple_gate=w_ple_gate[i])
        xp, c1, p1, f1, bf16_wts = _prompt_layer(xp, p_prompt[i], wts)
        xs, c2, p2, f2 = _sample_layer(xs, p_sample, state_conv_a, state_pool_tm, state_ffn_conv, i, bf16_wts)
        prompt_states.append((c1, p1, f1))
        sample_states.append((c2, time_major(p2), f2))
    cat = lambda states, k: states[0][k] if depth == 1 else jnp.concatenate([s[k] for s in states], axis=0)
    return (xp, xs, cat(prompt_states, 0), cat(prompt_states, 1), cat(prompt_states, 2),
            cat(sample_states, 0), cat(sample_states, 1), cat(sample_states, 2))
```

```python
import functools
import math

import jax
import jax.numpy as jnp
from jax import lax
from jax.experimental import pallas as pl
from jax.experimental.pallas import tpu as pltpu

D_MODEL = 1024
D_FF = 2816
PLE_DIM = 256
CONV_W = 3
POOL_WINDOWS = (2, 4, 8, 16)
POOL_MAX = 16
EPS = 1e-6

N_CONV = CONV_W - 1
N_POOL = POOL_MAX - 1
CHUNK = 256
N_MIX_CHUNKS = D_MODEL // CHUNK
N_FF_CHUNKS = D_FF // CHUNK
TIME_STEPS = 64
SUB_STEPS = 32
SAMPLE_SUB_ROWS = 256
PROMPT_START_POS = 0
SAMPLE_START_POS = 16384
MIXER_NORM_AHEAD_CHUNK = 2
FFN_NORM_AHEAD_CHUNK = 8
FFN_FINISH_CHUNK = 1
WEIGHT_CHUNK_ROWS = 128
WEIGHT_STAGE_SLOTS = 6
VMEM_CAPACITY_BYTES = 64 * 1024 * 1024
VMEM_COMPILER_SCRATCH_BYTES = 6 * 1024 * 1024
VMEM_UNREQUESTABLE_BYTES = 4 * 1024 * 1024

F32 = jnp.float32
BF16 = jnp.bfloat16


def _nbytes(a):
    return math.prod(a.shape) * jnp.dtype(a.dtype).itemsize


def _vmem_limit(scratch, single_buffered=(), double_buffered=()):
    total = sum(_nbytes(s) for s in scratch if s.memory_space == pltpu.VMEM)
    total += sum(_nbytes(a) for a in single_buffered) + 2 * sum(_nbytes(a) for a in double_buffered)
    return min(total + VMEM_COMPILER_SCRATCH_BYTES, VMEM_CAPACITY_BYTES - VMEM_UNREQUESTABLE_BYTES)


def _rmsnorm(x, g):
    ms = jnp.mean(x * x, axis=-1, keepdims=True)
    return (x * lax.rsqrt(ms + EPS)) * g


def _dot(a, b):
    return jnp.dot(a, b, preferred_element_type=F32)


def _causal_conv(buf, cols, v, w_ref, halo, step, rows):
    buf[pl.ds(halo, rows), cols] = v
    y = buf[pl.ds(halo - 2 * step, rows), cols] * w_ref[0:1, cols]
    y = y + buf[pl.ds(halo - step, rows), cols] * w_ref[1:2, cols]
    return y + v * w_ref[2:3, cols]


def _mixer_blocks(n_sub, load_x, store_out, w, conv_in, pool_diff):
    def normed(sb):
        x = load_x(sb)
        return x, _rmsnorm(x, w["g_pre"][...]).astype(BF16)

    def finish(sb, x, mix):
        store_out(sb, x + _rmsnorm(mix, w["g_post"][...]))

    cur = normed(0)
    pending = None
    for sb in range(n_sub):
        x, xb = cur

        def proj(k, j, xb=xb):
            lo = k * D_MODEL + j * CHUNK
            return _dot(xb, w["w_in"][:, lo:lo + CHUNK])

        def branch_proj(j):
            return tuple(proj(k, j) for k in range(4))

        def gate_proj(j):
            return proj(4, j), proj(5, j)

        nxt = branch_proj(0)
        y_a = None
        y_p = []
        for j in range(N_MIX_CHUNKS):
            cols = slice(j * CHUNK, (j + 1) * CHUNK)
            b, c, h, u = nxt
            nxt = branch_proj(j + 1) if j + 1 < N_MIX_CHUNKS else gate_proj(0)
            if j == 0 and pending is not None:
                finish(*pending)
            if j == MIXER_NORM_AHEAD_CHUNK and sb + 1 < n_sub:
                cur = normed(sb + 1)
            z = (b * conv_in(sb, j, c * h)).astype(BF16)
            t = _dot(z, w["w_a_out"][cols, :])
            y_a = t if y_a is None else y_a + t
            d = pool_diff(sb, j, u).astype(BF16)
            y_p.append(_dot(d, w["pool_w"][cols, :]) * w["pool_scale"][:, cols])
        mix = None
        for j in range(N_MIX_CHUNKS):
            cols = slice(j * CHUNK, (j + 1) * CHUNK)
            ga, gp = nxt
            if j + 1 < N_MIX_CHUNKS:
                nxt = gate_proj(j + 1)
            merged = jax.nn.sigmoid(ga) * y_a[:, cols] + jax.nn.sigmoid(gp) * y_p[j]
            t = _dot(merged.astype(BF16), w["w_o"][cols, :])
            mix = t if mix is None else mix + t
        pending = (sb, x, mix)
    finish(*pending)


def _ffn_blocks(n_sub, load_x, load_p, store_out, w, conv_in):
    def normed(sb):
        x = load_x(sb)
        return x, _rmsnorm(x, w["g_pre"][...]).astype(BF16)

    def finish(sb, x, f, ple):
        x = x + _rmsnorm(f, w["g_post"][...])
        gate = jax.nn.sigmoid(_dot(x.astype(BF16), w["w_gate"][...]))
        store_out(sb, x + gate * ple)

    cur = normed(0)
    pending = None
    for sb in range(n_sub):
        x, hb = cur

        def up(j, hb=hb):
            return (_dot(hb, w["w_up"][:, j * CHUNK:(j + 1) * CHUNK]),
                    _dot(hb, w["w_up"][:, D_FF + j * CHUNK:D_FF + (j + 1) * CHUNK]))

        ple = _dot(load_p(sb).astype(BF16), w["w_proj"][...])
        nxt = up(0)
        f = None
        for j in range(N_FF_CHUNKS):
            cols = slice(j * CHUNK, (j + 1) * CHUNK)
            a, g = nxt
            if j + 1 < N_FF_CHUNKS:
                nxt = up(j + 1)
            if j == FFN_FINISH_CHUNK and pending is not None:
                finish(*pending)
            if j == FFN_NORM_AHEAD_CHUNK and sb + 1 < n_sub:
                cur = normed(sb + 1)
            h = jax.nn.gelu(conv_in(sb, j, a), approximate=True) * g
            t = _dot(h.astype(BF16), w["w_down"][cols, :])
            f = t if f is None else f + t
        pending = (sb, x, f, ple)
    finish(*pending)


def _block_copy(hbm, buf, sem, block, slot, seq, steps, *, to_hbm=False):
    hbm_view = hbm.at[seq, pl.ds(block * steps, steps), :]
    buf_view = buf.at[slot, :, seq, :]
    src, dst = (buf_view, hbm_view) if to_hbm else (hbm_view, buf_view)
    return pltpu.make_async_copy(src, dst, sem.at[slot, seq])


def _start_block_fetch(hbm, buf, sem, block, slot, steps, n_seq):
    for s in range(n_seq):
        _block_copy(hbm, buf, sem, block, slot, s, steps).start()


def _fetch_block(hbm, buf, sem, steps, n_seq):
    i = pl.program_id(0)
    slot = i % 2

    @pl.when(i + 1 < pl.num_programs(0))
    def _():
        _start_block_fetch(hbm, buf, sem, i + 1, 1 - slot, steps, n_seq)

    for s in range(n_seq):
        _block_copy(hbm, buf, sem, i, slot, s, steps).wait()
    return slot


def _stream_weight_as_bf16(w_hbm, w_vmem, stage, sem):
    n_rows, width = w_hbm.shape
    n_slots, chunk, _ = stage.shape
    n_chunks = n_rows // chunk
    assert n_chunks * chunk == n_rows and width <= stage.shape[2]

    def chunk_copy(c, slot):
        return pltpu.make_async_copy(w_hbm.at[pl.ds(c * chunk, chunk), :], stage.at[slot, :, 0:width], sem.at[slot])

    for c in range(min(n_slots, n_chunks)):
        chunk_copy(c, c).start()

    def body(c, carry):
        slot = c % n_slots
        chunk_copy(c, slot).wait()
        w_vmem[pl.ds(pl.multiple_of(c * chunk, chunk), chunk), :] = stage[slot, :, 0:width].astype(BF16)

        @pl.when(c + n_slots < n_chunks)
        def _():
            chunk_copy(c + n_slots, slot).start()

        return carry

    lax.fori_loop(0, n_chunks, body, 0)


def _load_weights(pairs, exports, stage, stage_sem, export_sem):
    for w_hbm, w_vmem in pairs:
        _stream_weight_as_bf16(w_hbm, w_vmem, stage, stage_sem)
    for k, ((_, w_vmem), w_out) in enumerate(zip(pairs, exports)):
        pltpu.make_async_copy(w_vmem, w_out, export_sem.at[k]).start()


def _finish_weight_exports(pairs, exports, export_sem):
    for k, ((_, w_vmem), w_out) in enumerate(zip(pairs, exports)):
        pltpu.make_async_copy(w_vmem, w_out, export_sem.at[k]).wait()


def _mixer_prompt_kernel(x_hbm, g_pre, w_in_hbm, conv_w, w_a_out_hbm, pool_w_hbm, pool_scale, w_o_hbm, g_post,
                         x1_ref, nconv_ref, npool_ref, w_in_out, w_a_out_out, pool_w_out, w_o_out,
                         xbuf, xsem, cbuf, ubuf, e0, e1, e2, inv_cnt,
                         w_in, w_a_out, pool_w, w_o, stage, stage_sem, export_sem,
                         *, steps, sub_steps, n_seq):
    i = pl.program_id(0)
    weight_pairs = ((w_in_hbm, w_in), (w_a_out_hbm, w_a_out), (pool_w_hbm, pool_w), (w_o_hbm, w_o))
    weight_exports = (w_in_out, w_a_out_out, pool_w_out, w_o_out)
    sub_rows = sub_steps * n_seq
    n_sub = steps // sub_steps
    conv_halo = N_CONV * n_seq
    pool_halo = N_POOL * n_seq
    head = POOL_MAX * n_seq
    ebufs = (e0, e1, e2)

    @pl.when(i == 0)
    def _():
        _start_block_fetch(x_hbm, xbuf, xsem, 0, 0, steps, n_seq)
        cbuf[0:conv_halo, :] = jnp.zeros((conv_halo, D_MODEL), F32)
        ubuf[0:pool_halo, :] = jnp.zeros((pool_halo, D_MODEL), F32)
        pos = lax.broadcasted_iota(jnp.int32, (POOL_MAX, n_seq, CHUNK), 0).reshape(head, CHUNK)
        for j, window in enumerate(POOL_WINDOWS):
            inv_cnt[j] = 1.0 / jnp.minimum(pos + 1, window).astype(F32)
        _load_weights(weight_pairs, weight_exports, stage, stage_sem, export_sem)

    @pl.when(i == 1)
    def _():
        for j, window in enumerate(POOL_WINDOWS):
            inv_cnt[j] = jnp.full((head, CHUNK), 1.0 / window, F32)

    slot = _fetch_block(x_hbm, xbuf, xsem, steps, n_seq)
    w = dict(g_pre=g_pre, w_in=w_in, w_a_out=w_a_out, pool_w=pool_w, pool_scale=pool_scale, w_o=w_o,
             g_post=g_post)

    def conv_in(sb, j, v):
        cols = slice(j * CHUNK, (j + 1) * CHUNK)
        y = _causal_conv(cbuf, cols, v, conv_w, conv_halo, n_seq, sub_rows)
        cbuf[0:conv_halo, cols] = cbuf[pl.ds(sub_rows, conv_halo), cols]
        return y

    def pool_diff(sb, j, u):
        cols = slice(j * CHUNK, (j + 1) * CHUNK)
        window = POOL_WINDOWS[j]
        ubuf[pl.ds(pool_halo, sub_rows), cols] = u
        n_stages = j + 1
        tsum = None
        for k in range(n_stages):
            lo = pool_halo - (window - (2 << k)) * n_seq
            n = pool_halo + sub_rows - lo
            shift = (1 << k) * n_seq
            if k == 0:
                tsum = ubuf[pl.ds(lo, n), cols] + ubuf[pl.ds(lo - shift, n), cols]
            else:
                tsum = ebufs[k - 1][pl.ds(lo, n), cols] + ebufs[k - 1][pl.ds(lo - shift, n), cols]
            if k < n_stages - 1:
                ebufs[k][pl.ds(lo, n), cols] = tsum
        keep = (window - 1) * n_seq
        ubuf[pl.ds(pool_halo - keep, keep), cols] = ubuf[pl.ds(pool_halo + sub_rows - keep, keep), cols]
        if sb > 0:
            return tsum * (1.0 / window) - u
        return jnp.concatenate([tsum[0:head, :] * inv_cnt[j] - u[0:head, :],
                                tsum[head:, :] * (1.0 / window) - u[head:, :]], axis=0)

    def load_x(sb):
        return xbuf[slot, sb * sub_steps:(sb + 1) * sub_steps].reshape(sub_rows, D_MODEL)

    def store_x1(sb, value):
        x1_ref[sb * sub_rows:(sb + 1) * sub_rows, :] = value

    _mixer_blocks(n_sub, load_x, store_x1, w, conv_in, pool_diff)

    @pl.when(i == pl.num_programs(0) - 1)
    def _():
        for k in range(N_CONV):
            nconv_ref[:, k, :] = cbuf[k * n_seq:(k + 1) * n_seq, :]
        npool_ref[...] = ubuf[pl.ds(sub_rows, pool_halo), :].reshape(N_POOL, n_seq, D_MODEL)
        _finish_weight_exports(weight_pairs, weight_exports, export_sem)


def _ffn_prompt_kernel(x_ref, p_hbm, g_pre, w_up_hbm, conv_w, w_down_hbm, g_post, w_proj_hbm, w_gate_hbm,
                       y_hbm, nffn_ref, w_up_out, w_down_out, w_proj_out, w_gate_out,
                       pbuf, psem, ybuf, ysem, fbuf,
                       w_up, w_down, w_proj, w_gate, stage, stage_sem, export_sem,
                       *, steps, sub_steps, n_seq):
    i = pl.program_id(0)
    last = pl.num_programs(0) - 1
    sub_rows = sub_steps * n_seq
    n_sub = steps // sub_steps
    conv_halo = N_CONV * n_seq
    weight_pairs = ((w_up_hbm, w_up), (w_down_hbm, w_down), (w_proj_hbm, w_proj), (w_gate_hbm, w_gate))
    weight_exports = (w_up_out, w_down_out, w_proj_out, w_gate_out)

    @pl.when(i == 0)
    def _():
        _start_block_fetch(p_hbm, pbuf, psem, 0, 0, steps, n_seq)
        fbuf[0:conv_halo, :] = jnp.zeros((conv_halo, D_FF), F32)
        _load_weights(weight_pairs, weight_exports, stage, stage_sem, export_sem)

    def wait_store(block, slot):
        for s in range(n_seq):
            _block_copy(y_hbm, ybuf, ysem, block, slot, s, steps, to_hbm=True).wait()

    slot = _fetch_block(p_hbm, pbuf, psem, steps, n_seq)

    @pl.when(i >= 2)
    def _():
        wait_store(i - 2, slot)

    w = dict(g_pre=g_pre, w_up=w_up, w_down=w_down, g_post=g_post, w_proj=w_proj, w_gate=w_gate)

    def conv_in(sb, j, v):
        cols = slice(j * CHUNK, (j + 1) * CHUNK)
        y = _causal_conv(fbuf, cols, v, conv_w, conv_halo, n_seq, sub_rows)
        fbuf[0:conv_halo, cols] = fbuf[pl.ds(sub_rows, conv_halo), cols]
        return y

    def load_x(sb):
        return x_ref[sb * sub_rows:(sb + 1) * sub_rows, :]

    def load_p(sb):
        return pbuf[slot, sb * sub_steps:(sb + 1) * sub_steps].reshape(sub_rows, PLE_DIM)

    def store_y(sb, value):
        ybuf[slot, sb * sub_steps:(sb + 1) * sub_steps] = value.reshape(sub_steps, n_seq, D_MODEL)

    _ffn_blocks(n_sub, load_x, load_p, store_y, w, conv_in)

    for s in range(n_seq):
        _block_copy(y_hbm, ybuf, ysem, i, slot, s, steps, to_hbm=True).start()

    @pl.when(i == last)
    def _():
        for k in range(N_CONV):
            nffn_ref[:, k, :] = fbuf[k * n_seq:(k + 1) * n_seq, :]
        wait_store(i - 1, 1 - slot)
        wait_store(i, slot)
        _finish_weight_exports(weight_pairs, weight_exports, export_sem)


def _resident(shape):
    return pl.BlockSpec(shape, lambda i: (0,) * len(shape), pipeline_mode=pl.Buffered(1))


def _prompt_layer(x, p, wts):
    n_seq, seq_len, _ = x.shape
    steps, sub_steps = TIME_STEPS, SUB_STEPS
    n_blocks = seq_len // steps
    assert n_blocks * steps == seq_len and n_blocks >= 2 and n_seq % 8 == 0
    assert steps % sub_steps == 0 and sub_steps > POOL_MAX
    assert PROMPT_START_POS == 0
    rows = steps * n_seq
    params = lambda limit: pltpu.CompilerParams(dimension_semantics=("arbitrary",), vmem_limit_bytes=limit)
    any_space = pl.BlockSpec(memory_space=pl.ANY)
    row_block = pl.BlockSpec((rows, D_MODEL), lambda i: (i, 0))
    row_block_shape = jax.ShapeDtypeStruct((rows, D_MODEL), F32)
    whole_block = lambda shape: pl.BlockSpec(shape, lambda i: (0,) * len(shape))
    conv_state = lambda width: (n_seq, N_CONV, width)
    pool_state = (N_POOL, n_seq, D_MODEL)

    weight_spec = lambda a: any_space if a.dtype == F32 and a.shape[0] >= WEIGHT_CHUNK_ROWS else _resident(a.shape)
    bf16_like = lambda ws: [jax.ShapeDtypeStruct(a.shape, BF16) for a in ws]
    vmem_bf16 = lambda ws: [pltpu.VMEM(a.shape, BF16) for a in ws]
    stage = lambda ws: [pltpu.VMEM((WEIGHT_STAGE_SLOTS, WEIGHT_CHUNK_ROWS, max(a.shape[1] for a in ws)), F32),
                        pltpu.SemaphoreType.DMA((WEIGHT_STAGE_SLOTS,)), pltpu.SemaphoreType.DMA((len(ws),))]

    mixer_w = (wts["g_pre_mix"], wts["w_in"], wts["conv_a_w"], wts["w_a_out"], wts["pool_w"],
               wts["pool_scale"], wts["w_o"], wts["g_post_mix"])
    mixer_mats = (wts["w_in"], wts["w_a_out"], wts["pool_w"], wts["w_o"])
    sub_rows = sub_steps * n_seq
    stage_rows = N_POOL * n_seq + sub_rows
    mixer_scratch = ([pltpu.VMEM((2, steps, n_seq, D_MODEL), F32), pltpu.SemaphoreType.DMA((2, n_seq)),
                      pltpu.VMEM((N_CONV * n_seq + sub_rows, D_MODEL), F32),
                      pltpu.VMEM((stage_rows, D_MODEL), F32),
                      pltpu.VMEM((stage_rows, D_MODEL), F32),
                      pltpu.VMEM((stage_rows, D_MODEL), F32),
                      pltpu.VMEM((stage_rows, D_MODEL), F32),
                      pltpu.VMEM((len(POOL_WINDOWS), POOL_MAX * n_seq, CHUNK), F32)]
                     + vmem_bf16(mixer_mats) + stage(mixer_mats))
    x1, nconv, npool, *mixer_bf16 = pl.pallas_call(
        functools.partial(_mixer_prompt_kernel, steps=steps, sub_steps=sub_steps, n_seq=n_seq),
        grid=(n_blocks,),
        in_specs=[any_space] + [weight_spec(a) for a in mixer_w],
        out_specs=[row_block, whole_block(conv_state(D_MODEL)), whole_block(pool_state)] + [any_space] * 4,
        out_shape=[jax.ShapeDtypeStruct((n_blocks * rows, D_MODEL), F32),
                   jax.ShapeDtypeStruct(conv_state(D_MODEL), F32),
                   jax.ShapeDtypeStruct(pool_state, F32)] + bf16_like(mixer_mats),
        scratch_shapes=mixer_scratch,
        compiler_params=params(_vmem_limit(mixer_scratch, double_buffered=[row_block_shape])),
        name="mixer_prompt",
    )(x, *mixer_w)

    ffn_w = (wts["g_pre_ffn"], wts["w_up"], wts["ffn_conv_w"], wts["w_down"], wts["g_post_ffn"],
             wts["w_ple_proj"], wts["w_ple_gate"])
    ffn_mats = (wts["w_up"], wts["w_down"], wts["w_ple_proj"], wts["w_ple_gate"])
    ffn_scratch = ([pltpu.VMEM((2, steps, n_seq, PLE_DIM), F32), pltpu.SemaphoreType.DMA((2, n_seq)),
                    pltpu.VMEM((2, steps, n_seq, D_MODEL), F32), pltpu.SemaphoreType.DMA((2, n_seq)),
                    pltpu.VMEM((N_CONV * n_seq + sub_rows, D_FF), F32)]
                   + vmem_bf16(ffn_mats) + stage(ffn_mats))
    y, nffn, *ffn_bf16 = pl.pallas_call(
        functools.partial(_ffn_prompt_kernel, steps=steps, sub_steps=sub_steps, n_seq=n_seq),
        grid=(n_blocks,),
        in_specs=[row_block, any_space] + [weight_spec(a) for a in ffn_w],
        out_specs=[any_space, whole_block(conv_state(D_FF))] + [any_space] * 4,
        out_shape=[jax.ShapeDtypeStruct((n_seq, seq_len, D_MODEL), F32),
                   jax.ShapeDtypeStruct(conv_state(D_FF), F32)] + bf16_like(ffn_mats),
        scratch_shapes=ffn_scratch,
        compiler_params=params(_vmem_limit(ffn_scratch, double_buffered=[row_block_shape])),
        name="ffn_prompt",
    )(x1, p, *ffn_w)
    bf16_wts = dict(wts, **dict(zip(("w_in", "w_a_out", "pool_w", "w_o"), mixer_bf16)),
                    **dict(zip(("w_up", "w_down", "w_ple_proj", "w_ple_gate"), ffn_bf16)))
    return y, nconv[None], jnp.transpose(npool, (1, 0, 2))[None], nffn[None], bf16_wts


def _mixer_sample_kernel(x_ref, sconv_ref, spool_ref, g_pre, w_in, conv_w, w_a_out, pool_w, pool_scale, w_o,
                         g_post, x1_ref, nconv_ref, npool_ref, cbuf, ubuf, *, batch, steps, sub_steps):
    sub_rows = sub_steps * batch
    for k in range(N_CONV):
        cbuf[k * batch:(k + 1) * batch, :] = sconv_ref[:, k, :]

    w = dict(g_pre=g_pre, w_in=w_in, w_a_out=w_a_out, pool_w=pool_w, pool_scale=pool_scale, w_o=w_o,
             g_post=g_post)
    def conv_in(sb, j, v):
        cols = slice(j * CHUNK, (j + 1) * CHUNK)
        return _causal_conv(cbuf, cols, v, conv_w, N_CONV * batch + sb * sub_rows, batch, sub_rows)

    def pool_diff(sb, j, u):
        t0 = sb * sub_steps
        cols = slice(j * CHUNK, (j + 1) * CHUNK)
        window = POOL_WINDOWS[j]
        ubuf[t0:t0 + sub_steps, :, cols] = u.reshape(sub_steps, batch, CHUNK)

        def ext(e):
            return spool_ref[e, :, cols] if e < N_POOL else ubuf[e - N_POOL, :, cols]

        diffs = []
        for t in range(t0, t0 + sub_steps):
            tsum = ext(N_POOL + t - window + 1)
            for e in range(N_POOL + t - window + 2, N_POOL + t + 1):
                tsum = tsum + ext(e)
            diffs.append(tsum * (1.0 / window) - ext(N_POOL + t))
        return jnp.concatenate(diffs, axis=0)

    def load_x(sb):
        return jnp.concatenate([x_ref[:, t, :] for t in range(sb * sub_steps, (sb + 1) * sub_steps)], axis=0)

    def store_x1(sb, value):
        x1_ref[sb * sub_rows:(sb + 1) * sub_rows, :] = value

    _mixer_blocks(steps // sub_steps, load_x, store_x1, w, conv_in, pool_diff)

    for k in range(N_CONV):
        nconv_ref[:, k, :] = cbuf[(steps + k) * batch:(steps + k + 1) * batch, :]
    for m in range(N_POOL):
        e = steps + m
        npool_ref[m] = spool_ref[e] if e < N_POOL else ubuf[e - N_POOL]


def _ffn_sample_kernel(x_ref, p_ref, sffn_ref, g_pre, w_up, conv_w, w_down, g_post, w_proj, w_gate,
                       y_ref, nffn_ref, fbuf, *, batch, steps, sub_steps):
    sub_rows = sub_steps * batch
    for k in range(N_CONV):
        fbuf[k * batch:(k + 1) * batch, :] = sffn_ref[:, k, :]

    w = dict(g_pre=g_pre, w_up=w_up, w_down=w_down, g_post=g_post, w_proj=w_proj, w_gate=w_gate)
    def conv_in(sb, j, v):
        cols = slice(j * CHUNK, (j + 1) * CHUNK)
        return _causal_conv(fbuf, cols, v, conv_w, N_CONV * batch + sb * sub_rows, batch, sub_rows)

    def load_x(sb):
        return x_ref[sb * sub_rows:(sb + 1) * sub_rows, :]

    def load_p(sb):
        return jnp.concatenate([p_ref[:, t, :] for t in range(sb * sub_steps, (sb + 1) * sub_steps)], axis=0)

    def store_y(sb, value):
        for t in range(sub_steps):
            y_ref[:, sb * sub_steps + t, :] = value[t * batch:(t + 1) * batch, :]

    _ffn_blocks(steps // sub_steps, load_x, load_p, store_y, w, conv_in)

    for k in range(N_CONV):
        nffn_ref[:, k, :] = fbuf[(steps + k) * batch:(steps + k + 1) * batch, :]


def _sample_layer(x, p, sconv, spool, sffn, layer, wts):
    batch, steps, _ = x.shape
    sub_steps = max(1, SAMPLE_SUB_ROWS // batch)
    assert steps % sub_steps == 0 and batch % 8 == 0
    assert SAMPLE_START_POS >= N_POOL
    rows = batch * steps
    params = lambda limit: pltpu.CompilerParams(vmem_limit_bytes=limit)
    whole = pl.BlockSpec(memory_space=pltpu.VMEM)
    of_layer = lambda a, k: pl.BlockSpec((None,) + a.shape[1:], lambda: (k, 0, 0, 0))
    one_layer = lambda a: jax.ShapeDtypeStruct((1,) + a.shape[1:], F32)
    x1_shape = jax.ShapeDtypeStruct((rows, D_MODEL), F32)

    mixer_w = (wts["g_pre_mix"], wts["w_in"], wts["conv_a_w"], wts["w_a_out"], wts["pool_w"],
               wts["pool_scale"], wts["w_o"], wts["g_post_mix"])
    mixer_scratch = [pltpu.VMEM(((N_CONV + steps) * batch, D_MODEL), F32), pltpu.VMEM((steps, batch, D_MODEL), F32)]
    mixer_out = [x1_shape, one_layer(sconv), one_layer(spool)]
    x1, nconv, npool = pl.pallas_call(
        functools.partial(_mixer_sample_kernel, batch=batch, steps=steps, sub_steps=sub_steps),
        in_specs=[whole, of_layer(sconv, layer), of_layer(spool, layer)] + [whole] * len(mixer_w),
        out_specs=[whole, of_layer(sconv, 0), of_layer(spool, 0)],
        out_shape=mixer_out,
        scratch_shapes=mixer_scratch,
        compiler_params=params(_vmem_limit(mixer_scratch, single_buffered=[x, *mixer_out, *mixer_out[1:], *mixer_w])),
        name="mixer_sample",
    )(x, sconv, spool, *mixer_w)

    ffn_w = (wts["g_pre_ffn"], wts["w_up"], wts["ffn_conv_w"], wts["w_down"], wts["g_post_ffn"],
             wts["w_ple_proj"], wts["w_ple_gate"])
    ffn_scratch = [pltpu.VMEM(((N_CONV + steps) * batch, D_FF), F32)]
    ffn_out = [jax.ShapeDtypeStruct(x.shape, F32), one_layer(sffn)]
    y, nffn = pl.pallas_call(
        functools.partial(_ffn_sample_kernel, batch=batch, steps=steps, sub_steps=sub_steps),
        in_specs=[whole, of_layer(p, layer), of_layer(sffn, layer)] + [whole] * len(ffn_w),
        out_specs=[whole, of_layer(sffn, 0)],
        out_shape=ffn_out,
        scratch_shapes=ffn_scratch,
        compiler_params=params(_vmem_limit(
            ffn_scratch, single_buffered=[x1_shape, one_layer(p), *ffn_out, ffn_out[1], *ffn_w])),
        name="ffn_sample",
    )(x1, p, sffn, *ffn_w)
    return y, nconv, npool, nffn


def kernel(x_prompt, x_sample, p_prompt, p_sample, state_conv_a, state_pool, state_ffn_conv, g_pre_mix, w_in,
           conv_a_w, w_a_out, pool_w, pool_scale, w_o, g_post_mix, g_pre_ffn, w_up, ffn_conv_w, w_down,
           g_post_ffn, w_ple_proj, w_ple_gate):
    depth = w_in.shape[0]
    xp, xs = x_prompt, x_sample
    prompt_states, sample_states = [], []
    time_major = lambda s: jnp.transpose(s, (0, 2, 1, 3))
    state_pool_tm = time_major(state_pool)
    for i in range(depth):
        wts = dict(
            g_pre_mix=g_pre_mix[i][None], g_post_mix=g_post_mix[i][None], g_pre_ffn=g_pre_ffn[i][None],
            g_post_ffn=g_post_ffn[i][None], pool_scale=pool_scale[i][None], conv_a_w=conv_a_w[i],
            ffn_conv_w=ffn_conv_w[i],
            w_in=w_in[i], w_a_out=w_a_out[i], pool_w=pool_w[i].reshape(D_MODEL, CHUNK),
            w_o=w_o[i], w_up=w_up[i], w_down=w_down[i], w_ple_proj=w_ple_proj[i], w_ple_gate=w_ple_gate[i])
        xp, c1, p1, f1, bf16_wts = _prompt_layer(xp, p_prompt[i], wts)
        xs, c2, p2, f2 = _sample_layer(xs, p_sample, state_conv_a, state_pool_tm, state_ffn_conv, i, bf16_wts)
        prompt_states.append((c1, p1, f1))
        sample_states.append((c2, time_major(p2), f2))
    cat = lambda states, k: states[0][k] if depth == 1 else jnp.concatenate([s[k] for s in states], axis=0)
    return (xp, xs, cat(prompt_states, 0), cat(prompt_states, 1), cat(prompt_states, 2),
            cat(sample_states, 0), cat(sample_states, 1), cat(sample_states, 2))
```

```python
import functools
import math

import jax
import jax.numpy as jnp
from jax import lax
from jax.experimental import pallas as pl
from jax.experimental.pallas import tpu as pltpu

D_MODEL = 1024
D_FF = 2816
PLE_DIM = 256
CONV_W = 3
POOL_WINDOWS = (2, 4, 8, 16)
POOL_MAX = 16
EPS = 1e-6

N_CONV = CONV_W - 1
N_POOL = POOL_MAX - 1
CHUNK = 256
N_MIX_CHUNKS = D_MODEL // CHUNK
N_FF_CHUNKS = D_FF // CHUNK
TIME_STEPS = 64
SUB_STEPS = 32
SAMPLE_SUB_ROWS = 256
PROMPT_START_POS = 0
SAMPLE_START_POS = 16384
MIXER_NORM_AHEAD_CHUNK = 2
FFN_NORM_AHEAD_CHUNK = 8
FFN_FINISH_CHUNK = 1
WEIGHT_CHUNK_ROWS = 128
WEIGHT_STAGE_SLOTS = 4
VMEM_CAPACITY_BYTES = 64 * 1024 * 1024
VMEM_COMPILER_SCRATCH_BYTES = 6 * 1024 * 1024
VMEM_UNREQUESTABLE_BYTES = 4 * 1024 * 1024

F32 = jnp.float32
BF16 = jnp.bfloat16


def _nbytes(a):
    return math.prod(a.shape) * jnp.dtype(a.dtype).itemsize


def _vmem_limit(scratch, single_buffered=(), double_buffered=()):
    total = sum(_nbytes(s) for s in scratch if s.memory_space == pltpu.VMEM)
    total += sum(_nbytes(a) for a in single_buffered) + 2 * sum(_nbytes(a) for a in double_buffered)
    return min(total + VMEM_COMPILER_SCRATCH_BYTES, VMEM_CAPACITY_BYTES - VMEM_UNREQUESTABLE_BYTES)


def _rmsnorm(x, g):
    ms = jnp.mean(x * x, axis=-1, keepdims=True)
    return (x * lax.rsqrt(ms + EPS)) * g


def _dot(a, b):
    return jnp.dot(a, b, preferred_element_type=F32)


def _causal_conv(buf, cols, v, w_ref, halo, step, rows):
    buf[pl.ds(halo, rows), cols] = v
    y = buf[pl.ds(halo - 2 * step, rows), cols] * w_ref[0:1, cols]
    y = y + buf[pl.ds(halo - step, rows), cols] * w_ref[1:2, cols]
    return y + v * w_ref[2:3, cols]


def _mixer_blocks(n_sub, load_x, store_out, w, conv_in, pool_diff):
    def normed(sb):
        x = load_x(sb)
        return x, _rmsnorm(x, w["g_pre"][...]).astype(BF16)

    def finish(sb, x, mix):
        store_out(sb, x + _rmsnorm(mix, w["g_post"][...]))

    cur = normed(0)
    pending = None
    for sb in range(n_sub):
        x, xb = cur

        def proj(k, j, xb=xb):
            lo = k * D_MODEL + j * CHUNK
            return _dot(xb, w["w_in"][:, lo:lo + CHUNK])

        def branch_proj(j):
            return tuple(proj(k, j) for k in range(4))

        def gate_proj(j):
            return proj(4, j), proj(5, j)

        nxt = branch_proj(0)
        y_a = None
        y_p = []
        for j in range(N_MIX_CHUNKS):
            cols = slice(j * CHUNK, (j + 1) * CHUNK)
            b, c, h, u = nxt
            nxt = branch_proj(j + 1) if j + 1 < N_MIX_CHUNKS else gate_proj(0)
            if j == 0 and pending is not None:
                finish(*pending)
            if j == MIXER_NORM_AHEAD_CHUNK and sb + 1 < n_sub:
                cur = normed(sb + 1)
            z = (b * conv_in(sb, j, c * h)).astype(BF16)
            t = _dot(z, w["w_a_out"][cols, :])
            y_a = t if y_a is None else y_a + t
            d = pool_diff(sb, j, u).astype(BF16)
            y_p.append(_dot(d, w["pool_w"][cols, :]) * w["pool_scale"][:, cols])
        mix = None
        for j in range(N_MIX_CHUNKS):
            cols = slice(j * CHUNK, (j + 1) * CHUNK)
            ga, gp = nxt
            if j + 1 < N_MIX_CHUNKS:
                nxt = gate_proj(j + 1)
            merged = jax.nn.sigmoid(ga) * y_a[:, cols] + jax.nn.sigmoid(gp) * y_p[j]
            t = _dot(merged.astype(BF16), w["w_o"][cols, :])
            mix = t if mix is None else mix + t
        pending = (sb, x, mix)
    finish(*pending)


def _ffn_blocks(n_sub, load_x, load_p, store_out, w, conv_in):
    def normed(sb):
        x = load_x(sb)
        return x, _rmsnorm(x, w["g_pre"][...]).astype(BF16)

    def embedding(sb):
        return _dot(load_p(sb).astype(BF16), w["w_proj"][...])

    def finish(sb, x, f, ple):
        x = x + _rmsnorm(f, w["g_post"][...])
        gate = jax.nn.sigmoid(_dot(x.astype(BF16), w["w_gate"][...]))
        store_out(sb, x + gate * (embedding(sb) if ple is None else ple))

    cur = normed(0)
    pending = None
    for sb in range(n_sub):
        x, hb = cur

        def up(j, hb=hb):
            return (_dot(hb, w["w_up"][:, j * CHUNK:(j + 1) * CHUNK]),
                    _dot(hb, w["w_up"][:, D_FF + j * CHUNK:D_FF + (j + 1) * CHUNK]))

        ple = embedding(sb) if sb + 1 < n_sub else None
        nxt = up(0)
        f = None
        for j in range(N_FF_CHUNKS):
            cols = slice(j * CHUNK, (j + 1) * CHUNK)
            a, g = nxt
            if j + 1 < N_FF_CHUNKS:
                nxt = up(j + 1)
            if j == FFN_FINISH_CHUNK and pending is not None:
                finish(*pending)
            if j == FFN_NORM_AHEAD_CHUNK and sb + 1 < n_sub:
                cur = normed(sb + 1)
            h = jax.nn.gelu(conv_in(sb, j, a), approximate=True) * g
            t = _dot(h.astype(BF16), w["w_down"][cols, :])
            f = t if f is None else f + t
        pending = (sb, x, f, ple)
    finish(*pending)


def _block_copy(hbm, buf, sem, block, slot, seq, steps, *, to_hbm=False):
    hbm_view = hbm.at[seq, pl.ds(block * steps, steps), :]
    buf_view = buf.at[slot, :, seq, :]
    src, dst = (buf_view, hbm_view) if to_hbm else (hbm_view, buf_view)
    return pltpu.make_async_copy(src, dst, sem.at[slot, seq])


def _start_block_fetch(hbm, buf, sem, block, slot, steps, n_seq):
    for s in range(n_seq):
        _block_copy(hbm, buf, sem, block, slot, s, steps).start()


def _fetch_block(hbm, buf, sem, steps, n_seq):
    i = pl.program_id(0)
    slot = i % 2

    @pl.when(i + 1 < pl.num_programs(0))
    def _():
        _start_block_fetch(hbm, buf, sem, i + 1, 1 - slot, steps, n_seq)

    for s in range(n_seq):
        _block_copy(hbm, buf, sem, i, slot, s, steps).wait()
    return slot


def _stream_weight_as_bf16(w_hbm, w_vmem, stage, sem):
    n_rows, width = w_hbm.shape
    n_slots, chunk, _ = stage.shape
    n_chunks = n_rows // chunk
    assert n_chunks * chunk == n_rows and width <= stage.shape[2]

    def chunk_copy(c, slot):
        return pltpu.make_async_copy(w_hbm.at[pl.ds(c * chunk, chunk), :], stage.at[slot, :, 0:width], sem.at[slot])

    for c in range(min(n_slots, n_chunks)):
        chunk_copy(c, c).start()

    def body(c, carry):
        slot = c % n_slots
        chunk_copy(c, slot).wait()
        w_vmem[pl.ds(pl.multiple_of(c * chunk, chunk), chunk), :] = stage[slot, :, 0:width].astype(BF16)

        @pl.when(c + n_slots < n_chunks)
        def _():
            chunk_copy(c + n_slots, slot).start()

        return carry

    lax.fori_loop(0, n_chunks, body, 0)


def _load_weights(pairs, exports, stage, stage_sem, export_sem):
    for w_hbm, w_vmem in pairs:
        _stream_weight_as_bf16(w_hbm, w_vmem, stage, stage_sem)
    for k, ((_, w_vmem), w_out) in enumerate(zip(pairs, exports)):
        pltpu.make_async_copy(w_vmem, w_out, export_sem.at[k]).start()


def _finish_weight_exports(pairs, exports, export_sem):
    for k, ((_, w_vmem), w_out) in enumerate(zip(pairs, exports)):
        pltpu.make_async_copy(w_vmem, w_out, export_sem.at[k]).wait()


def _mixer_prompt_kernel(x_hbm, g_pre, w_in_hbm, conv_w, w_a_out_hbm, pool_w_hbm, pool_scale, w_o_hbm, g_post,
                         x1_ref, nconv_ref, npool_ref, w_in_out, w_a_out_out, pool_w_out, w_o_out,
                         xbuf, xsem, cbuf, ubuf, e0, e1, e2, inv_cnt,
                         w_in, w_a_out, pool_w, w_o, stage, stage_sem, export_sem,
                         *, steps, sub_steps, n_seq):
    i = pl.program_id(0)
    weight_pairs = ((w_in_hbm, w_in), (w_a_out_hbm, w_a_out), (pool_w_hbm, pool_w), (w_o_hbm, w_o))
    weight_exports = (w_in_out, w_a_out_out, pool_w_out, w_o_out)
    sub_rows = sub_steps * n_seq
    n_sub = steps // sub_steps
    conv_halo = N_CONV * n_seq
    pool_halo = N_POOL * n_seq
    head = POOL_MAX * n_seq
    ebufs = (e0, e1, e2)

    @pl.when(i == 0)
    def _():
        _start_block_fetch(x_hbm, xbuf, xsem, 0, 0, steps, n_seq)
        cbuf[0:conv_halo, :] = jnp.zeros((conv_halo, D_MODEL), F32)
        ubuf[0:pool_halo, :] = jnp.zeros((pool_halo, D_MODEL), F32)
        pos = lax.broadcasted_iota(jnp.int32, (POOL_MAX, n_seq, CHUNK), 0).reshape(head, CHUNK)
        for j, window in enumerate(POOL_WINDOWS):
            inv_cnt[j] = 1.0 / jnp.minimum(pos + 1, window).astype(F32)
        _load_weights(weight_pairs, weight_exports, stage, stage_sem, export_sem)

    @pl.when(i == 1)
    def _():
        for j, window in enumerate(POOL_WINDOWS):
            inv_cnt[j] = jnp.full((head, CHUNK), 1.0 / window, F32)

    slot = _fetch_block(x_hbm, xbuf, xsem, steps, n_seq)
    w = dict(g_pre=g_pre, w_in=w_in, w_a_out=w_a_out, pool_w=pool_w, pool_scale=pool_scale, w_o=w_o,
             g_post=g_post)

    def conv_in(sb, j, v):
        cols = slice(j * CHUNK, (j + 1) * CHUNK)
        y = _causal_conv(cbuf, cols, v, conv_w, conv_halo, n_seq, sub_rows)
        cbuf[0:conv_halo, cols] = cbuf[pl.ds(sub_rows, conv_halo), cols]
        return y

    def pool_diff(sb, j, u):
        cols = slice(j * CHUNK, (j + 1) * CHUNK)
        window = POOL_WINDOWS[j]
        ubuf[pl.ds(pool_halo, sub_rows), cols] = u
        n_stages = j + 1
        tsum = None
        for k in range(n_stages):
            lo = pool_halo - (window - (2 << k)) * n_seq
            n = pool_halo + sub_rows - lo
            shift = (1 << k) * n_seq
            if k == 0:
                tsum = ubuf[pl.ds(lo, n), cols] + ubuf[pl.ds(lo - shift, n), cols]
            else:
                tsum = ebufs[k - 1][pl.ds(lo, n), cols] + ebufs[k - 1][pl.ds(lo - shift, n), cols]
            if k < n_stages - 1:
                ebufs[k][pl.ds(lo, n), cols] = tsum
        keep = (window - 1) * n_seq
        ubuf[pl.ds(pool_halo - keep, keep), cols] = ubuf[pl.ds(pool_halo + sub_rows - keep, keep), cols]
        if sb > 0:
            return tsum * (1.0 / window) - u
        return jnp.concatenate([tsum[0:head, :] * inv_cnt[j] - u[0:head, :],
                                tsum[head:, :] * (1.0 / window) - u[head:, :]], axis=0)

    def load_x(sb):
        return xbuf[slot, sb * sub_steps:(sb + 1) * sub_steps].reshape(sub_rows, D_MODEL)

    def store_x1(sb, value):
        x1_ref[sb * sub_rows:(sb + 1) * sub_rows, :] = value

    _mixer_blocks(n_sub, load_x, store_x1, w, conv_in, pool_diff)

    @pl.when(i == pl.num_programs(0) - 1)
    def _():
        for k in range(N_CONV):
            nconv_ref[:, k, :] = cbuf[k * n_seq:(k + 1) * n_seq, :]
        npool_ref[...] = ubuf[pl.ds(sub_rows, pool_halo), :].reshape(N_POOL, n_seq, D_MODEL)
        _finish_weight_exports(weight_pairs, weight_exports, export_sem)


def _ffn_prompt_kernel(x_ref, p_hbm, g_pre, w_up_hbm, conv_w, w_down_hbm, g_post, w_proj_hbm, w_gate_hbm,
                       y_hbm, nffn_ref, w_up_out, w_down_out, w_proj_out, w_gate_out,
                       pbuf, psem, ybuf, ysem, fbuf,
                       w_up, w_down, w_proj, w_gate, stage, stage_sem, export_sem,
                       *, steps, sub_steps, n_seq):
    i = pl.program_id(0)
    last = pl.num_programs(0) - 1
    sub_rows = sub_steps * n_seq
    n_sub = steps // sub_steps
    conv_halo = N_CONV * n_seq
    weight_pairs = ((w_up_hbm, w_up), (w_down_hbm, w_down), (w_proj_hbm, w_proj), (w_gate_hbm, w_gate))
    weight_exports = (w_up_out, w_down_out, w_proj_out, w_gate_out)

    @pl.when(i == 0)
    def _():
        _start_block_fetch(p_hbm, pbuf, psem, 0, 0, steps, n_seq)
        fbuf[0:conv_halo, :] = jnp.zeros((conv_halo, D_FF), F32)
        _load_weights(weight_pairs, weight_exports, stage, stage_sem, export_sem)

    def wait_store(block, slot):
        for s in range(n_seq):
            _block_copy(y_hbm, ybuf, ysem, block, slot, s, steps, to_hbm=True).wait()

    slot = _fetch_block(p_hbm, pbuf, psem, steps, n_seq)

    @pl.when(i >= 2)
    def _():
        wait_store(i - 2, slot)

    w = dict(g_pre=g_pre, w_up=w_up, w_down=w_down, g_post=g_post, w_proj=w_proj, w_gate=w_gate)

    def conv_in(sb, j, v):
        cols = slice(j * CHUNK, (j + 1) * CHUNK)
        y = _causal_conv(fbuf, cols, v, conv_w, conv_halo, n_seq, sub_rows)
        fbuf[0:conv_halo, cols] = fbuf[pl.ds(sub_rows, conv_halo), cols]
        return y

    def load_x(sb):
        return x_ref[sb * sub_rows:(sb + 1) * sub_rows, :]

    def load_p(sb):
        return pbuf[slot, sb * sub_steps:(sb + 1) * sub_steps].reshape(sub_rows, PLE_DIM)

    def store_y(sb, value):
        ybuf[slot, sb * sub_steps:(sb + 1) * sub_steps] = value.reshape(sub_steps, n_seq, D_MODEL)

    _ffn_blocks(n_sub, load_x, load_p, store_y, w, conv_in)

    for s in range(n_seq):
        _block_copy(y_hbm, ybuf, ysem, i, slot, s, steps, to_hbm=True).start()

    @pl.when(i == last)
    def _():
        for k in range(N_CONV):
            nffn_ref[:, k, :] = fbuf[k * n_seq:(k + 1) * n_seq, :]
        wait_store(i - 1, 1 - slot)
        wait_store(i, slot)
        _finish_weight_exports(weight_pairs, weight_exports, export_sem)


def _resident(shape):
    return pl.BlockSpec(shape, lambda i: (0,) * len(shape), pipeline_mode=pl.Buffered(1))


def _prompt_layer(x, p, wts):
    n_seq, seq_len, _ = x.shape
    steps, sub_steps = TIME_STEPS, SUB_STEPS
    n_blocks = seq_len // steps
    assert n_blocks * steps == seq_len and n_blocks >= 2 and n_seq % 8 == 0
    assert steps % sub_steps == 0 and sub_steps > POOL_MAX
    assert PROMPT_START_POS == 0
    rows = steps * n_seq
    params = lambda limit: pltpu.CompilerParams(dimension_semantics=("arbitrary",), vmem_limit_bytes=limit)
    any_space = pl.BlockSpec(memory_space=pl.ANY)
    row_block = pl.BlockSpec((rows, D_MODEL), lambda i: (i, 0))
    row_block_shape = jax.ShapeDtypeStruct((rows, D_MODEL), F32)
    whole_block = lambda shape: pl.BlockSpec(shape, lambda i: (0,) * len(shape))
    conv_state = lambda width: (n_seq, N_CONV, width)
    pool_state = (N_POOL, n_seq, D_MODEL)

    weight_spec = lambda a: any_space if a.dtype == F32 and a.shape[0] >= WEIGHT_CHUNK_ROWS else _resident(a.shape)
    bf16_like = lambda ws: [jax.ShapeDtypeStruct(a.shape, BF16) for a in ws]
    vmem_bf16 = lambda ws: [pltpu.VMEM(a.shape, BF16) for a in ws]
    stage = lambda ws: [pltpu.VMEM((WEIGHT_STAGE_SLOTS, WEIGHT_CHUNK_ROWS, max(a.shape[1] for a in ws)), F32),
                        pltpu.SemaphoreType.DMA((WEIGHT_STAGE_SLOTS,)), pltpu.SemaphoreType.DMA((len(ws),))]

    mixer_w = (wts["g_pre_mix"], wts["w_in"], wts["conv_a_w"], wts["w_a_out"], wts["pool_w"],
               wts["pool_scale"], wts["w_o"], wts["g_post_mix"])
    mixer_mats = (wts["w_in"], wts["w_a_out"], wts["pool_w"], wts["w_o"])
    sub_rows = sub_steps * n_seq
    stage_rows = N_POOL * n_seq + sub_rows
    mixer_scratch = ([pltpu.VMEM((2, steps, n_seq, D_MODEL), F32), pltpu.SemaphoreType.DMA((2, n_seq)),
                      pltpu.VMEM((N_CONV * n_seq + sub_rows, D_MODEL), F32),
                      pltpu.VMEM((stage_rows, D_MODEL), F32),
                      pltpu.VMEM((stage_rows, D_MODEL), F32),
                      pltpu.VMEM((stage_rows, D_MODEL), F32),
                      pltpu.VMEM((stage_rows, D_MODEL), F32),
                      pltpu.VMEM((len(POOL_WINDOWS), POOL_MAX * n_seq, CHUNK), F32)]
                     + vmem_bf16(mixer_mats) + stage(mixer_mats))
    x1, nconv, npool, *mixer_bf16 = pl.pallas_call(
        functools.partial(_mixer_prompt_kernel, steps=steps, sub_steps=sub_steps, n_seq=n_seq),
        grid=(n_blocks,),
        in_specs=[any_space] + [weight_spec(a) for a in mixer_w],
        out_specs=[row_block, whole_block(conv_state(D_MODEL)), whole_block(pool_state)] + [any_space] * 4,
        out_shape=[jax.ShapeDtypeStruct((n_blocks * rows, D_MODEL), F32),
                   jax.ShapeDtypeStruct(conv_state(D_MODEL), F32),
                   jax.ShapeDtypeStruct(pool_state, F32)] + bf16_like(mixer_mats),
        scratch_shapes=mixer_scratch,
        compiler_params=params(_vmem_limit(mixer_scratch, double_buffered=[row_block_shape])),
        name="mixer_prompt",
    )(x, *mixer_w)

    ffn_w = (wts["g_pre_ffn"], wts["w_up"], wts["ffn_conv_w"], wts["w_down"], wts["g_post_ffn"],
             wts["w_ple_proj"], wts["w_ple_gate"])
    ffn_mats = (wts["w_up"], wts["w_down"], wts["w_ple_proj"], wts["w_ple_gate"])
    ffn_scratch = ([pltpu.VMEM((2, steps, n_seq, PLE_DIM), F32), pltpu.SemaphoreType.DMA((2, n_seq)),
                    pltpu.VMEM((2, steps, n_seq, D_MODEL), F32), pltpu.SemaphoreType.DMA((2, n_seq)),
                    pltpu.VMEM((N_CONV * n_seq + sub_rows, D_FF), F32)]
                   + vmem_bf16(ffn_mats) + stage(ffn_mats))
    y, nffn, *ffn_bf16 = pl.pallas_call(
        functools.partial(_ffn_prompt_kernel, steps=steps, sub_steps=sub_steps, n_seq=n_seq),
        grid=(n_blocks,),
        in_specs=[row_block, any_space] + [weight_spec(a) for a in ffn_w],
        out_specs=[any_space, whole_block(conv_state(D_FF))] + [any_space] * 4,
        out_shape=[jax.ShapeDtypeStruct((n_seq, seq_len, D_MODEL), F32),
                   jax.ShapeDtypeStruct(conv_state(D_FF), F32)] + bf16_like(ffn_mats),
        scratch_shapes=ffn_scratch,
        compiler_params=params(_vmem_limit(ffn_scratch, double_buffered=[row_block_shape])),
        name="ffn_prompt",
    )(x1, p, *ffn_w)
    bf16_wts = dict(wts, **dict(zip(("w_in", "w_a_out", "pool_w", "w_o"), mixer_bf16)),
                    **dict(zip(("w_up", "w_down", "w_ple_proj", "w_ple_gate"), ffn_bf16)))
    return y, nconv[None], jnp.transpose(npool, (1, 0, 2))[None], nffn[None], bf16_wts


def _mixer_sample_kernel(x_ref, sconv_ref, spool_ref, g_pre, w_in, conv_w, w_a_out, pool_w, pool_scale, w_o,
                         g_post, x1_ref, nconv_ref, npool_ref, cbuf, ubuf, *, batch, steps, sub_steps):
    sub_rows = sub_steps * batch
    for k in range(N_CONV):
        cbuf[k * batch:(k + 1) * batch, :] = sconv_ref[:, k, :]

    w = dict(g_pre=g_pre, w_in=w_in, w_a_out=w_a_out, pool_w=pool_w, pool_scale=pool_scale, w_o=w_o,
             g_post=g_post)
    def conv_in(sb, j, v):
        cols = slice(j * CHUNK, (j + 1) * CHUNK)
        return _causal_conv(cbuf, cols, v, conv_w, N_CONV * batch + sb * sub_rows, batch, sub_rows)

    def pool_diff(sb, j, u):
        t0 = sb * sub_steps
        cols = slice(j * CHUNK, (j + 1) * CHUNK)
        window = POOL_WINDOWS[j]
        ubuf[t0:t0 + sub_steps, :, cols] = u.reshape(sub_steps, batch, CHUNK)

        def ext(e):
            return spool_ref[e, :, cols] if e < N_POOL else ubuf[e - N_POOL, :, cols]

        diffs = []
        for t in range(t0, t0 + sub_steps):
            tsum = ext(N_POOL + t - window + 1)
            for e in range(N_POOL + t - window + 2, N_POOL + t + 1):
                tsum = tsum + ext(e)
            diffs.append(tsum * (1.0 / window) - ext(N_POOL + t))
        return jnp.concatenate(diffs, axis=0)

    def load_x(sb):
        return jnp.concatenate([x_ref[:, t, :] for t in range(sb * sub_steps, (sb + 1) * sub_steps)], axis=0)

    def store_x1(sb, value):
        x1_ref[sb * sub_rows:(sb + 1) * sub_rows, :] = value

    _mixer_blocks(steps // sub_steps, load_x, store_x1, w, conv_in, pool_diff)

    for k in range(N_CONV):
        nconv_ref[:, k, :] = cbuf[(steps + k) * batch:(steps + k + 1) * batch, :]
    for m in range(N_POOL):
        e = steps + m
        npool_ref[m] = spool_ref[e] if e < N_POOL else ubuf[e - N_POOL]


def _ffn_sample_kernel(x_ref, p_ref, sffn_ref, g_pre, w_up, conv_w, w_down, g_post, w_proj, w_gate,
                       y_ref, nffn_ref, fbuf, *, batch, steps, sub_steps):
    sub_rows = sub_steps * batch
    for k in range(N_CONV):
        fbuf[k * batch:(k + 1) * batch, :] = sffn_ref[:, k, :]

    w = dict(g_pre=g_pre, w_up=w_up, w_down=w_down, g_post=g_post, w_proj=w_proj, w_gate=w_gate)
    def conv_in(sb, j, v):
        cols = slice(j * CHUNK, (j + 1) * CHUNK)
        return _causal_conv(fbuf, cols, v, conv_w, N_CONV * batch + sb * sub_rows, batch, sub_rows)

    def load_x(sb):
        return x_ref[sb * sub_rows:(sb + 1) * sub_rows, :]

    def load_p(sb):
        return jnp.concatenate([p_ref[:, t, :] for t in range(sb * sub_steps, (sb + 1) * sub_steps)], axis=0)

    def store_y(sb, value):
        for t in range(sub_steps):
            y_ref[:, sb * sub_steps + t, :] = value[t * batch:(t + 1) * batch, :]

    _ffn_blocks(steps // sub_steps, load_x, load_p, store_y, w, conv_in)

    for k in range(N_CONV):
        nffn_ref[:, k, :] = fbuf[(steps + k) * batch:(steps + k + 1) * batch, :]


def _sample_layer(x, p, sconv, spool, sffn, layer, wts):
    batch, steps, _ = x.shape
    sub_steps = max(1, SAMPLE_SUB_ROWS // batch)
    assert steps % sub_steps == 0 and batch % 8 == 0
    assert SAMPLE_START_POS >= N_POOL
    rows = batch * steps
    params = lambda limit: pltpu.CompilerParams(vmem_limit_bytes=limit)
    whole = pl.BlockSpec(memory_space=pltpu.VMEM)
    of_layer = lambda a, k: pl.BlockSpec((None,) + a.shape[1:], lambda: (k, 0, 0, 0))
    one_layer = lambda a: jax.ShapeDtypeStruct((1,) + a.shape[1:], F32)
    x1_shape = jax.ShapeDtypeStruct((rows, D_MODEL), F32)

    mixer_w = (wts["g_pre_mix"], wts["w_in"], wts["conv_a_w"], wts["w_a_out"], wts["pool_w"],
               wts["pool_scale"], wts["w_o"], wts["g_post_mix"])
    mixer_scratch = [pltpu.VMEM(((N_CONV + steps) * batch, D_MODEL), F32), pltpu.VMEM((steps, batch, D_MODEL), F32)]
    mixer_out = [x1_shape, one_layer(sconv), one_layer(spool)]
    x1, nconv, npool = pl.pallas_call(
        functools.partial(_mixer_sample_kernel, batch=batch, steps=steps, sub_steps=sub_steps),
        in_specs=[whole, of_layer(sconv, layer), of_layer(spool, layer)] + [whole] * len(mixer_w),
        out_specs=[whole, of_layer(sconv, 0), of_layer(spool, 0)],
        out_shape=mixer_out,
        scratch_shapes=mixer_scratch,
        compiler_params=params(_vmem_limit(mixer_scratch, single_buffered=[x, *mixer_out, *mixer_out[1:], *mixer_w])),
        name="mixer_sample",
    )(x, sconv, spool, *mixer_w)

    ffn_w = (wts["g_pre_ffn"], wts["w_up"], wts["ffn_conv_w"], wts["w_down"], wts["g_post_ffn"],
             wts["w_ple_proj"], wts["w_ple_gate"])
    ffn_scratch = [pltpu.VMEM(((N_CONV + steps) * batch, D_FF), F32)]
    ffn_out = [jax.ShapeDtypeStruct(x.shape, F32), one_layer(sffn)]
    y, nffn = pl.pallas_call(
        functools.partial(_ffn_sample_kernel, batch=batch, steps=steps, sub_steps=sub_steps),
        in_specs=[whole, of_layer(p, layer), of_layer(sffn, layer)] + [whole] * len(ffn_w),
        out_specs=[whole, of_layer(sffn, 0)],
        out_shape=ffn_out,
        scratch_shapes=ffn_scratch,
        compiler_params=params(_vmem_limit(
            ffn_scratch, single_buffered=[x1_shape, one_layer(p), *ffn_out, ffn_out[1], *ffn_w])),
        name="ffn_sample",
    )(x1, p, sffn, *ffn_w)
    return y, nconv, npool, nffn


def kernel(x_prompt, x_sample, p_prompt, p_sample, state_conv_a, state_pool, state_ffn_conv, g_pre_mix, w_in,
           conv_a_w, w_a_out, pool_w, pool_scale, w_o, g_post_mix, g_pre_ffn, w_up, ffn_conv_w, w_down,
           g_post_ffn, w_ple_proj, w_ple_gate):
    depth = w_in.shape[0]
    xp, xs = x_prompt, x_sample
    prompt_states, sample_states = [], []
    time_major = lambda s: jnp.transpose(s, (0, 2, 1, 3))
    state_pool_tm = time_major(state_pool)
    for i in range(depth):
        wts = dict(
            g_pre_mix=g_pre_mix[i][None], g_post_mix=g_post_mix[i][None], g_pre_ffn=g_pre_ffn[i][None],
            g_post_ffn=g_post_ffn[i][None], pool_scale=pool_scale[i][None], conv_a_w=conv_a_w[i],
            ffn_conv_w=ffn_conv_w[i],
            w_in=w_in[i], w_a_out=w_a_out[i], pool_w=pool_w[i].reshape(D_MODEL, CHUNK),
            w_o=w_o[i], w_up=w_up[i], w_down=w_down[i], w_ple_proj=w_ple_proj[i], w_ple_gate=w_ple_gate[i])
        xp, c1, p1, f1, bf16_wts = _prompt_layer(xp, p_prompt[i], wts)
        xs, c2, p2, f2 = _sample_layer(xs, p_sample, state_conv_a, state_pool_tm, state_ffn_conv, i, bf16_wts)
        prompt_states.append((c1, p1, f1))
        sample_states.append((c2, time_major(p2), f2))
    cat = lambda states, k: states[0][k] if depth == 1 else jnp.concatenate([s[k] for s in states], axis=0)
    return (xp, xs, cat(prompt_states, 0), cat(prompt_states, 1), cat(prompt_states, 2),
            cat(sample_states, 0), cat(sample_states, 1), cat(sample_states, 2))
```

```python
import functools
import math

import jax
import jax.numpy as jnp
from jax import lax
from jax.experimental import pallas as pl
from jax.experimental.pallas import tpu as pltpu

D_MODEL = 1024
D_FF = 2816
PLE_DIM = 256
CONV_W = 3
POOL_WINDOWS = (2, 4, 8, 16)
POOL_MAX = 16
EPS = 1e-6

N_CONV = CONV_W - 1
N_POOL = POOL_MAX - 1
CHUNK = 256
N_MIX_CHUNKS = D_MODEL // CHUNK
N_FF_CHUNKS = D_FF // CHUNK
TIME_STEPS = 64
SUB_STEPS = 32
SAMPLE_SUB_ROWS = 256
PROMPT_START_POS = 0
SAMPLE_START_POS = 16384
MIXER_NORM_AHEAD_CHUNK = 2
FFN_NORM_AHEAD_CHUNK = 8
FFN_FINISH_CHUNK = 1
WEIGHT_CHUNK_ROWS = 128
WEIGHT_STAGE_SLOTS = 4
VMEM_CAPACITY_BYTES = 64 * 1024 * 1024
VMEM_COMPILER_SCRATCH_BYTES = 6 * 1024 * 1024
VMEM_UNREQUESTABLE_BYTES = 4 * 1024 * 1024

F32 = jnp.float32
BF16 = jnp.bfloat16


def _nbytes(a):
    return math.prod(a.shape) * jnp.dtype(a.dtype).itemsize


def _vmem_limit(scratch, single_buffered=(), double_buffered=()):
    total = sum(_nbytes(s) for s in scratch if s.memory_space == pltpu.VMEM)
    total += sum(_nbytes(a) for a in single_buffered) + 2 * sum(_nbytes(a) for a in double_buffered)
    return min(total + VMEM_COMPILER_SCRATCH_BYTES, VMEM_CAPACITY_BYTES - VMEM_UNREQUESTABLE_BYTES)


def _rmsnorm(x, g):
    ms = jnp.mean(x * x, axis=-1, keepdims=True)
    return (x * lax.rsqrt(ms + EPS)) * g


def _dot(a, b):
    return jnp.dot(a, b, preferred_element_type=F32)


def _causal_conv(buf, cols, v, w_ref, halo, step, rows):
    buf[pl.ds(halo, rows), cols] = v
    y = buf[pl.ds(halo - 2 * step, rows), cols] * w_ref[0:1, cols]
    y = y + buf[pl.ds(halo - step, rows), cols] * w_ref[1:2, cols]
    return y + v * w_ref[2:3, cols]


def _mixer_blocks(n_sub, load_x, store_out, w, conv_in, pool_diff):
    def normed(sb):
        x = load_x(sb)
        return x, _rmsnorm(x, w["g_pre"][...]).astype(BF16)

    def finish(sb, x, mix):
        store_out(sb, x + _rmsnorm(mix, w["g_post"][...]))

    cur = normed(0)
    pending = None
    for sb in range(n_sub):
        x, xb = cur

        def proj(k, j, xb=xb):
            lo = k * D_MODEL + j * CHUNK
            return _dot(xb, w["w_in"][:, lo:lo + CHUNK])

        def branch_proj(j):
            return tuple(proj(k, j) for k in range(4))

        def gate_proj(j):
            return proj(4, j), proj(5, j)

        nxt = branch_proj(0)
        y_a = None
        y_p = []
        for j in range(N_MIX_CHUNKS):
            cols = slice(j * CHUNK, (j + 1) * CHUNK)
            b, c, h, u = nxt
            nxt = branch_proj(j + 1) if j + 1 < N_MIX_CHUNKS else gate_proj(0)
            if j == 0 and pending is not None:
                finish(*pending)
            if j == MIXER_NORM_AHEAD_CHUNK and sb + 1 < n_sub:
                cur = normed(sb + 1)
            z = (b * conv_in(sb, j, c * h)).astype(BF16)
            t = _dot(z, w["w_a_out"][cols, :])
            y_a = t if y_a is None else y_a + t
            d = pool_diff(sb, j, u).astype(BF16)
            y_p.append(_dot(d, w["pool_w"][cols, :]) * w["pool_scale"][:, cols])
        mix = None
        for j in range(N_MIX_CHUNKS):
            cols = slice(j * CHUNK, (j + 1) * CHUNK)
            ga, gp = nxt
            if j + 1 < N_MIX_CHUNKS:
                nxt = gate_proj(j + 1)
            merged = jax.nn.sigmoid(ga) * y_a[:, cols] + jax.nn.sigmoid(gp) * y_p[j]
            t = _dot(merged.astype(BF16), w["w_o"][cols, :])
            mix = t if mix is None else mix + t
        pending = (sb, x, mix)
    finish(*pending)


def _ffn_blocks(n_sub, load_x, load_p, store_out, w, conv_in):
    def normed(sb):
        x = load_x(sb)
        return x, _rmsnorm(x, w["g_pre"][...]).astype(BF16)

    def embedding(sb):
        return _dot(load_p(sb).astype(BF16), w["w_proj"][...])

    def finish(sb, x, f, ple):
        x = x + _rmsnorm(f, w["g_post"][...])
        gate = jax.nn.sigmoid(_dot(x.astype(BF16), w["w_gate"][...]))
        store_out(sb, x + gate * (embedding(sb) if ple is None else ple))

    cur = normed(0)
    pending = None
    for sb in range(n_sub):
        x, hb = cur

        def up(j, hb=hb):
            return (_dot(hb, w["w_up"][:, j * CHUNK:(j + 1) * CHUNK]),
                    _dot(hb, w["w_up"][:, D_FF + j * CHUNK:D_FF + (j + 1) * CHUNK]))

        ple = embedding(sb) if sb + 1 < n_sub else None
        nxt = up(0)
        f = None
        for j in range(N_FF_CHUNKS):
            cols = slice(j * CHUNK, (j + 1) * CHUNK)
            a, g = nxt
            if j + 1 < N_FF_CHUNKS:
                nxt = up(j + 1)
            if j == FFN_FINISH_CHUNK and pending is not None:
                finish(*pending)
            if j == FFN_NORM_AHEAD_CHUNK and sb + 1 < n_sub:
                cur = normed(sb + 1)
            h = jax.nn.gelu(conv_in(sb, j, a), approximate=True) * g
            t = _dot(h.astype(BF16), w["w_down"][cols, :])
            f = t if f is None else f + t
        pending = (sb, x, f, ple)
    finish(*pending)


def _block_copy(hbm, buf, sem, block, slot, seq, steps, *, to_hbm=False):
    hbm_view = hbm.at[seq, pl.ds(block * steps, steps), :]
    buf_view = buf.at[slot, :, seq, :]
    src, dst = (buf_view, hbm_view) if to_hbm else (hbm_view, buf_view)
    return pltpu.make_async_copy(src, dst, sem.at[slot, seq])


def _start_block_fetch(hbm, buf, sem, block, slot, steps, n_seq):
    for s in range(n_seq):
        _block_copy(hbm, buf, sem, block, slot, s, steps).start()


def _fetch_block(hbm, buf, sem, steps, n_seq):
    i = pl.program_id(0)
    slot = i % 2

    @pl.when(i + 1 < pl.num_programs(0))
    def _():
        _start_block_fetch(hbm, buf, sem, i + 1, 1 - slot, steps, n_seq)

    for s in range(n_seq):
        _block_copy(hbm, buf, sem, i, slot, s, steps).wait()
    return slot


def _load_weights(pairs, exports, stage, stage_sem, export_sem):
    n_slots, chunk, stage_width = stage.shape
    plan = []
    for w_hbm, w_vmem in pairs:
        n_rows, width = w_hbm.shape
        assert n_rows % chunk == 0 and width <= stage_width
        plan += [(w_hbm, w_vmem, row) for row in range(0, n_rows, chunk)]

    def chunk_copy(k):
        w_hbm, _, row = plan[k]
        slot = k % n_slots
        return pltpu.make_async_copy(w_hbm.at[row:row + chunk, :], stage.at[slot, :, 0:w_hbm.shape[1]],
                                     stage_sem.at[slot])

    for k in range(min(n_slots, len(plan))):
        chunk_copy(k).start()
    for k, (w_hbm, w_vmem, row) in enumerate(plan):
        chunk_copy(k).wait()
        w_vmem[row:row + chunk, :] = stage[k % n_slots, :, 0:w_hbm.shape[1]].astype(BF16)
        if k + n_slots < len(plan):
            chunk_copy(k + n_slots).start()
    for k, ((_, w_vmem), w_out) in enumerate(zip(pairs, exports)):
        pltpu.make_async_copy(w_vmem, w_out, export_sem.at[k]).start()


def _finish_weight_exports(pairs, exports, export_sem):
    for k, ((_, w_vmem), w_out) in enumerate(zip(pairs, exports)):
        pltpu.make_async_copy(w_vmem, w_out, export_sem.at[k]).wait()


def _mixer_prompt_kernel(x_hbm, g_pre, w_in_hbm, conv_w, w_a_out_hbm, pool_w_hbm, pool_scale, w_o_hbm, g_post,
                         x1_ref, nconv_ref, npool_ref, w_in_out, w_a_out_out, pool_w_out, w_o_out,
                         xbuf, xsem, cbuf, ubuf, e0, e1, e2, inv_cnt,
                         w_in, w_a_out, pool_w, w_o, stage, stage_sem, export_sem,
                         *, steps, sub_steps, n_seq):
    i = pl.program_id(0)
    weight_pairs = ((w_in_hbm, w_in), (w_a_out_hbm, w_a_out), (pool_w_hbm, pool_w), (w_o_hbm, w_o))
    weight_exports = (w_in_out, w_a_out_out, pool_w_out, w_o_out)
    sub_rows = sub_steps * n_seq
    n_sub = steps // sub_steps
    conv_halo = N_CONV * n_seq
    pool_halo = N_POOL * n_seq
    head = POOL_MAX * n_seq
    ebufs = (e0, e1, e2)

    @pl.when(i == 0)
    def _():
        _start_block_fetch(x_hbm, xbuf, xsem, 0, 0, steps, n_seq)
        cbuf[0:conv_halo, :] = jnp.zeros((conv_halo, D_MODEL), F32)
        ubuf[0:pool_halo, :] = jnp.zeros((pool_halo, D_MODEL), F32)
        pos = lax.broadcasted_iota(jnp.int32, (POOL_MAX, n_seq, CHUNK), 0).reshape(head, CHUNK)
        for j, window in enumerate(POOL_WINDOWS):
            inv_cnt[j] = 1.0 / jnp.minimum(pos + 1, window).astype(F32)
        _load_weights(weight_pairs, weight_exports, stage, stage_sem, export_sem)

    @pl.when(i == 1)
    def _():
        for j, window in enumerate(POOL_WINDOWS):
            inv_cnt[j] = jnp.full((head, CHUNK), 1.0 / window, F32)

    slot = _fetch_block(x_hbm, xbuf, xsem, steps, n_seq)
    w = dict(g_pre=g_pre, w_in=w_in, w_a_out=w_a_out, pool_w=pool_w, pool_scale=pool_scale, w_o=w_o,
             g_post=g_post)

    def conv_in(sb, j, v):
        cols = slice(j * CHUNK, (j + 1) * CHUNK)
        y = _causal_conv(cbuf, cols, v, conv_w, conv_halo, n_seq, sub_rows)
        cbuf[0:conv_halo, cols] = cbuf[pl.ds(sub_rows, conv_halo), cols]
        return y

    def pool_diff(sb, j, u):
        cols = slice(j * CHUNK, (j + 1) * CHUNK)
        window = POOL_WINDOWS[j]
        ubuf[pl.ds(pool_halo, sub_rows), cols] = u
        n_stages = j + 1
        tsum = None
        for k in range(n_stages):
            lo = pool_halo - (window - (2 << k)) * n_seq
            n = pool_halo + sub_rows - lo
            shift = (1 << k) * n_seq
            if k == 0:
                tsum = ubuf[pl.ds(lo, n), cols] + ubuf[pl.ds(lo - shift, n), cols]
            else:
                tsum = ebufs[k - 1][pl.ds(lo, n), cols] + ebufs[k - 1][pl.ds(lo - shift, n), cols]
            if k < n_stages - 1:
                ebufs[k][pl.ds(lo, n), cols] = tsum
        keep = (window - 1) * n_seq
        ubuf[pl.ds(pool_halo - keep, keep), cols] = ubuf[pl.ds(pool_halo + sub_rows - keep, keep), cols]
        if sb > 0:
            return tsum * (1.0 / window) - u
        return jnp.concatenate([tsum[0:head, :] * inv_cnt[j] - u[0:head, :],
                                tsum[head:, :] * (1.0 / window) - u[head:, :]], axis=0)

    def load_x(sb):
        return xbuf[slot, sb * sub_steps:(sb + 1) * sub_steps].reshape(sub_rows, D_MODEL)

    def store_x1(sb, value):
        x1_ref[sb * sub_rows:(sb + 1) * sub_rows, :] = value

    _mixer_blocks(n_sub, load_x, store_x1, w, conv_in, pool_diff)

    @pl.when(i == pl.num_programs(0) - 1)
    def _():
        for k in range(N_CONV):
            nconv_ref[:, k, :] = cbuf[k * n_seq:(k + 1) * n_seq, :]
        npool_ref[...] = ubuf[pl.ds(sub_rows, pool_halo), :].reshape(N_POOL, n_seq, D_MODEL)
        _finish_weight_exports(weight_pairs, weight_exports, export_sem)


def _ffn_prompt_kernel(x_ref, p_hbm, g_pre, w_up_hbm, conv_w, w_down_hbm, g_post, w_proj_hbm, w_gate_hbm,
                       y_hbm, nffn_ref, w_up_out, w_down_out, w_proj_out, w_gate_out,
                       pbuf, psem, ybuf, ysem, fbuf,
                       w_up, w_down, w_proj, w_gate, stage, stage_sem, export_sem,
                       *, steps, sub_steps, n_seq):
    i = pl.program_id(0)
    last = pl.num_programs(0) - 1
    sub_rows = sub_steps * n_seq
    n_sub = steps // sub_steps
    conv_halo = N_CONV * n_seq
    weight_pairs = ((w_up_hbm, w_up), (w_down_hbm, w_down), (w_proj_hbm, w_proj), (w_gate_hbm, w_gate))
    weight_exports = (w_up_out, w_down_out, w_proj_out, w_gate_out)

    @pl.when(i == 0)
    def _():
        _start_block_fetch(p_hbm, pbuf, psem, 0, 0, steps, n_seq)
        fbuf[0:conv_halo, :] = jnp.zeros((conv_halo, D_FF), F32)
        _load_weights(weight_pairs, weight_exports, stage, stage_sem, export_sem)

    def wait_store(block, slot):
        for s in range(n_seq):
            _block_copy(y_hbm, ybuf, ysem, block, slot, s, steps, to_hbm=True).wait()

    slot = _fetch_block(p_hbm, pbuf, psem, steps, n_seq)

    @pl.when(i >= 2)
    def _():
        wait_store(i - 2, slot)

    w = dict(g_pre=g_pre, w_up=w_up, w_down=w_down, g_post=g_post, w_proj=w_proj, w_gate=w_gate)

    def conv_in(sb, j, v):
        cols = slice(j * CHUNK, (j + 1) * CHUNK)
        y = _causal_conv(fbuf, cols, v, conv_w, conv_halo, n_seq, sub_rows)
        fbuf[0:conv_halo, cols] = fbuf[pl.ds(sub_rows, conv_halo), cols]
        return y

    def load_x(sb):
        return x_ref[sb * sub_rows:(sb + 1) * sub_rows, :]

    def load_p(sb):
        return pbuf[slot, sb * sub_steps:(sb + 1) * sub_steps].reshape(sub_rows, PLE_DIM)

    def store_y(sb, value):
        ybuf[slot, sb * sub_steps:(sb + 1) * sub_steps] = value.reshape(sub_steps, n_seq, D_MODEL)

    _ffn_blocks(n_sub, load_x, load_p, store_y, w, conv_in)

    for s in range(n_seq):
        _block_copy(y_hbm, ybuf, ysem, i, slot, s, steps, to_hbm=True).start()

    @pl.when(i == last)
    def _():
        for k in range(N_CONV):
            nffn_ref[:, k, :] = fbuf[k * n_seq:(k + 1) * n_seq, :]
        wait_store(i - 1, 1 - slot)
        wait_store(i, slot)
        _finish_weight_exports(weight_pairs, weight_exports, export_sem)


def _resident(shape):
    return pl.BlockSpec(shape, lambda i: (0,) * len(shape), pipeline_mode=pl.Buffered(1))


def _prompt_layer(x, p, wts):
    n_seq, seq_len, _ = x.shape
    steps, sub_steps = TIME_STEPS, SUB_STEPS
    n_blocks = seq_len // steps
    assert n_blocks * steps == seq_len and n_blocks >= 2 and n_seq % 8 == 0
    assert steps % sub_steps == 0 and sub_steps > POOL_MAX
    assert PROMPT_START_POS == 0
    rows = steps * n_seq
    params = lambda limit: pltpu.CompilerParams(dimension_semantics=("arbitrary",), vmem_limit_bytes=limit)
    any_space = pl.BlockSpec(memory_space=pl.ANY)
    row_block = pl.BlockSpec((rows, D_MODEL), lambda i: (i, 0))
    row_block_shape = jax.ShapeDtypeStruct((rows, D_MODEL), F32)
    whole_block = lambda shape: pl.BlockSpec(shape, lambda i: (0,) * len(shape))
    conv_state = lambda width: (n_seq, N_CONV, width)
    pool_state = (N_POOL, n_seq, D_MODEL)

    weight_spec = lambda a: any_space if a.dtype == F32 and a.shape[0] >= WEIGHT_CHUNK_ROWS else _resident(a.shape)
    bf16_like = lambda ws: [jax.ShapeDtypeStruct(a.shape, BF16) for a in ws]
    vmem_bf16 = lambda ws: [pltpu.VMEM(a.shape, BF16) for a in ws]
    stage = lambda ws: [pltpu.VMEM((WEIGHT_STAGE_SLOTS, WEIGHT_CHUNK_ROWS, max(a.shape[1] for a in ws)), F32),
                        pltpu.SemaphoreType.DMA((WEIGHT_STAGE_SLOTS,)), pltpu.SemaphoreType.DMA((len(ws),))]

    mixer_w = (wts["g_pre_mix"], wts["w_in"], wts["conv_a_w"], wts["w_a_out"], wts["pool_w"],
               wts["pool_scale"], wts["w_o"], wts["g_post_mix"])
    mixer_mats = (wts["w_in"], wts["w_a_out"], wts["pool_w"], wts["w_o"])
    sub_rows = sub_steps * n_seq
    stage_rows = N_POOL * n_seq + sub_rows
    mixer_scratch = ([pltpu.VMEM((2, steps, n_seq, D_MODEL), F32), pltpu.SemaphoreType.DMA((2, n_seq)),
                      pltpu.VMEM((N_CONV * n_seq + sub_rows, D_MODEL), F32),
                      pltpu.VMEM((stage_rows, D_MODEL), F32),
                      pltpu.VMEM((stage_rows, D_MODEL), F32),
                      pltpu.VMEM((stage_rows, D_MODEL), F32),
                      pltpu.VMEM((stage_rows, D_MODEL), F32),
                      pltpu.VMEM((len(POOL_WINDOWS), POOL_MAX * n_seq, CHUNK), F32)]
                     + vmem_bf16(mixer_mats) + stage(mixer_mats))
    x1, nconv, npool, *mixer_bf16 = pl.pallas_call(
        functools.partial(_mixer_prompt_kernel, steps=steps, sub_steps=sub_steps, n_seq=n_seq),
        grid=(n_blocks,),
        in_specs=[any_space] + [weight_spec(a) for a in mixer_w],
        out_specs=[row_block, whole_block(conv_state(D_MODEL)), whole_block(pool_state)] + [any_space] * 4,
        out_shape=[jax.ShapeDtypeStruct((n_blocks * rows, D_MODEL), F32),
                   jax.ShapeDtypeStruct(conv_state(D_MODEL), F32),
                   jax.ShapeDtypeStruct(pool_state, F32)] + bf16_like(mixer_mats),
        scratch_shapes=mixer_scratch,
        compiler_params=params(_vmem_limit(mixer_scratch, double_buffered=[row_block_shape])),
        name="mixer_prompt",
    )(x, *mixer_w)

    ffn_w = (wts["g_pre_ffn"], wts["w_up"], wts["ffn_conv_w"], wts["w_down"], wts["g_post_ffn"],
             wts["w_ple_proj"], wts["w_ple_gate"])
    ffn_mats = (wts["w_up"], wts["w_down"], wts["w_ple_proj"], wts["w_ple_gate"])
    ffn_scratch = ([pltpu.VMEM((2, steps, n_seq, PLE_DIM), F32), pltpu.SemaphoreType.DMA((2, n_seq)),
                    pltpu.VMEM((2, steps, n_seq, D_MODEL), F32), pltpu.SemaphoreType.DMA((2, n_seq)),
                    pltpu.VMEM((N_CONV * n_seq + sub_rows, D_FF), F32)]
                   + vmem_bf16(ffn_mats) + stage(ffn_mats))
    y, nffn, *ffn_bf16 = pl.pallas_call(
        functools.partial(_ffn_prompt_kernel, steps=steps, sub_steps=sub_steps, n_seq=n_seq),
        grid=(n_blocks,),
        in_specs=[row_block, any_space] + [weight_spec(a) for a in ffn_w],
        out_specs=[any_space, whole_block(conv_state(D_FF))] + [any_space] * 4,
        out_shape=[jax.ShapeDtypeStruct((n_seq, seq_len, D_MODEL), F32),
                   jax.ShapeDtypeStruct(conv_state(D_FF), F32)] + bf16_like(ffn_mats),
        scratch_shapes=ffn_scratch,
        compiler_params=params(_vmem_limit(ffn_scratch, double_buffered=[row_block_shape])),
        name="ffn_prompt",
    )(x1, p, *ffn_w)
    bf16_wts = dict(wts, **dict(zip(("w_in", "w_a_out", "pool_w", "w_o"), mixer_bf16)),
                    **dict(zip(("w_up", "w_down", "w_ple_proj", "w_ple_gate"), ffn_bf16)))
    return y, nconv[None], jnp.transpose(npool, (1, 0, 2))[None], nffn[None], bf16_wts


def _mixer_sample_kernel(x_ref, sconv_ref, spool_ref, g_pre, w_in, conv_w, w_a_out, pool_w, pool_scale, w_o,
                         g_post, x1_ref, nconv_ref, npool_ref, cbuf, ubuf, *, batch, steps, sub_steps):
    sub_rows = sub_steps * batch
    for k in range(N_CONV):
        cbuf[k * batch:(k + 1) * batch, :] = sconv_ref[:, k, :]

    w = dict(g_pre=g_pre, w_in=w_in, w_a_out=w_a_out, pool_w=pool_w, pool_scale=pool_scale, w_o=w_o,
             g_post=g_post)
    def conv_in(sb, j, v):
        cols = slice(j * CHUNK, (j + 1) * CHUNK)
        return _causal_conv(cbuf, cols, v, conv_w, N_CONV * batch + sb * sub_rows, batch, sub_rows)

    def pool_diff(sb, j, u):
        t0 = sb * sub_steps
        cols = slice(j * CHUNK, (j + 1) * CHUNK)
        window = POOL_WINDOWS[j]
        ubuf[t0:t0 + sub_steps, :, cols] = u.reshape(sub_steps, batch, CHUNK)

        def ext(e):
            return spool_ref[e, :, cols] if e < N_POOL else ubuf[e - N_POOL, :, cols]

        diffs = []
        for t in range(t0, t0 + sub_steps):
            tsum = ext(N_POOL + t - window + 1)
            for e in range(N_POOL + t - window + 2, N_POOL + t + 1):
                tsum = tsum + ext(e)
            diffs.append(tsum * (1.0 / window) - ext(N_POOL + t))
        return jnp.concatenate(diffs, axis=0)

    def load_x(sb):
        return jnp.concatenate([x_ref[:, t, :] for t in range(sb * sub_steps, (sb + 1) * sub_steps)], axis=0)

    def store_x1(sb, value):
        x1_ref[sb * sub_rows:(sb + 1) * sub_rows, :] = value

    _mixer_blocks(steps // sub_steps, load_x, store_x1, w, conv_in, pool_diff)

    for k in range(N_CONV):
        nconv_ref[:, k, :] = cbuf[(steps + k) * batch:(steps + k + 1) * batch, :]
    for m in range(N_POOL):
        e = steps + m
        npool_ref[m] = spool_ref[e] if e < N_POOL else ubuf[e - N_POOL]


def _ffn_sample_kernel(x_ref, p_ref, sffn_ref, g_pre, w_up, conv_w, w_down, g_post, w_proj, w_gate,
                       y_ref, nffn_ref, fbuf, *, batch, steps, sub_steps):
    sub_rows = sub_steps * batch
    for k in range(N_CONV):
        fbuf[k * batch:(k + 1) * batch, :] = sffn_ref[:, k, :]

    w = dict(g_pre=g_pre, w_up=w_up, w_down=w_down, g_post=g_post, w_proj=w_proj, w_gate=w_gate)
    def conv_in(sb, j, v):
        cols = slice(j * CHUNK, (j + 1) * CHUNK)
        return _causal_conv(fbuf, cols, v, conv_w, N_CONV * batch + sb * sub_rows, batch, sub_rows)

    def load_x(sb):
        return x_ref[sb * sub_rows:(sb + 1) * sub_rows, :]

    def load_p(sb):
        return jnp.concatenate([p_ref[:, t, :] for t in range(sb * sub_steps, (sb + 1) * sub_steps)], axis=0)

    def store_y(sb, value):
        for t in range(sub_steps):
            y_ref[:, sb * sub_steps + t, :] = value[t * batch:(t + 1) * batch, :]

    _ffn_blocks(steps // sub_steps, load_x, load_p, store_y, w, conv_in)

    for k in range(N_CONV):
        nffn_ref[:, k, :] = fbuf[(steps + k) * batch:(steps + k + 1) * batch, :]


def _sample_layer(x, p, sconv, spool, sffn, layer, wts):
    batch, steps, _ = x.shape
    sub_steps = max(1, SAMPLE_SUB_ROWS // batch)
    assert steps % sub_steps == 0 and batch % 8 == 0
    assert SAMPLE_START_POS >= N_POOL
    rows = batch * steps
    params = lambda limit: pltpu.CompilerParams(vmem_limit_bytes=limit)
    whole = pl.BlockSpec(memory_space=pltpu.VMEM)
    of_layer = lambda a, k: pl.BlockSpec((None,) + a.shape[1:], lambda: (k, 0, 0, 0))
    one_layer = lambda a: jax.ShapeDtypeStruct((1,) + a.shape[1:], F32)
    x1_shape = jax.ShapeDtypeStruct((rows, D_MODEL), F32)

    mixer_w = (wts["g_pre_mix"], wts["w_in"], wts["conv_a_w"], wts["w_a_out"], wts["pool_w"],
               wts["pool_scale"], wts["w_o"], wts["g_post_mix"])
    mixer_scratch = [pltpu.VMEM(((N_CONV + steps) * batch, D_MODEL), F32), pltpu.VMEM((steps, batch, D_MODEL), F32)]
    mixer_out = [x1_shape, one_layer(sconv), one_layer(spool)]
    x1, nconv, npool = pl.pallas_call(
        functools.partial(_mixer_sample_kernel, batch=batch, steps=steps, sub_steps=sub_steps),
        in_specs=[whole, of_layer(sconv, layer), of_layer(spool, layer)] + [whole] * len(mixer_w),
        out_specs=[whole, of_layer(sconv, 0), of_layer(spool, 0)],
        out_shape=mixer_out,
        scratch_shapes=mixer_scratch,
        compiler_params=params(_vmem_limit(mixer_scratch, single_buffered=[x, *mixer_out, *mixer_out[1:], *mixer_w])),
        name="mixer_sample",
    )(x, sconv, spool, *mixer_w)

    ffn_w = (wts["g_pre_ffn"], wts["w_up"], wts["ffn_conv_w"], wts["w_down"], wts["g_post_ffn"],
             wts["w_ple_proj"], wts["w_ple_gate"])
    ffn_scratch = [pltpu.VMEM(((N_CONV + steps) * batch, D_FF), F32)]
    ffn_out = [jax.ShapeDtypeStruct(x.shape, F32), one_layer(sffn)]
    y, nffn = pl.pallas_call(
        functools.partial(_ffn_sample_kernel, batch=batch, steps=steps, sub_steps=sub_steps),
        in_specs=[whole, of_layer(p, layer), of_layer(sffn, layer)] + [whole] * len(ffn_w),
        out_specs=[whole, of_layer(sffn, 0)],
        out_shape=ffn_out,
        scratch_shapes=ffn_scratch,
        compiler_params=params(_vmem_limit(
            ffn_scratch, single_buffered=[x1_shape, one_layer(p), *ffn_out, ffn_out[1], *ffn_w])),
        name="ffn_sample",
    )(x1, p, sffn, *ffn_w)
    return y, nconv, npool, nffn


def kernel(x_prompt, x_sample, p_prompt, p_sample, state_conv_a, state_pool, state_ffn_conv, g_pre_mix, w_in,
           conv_a_w, w_a_out, pool_w, pool_scale, w_o, g_post_mix, g_pre_ffn, w_up, ffn_conv_w, w_down,
           g_post_ffn, w_ple_proj, w_ple_gate):
    depth = w_in.shape[0]
    xp, xs = x_prompt, x_sample
    prompt_states, sample_states = [], []
    time_major = lambda s: jnp.transpose(s, (0, 2, 1, 3))
    state_pool_tm = time_major(state_pool)
    for i in range(depth):
        wts = dict(
            g_pre_mix=g_pre_mix[i][None], g_post_mix=g_post_mix[i][None], g_pre_ffn=g_pre_ffn[i][None],
            g_post_ffn=g_post_ffn[i][None], pool_scale=pool_scale[i][None], conv_a_w=conv_a_w[i],
            ffn_conv_w=ffn_conv_w[i],
            w_in=w_in[i], w_a_out=w_a_out[i], pool_w=pool_w[i].reshape(D_MODEL, CHUNK),
            w_o=w_o[i], w_up=w_up[i], w_down=w_down[i], w_ple_proj=w_ple_proj[i], w_ple_gate=w_ple_gate[i])
        xp, c1, p1, f1, bf16_wts = _prompt_layer(xp, p_prompt[i], wts)
        xs, c2, p2, f2 = _sample_layer(xs, p_sample, state_conv_a, state_pool_tm, state_ffn_conv, i, bf16_wts)
        prompt_states.append((c1, p1, f1))
        sample_states.append((c2, time_major(p2), f2))
    cat = lambda states, k: states[0][k] if depth == 1 else jnp.concatenate([s[k] for s in states], axis=0)
    return (xp, xs, cat(prompt_states, 0), cat(prompt_states, 1), cat(prompt_states, 2),
            cat(sample_states, 0), cat(sample_states, 1), cat(sample_states, 2))
```

```python
import functools
import math

import jax
import jax.numpy as jnp
from jax import lax
from jax.experimental import pallas as pl
from jax.experimental.pallas import tpu as pltpu

D_MODEL = 1024
D_FF = 2816
PLE_DIM = 256
CONV_W = 3
POOL_WINDOWS = (2, 4, 8, 16)
POOL_MAX = 16
EPS = 1e-6

N_CONV = CONV_W - 1
N_POOL = POOL_MAX - 1
CHUNK = 256
N_MIX_CHUNKS = D_MODEL // CHUNK
N_FF_CHUNKS = D_FF // CHUNK
TIME_STEPS = 64
SUB_STEPS = 32
SAMPLE_SUB_ROWS = 256
PROMPT_START_POS = 0
SAMPLE_START_POS = 16384
MIXER_NORM_AHEAD_CHUNK = 2
FFN_NORM_AHEAD_CHUNK = 8
FFN_FINISH_CHUNK = 1
WEIGHT_CHUNK_ROWS = 128
WEIGHT_STAGE_SLOTS = 4
VMEM_CAPACITY_BYTES = 64 * 1024 * 1024
VMEM_COMPILER_SCRATCH_BYTES = 6 * 1024 * 1024
VMEM_UNREQUESTABLE_BYTES = 4 * 1024 * 1024

F32 = jnp.float32
BF16 = jnp.bfloat16


def _nbytes(a):
    return math.prod(a.shape) * jnp.dtype(a.dtype).itemsize


def _vmem_limit(scratch, single_buffered=(), double_buffered=()):
    total = sum(_nbytes(s) for s in scratch if s.memory_space == pltpu.VMEM)
    total += sum(_nbytes(a) for a in single_buffered) + 2 * sum(_nbytes(a) for a in double_buffered)
    return min(total + VMEM_COMPILER_SCRATCH_BYTES, VMEM_CAPACITY_BYTES - VMEM_UNREQUESTABLE_BYTES)


def _rmsnorm(x, g):
    ms = jnp.mean(x * x, axis=-1, keepdims=True)
    return (x * lax.rsqrt(ms + EPS)) * g


def _dot(a, b):
    return jnp.dot(a, b, preferred_element_type=F32)


def _causal_conv(buf, cols, v, w_ref, halo, step, rows):
    buf[pl.ds(halo, rows), cols] = v
    y = buf[pl.ds(halo - 2 * step, rows), cols] * w_ref[0:1, cols]
    y = y + buf[pl.ds(halo - step, rows), cols] * w_ref[1:2, cols]
    return y + v * w_ref[2:3, cols]


def _mixer_blocks(n_sub, load_x, store_out, w, conv_in, pool_diff):
    def normed(sb):
        x = load_x(sb)
        return x, _rmsnorm(x, w["g_pre"][...]).astype(BF16)

    def finish(sb, x, mix):
        store_out(sb, x + _rmsnorm(mix, w["g_post"][...]))

    cur = normed(0)
    pending = None
    for sb in range(n_sub):
        x, xb = cur

        def proj(k, j, xb=xb):
            lo = k * D_MODEL + j * CHUNK
            return _dot(xb, w["w_in"][:, lo:lo + CHUNK])

        def branch_proj(j):
            return tuple(proj(k, j) for k in range(4))

        def gate_proj(j):
            return proj(4, j), proj(5, j)

        nxt = branch_proj(0)
        y_a = None
        y_p = []
        for j in range(N_MIX_CHUNKS):
            cols = slice(j * CHUNK, (j + 1) * CHUNK)
            b, c, h, u = nxt
            nxt = branch_proj(j + 1) if j + 1 < N_MIX_CHUNKS else gate_proj(0)
            if j == 0 and pending is not None:
                finish(*pending)
            if j == MIXER_NORM_AHEAD_CHUNK and sb + 1 < n_sub:
                cur = normed(sb + 1)
            z = (b * conv_in(sb, j, c * h)).astype(BF16)
            t = _dot(z, w["w_a_out"][cols, :])
            y_a = t if y_a is None else y_a + t
            d = pool_diff(sb, j, u).astype(BF16)
            y_p.append(_dot(d, w["pool_w"][cols, :]) * w["pool_scale"][:, cols])
        mix = None
        for j in range(N_MIX_CHUNKS):
            cols = slice(j * CHUNK, (j + 1) * CHUNK)
            ga, gp = nxt
            if j + 1 < N_MIX_CHUNKS:
                nxt = gate_proj(j + 1)
            merged = jax.nn.sigmoid(ga) * y_a[:, cols] + jax.nn.sigmoid(gp) * y_p[j]
            t = _dot(merged.astype(BF16), w["w_o"][cols, :])
            mix = t if mix is None else mix + t
        pending = (sb, x, mix)
    finish(*pending)


def _ffn_blocks(n_sub, load_x, load_p, store_out, w, conv_in):
    def normed(sb):
        x = load_x(sb)
        return x, _rmsnorm(x, w["g_pre"][...]).astype(BF16)

    def embedding(sb):
        return _dot(load_p(sb).astype(BF16), w["w_proj"][...])

    def finish(sb, x, f, ple):
        x = x + _rmsnorm(f, w["g_post"][...])
        gate = jax.nn.sigmoid(_dot(x.astype(BF16), w["w_gate"][...]))
        store_out(sb, x + gate * (embedding(sb) if ple is None else ple))

    cur = normed(0)
    pending = None
    for sb in range(n_sub):
        x, hb = cur

        def up(j, hb=hb):
            return (_dot(hb, w["w_up"][:, j * CHUNK:(j + 1) * CHUNK]),
                    _dot(hb, w["w_up"][:, D_FF + j * CHUNK:D_FF + (j + 1) * CHUNK]))

        ple = embedding(sb) if sb + 1 < n_sub else None
        nxt = up(0)
        f = None
        for j in range(N_FF_CHUNKS):
            cols = slice(j * CHUNK, (j + 1) * CHUNK)
            a, g = nxt
            if j + 1 < N_FF_CHUNKS:
                nxt = up(j + 1)
            if j == FFN_FINISH_CHUNK and pending is not None:
                finish(*pending)
            if j == FFN_NORM_AHEAD_CHUNK and sb + 1 < n_sub:
                cur = normed(sb + 1)
            h = jax.nn.gelu(conv_in(sb, j, a), approximate=True) * g
            t = _dot(h.astype(BF16), w["w_down"][cols, :])
            f = t if f is None else f + t
        pending = (sb, x, f, ple)
    finish(*pending)


def _block_copy(hbm, buf, sem, block, slot, seq, steps, *, to_hbm=False):
    hbm_view = hbm.at[seq, pl.ds(block * steps, steps), :]
    buf_view = buf.at[slot, :, seq, :]
    src, dst = (buf_view, hbm_view) if to_hbm else (hbm_view, buf_view)
    return pltpu.make_async_copy(src, dst, sem.at[slot, seq])


def _start_block_fetch(hbm, buf, sem, block, slot, steps, n_seq, *, priority=0):
    for s in range(n_seq):
        _block_copy(hbm, buf, sem, block, slot, s, steps).start(priority=priority)


def _fetch_block(hbm, buf, sem, steps, n_seq):
    i = pl.program_id(0)
    slot = i % 2

    @pl.when(i + 1 < pl.num_programs(0))
    def _():
        _start_block_fetch(hbm, buf, sem, i + 1, 1 - slot, steps, n_seq)

    for s in range(n_seq):
        _block_copy(hbm, buf, sem, i, slot, s, steps).wait()
    return slot


def _load_weights(pairs, exports, stage, stage_sem, export_sem):
    n_slots, chunk, stage_width = stage.shape
    plan = []
    for w_hbm, w_vmem in pairs:
        n_rows, width = w_hbm.shape
        assert n_rows % chunk == 0 and width <= stage_width
        plan += [(w_hbm, w_vmem, row) for row in range(0, n_rows, chunk)]

    def chunk_copy(k):
        w_hbm, _, row = plan[k]
        slot = k % n_slots
        return pltpu.make_async_copy(w_hbm.at[row:row + chunk, :], stage.at[slot, :, 0:w_hbm.shape[1]],
                                     stage_sem.at[slot])

    for k in range(min(n_slots, len(plan))):
        chunk_copy(k).start()
    for k, (w_hbm, w_vmem, row) in enumerate(plan):
        chunk_copy(k).wait()
        w_vmem[row:row + chunk, :] = stage[k % n_slots, :, 0:w_hbm.shape[1]].astype(BF16)
        if k + n_slots < len(plan):
            chunk_copy(k + n_slots).start()
    for k, ((_, w_vmem), w_out) in enumerate(zip(pairs, exports)):
        pltpu.make_async_copy(w_vmem, w_out, export_sem.at[k]).start()


def _finish_weight_exports(pairs, exports, export_sem):
    for k, ((_, w_vmem), w_out) in enumerate(zip(pairs, exports)):
        pltpu.make_async_copy(w_vmem, w_out, export_sem.at[k]).wait()


def _mixer_prompt_kernel(x_hbm, g_pre, w_in_hbm, conv_w, w_a_out_hbm, pool_w_hbm, pool_scale, w_o_hbm, g_post,
                         x1_ref, nconv_ref, npool_ref, w_in_out, w_a_out_out, pool_w_out, w_o_out,
                         xbuf, xsem, cbuf, ubuf, e0, e1, e2, inv_cnt,
                         w_in, w_a_out, pool_w, w_o, stage, stage_sem, export_sem,
                         *, steps, sub_steps, n_seq):
    i = pl.program_id(0)
    weight_pairs = ((w_in_hbm, w_in), (w_a_out_hbm, w_a_out), (pool_w_hbm, pool_w), (w_o_hbm, w_o))
    weight_exports = (w_in_out, w_a_out_out, pool_w_out, w_o_out)
    sub_rows = sub_steps * n_seq
    n_sub = steps // sub_steps
    conv_halo = N_CONV * n_seq
    pool_halo = N_POOL * n_seq
    head = POOL_MAX * n_seq
    ebufs = (e0, e1, e2)

    @pl.when(i == 0)
    def _():
        _start_block_fetch(x_hbm, xbuf, xsem, 0, 0, steps, n_seq, priority=1)
        cbuf[0:conv_halo, :] = jnp.zeros((conv_halo, D_MODEL), F32)
        ubuf[0:pool_halo, :] = jnp.zeros((pool_halo, D_MODEL), F32)
        pos = lax.broadcasted_iota(jnp.int32, (POOL_MAX, n_seq, CHUNK), 0).reshape(head, CHUNK)
        for j, window in enumerate(POOL_WINDOWS):
            inv_cnt[j] = 1.0 / jnp.minimum(pos + 1, window).astype(F32)
        _load_weights(weight_pairs, weight_exports, stage, stage_sem, export_sem)

    @pl.when(i == 1)
    def _():
        for j, window in enumerate(POOL_WINDOWS):
            inv_cnt[j] = jnp.full((head, CHUNK), 1.0 / window, F32)

    slot = _fetch_block(x_hbm, xbuf, xsem, steps, n_seq)
    w = dict(g_pre=g_pre, w_in=w_in, w_a_out=w_a_out, pool_w=pool_w, pool_scale=pool_scale, w_o=w_o,
             g_post=g_post)

    def conv_in(sb, j, v):
        cols = slice(j * CHUNK, (j + 1) * CHUNK)
        y = _causal_conv(cbuf, cols, v, conv_w, conv_halo, n_seq, sub_rows)
        cbuf[0:conv_halo, cols] = cbuf[pl.ds(sub_rows, conv_halo), cols]
        return y

    def pool_diff(sb, j, u):
        cols = slice(j * CHUNK, (j + 1) * CHUNK)
        window = POOL_WINDOWS[j]
        ubuf[pl.ds(pool_halo, sub_rows), cols] = u
        n_stages = j + 1
        tsum = None
        for k in range(n_stages):
            lo = pool_halo - (window - (2 << k)) * n_seq
            n = pool_halo + sub_rows - lo
            shift = (1 << k) * n_seq
            if k == 0:
                tsum = ubuf[pl.ds(lo, n), cols] + ubuf[pl.ds(lo - shift, n), cols]
            else:
                tsum = ebufs[k - 1][pl.ds(lo, n), cols] + ebufs[k - 1][pl.ds(lo - shift, n), cols]
            if k < n_stages - 1:
                ebufs[k][pl.ds(lo, n), cols] = tsum
        keep = (window - 1) * n_seq
        ubuf[pl.ds(pool_halo - keep, keep), cols] = ubuf[pl.ds(pool_halo + sub_rows - keep, keep), cols]
        if sb > 0:
            return tsum * (1.0 / window) - u
        return jnp.concatenate([tsum[0:head, :] * inv_cnt[j] - u[0:head, :],
                                tsum[head:, :] * (1.0 / window) - u[head:, :]], axis=0)

    def load_x(sb):
        return xbuf[slot, sb * sub_steps:(sb + 1) * sub_steps].reshape(sub_rows, D_MODEL)

    def store_x1(sb, value):
        x1_ref[sb * sub_rows:(sb + 1) * sub_rows, :] = value

    _mixer_blocks(n_sub, load_x, store_x1, w, conv_in, pool_diff)

    @pl.when(i == pl.num_programs(0) - 1)
    def _():
        for k in range(N_CONV):
            nconv_ref[:, k, :] = cbuf[k * n_seq:(k + 1) * n_seq, :]
        npool_ref[...] = ubuf[pl.ds(sub_rows, pool_halo), :].reshape(N_POOL, n_seq, D_MODEL)
        _finish_weight_exports(weight_pairs, weight_exports, export_sem)


def _ffn_prompt_kernel(x_ref, p_hbm, g_pre, w_up_hbm, conv_w, w_down_hbm, g_post, w_proj_hbm, w_gate_hbm,
                       y_hbm, nffn_ref, w_up_out, w_down_out, w_proj_out, w_gate_out,
                       pbuf, psem, ybuf, ysem, fbuf,
                       w_up, w_down, w_proj, w_gate, stage, stage_sem, export_sem,
                       *, steps, sub_steps, n_seq):
    i = pl.program_id(0)
    last = pl.num_programs(0) - 1
    sub_rows = sub_steps * n_seq
    n_sub = steps // sub_steps
    conv_halo = N_CONV * n_seq
    weight_pairs = ((w_up_hbm, w_up), (w_down_hbm, w_down), (w_proj_hbm, w_proj), (w_gate_hbm, w_gate))
    weight_exports = (w_up_out, w_down_out, w_proj_out, w_gate_out)

    @pl.when(i == 0)
    def _():
        _start_block_fetch(p_hbm, pbuf, psem, 0, 0, steps, n_seq, priority=1)
        fbuf[0:conv_halo, :] = jnp.zeros((conv_halo, D_FF), F32)
        _load_weights(weight_pairs, weight_exports, stage, stage_sem, export_sem)

    def wait_store(block, slot):
        for s in range(n_seq):
            _block_copy(y_hbm, ybuf, ysem, block, slot, s, steps, to_hbm=True).wait()

    slot = _fetch_block(p_hbm, pbuf, psem, steps, n_seq)

    @pl.when(i >= 2)
    def _():
        wait_store(i - 2, slot)

    w = dict(g_pre=g_pre, w_up=w_up, w_down=w_down, g_post=g_post, w_proj=w_proj, w_gate=w_gate)

    def conv_in(sb, j, v):
        cols = slice(j * CHUNK, (j + 1) * CHUNK)
        y = _causal_conv(fbuf, cols, v, conv_w, conv_halo, n_seq, sub_rows)
        fbuf[0:conv_halo, cols] = fbuf[pl.ds(sub_rows, conv_halo), cols]
        return y

    def load_x(sb):
        return x_ref[sb * sub_rows:(sb + 1) * sub_rows, :]

    def load_p(sb):
        return pbuf[slot, sb * sub_steps:(sb + 1) * sub_steps].reshape(sub_rows, PLE_DIM)

    def store_y(sb, value):
        ybuf[slot, sb * sub_steps:(sb + 1) * sub_steps] = value.reshape(sub_steps, n_seq, D_MODEL)

    _ffn_blocks(n_sub, load_x, load_p, store_y, w, conv_in)

    for s in range(n_seq):
        _block_copy(y_hbm, ybuf, ysem, i, slot, s, steps, to_hbm=True).start()

    @pl.when(i == last)
    def _():
        for k in range(N_CONV):
            nffn_ref[:, k, :] = fbuf[k * n_seq:(k + 1) * n_seq, :]
        wait_store(i - 1, 1 - slot)
        wait_store(i, slot)
        _finish_weight_exports(weight_pairs, weight_exports, export_sem)


def _resident(shape):
    return pl.BlockSpec(shape, lambda i: (0,) * len(shape), pipeline_mode=pl.Buffered(1))


def _prompt_layer(x, p, wts):
    n_seq, seq_len, _ = x.shape
    steps, sub_steps = TIME_STEPS, SUB_STEPS
    n_blocks = seq_len // steps
    assert n_blocks * steps == seq_len and n_blocks >= 2 and n_seq % 8 == 0
    assert steps % sub_steps == 0 and sub_steps > POOL_MAX
    assert PROMPT_START_POS == 0
    rows = steps * n_seq
    params = lambda limit: pltpu.CompilerParams(dimension_semantics=("arbitrary",), vmem_limit_bytes=limit)
    any_space = pl.BlockSpec(memory_space=pl.ANY)
    row_block = pl.BlockSpec((rows, D_MODEL), lambda i: (i, 0))
    row_block_shape = jax.ShapeDtypeStruct((rows, D_MODEL), F32)
    whole_block = lambda shape: pl.BlockSpec(shape, lambda i: (0,) * len(shape))
    conv_state = lambda width: (n_seq, N_CONV, width)
    pool_state = (N_POOL, n_seq, D_MODEL)

    weight_spec = lambda a: any_space if a.dtype == F32 and a.shape[0] >= WEIGHT_CHUNK_ROWS else _resident(a.shape)
    bf16_like = lambda ws: [jax.ShapeDtypeStruct(a.shape, BF16) for a in ws]
    vmem_bf16 = lambda ws: [pltpu.VMEM(a.shape, BF16) for a in ws]
    stage = lambda ws: [pltpu.VMEM((WEIGHT_STAGE_SLOTS, WEIGHT_CHUNK_ROWS, max(a.shape[1] for a in ws)), F32),
                        pltpu.SemaphoreType.DMA((WEIGHT_STAGE_SLOTS,)), pltpu.SemaphoreType.DMA((len(ws),))]

    mixer_w = (wts["g_pre_mix"], wts["w_in"], wts["conv_a_w"], wts["w_a_out"], wts["pool_w"],
               wts["pool_scale"], wts["w_o"], wts["g_post_mix"])
    mixer_mats = (wts["w_in"], wts["w_a_out"], wts["pool_w"], wts["w_o"])
    sub_rows = sub_steps * n_seq
    stage_rows = N_POOL * n_seq + sub_rows
    mixer_scratch = ([pltpu.VMEM((2, steps, n_seq, D_MODEL), F32), pltpu.SemaphoreType.DMA((2, n_seq)),
                      pltpu.VMEM((N_CONV * n_seq + sub_rows, D_MODEL), F32),
                      pltpu.VMEM((stage_rows, D_MODEL), F32),
                      pltpu.VMEM((stage_rows, D_MODEL), F32),
                      pltpu.VMEM((stage_rows, D_MODEL), F32),
                      pltpu.VMEM((stage_rows, D_MODEL), F32),
                      pltpu.VMEM((len(POOL_WINDOWS), POOL_MAX * n_seq, CHUNK), F32)]
                     + vmem_bf16(mixer_mats) + stage(mixer_mats))
    x1, nconv, npool, *mixer_bf16 = pl.pallas_call(
        functools.partial(_mixer_prompt_kernel, steps=steps, sub_steps=sub_steps, n_seq=n_seq),
        grid=(n_blocks,),
        in_specs=[any_space] + [weight_spec(a) for a in mixer_w],
        out_specs=[row_block, whole_block(conv_state(D_MODEL)), whole_block(pool_state)] + [any_space] * 4,
        out_shape=[jax.ShapeDtypeStruct((n_blocks * rows, D_MODEL), F32),
                   jax.ShapeDtypeStruct(conv_state(D_MODEL), F32),
                   jax.ShapeDtypeStruct(pool_state, F32)] + bf16_like(mixer_mats),
        scratch_shapes=mixer_scratch,
        compiler_params=params(_vmem_limit(mixer_scratch, double_buffered=[row_block_shape])),
        name="mixer_prompt",
    )(x, *mixer_w)

    ffn_w = (wts["g_pre_ffn"], wts["w_up"], wts["ffn_conv_w"], wts["w_down"], wts["g_post_ffn"],
             wts["w_ple_proj"], wts["w_ple_gate"])
    ffn_mats = (wts["w_up"], wts["w_down"], wts["w_ple_proj"], wts["w_ple_gate"])
    ffn_scratch = ([pltpu.VMEM((2, steps, n_seq, PLE_DIM), F32), pltpu.SemaphoreType.DMA((2, n_seq)),
                    pltpu.VMEM((2, steps, n_seq, D_MODEL), F32), pltpu.SemaphoreType.DMA((2, n_seq)),
                    pltpu.VMEM((N_CONV * n_seq + sub_rows, D_FF), F32)]
                   + vmem_bf16(ffn_mats) + stage(ffn_mats))
    y, nffn, *ffn_bf16 = pl.pallas_call(
        functools.partial(_ffn_prompt_kernel, steps=steps, sub_steps=sub_steps, n_seq=n_seq),
        grid=(n_blocks,),
        in_specs=[row_block, any_space] + [weight_spec(a) for a in ffn_w],
        out_specs=[any_space, whole_block(conv_state(D_FF))] + [any_space] * 4,
        out_shape=[jax.ShapeDtypeStruct((n_seq, seq_len, D_MODEL), F32),
                   jax.ShapeDtypeStruct(conv_state(D_FF), F32)] + bf16_like(ffn_mats),
        scratch_shapes=ffn_scratch,
        compiler_params=params(_vmem_limit(ffn_scratch, double_buffered=[row_block_shape])),
        name="ffn_prompt",
    )(x1, p, *ffn_w)
    bf16_wts = dict(wts, **dict(zip(("w_in", "w_a_out", "pool_w", "w_o"), mixer_bf16)),
                    **dict(zip(("w_up", "w_down", "w_ple_proj", "w_ple_gate"), ffn_bf16)))
    return y, nconv[None], jnp.transpose(npool, (1, 0, 2))[None], nffn[None], bf16_wts


def _mixer_sample_kernel(x_ref, sconv_ref, spool_ref, g_pre, w_in, conv_w, w_a_out, pool_w, pool_scale, w_o,
                         g_post, x1_ref, nconv_ref, npool_ref, cbuf, ubuf, *, batch, steps, sub_steps):
    sub_rows = sub_steps * batch
    for k in range(N_CONV):
        cbuf[k * batch:(k + 1) * batch, :] = sconv_ref[:, k, :]

    w = dict(g_pre=g_pre, w_in=w_in, w_a_out=w_a_out, pool_w=pool_w, pool_scale=pool_scale, w_o=w_o,
             g_post=g_post)
    def conv_in(sb, j, v):
        cols = slice(j * CHUNK, (j + 1) * CHUNK)
        return _causal_conv(cbuf, cols, v, conv_w, N_CONV * batch + sb * sub_rows, batch, sub_rows)

    def pool_diff(sb, j, u):
        t0 = sb * sub_steps
        cols = slice(j * CHUNK, (j + 1) * CHUNK)
        window = POOL_WINDOWS[j]
        ubuf[t0:t0 + sub_steps, :, cols] = u.reshape(sub_steps, batch, CHUNK)

        def ext(e):
            return spool_ref[e, :, cols] if e < N_POOL else ubuf[e - N_POOL, :, cols]

        diffs = []
        for t in range(t0, t0 + sub_steps):
            tsum = ext(N_POOL + t - window + 1)
            for e in range(N_POOL + t - window + 2, N_POOL + t + 1):
                tsum = tsum + ext(e)
            diffs.append(tsum * (1.0 / window) - ext(N_POOL + t))
        return jnp.concatenate(diffs, axis=0)

    def load_x(sb):
        return jnp.concatenate([x_ref[:, t, :] for t in range(sb * sub_steps, (sb + 1) * sub_steps)], axis=0)

    def store_x1(sb, value):
        x1_ref[sb * sub_rows:(sb + 1) * sub_rows, :] = value

    _mixer_blocks(steps // sub_steps, load_x, store_x1, w, conv_in, pool_diff)

    for k in range(N_CONV):
        nconv_ref[:, k, :] = cbuf[(steps + k) * batch:(steps + k + 1) * batch, :]
    for m in range(N_POOL):
        e = steps + m
        npool_ref[m] = spool_ref[e] if e < N_POOL else ubuf[e - N_POOL]


def _ffn_sample_kernel(x_ref, p_ref, sffn_ref, g_pre, w_up, conv_w, w_down, g_post, w_proj, w_gate,
                       y_ref, nffn_ref, fbuf, *, batch, steps, sub_steps):
    sub_rows = sub_steps * batch
    for k in range(N_CONV):
        fbuf[k * batch:(k + 1) * batch, :] = sffn_ref[:, k, :]

    w = dict(g_pre=g_pre, w_up=w_up, w_down=w_down, g_post=g_post, w_proj=w_proj, w_gate=w_gate)
    def conv_in(sb, j, v):
        cols = slice(j * CHUNK, (j + 1) * CHUNK)
        return _causal_conv(fbuf, cols, v, conv_w, N_CONV * batch + sb * sub_rows, batch, sub_rows)

    def load_x(sb):
        return x_ref[sb * sub_rows:(sb + 1) * sub_rows, :]

    def load_p(sb):
        return jnp.concatenate([p_ref[:, t, :] for t in range(sb * sub_steps, (sb + 1) * sub_steps)], axis=0)

    def store_y(sb, value):
        for t in range(sub_steps):
            y_ref[:, sb * sub_steps + t, :] = value[t * batch:(t + 1) * batch, :]

    _ffn_blocks(steps // sub_steps, load_x, load_p, store_y, w, conv_in)

    for k in range(N_CONV):
        nffn_ref[:, k, :] = fbuf[(steps + k) * batch:(steps + k + 1) * batch, :]


def _sample_layer(x, p, sconv, spool, sffn, layer, wts):
    batch, steps, _ = x.shape
    sub_steps = max(1, SAMPLE_SUB_ROWS // batch)
    assert steps % sub_steps == 0 and batch % 8 == 0
    assert SAMPLE_START_POS >= N_POOL
    rows = batch * steps
    params = lambda limit: pltpu.CompilerParams(vmem_limit_bytes=limit)
    whole = pl.BlockSpec(memory_space=pltpu.VMEM)
    of_layer = lambda a, k: pl.BlockSpec((None,) + a.shape[1:], lambda: (k, 0, 0, 0))
    one_layer = lambda a: jax.ShapeDtypeStruct((1,) + a.shape[1:], F32)
    x1_shape = jax.ShapeDtypeStruct((rows, D_MODEL), F32)

    mixer_w = (wts["g_pre_mix"], wts["w_in"], wts["conv_a_w"], wts["w_a_out"], wts["pool_w"],
               wts["pool_scale"], wts["w_o"], wts["g_post_mix"])
    mixer_scratch = [pltpu.VMEM(((N_CONV + steps) * batch, D_MODEL), F32), pltpu.VMEM((steps, batch, D_MODEL), F32)]
    mixer_out = [x1_shape, one_layer(sconv), one_layer(spool)]
    x1, nconv, npool = pl.pallas_call(
        functools.partial(_mixer_sample_kernel, batch=batch, steps=steps, sub_steps=sub_steps),
        in_specs=[whole, of_layer(sconv, layer), of_layer(spool, layer)] + [whole] * len(mixer_w),
        out_specs=[whole, of_layer(sconv, 0), of_layer(spool, 0)],
        out_shape=mixer_out,
        scratch_shapes=mixer_scratch,
        compiler_params=params(_vmem_limit(mixer_scratch, single_buffered=[x, *mixer_out, *mixer_out[1:], *mixer_w])),
        name="mixer_sample",
    )(x, sconv, spool, *mixer_w)

    ffn_w = (wts["g_pre_ffn"], wts["w_up"], wts["ffn_conv_w"], wts["w_down"], wts["g_post_ffn"],
             wts["w_ple_proj"], wts["w_ple_gate"])
    ffn_scratch = [pltpu.VMEM(((N_CONV + steps) * batch, D_FF), F32)]
    ffn_out = [jax.ShapeDtypeStruct(x.shape, F32), one_layer(sffn)]
    y, nffn = pl.pallas_call(
        functools.partial(_ffn_sample_kernel, batch=batch, steps=steps, sub_steps=sub_steps),
        in_specs=[whole, of_layer(p, layer), of_layer(sffn, layer)] + [whole] * len(ffn_w),
        out_specs=[whole, of_layer(sffn, 0)],
        out_shape=ffn_out,
        scratch_shapes=ffn_scratch,
        compiler_params=params(_vmem_limit(
            ffn_scratch, single_buffered=[x1_shape, one_layer(p), *ffn_out, ffn_out[1], *ffn_w])),
        name="ffn_sample",
    )(x1, p, sffn, *ffn_w)
    return y, nconv, npool, nffn


def kernel(x_prompt, x_sample, p_prompt, p_sample, state_conv_a, state_pool, state_ffn_conv, g_pre_mix, w_in,
           conv_a_w, w_a_out, pool_w, pool_scale, w_o, g_post_mix, g_pre_ffn, w_up, ffn_conv_w, w_down,
           g_post_ffn, w_ple_proj, w_ple_gate):
    depth = w_in.shape[0]
    xp, xs = x_prompt, x_sample
    prompt_states, sample_states = [], []
    time_major = lambda s: jnp.transpose(s, (0, 2, 1, 3))
    state_pool_tm = time_major(state_pool)
    for i in range(depth):
        wts = dict(
            g_pre_mix=g_pre_mix[i][None], g_post_mix=g_post_mix[i][None], g_pre_ffn=g_pre_ffn[i][None],
            g_post_ffn=g_post_ffn[i][None], pool_scale=pool_scale[i][None], conv_a_w=conv_a_w[i],
            ffn_conv_w=ffn_conv_w[i],
            w_in=w_in[i], w_a_out=w_a_out[i], pool_w=pool_w[i].reshape(D_MODEL, CHUNK),
            w_o=w_o[i], w_up=w_up[i], w_down=w_down[i], w_ple_proj=w_ple_proj[i], w_ple_gate=w_ple_gate[i])
        xp, c1, p1, f1, bf16_wts = _prompt_layer(xp, p_prompt[i], wts)
        xs, c2, p2, f2 = _sample_layer(xs, p_sample, state_conv_a, state_pool_tm, state_ffn_conv, i, bf16_wts)
        prompt_states.append((c1, p1, f1))
        sample_states.append((c2, time_major(p2), f2))
    cat = lambda states, k: states[0][k] if depth == 1 else jnp.concatenate([s[k] for s in states], axis=0)
    return (xp, xs, cat(prompt_states, 0), cat(prompt_states, 1), cat(prompt_states, 2),
            cat(sample_states, 0), cat(sample_states, 1), cat(sample_states, 2))
```

```python
import functools
import math

import jax
import jax.numpy as jnp
from jax import lax
from jax.experimental import pallas as pl
from jax.experimental.pallas import tpu as pltpu

D_MODEL = 1024
D_FF = 2816
PLE_DIM = 256
CONV_W = 3
POOL_WINDOWS = (2, 4, 8, 16)
POOL_MAX = 16
EPS = 1e-6

N_CONV = CONV_W - 1
N_POOL = POOL_MAX - 1
CHUNK = 256
N_MIX_CHUNKS = D_MODEL // CHUNK
N_FF_CHUNKS = D_FF // CHUNK
TIME_STEPS = 64
SUB_STEPS = 32
SAMPLE_SUB_ROWS = 256
PROMPT_START_POS = 0
SAMPLE_START_POS = 16384
MIXER_NORM_AHEAD_CHUNK = 2
FFN_NORM_AHEAD_CHUNK = 8
FFN_FINISH_CHUNK = 1
WEIGHT_CHUNK_ROWS = 128
WEIGHT_STAGE_SLOTS = 4
VMEM_CAPACITY_BYTES = 64 * 1024 * 1024
VMEM_COMPILER_SCRATCH_BYTES = 6 * 1024 * 1024
VMEM_UNREQUESTABLE_BYTES = 4 * 1024 * 1024

F32 = jnp.float32
BF16 = jnp.bfloat16


def _nbytes(a):
    return math.prod(a.shape) * jnp.dtype(a.dtype).itemsize


def _vmem_limit(scratch, single_buffered=(), double_buffered=()):
    total = sum(_nbytes(s) for s in scratch if s.memory_space == pltpu.VMEM)
    total += sum(_nbytes(a) for a in single_buffered) + 2 * sum(_nbytes(a) for a in double_buffered)
    return min(total + VMEM_COMPILER_SCRATCH_BYTES, VMEM_CAPACITY_BYTES - VMEM_UNREQUESTABLE_BYTES)


def _rmsnorm(x, g):
    ms = jnp.mean(x * x, axis=-1, keepdims=True)
    return (x * lax.rsqrt(ms + EPS)) * g


def _dot(a, b):
    return jnp.dot(a, b, preferred_element_type=F32)


def _causal_conv(buf, cols, v, w_ref, halo, step, rows):
    buf[pl.ds(halo, rows), cols] = v
    y = buf[pl.ds(halo - 2 * step, rows), cols] * w_ref[0:1, cols]
    y = y + buf[pl.ds(halo - step, rows), cols] * w_ref[1:2, cols]
    return y + v * w_ref[2:3, cols]


def _mixer_blocks(n_sub, load_x, store_out, w, conv_in, pool_diff):
    def normed(sb):
        x = load_x(sb)
        return x, _rmsnorm(x, w["g_pre"][...]).astype(BF16)

    def finish(sb, x, mix):
        store_out(sb, x + _rmsnorm(mix, w["g_post"][...]))

    cur = normed(0)
    pending = None
    for sb in range(n_sub):
        x, xb = cur

        def proj(k, j, xb=xb):
            lo = k * D_MODEL + j * CHUNK
            return _dot(xb, w["w_in"][:, lo:lo + CHUNK])

        def branch_proj(j):
            return tuple(proj(k, j) for k in range(4))

        def gate_proj(j):
            return proj(4, j), proj(5, j)

        nxt = branch_proj(0)
        y_a = None
        y_p = []
        for j in range(N_MIX_CHUNKS):
            cols = slice(j * CHUNK, (j + 1) * CHUNK)
            b, c, h, u = nxt
            nxt = branch_proj(j + 1) if j + 1 < N_MIX_CHUNKS else gate_proj(0)
            if j == 0 and pending is not None:
                finish(*pending)
            if j == MIXER_NORM_AHEAD_CHUNK and sb + 1 < n_sub:
                cur = normed(sb + 1)
            z = (b * conv_in(sb, j, c * h)).astype(BF16)
            t = _dot(z, w["w_a_out"][cols, :])
            y_a = t if y_a is None else y_a + t
            d = pool_diff(sb, j, u).astype(BF16)
            y_p.append(_dot(d, w["pool_w"][cols, :]) * w["pool_scale"][:, cols])
        mix = None
        for j in range(N_MIX_CHUNKS):
            cols = slice(j * CHUNK, (j + 1) * CHUNK)
            ga, gp = nxt
            if j + 1 < N_MIX_CHUNKS:
                nxt = gate_proj(j + 1)
            merged = jax.nn.sigmoid(ga) * y_a[:, cols] + jax.nn.sigmoid(gp) * y_p[j]
            t = _dot(merged.astype(BF16), w["w_o"][cols, :])
            mix = t if mix is None else mix + t
        pending = (sb, x, mix)
    finish(*pending)


def _ffn_blocks(n_sub, load_x, load_p, store_out, w, conv_in):
    def normed(sb):
        x = load_x(sb)
        return x, _rmsnorm(x, w["g_pre"][...]).astype(BF16)

    def embedding(sb):
        return _dot(load_p(sb).astype(BF16), w["w_proj"][...])

    def finish(sb, x, f, ple):
        x = x + _rmsnorm(f, w["g_post"][...])
        gate = jax.nn.sigmoid(_dot(x.astype(BF16), w["w_gate"][...]))
        store_out(sb, x + gate * (embedding(sb) if ple is None else ple))

    cur = normed(0)
    pending = None
    for sb in range(n_sub):
        x, hb = cur

        def up(j, hb=hb):
            return (_dot(hb, w["w_up"][:, j * CHUNK:(j + 1) * CHUNK]),
                    _dot(hb, w["w_up"][:, D_FF + j * CHUNK:D_FF + (j + 1) * CHUNK]))

        ple = embedding(sb) if sb + 1 < n_sub else None
        nxt = up(0)
        f = None
        for j in range(N_FF_CHUNKS):
            cols = slice(j * CHUNK, (j + 1) * CHUNK)
            a, g = nxt
            if j + 1 < N_FF_CHUNKS:
                nxt = up(j + 1)
            if j == FFN_FINISH_CHUNK and pending is not None:
                finish(*pending)
            if j == FFN_NORM_AHEAD_CHUNK and sb + 1 < n_sub:
                cur = normed(sb + 1)
            h = jax.nn.gelu(conv_in(sb, j, a), approximate=True) * g
            t = _dot(h.astype(BF16), w["w_down"][cols, :])
            f = t if f is None else f + t
        pending = (sb, x, f, ple)
    finish(*pending)


def _block_copy(hbm, buf, sem, block, slot, seq, steps, *, to_hbm=False):
    hbm_view = hbm.at[seq, pl.ds(block * steps, steps), :]
    buf_view = buf.at[slot, :, seq, :]
    src, dst = (buf_view, hbm_view) if to_hbm else (hbm_view, buf_view)
    return pltpu.make_async_copy(src, dst, sem.at[slot, seq])


def _start_block_fetch(hbm, buf, sem, block, slot, steps, n_seq):
    for s in range(n_seq):
        _block_copy(hbm, buf, sem, block, slot, s, steps).start()


def _fetch_block(hbm, buf, sem, steps, n_seq):
    i = pl.program_id(0)
    slot = i % 2

    @pl.when(i + 1 < pl.num_programs(0))
    def _():
        _start_block_fetch(hbm, buf, sem, i + 1, 1 - slot, steps, n_seq)

    for s in range(n_seq):
        _block_copy(hbm, buf, sem, i, slot, s, steps).wait()
    return slot


def _load_weights(pairs, exports, stage, stage_sem, export_sem):
    n_slots, chunk, stage_width = stage.shape
    plan = []
    for w_hbm, w_vmem in pairs:
        n_rows, width = w_hbm.shape
        assert n_rows % chunk == 0 and width <= stage_width
        plan += [(w_hbm, w_vmem, row) for row in range(0, n_rows, chunk)]

    def chunk_copy(k):
        w_hbm, _, row = plan[k]
        slot = k % n_slots
        return pltpu.make_async_copy(w_hbm.at[row:row + chunk, :], stage.at[slot, :, 0:w_hbm.shape[1]],
                                     stage_sem.at[slot])

    for k in range(min(n_slots, len(plan))):
        chunk_copy(k).start()
    for k, (w_hbm, w_vmem, row) in enumerate(plan):
        chunk_copy(k).wait()
        w_vmem[row:row + chunk, :] = stage[k % n_slots, :, 0:w_hbm.shape[1]].astype(BF16)
        if k + n_slots < len(plan):
            chunk_copy(k + n_slots).start()
    for k, ((_, w_vmem), w_out) in enumerate(zip(pairs, exports)):
        pltpu.make_async_copy(w_vmem, w_out, export_sem.at[k]).start()


def _finish_weight_exports(pairs, exports, export_sem):
    for k, ((_, w_vmem), w_out) in enumerate(zip(pairs, exports)):
        pltpu.make_async_copy(w_vmem, w_out, export_sem.at[k]).wait()


def _mixer_prompt_kernel(x_hbm, g_pre, w_in_hbm, conv_w, w_a_out_hbm, pool_w_hbm, pool_scale, w_o_hbm, g_post,
                         x1_ref, nconv_ref, npool_ref, w_in_out, w_a_out_out, pool_w_out, w_o_out,
                         xbuf, xsem, cbuf, ubuf, e0, e1, e2, inv_cnt,
                         w_in, w_a_out, pool_w, w_o, stage, stage_sem, export_sem,
                         *, steps, sub_steps, n_seq):
    i = pl.program_id(0)
    weight_pairs = ((w_in_hbm, w_in), (w_a_out_hbm, w_a_out), (pool_w_hbm, pool_w), (w_o_hbm, w_o))
    weight_exports = (w_in_out, w_a_out_out, pool_w_out, w_o_out)
    sub_rows = sub_steps * n_seq
    n_sub = steps // sub_steps
    conv_halo = N_CONV * n_seq
    pool_halo = N_POOL * n_seq
    head = POOL_MAX * n_seq
    ebufs = (e0, e1, e2)

    @pl.when(i == 0)
    def _():
        _start_block_fetch(x_hbm, xbuf, xsem, 0, 0, steps, n_seq)
        cbuf[0:conv_halo, :] = jnp.zeros((conv_halo, D_MODEL), F32)
        ubuf[0:pool_halo, :] = jnp.zeros((pool_halo, D_MODEL), F32)
        pos = lax.broadcasted_iota(jnp.int32, (POOL_MAX, n_seq, CHUNK), 0).reshape(head, CHUNK)
        for j, window in enumerate(POOL_WINDOWS):
            inv_cnt[j] = 1.0 / jnp.minimum(pos + 1, window).astype(F32)
        _load_weights(weight_pairs, weight_exports, stage, stage_sem, export_sem)

    @pl.when(i == 1)
    def _():
        for j, window in enumerate(POOL_WINDOWS):
            inv_cnt[j] = jnp.full((head, CHUNK), 1.0 / window, F32)

    slot = _fetch_block(x_hbm, xbuf, xsem, steps, n_seq)
    w = dict(g_pre=g_pre, w_in=w_in, w_a_out=w_a_out, pool_w=pool_w, pool_scale=pool_scale, w_o=w_o,
             g_post=g_post)

    def conv_in(sb, j, v):
        cols = slice(j * CHUNK, (j + 1) * CHUNK)
        y = _causal_conv(cbuf, cols, v, conv_w, conv_halo, n_seq, sub_rows)
        cbuf[0:conv_halo, cols] = cbuf[pl.ds(sub_rows, conv_halo), cols]
        return y

    def pool_diff(sb, j, u):
        cols = slice(j * CHUNK, (j + 1) * CHUNK)
        window = POOL_WINDOWS[j]
        ubuf[pl.ds(pool_halo, sub_rows), cols] = u
        n_stages = j + 1
        tsum = None
        for k in range(n_stages):
            lo = pool_halo - (window - (2 << k)) * n_seq
            n = pool_halo + sub_rows - lo
            shift = (1 << k) * n_seq
            if k == 0:
                tsum = ubuf[pl.ds(lo, n), cols] + ubuf[pl.ds(lo - shift, n), cols]
            else:
                tsum = ebufs[k - 1][pl.ds(lo, n), cols] + ebufs[k - 1][pl.ds(lo - shift, n), cols]
            if k < n_stages - 1:
                ebufs[k][pl.ds(lo, n), cols] = tsum
        keep = (window - 1) * n_seq
        ubuf[pl.ds(pool_halo - keep, keep), cols] = ubuf[pl.ds(pool_halo + sub_rows - keep, keep), cols]
        if sb > 0:
            return tsum * (1.0 / window) - u
        return jnp.concatenate([tsum[0:head, :] * inv_cnt[j] - u[0:head, :],
                                tsum[head:, :] * (1.0 / window) - u[head:, :]], axis=0)

    def load_x(sb):
        return xbuf[slot, sb * sub_steps:(sb + 1) * sub_steps].reshape(sub_rows, D_MODEL)

    def store_x1(sb, value):
        x1_ref[sb * sub_rows:(sb + 1) * sub_rows, :] = value

    _mixer_blocks(n_sub, load_x, store_x1, w, conv_in, pool_diff)

    @pl.when(i == pl.num_programs(0) - 1)
    def _():
        for k in range(N_CONV):
            nconv_ref[:, k, :] = cbuf[k * n_seq:(k + 1) * n_seq, :]
        npool_ref[...] = ubuf[pl.ds(sub_rows, pool_halo), :].reshape(N_POOL, n_seq, D_MODEL)
        _finish_weight_exports(weight_pairs, weight_exports, export_sem)


def _ffn_prompt_kernel(x_ref, p_hbm, g_pre, w_up_hbm, conv_w, w_down_hbm, g_post, w_proj_hbm, w_gate_hbm,
                       y_hbm, nffn_ref, w_up_out, w_down_out, w_proj_out, w_gate_out,
                       pbuf, psem, ybuf, ysem, fbuf,
                       w_up, w_down, w_proj, w_gate, stage, stage_sem, export_sem,
                       *, steps, sub_steps, n_seq):
    i = pl.program_id(0)
    last = pl.num_programs(0) - 1
    sub_rows = sub_steps * n_seq
    n_sub = steps // sub_steps
    conv_halo = N_CONV * n_seq
    weight_pairs = ((w_up_hbm, w_up), (w_down_hbm, w_down), (w_proj_hbm, w_proj), (w_gate_hbm, w_gate))
    weight_exports = (w_up_out, w_down_out, w_proj_out, w_gate_out)

    @pl.when(i == 0)
    def _():
        _start_block_fetch(p_hbm, pbuf, psem, 0, 0, steps, n_seq)
        fbuf[0:conv_halo, :] = jnp.zeros((conv_halo, D_FF), F32)
        _load_weights(weight_pairs, weight_exports, stage, stage_sem, export_sem)

    def wait_store(block, slot):
        for s in range(n_seq):
            _block_copy(y_hbm, ybuf, ysem, block, slot, s, steps, to_hbm=True).wait()

    slot = _fetch_block(p_hbm, pbuf, psem, steps, n_seq)

    @pl.when(i >= 2)
    def _():
        wait_store(i - 2, slot)

    w = dict(g_pre=g_pre, w_up=w_up, w_down=w_down, g_post=g_post, w_proj=w_proj, w_gate=w_gate)

    def conv_in(sb, j, v):
        cols = slice(j * CHUNK, (j + 1) * CHUNK)
        y = _causal_conv(fbuf, cols, v, conv_w, conv_halo, n_seq, sub_rows)
        fbuf[0:conv_halo, cols] = fbuf[pl.ds(sub_rows, conv_halo), cols]
        return y

    def load_x(sb):
        return x_ref[sb * sub_rows:(sb + 1) * sub_rows, :]

    def load_p(sb):
        return pbuf[slot, sb * sub_steps:(sb + 1) * sub_steps].reshape(sub_rows, PLE_DIM)

    def store_y(sb, value):
        ybuf[slot, sb * sub_steps:(sb + 1) * sub_steps] = value.reshape(sub_steps, n_seq, D_MODEL)

    _ffn_blocks(n_sub, load_x, load_p, store_y, w, conv_in)

    for s in range(n_seq):
        _block_copy(y_hbm, ybuf, ysem, i, slot, s, steps, to_hbm=True).start()

    @pl.when(i == last)
    def _():
        for k in range(N_CONV):
            nffn_ref[:, k, :] = fbuf[k * n_seq:(k + 1) * n_seq, :]
        wait_store(i - 1, 1 - slot)
        wait_store(i, slot)
        _finish_weight_exports(weight_pairs, weight_exports, export_sem)


def _resident(shape):
    return pl.BlockSpec(shape, lambda i: (0,) * len(shape), pipeline_mode=pl.Buffered(1))


def _prompt_layer(x, p, wts):
    n_seq, seq_len, _ = x.shape
    steps, sub_steps = TIME_STEPS, SUB_STEPS
    n_blocks = seq_len // steps
    assert n_blocks * steps == seq_len and n_blocks >= 2 and n_seq % 8 == 0
    assert steps % sub_steps == 0 and sub_steps > POOL_MAX
    assert PROMPT_START_POS == 0
    rows = steps * n_seq
    params = lambda limit: pltpu.CompilerParams(dimension_semantics=("arbitrary",), vmem_limit_bytes=limit)
    any_space = pl.BlockSpec(memory_space=pl.ANY)
    row_block = pl.BlockSpec((rows, D_MODEL), lambda i: (i, 0))
    row_block_shape = jax.ShapeDtypeStruct((rows, D_MODEL), F32)
    whole_block = lambda shape: pl.BlockSpec(shape, lambda i: (0,) * len(shape))
    conv_state = lambda width: (n_seq, N_CONV, width)
    pool_state = (N_POOL, n_seq, D_MODEL)

    weight_spec = lambda a: any_space if a.dtype == F32 and a.shape[0] >= WEIGHT_CHUNK_ROWS else _resident(a.shape)
    bf16_like = lambda ws: [jax.ShapeDtypeStruct(a.shape, BF16) for a in ws]
    vmem_bf16 = lambda ws: [pltpu.VMEM(a.shape, BF16) for a in ws]
    stage = lambda ws: [pltpu.VMEM((WEIGHT_STAGE_SLOTS, WEIGHT_CHUNK_ROWS, max(a.shape[1] for a in ws)), F32),
                        pltpu.SemaphoreType.DMA((WEIGHT_STAGE_SLOTS,)), pltpu.SemaphoreType.DMA((len(ws),))]

    mixer_w = (wts["g_pre_mix"], wts["w_in"], wts["conv_a_w"], wts["w_a_out"], wts["pool_w"],
               wts["pool_scale"], wts["w_o"], wts["g_post_mix"])
    mixer_mats = (wts["w_in"], wts["w_a_out"], wts["pool_w"], wts["w_o"])
    sub_rows = sub_steps * n_seq
    stage_rows = N_POOL * n_seq + sub_rows
    mixer_scratch = ([pltpu.VMEM((2, steps, n_seq, D_MODEL), F32), pltpu.SemaphoreType.DMA((2, n_seq)),
                      pltpu.VMEM((N_CONV * n_seq + sub_rows, D_MODEL), F32),
                      pltpu.VMEM((stage_rows, D_MODEL), F32),
                      pltpu.VMEM((stage_rows, D_MODEL), F32),
                      pltpu.VMEM((stage_rows, D_MODEL), F32),
                      pltpu.VMEM((stage_rows, D_MODEL), F32),
                      pltpu.VMEM((len(POOL_WINDOWS), POOL_MAX * n_seq, CHUNK), F32)]
                     + vmem_bf16(mixer_mats) + stage(mixer_mats))
    x1, nconv, npool, *mixer_bf16 = pl.pallas_call(
        functools.partial(_mixer_prompt_kernel, steps=steps, sub_steps=sub_steps, n_seq=n_seq),
        grid=(n_blocks,),
        in_specs=[any_space] + [weight_spec(a) for a in mixer_w],
        out_specs=[row_block, whole_block(conv_state(D_MODEL)), whole_block(pool_state)] + [any_space] * 4,
        out_shape=[jax.ShapeDtypeStruct((n_blocks * rows, D_MODEL), F32),
                   jax.ShapeDtypeStruct(conv_state(D_MODEL), F32),
                   jax.ShapeDtypeStruct(pool_state, F32)] + bf16_like(mixer_mats),
        scratch_shapes=mixer_scratch,
        compiler_params=params(_vmem_limit(mixer_scratch, double_buffered=[row_block_shape])),
        name="mixer_prompt",
    )(x, *mixer_w)

    ffn_w = (wts["g_pre_ffn"], wts["w_up"], wts["ffn_conv_w"], wts["w_down"], wts["g_post_ffn"],
             wts["w_ple_proj"], wts["w_ple_gate"])
    ffn_mats = (wts["w_up"], wts["w_down"], wts["w_ple_proj"], wts["w_ple_gate"])
    ffn_scratch = ([pltpu.VMEM((2, steps, n_seq, PLE_DIM), F32), pltpu.SemaphoreType.DMA((2, n_seq)),
                    pltpu.VMEM((2, steps, n_seq, D_MODEL), F32), pltpu.SemaphoreType.DMA((2, n_seq)),
                    pltpu.VMEM((N_CONV * n_seq + sub_rows, D_FF), F32)]
                   + vmem_bf16(ffn_mats) + stage(ffn_mats))
    y, nffn, *ffn_bf16 = pl.pallas_call(
        functools.partial(_ffn_prompt_kernel, steps=steps, sub_steps=sub_steps, n_seq=n_seq),
        grid=(n_blocks,),
        in_specs=[row_block, any_space] + [weight_spec(a) for a in ffn_w],
        out_specs=[any_space, whole_block(conv_state(D_FF))] + [any_space] * 4,
        out_shape=[jax.ShapeDtypeStruct((n_seq, seq_len, D_MODEL), F32),
                   jax.ShapeDtypeStruct(conv_state(D_FF), F32)] + bf16_like(ffn_mats),
        scratch_shapes=ffn_scratch,
        compiler_params=params(_vmem_limit(ffn_scratch, double_buffered=[row_block_shape])),
        name="ffn_prompt",
    )(x1, p, *ffn_w)
    bf16_wts = dict(wts, **dict(zip(("w_in", "w_a_out", "pool_w", "w_o"), mixer_bf16)),
                    **dict(zip(("w_up", "w_down", "w_ple_proj", "w_ple_gate"), ffn_bf16)))
    return y, nconv[None], jnp.transpose(npool, (1, 0, 2))[None], nffn[None], bf16_wts


class _AwaitedRef:
    def __init__(self, hbm, vmem, sem):
        self.ref = vmem
        self.copy = pltpu.make_async_copy(hbm, vmem, sem)
        self.copy.start()

    def __getitem__(self, idx):
        if self.copy is not None:
            self.copy.wait()
            self.copy = None
        return self.ref[idx]


def _mixer_sample_kernel(x_ref, sconv_ref, spool_ref, g_pre, w_in, conv_w, w_a_out, pool_w, pool_scale, w_o,
                         g_post, x1_ref, nconv_ref, npool_ref, cbuf, ubuf, w_a_out_v, pool_w_v, w_o_v, wsem,
                         *, batch, steps, sub_steps):
    sub_rows = sub_steps * batch
    w_a_out = _AwaitedRef(w_a_out, w_a_out_v, wsem.at[0])
    pool_w = _AwaitedRef(pool_w, pool_w_v, wsem.at[1])
    w_o = _AwaitedRef(w_o, w_o_v, wsem.at[2])
    for k in range(N_CONV):
        cbuf[k * batch:(k + 1) * batch, :] = sconv_ref[:, k, :]

    w = dict(g_pre=g_pre, w_in=w_in, w_a_out=w_a_out, pool_w=pool_w, pool_scale=pool_scale, w_o=w_o,
             g_post=g_post)
    def conv_in(sb, j, v):
        cols = slice(j * CHUNK, (j + 1) * CHUNK)
        return _causal_conv(cbuf, cols, v, conv_w, N_CONV * batch + sb * sub_rows, batch, sub_rows)

    def pool_diff(sb, j, u):
        t0 = sb * sub_steps
        cols = slice(j * CHUNK, (j + 1) * CHUNK)
        window = POOL_WINDOWS[j]
        ubuf[t0:t0 + sub_steps, :, cols] = u.reshape(sub_steps, batch, CHUNK)

        def ext(e):
            return spool_ref[e, :, cols] if e < N_POOL else ubuf[e - N_POOL, :, cols]

        diffs = []
        for t in range(t0, t0 + sub_steps):
            tsum = ext(N_POOL + t - window + 1)
            for e in range(N_POOL + t - window + 2, N_POOL + t + 1):
                tsum = tsum + ext(e)
            diffs.append(tsum * (1.0 / window) - ext(N_POOL + t))
        return jnp.concatenate(diffs, axis=0)

    def load_x(sb):
        return jnp.concatenate([x_ref[:, t, :] for t in range(sb * sub_steps, (sb + 1) * sub_steps)], axis=0)

    def store_x1(sb, value):
        x1_ref[sb * sub_rows:(sb + 1) * sub_rows, :] = value

    _mixer_blocks(steps // sub_steps, load_x, store_x1, w, conv_in, pool_diff)

    for k in range(N_CONV):
        nconv_ref[:, k, :] = cbuf[(steps + k) * batch:(steps + k + 1) * batch, :]
    for m in range(N_POOL):
        e = steps + m
        npool_ref[m] = spool_ref[e] if e < N_POOL else ubuf[e - N_POOL]


def _ffn_sample_kernel(x_ref, p_ref, sffn_ref, g_pre, w_up, conv_w, w_down, g_post, w_proj, w_gate,
                       y_ref, nffn_ref, fbuf, w_down_v, w_proj_v, w_gate_v, wsem, *, batch, steps, sub_steps):
    sub_rows = sub_steps * batch
    w_proj = _AwaitedRef(w_proj, w_proj_v, wsem.at[0])
    w_down = _AwaitedRef(w_down, w_down_v, wsem.at[1])
    w_gate = _AwaitedRef(w_gate, w_gate_v, wsem.at[2])
    for k in range(N_CONV):
        fbuf[k * batch:(k + 1) * batch, :] = sffn_ref[:, k, :]

    w = dict(g_pre=g_pre, w_up=w_up, w_down=w_down, g_post=g_post, w_proj=w_proj, w_gate=w_gate)
    def conv_in(sb, j, v):
        cols = slice(j * CHUNK, (j + 1) * CHUNK)
        return _causal_conv(fbuf, cols, v, conv_w, N_CONV * batch + sb * sub_rows, batch, sub_rows)

    def load_x(sb):
        return x_ref[sb * sub_rows:(sb + 1) * sub_rows, :]

    def load_p(sb):
        return jnp.concatenate([p_ref[:, t, :] for t in range(sb * sub_steps, (sb + 1) * sub_steps)], axis=0)

    def store_y(sb, value):
        for t in range(sub_steps):
            y_ref[:, sb * sub_steps + t, :] = value[t * batch:(t + 1) * batch, :]

    _ffn_blocks(steps // sub_steps, load_x, load_p, store_y, w, conv_in)

    for k in range(N_CONV):
        nffn_ref[:, k, :] = fbuf[(steps + k) * batch:(steps + k + 1) * batch, :]


def _sample_layer(x, p, sconv, spool, sffn, layer, wts):
    batch, steps, _ = x.shape
    sub_steps = max(1, SAMPLE_SUB_ROWS // batch)
    assert steps % sub_steps == 0 and batch % 8 == 0
    assert SAMPLE_START_POS >= N_POOL
    rows = batch * steps
    params = lambda limit: pltpu.CompilerParams(vmem_limit_bytes=limit)
    whole = pl.BlockSpec(memory_space=pltpu.VMEM)
    of_layer = lambda a, k: pl.BlockSpec((None,) + a.shape[1:], lambda: (k, 0, 0, 0))
    one_layer = lambda a: jax.ShapeDtypeStruct((1,) + a.shape[1:], F32)
    x1_shape = jax.ShapeDtypeStruct((rows, D_MODEL), F32)

    mixer_w = (wts["g_pre_mix"], wts["w_in"], wts["conv_a_w"], wts["w_a_out"], wts["pool_w"],
               wts["pool_scale"], wts["w_o"], wts["g_post_mix"])
    in_hbm = pl.BlockSpec(memory_space=pl.ANY)
    late = lambda ws: [pltpu.VMEM(a.shape, a.dtype) for a in ws] + [pltpu.SemaphoreType.DMA((len(ws),))]
    mixer_late = ("w_a_out", "pool_w", "w_o")
    mixer_scratch = ([pltpu.VMEM(((N_CONV + steps) * batch, D_MODEL), F32), pltpu.VMEM((steps, batch, D_MODEL), F32)]
                     + late([wts[k] for k in mixer_late]))
    mixer_out = [x1_shape, one_layer(sconv), one_layer(spool)]
    mixer_early = [wts[k] for k in ("g_pre_mix", "w_in", "conv_a_w", "pool_scale", "g_post_mix")]
    x1, nconv, npool = pl.pallas_call(
        functools.partial(_mixer_sample_kernel, batch=batch, steps=steps, sub_steps=sub_steps),
        in_specs=([whole, of_layer(sconv, layer), of_layer(spool, layer)]
                  + [whole, whole, whole, in_hbm, in_hbm, whole, in_hbm, whole]),
        out_specs=[whole, of_layer(sconv, 0), of_layer(spool, 0)],
        out_shape=mixer_out,
        scratch_shapes=mixer_scratch,
        compiler_params=params(_vmem_limit(mixer_scratch,
                                           single_buffered=[x, *mixer_out, *mixer_out[1:], *mixer_early])),
        name="mixer_sample",
    )(x, sconv, spool, *mixer_w)

    ffn_w = (wts["g_pre_ffn"], wts["w_up"], wts["ffn_conv_w"], wts["w_down"], wts["g_post_ffn"],
             wts["w_ple_proj"], wts["w_ple_gate"])
    ffn_late = ("w_down", "w_ple_proj", "w_ple_gate")
    ffn_scratch = [pltpu.VMEM(((N_CONV + steps) * batch, D_FF), F32)] + late([wts[k] for k in ffn_late])
    ffn_out = [jax.ShapeDtypeStruct(x.shape, F32), one_layer(sffn)]
    ffn_early = [wts[k] for k in ("g_pre_ffn", "w_up", "ffn_conv_w", "g_post_ffn")]
    y, nffn = pl.pallas_call(
        functools.partial(_ffn_sample_kernel, batch=batch, steps=steps, sub_steps=sub_steps),
        in_specs=([whole, of_layer(p, layer), of_layer(sffn, layer)]
                  + [whole, whole, whole, in_hbm, whole, in_hbm, in_hbm]),
        out_specs=[whole, of_layer(sffn, 0)],
        out_shape=ffn_out,
        scratch_shapes=ffn_scratch,
        compiler_params=params(_vmem_limit(
            ffn_scratch, single_buffered=[x1_shape, one_layer(p), *ffn_out, ffn_out[1], *ffn_early])),
        name="ffn_sample",
    )(x1, p, sffn, *ffn_w)
    return y, nconv, npool, nffn


def kernel(x_prompt, x_sample, p_prompt, p_sample, state_conv_a, state_pool, state_ffn_conv, g_pre_mix, w_in,
           conv_a_w, w_a_out, pool_w, pool_scale, w_o, g_post_mix, g_pre_ffn, w_up, ffn_conv_w, w_down,
           g_post_ffn, w_ple_proj, w_ple_gate):
    depth = w_in.shape[0]
    xp, xs = x_prompt, x_sample
    prompt_states, sample_states = [], []
    time_major = lambda s: jnp.transpose(s, (0, 2, 1, 3))
    state_pool_tm = time_major(state_pool)
    for i in range(depth):
        wts = dict(
            g_pre_mix=g_pre_mix[i][None], g_post_mix=g_post_mix[i][None], g_pre_ffn=g_pre_ffn[i][None],
            g_post_ffn=g_post_ffn[i][None], pool_scale=pool_scale[i][None], conv_a_w=conv_a_w[i],
            ffn_conv_w=ffn_conv_w[i],
            w_in=w_in[i], w_a_out=w_a_out[i], pool_w=pool_w[i].reshape(D_MODEL, CHUNK),
            w_o=w_o[i], w_up=w_up[i], w_down=w_down[i], w_ple_proj=w_ple_proj[i], w_ple_gate=w_ple_gate[i])
        xp, c1, p1, f1, bf16_wts = _prompt_layer(xp, p_prompt[i], wts)
        xs, c2, p2, f2 = _sample_layer(xs, p_sample, state_conv_a, state_pool_tm, state_ffn_conv, i, bf16_wts)
        prompt_states.append((c1, p1, f1))
        sample_states.append((c2, time_major(p2), f2))
    cat = lambda states, k: states[0][k] if depth == 1 else jnp.concatenate([s[k] for s in states], axis=0)
    return (xp, xs, cat(prompt_states, 0), cat(prompt_states, 1), cat(prompt_states, 2),
            cat(sample_states, 0), cat(sample_states, 1), cat(sample_states, 2))
```
